```python
import jax, jax.numpy as jnp
from jax import lax
import numpy as np

D_MODEL = 2048
BATCH = 4
SEQ = 2048
DEPTH = 1
DEC_BATCH = 8
DEC_SEQ = 8
PAST_LEN = 16384
PAGE_SIZE = 128

N_HEADS = 8
HEAD_DIM = 128
ATTN_WIDTH = N_HEADS * HEAD_DIM
ATTN_SCALE = HEAD_DIM ** -0.5
Q_BLOCK = 128
FORGET_BIAS = 3.0
POOL_WINDOWS = (2, 4, 8, 16)
N_POOL_GROUPS = len(POOL_WINDOWS)
POOL_WIDTH = D_MODEL // 2
GW = POOL_WIDTH // N_POOL_GROUPS
POOL_HIST = max(POOL_WINDOWS) - 1
N_EXPERTS = 32
TOP_K = 4
D_EXPERT = D_MODEL
SWIGLU_LIMIT = 7.0
SWIGLU_ALPHA = 1.702
MOE_BLOCK = 128
NORM_EPS = 1e-6
Q_OFF = 0
K_OFF = ATTN_WIDTH
V_OFF = 2 * ATTN_WIDTH
F_OFF = 3 * ATTN_WIDTH
P_OFF = F_OFF + N_HEADS
GA_OFF = P_OFF + POOL_WIDTH
GB_OFF = GA_OFF + D_MODEL
IN_WIDTH = GB_OFF + D_MODEL

kernel_name = 'fox_pool_gated_hybrid_moe_step'


def rms_norm(x, w):
    xf = x.astype(jnp.float32)
    y = xf * lax.rsqrt(jnp.mean(xf * xf, axis=-1, keepdims=True) + NORM_EPS)
    return (y * w.astype(jnp.float32)).astype(x.dtype)


def input_projection(xn, w_in, b_forget, q_norm_w, k_norm_w):
    b, l, _ = xn.shape
    z = xn @ w_in
    heads = (b, l, N_HEADS, HEAD_DIM)
    q = rms_norm(z[..., Q_OFF:K_OFF].reshape(heads), q_norm_w)
    k = rms_norm(z[..., K_OFF:V_OFF].reshape(heads), k_norm_w)
    v = z[..., V_OFF:F_OFF].reshape(heads)
    logf = jax.nn.log_sigmoid((z[..., F_OFF:P_OFF] + b_forget).astype(jnp.float32))
    u = z[..., P_OFF:GA_OFF]
    gate_a = jax.nn.sigmoid(z[..., GA_OFF:GB_OFF])
    gate_b = jax.nn.sigmoid(z[..., GB_OFF:IN_WIDTH])
    return q, k, v, logf, u, gate_a, gate_b


def fox_attention_prompt(q, k, v, logf):
    b, l, h, d = q.shape
    nb = l // Q_BLOCK
    c = jnp.cumsum(logf, axis=1)
    ct = c.transpose(0, 2, 1)
    qb = q.reshape(b, nb, Q_BLOCK, h, d).transpose(1, 0, 2, 3, 4)
    cb = c.reshape(b, nb, Q_BLOCK, h).transpose(1, 0, 2, 3)
    kpos = jnp.arange(l)

    def block(args):
        i, qi, ci = args
        s = jnp.einsum('bqhd,bkhd->bhqk', qi, k, preferred_element_type=jnp.float32) * ATTN_SCALE
        s = s + ci.transpose(0, 2, 1)[..., None] - ct[:, :, None, :]
        qpos = i * Q_BLOCK + jnp.arange(Q_BLOCK)
        s = jnp.where(kpos[None, :] <= qpos[:, None], s, -jnp.inf)
        p = jax.nn.softmax(s, axis=-1).astype(v.dtype)
        return jnp.einsum('bhqk,bkhd->bqhd', p, v)

    o = lax.map(block, (jnp.arange(nb), qb, cb))
    return o.transpose(1, 0, 2, 3, 4).reshape(b, l, ATTN_WIDTH)


def fox_attention_sample(q, k, v, logf, k_past, v_past, logf_past):
    bd, t = q.shape[0], q.shape[1]
    n_past = k_past.shape[1]
    logf_past = logf_past.astype(jnp.float32)
    d_past = lax.cumsum(logf_past, axis=1, reverse=True) - logf_past
    c_new = jnp.cumsum(logf, axis=1)
    cn = c_new.transpose(0, 2, 1)
    dp = d_past.transpose(0, 2, 1)
    s_past = jnp.einsum('bthd,bshd->bhts', q, k_past, preferred_element_type=jnp.float32) * ATTN_SCALE
    s_past = s_past + cn[..., None] + dp[:, :, None, :]
    s_new = jnp.einsum('bthd,bshd->bhts', q, k, preferred_element_type=jnp.float32) * ATTN_SCALE
    s_new = s_new + cn[..., None] - cn[:, :, None, :]
    s_new = jnp.where(jnp.tril(jnp.ones((t, t), dtype=bool)), s_new, -jnp.inf)
    p = jax.nn.softmax(jnp.concatenate([s_past, s_new], axis=-1), axis=-1).astype(v.dtype)
    o = (jnp.einsum('bhts,bshd->bthd', p[..., :n_past], v_past)
         + jnp.einsum('bhts,bshd->bthd', p[..., n_past:], v))
    return o.reshape(bd, t, ATTN_WIDTH)


def pool_branch(u_ext, first_pos, w_pool, pool_scale):
    b, l, _ = u_ext.shape
    uf = u_ext.astype(jnp.float32)
    cs = jnp.concatenate([jnp.zeros((b, 1, POOL_WIDTH), jnp.float32), jnp.cumsum(uf, axis=1)], axis=1)
    idx = jnp.arange(l)
    pos = first_pos + idx
    groups = []
    for g, w in enumerate(POOL_WINDOWS):
        csg = cs[..., g * GW:(g + 1) * GW]
        lo = jnp.maximum(idx + 1 - w, 0)
        wsum = csg[:, 1:] - csg[:, lo]
        count = jnp.minimum(pos + 1, w).astype(jnp.float32)
        groups.append(wsum / count[None, :, None])
    mean = jnp.stack(groups, axis=2)
    mixed = mean - uf.reshape(b, l, N_POOL_GROUPS, GW)
    y = jnp.einsum('blgc,gcd->blgd', mixed, w_pool.astype(jnp.float32)).reshape(b, l, POOL_WIDTH)
    return (y * pool_scale.astype(jnp.float32)).astype(u_ext.dtype)


def merge_branches(attn, pool, gate_a, gate_b, w_branch_a, w_branch_b, w_out):
    merged = gate_a * (attn @ w_branch_a) + gate_b * (pool @ w_branch_b)
    return merged @ w_out


def moe_ffn(x, router_w, router_b, w_gu, b_gu, w_dn, b_dn):
    shp = x.shape
    d = shp[-1]
    xf = x.reshape(-1, d)
    n_tok = xf.shape[0]
    logits = (xf @ router_w).astype(jnp.float32) + router_b.astype(jnp.float32)
    top_val, top_idx = lax.top_k(logits, TOP_K)
    gates = jax.nn.softmax(top_val, axis=-1)
    n_assign = n_tok * TOP_K
    flat_e = top_idx.reshape(-1)
    flat_tok = jnp.arange(n_assign, dtype=jnp.int32) // TOP_K
    order = jnp.argsort(flat_e)
    sorted_e = flat_e[order]
    counts = jnp.bincount(flat_e, length=N_EXPERTS)
    padded = (counts + MOE_BLOCK - 1) // MOE_BLOCK * MOE_BLOCK
    pad_end = jnp.cumsum(padded)
    pad_start = pad_end - padded
    sort_start = jnp.cumsum(counts) - counts
    rank = jnp.arange(n_assign) - sort_start[sorted_e]
    dest = pad_start[sorted_e] + rank
    n_blocks = -(-n_assign // MOE_BLOCK) + N_EXPERTS
    slot_tok = jnp.full((n_blocks * MOE_BLOCK,), n_tok, jnp.int32).at[dest].set(flat_tok[order])
    block_expert = jnp.minimum(
        jnp.searchsorted(pad_end, jnp.arange(n_blocks) * MOE_BLOCK, side='right'), N_EXPERTS - 1)
    x_pad = jnp.concatenate([xf, jnp.zeros((1, d), xf.dtype)], axis=0)
    xb = x_pad[slot_tok].reshape(n_blocks, MOE_BLOCK, d)

    def expert_block(args):
        xe, e = args
        hdn = xe @ w_gu[e] + b_gu[e]
        gate = jnp.minimum(hdn[:, :D_EXPERT], SWIGLU_LIMIT)
        up = jnp.clip(hdn[:, D_EXPERT:], -SWIGLU_LIMIT, SWIGLU_LIMIT)
        act = (up + 1.0) * (gate * jax.nn.sigmoid(SWIGLU_ALPHA * gate))
        return act @ w_dn[e] + b_dn[e]

    yb = lax.map(expert_block, (xb, block_expert)).reshape(-1, d)
    y_sorted = yb[dest]
    y_assign = jnp.zeros_like(y_sorted).at[order].set(y_sorted)
    y = jnp.einsum('nkd,nk->nd', y_assign.reshape(n_tok, TOP_K, d), gates.astype(y_assign.dtype))
    return y.reshape(shp)


def setup_inputs(seed: int = 0) -> dict:
    key = jax.random.key(seed)
    ks = jax.random.split(key, 24)
    f32 = jnp.float32
    n_pages = PAST_LEN // PAGE_SIZE
    n_used = DEC_BATCH * n_pages
    n_phys = n_used + (n_used + 3) // 4

    def nrm(k, shape, scale=1.0):
        return jax.random.normal(k, shape, f32) * scale

    page_table = jax.random.permutation(ks[0], n_phys)[:n_used].reshape(DEC_BATCH, n_pages).astype(jnp.int32)
    return {
        'x_prompt': nrm(ks[1], (BATCH, SEQ, D_MODEL)),
        'x_sample': nrm(ks[2], (DEC_BATCH, DEC_SEQ, D_MODEL)),
        'cache_k': nrm(ks[3], (DEPTH, n_phys, PAGE_SIZE, N_HEADS, HEAD_DIM)),
        'cache_v': nrm(ks[4], (DEPTH, n_phys, PAGE_SIZE, N_HEADS, HEAD_DIM)),
        'cache_logf': jax.nn.log_sigmoid(FORGET_BIAS + nrm(ks[5], (DEPTH, n_phys, PAGE_SIZE, N_HEADS))),
        'state_pool': nrm(ks[6], (DEPTH, DEC_BATCH, POOL_HIST, POOL_WIDTH)),
        'page_table': page_table,
        'norm1_w': 1.0 + nrm(ks[7], (DEPTH, D_MODEL), 0.05),
        'w_in': nrm(ks[8], (DEPTH, D_MODEL, IN_WIDTH), D_MODEL ** -0.5),
        'b_forget': FORGET_BIAS + nrm(ks[9], (DEPTH, N_HEADS), 0.5),
        'q_norm_w': 1.0 + nrm(ks[10], (DEPTH, HEAD_DIM), 0.05),
        'k_norm_w': 1.0 + nrm(ks[11], (DEPTH, HEAD_DIM), 0.05),
        'w_pool': nrm(ks[12], (DEPTH, N_POOL_GROUPS, GW, GW), GW ** -0.5),
        'pool_scale': 1.0 + nrm(ks[13], (DEPTH, POOL_WIDTH), 0.05),
        'w_branch_a': nrm(ks[14], (DEPTH, ATTN_WIDTH, D_MODEL), ATTN_WIDTH ** -0.5),
        'w_branch_b': nrm(ks[15], (DEPTH, POOL_WIDTH, D_MODEL), POOL_WIDTH ** -0.5),
        'w_out': nrm(ks[16], (DEPTH, D_MODEL, D_MODEL), D_MODEL ** -0.5),
        'norm2_w': 1.0 + nrm(ks[17], (DEPTH, D_MODEL), 0.05),
        'router_w': nrm(ks[18], (DEPTH, D_MODEL, N_EXPERTS), D_MODEL ** -0.5),
        'router_b': nrm(ks[19], (DEPTH, N_EXPERTS), 0.01),
        'w_gu': nrm(ks[20], (DEPTH, N_EXPERTS, D_MODEL, 2 * D_EXPERT), D_MODEL ** -0.5),
        'b_gu': nrm(ks[21], (DEPTH, N_EXPERTS, 2 * D_EXPERT), 0.01),
        'w_dn': nrm(ks[22], (DEPTH, N_EXPERTS, D_EXPERT, D_MODEL), D_EXPERT ** -0.5),
        'b_dn': nrm(ks[23], (DEPTH, N_EXPERTS, D_MODEL), 0.01),
    }


def reference(x_prompt, x_sample, cache_k, cache_v, cache_logf, state_pool, page_table,
              norm1_w, w_in, b_forget, q_norm_w, k_norm_w, w_pool, pool_scale,
              w_branch_a, w_branch_b, w_out, norm2_w, router_w, router_b,
              w_gu, b_gu, w_dn, b_dn):
    n_past = page_table.shape[1] * PAGE_SIZE
    h_p, h_s = x_prompt, x_sample
    kp, vp, fp, pp, ks, vs, fs, ps = [], [], [], [], [], [], [], []
    for l in range(DEPTH):
        q, k, v, logf, u, ga, gb = input_projection(
            rms_norm(h_p, norm1_w[l]), w_in[l], b_forget[l], q_norm_w[l], k_norm_w[l])
        attn = fox_attention_prompt(q, k, v, logf)
        pool = pool_branch(u, 0, w_pool[l], pool_scale[l])
        h_p = h_p + merge_branches(attn, pool, ga, gb, w_branch_a[l], w_branch_b[l], w_out[l])
        h_p = h_p + moe_ffn(rms_norm(h_p, norm2_w[l]), router_w[l], router_b[l],
                            w_gu[l], b_gu[l], w_dn[l], b_dn[l])
        kp.append(k)
        vp.append(v)
        fp.append(logf)
        pp.append(u[:, -POOL_HIST:])
        q, k, v, logf, u, ga, gb = input_projection(
            rms_norm(h_s, norm1_w[l]), w_in[l], b_forget[l], q_norm_w[l], k_norm_w[l])
        bd = q.shape[0]
        k_past = cache_k[l][page_table].reshape(bd, n_past, N_HEADS, HEAD_DIM)
        v_past = cache_v[l][page_table].reshape(bd, n_past, N_HEADS, HEAD_DIM)
        f_past = cache_logf[l][page_table].reshape(bd, n_past, N_HEADS)
        attn = fox_attention_sample(q, k, v, logf, k_past, v_past, f_past)
        u_ext = jnp.concatenate([state_pool[l].astype(u.dtype), u], axis=1)
        pool = pool_branch(u_ext, n_past - POOL_HIST, w_pool[l], pool_scale[l])[:, POOL_HIST:]
        h_s = h_s + merge_branches(attn, pool, ga, gb, w_branch_a[l], w_branch_b[l], w_out[l])
        h_s = h_s + moe_ffn(rms_norm(h_s, norm2_w[l]), router_w[l], router_b[l],
                            w_gu[l], b_gu[l], w_dn[l], b_dn[l])
        ks.append(k)
        vs.append(v)
        fs.append(logf)
        ps.append(u_ext[:, -POOL_HIST:])
    return (h_p, h_s, jnp.stack(kp), jnp.stack(vp), jnp.stack(fp), jnp.stack(pp),
            jnp.stack(ks), jnp.stack(vs), jnp.stack(fs), jnp.stack(ps))
```

```python
import functools

import jax
import jax.numpy as jnp
from jax import lax
from jax.experimental import pallas as pl
from jax.experimental.pallas import tpu as pltpu

F32 = jnp.float32
BF16 = jnp.bfloat16
I32 = jnp.int32

D_MODEL = 2048
N_HEADS = 8
HEAD_DIM = 128
ATTN_WIDTH = N_HEADS * HEAD_DIM
ATTN_SCALE = HEAD_DIM ** -0.5
FORGET_LANES = 128
POOL_WINDOWS = (2, 4, 8, 16)
POOL_WIDTH = D_MODEL // 2
GW = POOL_WIDTH // len(POOL_WINDOWS)
POOL_HIST = max(POOL_WINDOWS) - 1
POOL_HALO = 16
N_EXPERTS = 32
TOP_K = 4
D_EXPERT = D_MODEL
SWIGLU_LIMIT = 7.0
SWIGLU_ALPHA = 1.702
NORM_EPS = 1e-6
PAGE_SIZE = 128
NEG_BIG = -1e30

ZQ, ZK, ZV, ZU = 0, 1, 2, 3
Z_WIDTH = 3 * ATTN_WIDTH + POOL_WIDTH + 2 * D_MODEL

MOE_SUB = 128
MOE_NSUB = 9
MOE_TM = MOE_SUB * MOE_NSUB
MOE_TN = 512
VMEM_LIMIT = 56 * 1024 * 1024

HIGHEST = lax.Precision.HIGHEST


def _cparams(sem, vmem=VMEM_LIMIT):
    return pltpu.CompilerParams(dimension_semantics=sem, vmem_limit_bytes=vmem)


def _log_sigmoid(x):
    return jnp.minimum(x, 0.0) - jnp.log1p(jnp.exp(-jnp.abs(x)))


def _inproj_kernel(x_ref, n1w_ref, w_ref, wf_ref, bf_ref, qnw_ref, knw_ref,
                   z_ref, lf_ref, xn_ref):
    j = pl.program_id(1)

    @pl.when(j == 0)
    def _():
        x = x_ref[...]
        ms = jnp.mean(x * x, axis=-1, keepdims=True)
        xn = x * lax.rsqrt(ms + NORM_EPS) * n1w_ref[...]
        xn_ref[...] = xn.astype(BF16)
        zf = jnp.dot(xn_ref[...], wf_ref[...], preferred_element_type=F32) + bf_ref[...]
        lf_ref[...] = _log_sigmoid(zf)

    z = jnp.dot(xn_ref[...], w_ref[...], preferred_element_type=F32)

    @pl.when(j <= ZK)
    def _():
        nw = jnp.where(j == ZQ, qnw_ref[...], knw_ref[...])
        for h in range(N_HEADS):
            sl = slice(h * HEAD_DIM, (h + 1) * HEAD_DIM)
            zh = z[:, sl]
            ms = jnp.mean(zh * zh, axis=-1, keepdims=True)
            z_ref[:, sl] = zh * lax.rsqrt(ms + NORM_EPS) * nw[:, sl]

    @pl.when((j == ZV) | (j == ZU))
    def _():
        z_ref[...] = z

    @pl.when(j > ZU)
    def _():
        z_ref[...] = jax.nn.sigmoid(z)


def _inproj(x, n1w, w_main, w_f, b_f, qnw, knw, tm):
    m = x.shape[0]
    tn = ATTN_WIDTH
    return pl.pallas_call(
        _inproj_kernel,
        grid=(m // tm, Z_WIDTH // tn),
        in_specs=[
            pl.BlockSpec((tm, D_MODEL), lambda i, j: (i, 0)),
            pl.BlockSpec((1, D_MODEL), lambda i, j: (0, 0)),
            pl.BlockSpec((D_MODEL, tn), lambda i, j: (0, j)),
            pl.BlockSpec((D_MODEL, FORGET_LANES), lambda i, j: (0, 0)),
            pl.BlockSpec((1, FORGET_LANES), lambda i, j: (0, 0)),
            pl.BlockSpec((1, tn), lambda i, j: (0, 0)),
            pl.BlockSpec((1, tn), lambda i, j: (0, 0)),
        ],
        out_specs=[
            pl.BlockSpec((tm, tn), lambda i, j: (i, j)),
            pl.BlockSpec((tm, FORGET_LANES), lambda i, j: (i, 0)),
        ],
        out_shape=[
            jax.ShapeDtypeStruct((m, Z_WIDTH), F32),
            jax.ShapeDtypeStruct((m, FORGET_LANES), F32),
        ],
        scratch_shapes=[pltpu.VMEM((tm, D_MODEL), BF16)],
        compiler_params=_cparams(("arbitrary", "arbitrary")),
        name="inproj",
    )(x, n1w, w_main, w_f, b_f, qnw, knw)


def _cumsum_kernel(lf_ref, c_ref, carry_ref, *, bl, seg_len):
    i = pl.program_id(0)
    r = lax.broadcasted_iota(I32, (bl, bl), 0)
    c = lax.broadcasted_iota(I32, (bl, bl), 1)
    mask = c <= r
    if seg_len < bl:
        mask = mask & ((r // seg_len) == (c // seg_len))
    tri = mask.astype(F32)
    cs = jnp.dot(tri, lf_ref[...], preferred_element_type=F32, precision=HIGHEST)
    if seg_len > bl:
        @pl.when((i * bl) % seg_len == 0)
        def _():
            carry_ref[...] = jnp.zeros_like(carry_ref)
        cs = cs + carry_ref[...]
        carry_ref[...] = cs[bl - 1:bl, :]
    c_ref[...] = cs


def _cumsum_rows(lf, bl, seg_len):
    m = lf.shape[0]
    return pl.pallas_call(
        functools.partial(_cumsum_kernel, bl=bl, seg_len=seg_len),
        grid=(m // bl,),
        in_specs=[pl.BlockSpec((bl, FORGET_LANES), lambda i: (i, 0))],
        out_specs=pl.BlockSpec((bl, FORGET_LANES), lambda i: (i, 0)),
        out_shape=jax.ShapeDtypeStruct((m, FORGET_LANES), F32),
        scratch_shapes=[pltpu.VMEM((1, FORGET_LANES), F32)],
        compiler_params=_cparams(("arbitrary",)),
        name="logf_cumsum",
    )(lf)


def _pattn_kernel(q_ref, k_ref, v_ref, c_ref, ct_ref, o_ref, m_ref, l_ref, acc_ref, *, tq, tk):
    qi = pl.program_id(1)
    ki = pl.program_id(2)

    @pl.when(ki == 0)
    def _():
        m_ref[...] = jnp.full_like(m_ref, -jnp.inf)
        l_ref[...] = jnp.zeros_like(l_ref)
        acc_ref[...] = jnp.zeros_like(acc_ref)

    @pl.when(ki <= qi)
    def _():
        row = qi * tq + lax.broadcasted_iota(I32, (tq, tk), 0)
        col = ki * tk + lax.broadcasted_iota(I32, (tq, tk), 1)
        visible = col <= row
        for h in range(N_HEADS):
            sl = slice(h * HEAD_DIM, (h + 1) * HEAD_DIM)
            q = q_ref[0, :, sl].astype(BF16)
            k = k_ref[0, :, sl].astype(BF16)
            s = lax.dot_general(q, k, (((1,), (1,)), ((), ())),
                                preferred_element_type=F32) * ATTN_SCALE
            s = s + c_ref[0, :, h:h + 1] - ct_ref[0, h:h + 1, :]
            s = jnp.where(visible, s, -jnp.inf)
            m_prev = m_ref[h]
            m_new = jnp.maximum(m_prev, jnp.max(s, axis=-1, keepdims=True))
            alpha = jnp.exp(m_prev - m_new)
            p = jnp.exp(s - m_new[:, 0:1])
            l_ref[h] = alpha * l_ref[h] + jnp.sum(p, axis=-1, keepdims=True)
            pv = jnp.dot(p.astype(BF16), v_ref[0, :, sl].astype(BF16),
                         preferred_element_type=F32)
            acc_ref[:, sl] = alpha * acc_ref[:, sl] + pv
            m_ref[h] = m_new

    @pl.when(ki == qi)
    def _():
        for h in range(N_HEADS):
            sl = slice(h * HEAD_DIM, (h + 1) * HEAD_DIM)
            o_ref[0, :, sl] = (acc_ref[:, sl] / l_ref[h]).astype(o_ref.dtype)


def _prompt_attention(z3, c3, ct3, tq=512):
    b, l, _ = z3.shape
    tk = tq
    nq = l // tq
    kv_idx = lambda col: (lambda bi, qi, ki: (bi, jnp.minimum(ki, qi), col))
    return pl.pallas_call(
        functools.partial(_pattn_kernel, tq=tq, tk=tk),
        grid=(b, nq, nq),
        in_specs=[
            pl.BlockSpec((1, tq, ATTN_WIDTH), lambda bi, qi, ki: (bi, qi, ZQ)),
            pl.BlockSpec((1, tk, ATTN_WIDTH), kv_idx(ZK)),
            pl.BlockSpec((1, tk, ATTN_WIDTH), kv_idx(ZV)),
            pl.BlockSpec((1, tq, FORGET_LANES), lambda bi, qi, ki: (bi, qi, 0)),
            pl.BlockSpec((1, N_HEADS, tk), lambda bi, qi, ki: (bi, 0, jnp.minimum(ki, qi))),
        ],
        out_specs=pl.BlockSpec((1, tq, ATTN_WIDTH), lambda bi, qi, ki: (bi, qi, 0)),
        out_shape=jax.ShapeDtypeStruct((b, l, ATTN_WIDTH), BF16),
        scratch_shapes=[
            pltpu.VMEM((N_HEADS, tq, HEAD_DIM), F32),
            pltpu.VMEM((N_HEADS, tq, HEAD_DIM), F32),
            pltpu.VMEM((tq, ATTN_WIDTH), F32),
        ],
        compiler_params=_cparams(("arbitrary", "arbitrary", "arbitrary")),
        name="prompt_attention",
    )(z3, z3, z3, c3, ct3)


def _heads_to_lanes(ref):
    parts = [ref[0, pl.ds(h, PAGE_SIZE, stride=N_HEADS), :] for h in range(N_HEADS)]
    return jnp.concatenate(parts, axis=1).astype(BF16)


def _rows_per_head(x):
    return jnp.concatenate(
        [jnp.broadcast_to(x[h:h + 1, :], (8, x.shape[1])) for h in range(N_HEADS)], axis=0)


def _sattn_kernel(pt_ref, qbd_ref, cncol_ref, cnkeys_ref, knew_ref, vnew_ref, *rest, g_pages):
    del pt_ref
    k_refs = rest[0:g_pages]
    v_refs = rest[g_pages:2 * g_pages]
    lf_refs = rest[2 * g_pages:3 * g_pages]
    o_ref = rest[3 * g_pages]
    m_ref, l_ref, acc_ref, carry_ref = rest[3 * g_pages + 1:]
    jj = pl.program_id(1)
    nj = pl.num_programs(1)
    rows = N_HEADS * 8

    @pl.when(jj == 0)
    def _():
        m_ref[...] = jnp.full_like(m_ref, -jnp.inf)
        l_ref[...] = jnp.zeros_like(l_ref)
        acc_ref[...] = jnp.zeros_like(acc_ref)
        carry_ref[...] = jnp.zeros_like(carry_ref)

    qbd = qbd_ref[0]
    cncol = cncol_ref[0]
    later = (lax.broadcasted_iota(I32, (PAGE_SIZE, PAGE_SIZE), 0)
             > lax.broadcasted_iota(I32, (PAGE_SIZE, PAGE_SIZE), 1)).astype(F32)

    def scores(kcat):
        return lax.dot_general(qbd, kcat, (((1,), (1,)), ((), ())),
                               preferred_element_type=F32) * ATTN_SCALE

    def update(s, vcat, m, l, acc):
        m_new = jnp.maximum(m, jnp.max(s, axis=-1, keepdims=True))
        alpha = jnp.exp(m - m_new)
        p = jnp.exp(s - m_new)
        l = alpha * l + jnp.sum(p, axis=-1, keepdims=True)
        pv = jnp.dot(p.astype(BF16), vcat, preferred_element_type=F32)
        acc = jnp.concatenate([alpha] * N_HEADS, axis=1) * acc + pv
        return m_new, l, acc

    m, l, acc, carry = m_ref[...], l_ref[...], acc_ref[...], carry_ref[...]
    for g in range(g_pages):
        lf = _rows_per_head(lf_refs[g][0])
        suffix = jnp.dot(lf, later, preferred_element_type=F32, precision=HIGHEST)
        s = scores(_heads_to_lanes(k_refs[g])) + cncol + suffix + carry
        m, l, acc = update(s, _heads_to_lanes(v_refs[g]), m, l, acc)
        carry = carry + jnp.sum(lf, axis=-1, keepdims=True)

    @pl.when(jj < nj - 1)
    def _():
        m_ref[...] = m
        l_ref[...] = l
        acc_ref[...] = acc
        carry_ref[...] = carry

    @pl.when(jj == nj - 1)
    def _():
        s = scores(knew_ref[0].astype(BF16)) + cncol - cnkeys_ref[0]
        t_row = lax.broadcasted_iota(I32, (rows, PAGE_SIZE), 0) % 8
        s_col = lax.broadcasted_iota(I32, (rows, PAGE_SIZE), 1)
        s = jnp.where(s_col <= t_row, s, -jnp.inf)
        _, l2, acc2 = update(s, vnew_ref[0].astype(BF16), m, l, acc)
        o = acc2 / jnp.concatenate([l2] * N_HEADS, axis=1)
        for h in range(N_HEADS):
            sl = slice(h * HEAD_DIM, (h + 1) * HEAD_DIM)
            o_ref[0, :, sl] = o[h * 8:(h + 1) * 8, sl].astype(o_ref.dtype)


def _sample_attention(page_table, qbd, cncol, cnkeys, knew, vnew, ck, cv, clf_t, g_pages=4):
    bd, n_pages = page_table.shape
    rows = N_HEADS * 8
    pt_flat = page_table.reshape(-1)

    def page_idx(g):
        return lambda b, jj, pt: (pt[b * n_pages + n_pages - 1 - (jj * g_pages + g)], 0, 0)

    per_b = lambda b, jj, pt: (b, 0, 0)
    in_specs = [
        pl.BlockSpec((1, rows, ATTN_WIDTH), per_b),
        pl.BlockSpec((1, rows, 1), per_b),
        pl.BlockSpec((1, rows, PAGE_SIZE), per_b),
        pl.BlockSpec((1, PAGE_SIZE, ATTN_WIDTH), per_b),
        pl.BlockSpec((1, PAGE_SIZE, ATTN_WIDTH), per_b),
    ]
    in_specs += [pl.BlockSpec((1, PAGE_SIZE * N_HEADS, HEAD_DIM), page_idx(g)) for g in range(g_pages)]
    in_specs += [pl.BlockSpec((1, PAGE_SIZE * N_HEADS, HEAD_DIM), page_idx(g)) for g in range(g_pages)]
    in_specs += [pl.BlockSpec((1, N_HEADS, PAGE_SIZE), page_idx(g)) for g in range(g_pages)]
    grid_spec = pltpu.PrefetchScalarGridSpec(
        num_scalar_prefetch=1,
        grid=(bd, n_pages // g_pages),
        in_specs=in_specs,
        out_specs=pl.BlockSpec((1, 8, ATTN_WIDTH), per_b),
        scratch_shapes=[
            pltpu.VMEM((rows, PAGE_SIZE), F32),
            pltpu.VMEM((rows, PAGE_SIZE), F32),
            pltpu.VMEM((rows, ATTN_WIDTH), F32),
            pltpu.VMEM((rows, PAGE_SIZE), F32),
        ],
    )
    return pl.pallas_call(
        functools.partial(_sattn_kernel, g_pages=g_pages),
        grid_spec=grid_spec,
        out_shape=jax.ShapeDtypeStruct((bd, 8, ATTN_WIDTH), BF16),
        compiler_params=_cparams(("arbitrary", "arbitrary")),
        name="sample_attention",
    )(pt_flat, qbd, cncol, cnkeys, knew, vnew,
      *([ck] * g_pages), *([cv] * g_pages), *([clf_t] * g_pages))


def _pool_kernel(halo_ref, u_ref, wp_ref, ps_ref, o_ref, *, tm, base_pos, zero_first_halo):
    i = pl.program_id(1)
    halo = halo_ref[0]
    if zero_first_halo:
        halo = jnp.where(i == 0, 0.0, halo)
    u = u_ref[0]
    ext = jnp.concatenate([halo, u], axis=0)
    pos = base_pos + i * tm + lax.broadcasted_iota(I32, (tm, 1), 0)
    for g, w in enumerate(POOL_WINDOWS):
        sl = slice(g * GW, (g + 1) * GW)
        s = ext[:, sl]
        k = 1
        while k < w:
            s = s + pltpu.roll(s, shift=k, axis=0)
            k *= 2
        wsum = s[POOL_HALO:, :]
        count = jnp.minimum(pos + 1, w).astype(F32)
        mixed = wsum / count - u[:, sl]
        y = jnp.dot(mixed.astype(BF16), wp_ref[g], preferred_element_type=F32)
        o_ref[0, :, sl] = (y * ps_ref[:, sl]).astype(o_ref.dtype)


def _pool_mix(halo_src, halo_spec, u_src, u_col, wp, ps, tm, base_pos, zero_first_halo):
    b, l, _ = u_src.shape
    return pl.pallas_call(
        functools.partial(_pool_kernel, tm=tm, base_pos=base_pos, zero_first_halo=zero_first_halo),
        grid=(b, l // tm),
        in_specs=[
            halo_spec,
            pl.BlockSpec((1, tm, POOL_WIDTH), lambda bi, i: (bi, i, u_col)),
            pl.BlockSpec((len(POOL_WINDOWS), GW, GW), lambda bi, i: (0, 0, 0)),
            pl.BlockSpec((1, POOL_WIDTH), lambda bi, i: (0, 0)),
        ],
        out_specs=pl.BlockSpec((1, tm, POOL_WIDTH), lambda bi, i: (bi, i, 0)),
        out_shape=jax.ShapeDtypeStruct((b, l, POOL_WIDTH), BF16),
        compiler_params=_cparams(("arbitrary", "arbitrary")),
        name="pool_mix",
    )(halo_src, u_src, wp, ps)


def _merge_kernel(attn_ref, pool_ref, ga_ref, gb_ref, x_ref, wa_ref, wb_ref, wo_ref,
                  n2w_ref, rw_ref, rb_ref, *rest, tm, aliased):
    h_ref, xn_ref, ti_ref, gt_ref = rest[-4:]
    a = jnp.dot(attn_ref[...], wa_ref[...], preferred_element_type=F32)
    p = jnp.dot(pool_ref[...], wb_ref[...], preferred_element_type=F32)
    merged = ga_ref[...] * a + gb_ref[...] * p
    o = jnp.dot(merged.astype(BF16), wo_ref[...], preferred_element_type=F32)
    h1 = x_ref[...] + o
    h_ref[...] = h1
    ms = jnp.mean(h1 * h1, axis=-1, keepdims=True)
    xn = h1 * lax.rsqrt(ms + NORM_EPS) * n2w_ref[...]
    xn_ref[...] = xn
    logits = jnp.dot(xn, rw_ref[...], preferred_element_type=F32, precision=HIGHEST) + rb_ref[...]
    lane = lax.broadcasted_iota(I32, (tm, 128), 1).astype(F32)
    vals, idxs = [], []
    work = logits
    for _ in range(TOP_K):
        mv = jnp.max(work, axis=-1, keepdims=True)
        ix = jnp.min(jnp.where(work == mv, lane, 128.0), axis=-1, keepdims=True)
        vals.append(mv)
        idxs.append(ix)
        work = jnp.where(lane == ix, NEG_BIG, work)
    es = [jnp.exp(v - vals[0]) for v in vals]
    den = es[0] + es[1] + es[2] + es[3]
    ti = jnp.zeros((tm, 128), F32)
    gt = jnp.zeros((tm, 128), F32)
    for k in range(TOP_K):
        ti = jnp.where(lane == float(k), idxs[k], ti)
        gt = jnp.where(lane == float(k), es[k] / den, gt)
    ti_ref[...] = ti.astype(I32)
    gt_ref[...] = gt


def _merge(attn, pool, z, x, wa, wb, wo, n2w, rw, rb, tm, xn_all, xn_row0, n_all):
    m = x.shape[0]
    aliased = xn_all is not None
    const = lambda i: (0, 0)
    single = dict(pipeline_mode=pl.Buffered(1))
    xn_blk0 = xn_row0 // tm
    in_specs = [
        pl.BlockSpec((tm, ATTN_WIDTH), lambda i: (i, 0)),
        pl.BlockSpec((tm, POOL_WIDTH), lambda i: (i, 0)),
        pl.BlockSpec((tm, D_MODEL), lambda i: (i, 2)),
        pl.BlockSpec((tm, D_MODEL), lambda i: (i, 3)),
        pl.BlockSpec((tm, D_MODEL), lambda i: (i, 0)),
        pl.BlockSpec((ATTN_WIDTH, D_MODEL), const, **single),
        pl.BlockSpec((POOL_WIDTH, D_MODEL), const, **single),
        pl.BlockSpec((D_MODEL, D_MODEL), const, **single),
        pl.BlockSpec((1, D_MODEL), const),
        pl.BlockSpec((D_MODEL, 128), const),
        pl.BlockSpec((1, 128), const),
    ]
    args = [attn, pool, z, z, x, wa, wb, wo, n2w, rw, rb]
    io_alias = {}
    if aliased:
        in_specs.append(pl.BlockSpec(memory_space=pl.ANY))
        args.append(xn_all)
        io_alias = {len(args) - 1: 1}
    return pl.pallas_call(
        functools.partial(_merge_kernel, tm=tm, aliased=aliased),
        grid=(m // tm,),
        in_specs=in_specs,
        out_specs=[
            pl.BlockSpec((tm, D_MODEL), lambda i: (i, 0)),
            pl.BlockSpec((tm, D_MODEL), lambda i: (xn_blk0 + i, 0)),
            pl.BlockSpec((tm, 128), lambda i: (i, 0)),
            pl.BlockSpec((tm, 128), lambda i: (i, 0)),
        ],
        out_shape=[
            jax.ShapeDtypeStruct((m, D_MODEL), F32),
            jax.ShapeDtypeStruct((n_all, D_MODEL), F32),
            jax.ShapeDtypeStruct((m, 128), I32),
            jax.ShapeDtypeStruct((m, 128), F32),
        ],
        input_output_aliases=io_alias,
        compiler_params=_cparams(("arbitrary",)),
        name="merge_router",
    )(*args)


def _rank_kernel(idx_ref, rank_ref, cnt_ref, carry_ref, *, tr):
    i = pl.program_id(0)

    @pl.when(i == 0)
    def _():
        carry_ref[...] = jnp.zeros_like(carry_ref)

    idx = idx_ref[...]
    lane = lax.broadcasted_iota(I32, (tr, 128), 1)
    hits = [idx[:, k:k + 1] == lane for k in range(TOP_K)]
    onehot = jnp.zeros((tr, 128), F32)
    for hk in hits:
        onehot = onehot + hk.astype(F32)
    earlier = (lax.broadcasted_iota(I32, (tr, tr), 1)
               < lax.broadcasted_iota(I32, (tr, tr), 0)).astype(BF16)
    before = jnp.dot(earlier, onehot.astype(BF16), preferred_element_type=F32) + carry_ref[...]
    out = jnp.zeros((tr, 128), I32)
    for k, hk in enumerate(hits):
        rk = jnp.sum(jnp.where(hk, before, 0.0), axis=-1, keepdims=True)
        out = jnp.where(lane == k, rk.astype(I32), out)
    rank_ref[...] = out
    carry_ref[...] = carry_ref[...] + jnp.sum(onehot, axis=0, keepdims=True)
    cnt_ref[...] = carry_ref[...]


def _expert_rank(top_idx_padded, tr=128):
    n = top_idx_padded.shape[0]
    return pl.pallas_call(
        functools.partial(_rank_kernel, tr=tr),
        grid=(n // tr,),
        in_specs=[pl.BlockSpec((tr, 128), lambda i: (i, 0))],
        out_specs=[
            pl.BlockSpec((tr, 128), lambda i: (i, 0)),
            pl.BlockSpec((1, 128), lambda i: (0, 0)),
        ],
        out_shape=[
            jax.ShapeDtypeStruct((n, 128), I32),
            jax.ShapeDtypeStruct((1, 128), F32),
        ],
        scratch_shapes=[pltpu.VMEM((1, 128), F32)],
        compiler_params=_cparams(("arbitrary",)),
        name="expert_rank",
    )(top_idx_padded)


def _moe_row_block_copy(src, dst, sub, sem):
    return pltpu.make_async_copy(src.at[pl.ds(0, MOE_SUB)], dst.at[pl.ds(sub * MOE_SUB, MOE_SUB)], sem)


def _moe_kernel(wi_e_ref, wi_row_ref, wi_nsub_ref, tok_ref, xn_hbm,
                wg_ref, wu_ref, bg_ref, bu_ref, wd_ref, bd_ref, ys_hbm,
                x_scr, acc_scr, wg_bf, wu_bf, wd_bf, sem_in, sem_out):
    del wi_e_ref
    w = pl.program_id(0)
    j = pl.program_id(1)
    nj = pl.num_programs(1)
    nsub = wi_nsub_ref[w]
    row0 = wi_row_ref[w]

    @pl.when((j == 0) & (nsub > 0))
    def _():
        def issue(r, carry):
            tok = tok_ref[0, 0, r]
            pltpu.make_async_copy(xn_hbm.at[pl.ds(tok, 1)], x_scr.at[pl.ds(r, 1)], sem_in).start()
            return carry
        lax.fori_loop(0, nsub * MOE_SUB, issue, 0)
        for s in range(MOE_NSUB):
            @pl.when(s < nsub)
            def _():
                acc_scr[s * MOE_SUB:(s + 1) * MOE_SUB, :] = jnp.broadcast_to(
                    bd_ref[0], (MOE_SUB, D_MODEL))
        for s in range(MOE_NSUB):
            @pl.when(s < nsub)
            def _():
                _moe_row_block_copy(xn_hbm, x_scr, s, sem_in).wait()

    @pl.when(nsub > 0)
    def _():
        wg_bf[...] = wg_ref[0].astype(BF16)
        wu_bf[...] = wu_ref[0].astype(BF16)
        wd_bf[...] = wd_ref[0].astype(BF16)
        for s in range(MOE_NSUB):
            @pl.when(s < nsub)
            def _():
                rows = slice(s * MOE_SUB, (s + 1) * MOE_SUB)
                xs = x_scr[rows, :].astype(BF16)
                gate = jnp.dot(xs, wg_bf[...], preferred_element_type=F32) + bg_ref[0]
                up = jnp.dot(xs, wu_bf[...], preferred_element_type=F32) + bu_ref[0]
                gate = jnp.minimum(gate, SWIGLU_LIMIT)
                up = jnp.clip(up, -SWIGLU_LIMIT, SWIGLU_LIMIT)
                act = (up + 1.0) * (gate * jax.nn.sigmoid(SWIGLU_ALPHA * gate))
                acc_scr[rows, :] += jnp.dot(act.astype(BF16), wd_bf[...], preferred_element_type=F32)

    @pl.when((j == nj - 1) & (nsub > 0))
    def _():
        for s in range(MOE_NSUB):
            @pl.when(s < nsub)
            def _():
                pltpu.make_async_copy(
                    acc_scr.at[pl.ds(s * MOE_SUB, MOE_SUB)],
                    ys_hbm.at[pl.ds(pl.multiple_of(row0 + s * MOE_SUB, MOE_SUB), MOE_SUB)],
                    sem_out).start()
        for s in range(MOE_NSUB):
            @pl.when(s < nsub)
            def _():
                pltpu.make_async_copy(
                    acc_scr.at[pl.ds(s * MOE_SUB, MOE_SUB)],
                    ys_hbm.at[pl.ds(pl.multiple_of(row0 + s * MOE_SUB, MOE_SUB), MOE_SUB)],
                    sem_out).wait()


def _moe_experts(wi_e, wi_row, wi_nsub, wi_tok, xn_all, w_gu, b_gu, w_dn, b_dn, n_slots):
    n_wi = wi_e.shape[0]
    nj = D_EXPERT // MOE_TN
    grid_spec = pltpu.PrefetchScalarGridSpec(
        num_scalar_prefetch=3,
        grid=(n_wi, nj),
        in_specs=[
            pl.BlockSpec((1, 1, MOE_TM), lambda w, j, e, r, n: (w, 0, 0), memory_space=pltpu.SMEM),
            pl.BlockSpec(memory_space=pl.ANY),
            pl.BlockSpec((1, D_MODEL, MOE_TN), lambda w, j, e, r, n: (e[w], 0, j)),
            pl.BlockSpec((1, D_MODEL, MOE_TN), lambda w, j, e, r, n: (e[w], 0, nj + j)),
            pl.BlockSpec((1, 1, MOE_TN), lambda w, j, e, r, n: (e[w], 0, j)),
            pl.BlockSpec((1, 1, MOE_TN), lambda w, j, e, r, n: (e[w], 0, nj + j)),
            pl.BlockSpec((1, MOE_TN, D_MODEL), lambda w, j, e, r, n: (e[w], j, 0)),
            pl.BlockSpec((1, 1, D_MODEL), lambda w, j, e, r, n: (e[w], 0, 0)),
        ],
        out_specs=pl.BlockSpec(memory_space=pl.ANY),
        scratch_shapes=[
            pltpu.VMEM((MOE_TM, D_MODEL), F32),
            pltpu.VMEM((MOE_TM, D_MODEL), F32),
            pltpu.VMEM((D_MODEL, MOE_TN), BF16),
            pltpu.VMEM((D_MODEL, MOE_TN), BF16),
            pltpu.VMEM((MOE_TN, D_MODEL), BF16),
            pltpu.SemaphoreType.DMA(()),
            pltpu.SemaphoreType.DMA(()),
        ],
    )
    return pl.pallas_call(
        _moe_kernel,
        grid_spec=grid_spec,
        out_shape=jax.ShapeDtypeStruct((n_slots, D_MODEL), F32),
        compiler_params=_cparams(("arbitrary", "arbitrary"), vmem=58 * 1024 * 1024),
        name="moe_experts",
    )(wi_e, wi_row, wi_nsub, wi_tok, xn_all, w_gu, w_gu,
      b_gu.reshape(N_EXPERTS, 1, 2 * D_EXPERT), b_gu.reshape(N_EXPERTS, 1, 2 * D_EXPERT),
      w_dn, b_dn.reshape(N_EXPERTS, 1, D_MODEL))


def _combine_kernel(dest_ref, gt_ref, h_ref, ys_hbm, o_ref, buf, sem, *, tt):
    def issue(r, carry):
        for k in range(TOP_K):
            slot = dest_ref[0, 0, r * TOP_K + k]
            pltpu.make_async_copy(ys_hbm.at[pl.ds(slot, 1)], buf.at[k, pl.ds(r, 1)], sem).start()
        return carry
    lax.fori_loop(0, tt, issue, 0)
    for k in range(TOP_K):
        pltpu.make_async_copy(ys_hbm.at[pl.ds(0, tt)], buf.at[k], sem).wait()
    gt = gt_ref[...]
    y = gt[:, 0:1] * buf[0]
    for k in range(1, TOP_K):
        y = y + gt[:, k:k + 1] * buf[k]
    o_ref[...] = h_ref[...] + y


def _combine(dest, gates, h1, ys, tt):
    m = h1.shape[0]
    nt = m // tt
    return pl.pallas_call(
        functools.partial(_combine_kernel, tt=tt),
        grid=(nt,),
        in_specs=[
            pl.BlockSpec((1, 1, tt * TOP_K), lambda i: (i, 0, 0), memory_space=pltpu.SMEM),
            pl.BlockSpec((tt, 128), lambda i: (i, 0)),
            pl.BlockSpec((tt, D_MODEL), lambda i: (i, 0)),
            pl.BlockSpec(memory_space=pl.ANY),
        ],
        out_specs=pl.BlockSpec((tt, D_MODEL), lambda i: (i, 0)),
        out_shape=jax.ShapeDtypeStruct((m, D_MODEL), F32),
        scratch_shapes=[pltpu.VMEM((TOP_K, tt, D_MODEL), F32), pltpu.SemaphoreType.DMA(())],
        compiler_params=_cparams(("arbitrary",)),
        name="moe_combine",
    )(dest.reshape(nt, 1, tt * TOP_K), gates, h1, ys)


def _routing_tables(top_idx, rank, counts, n_wi, n_slots):
    n_tok = top_idx.shape[0]
    padded = (counts + MOE_SUB - 1) // MOE_SUB * MOE_SUB
    pad_start = jnp.cumsum(padded) - padded
    dest = pad_start[top_idx] + rank
    tok_ids = jnp.broadcast_to(jnp.arange(n_tok, dtype=I32)[:, None], (n_tok, TOP_K))
    slot_tok = jnp.zeros((n_slots,), I32).at[dest.reshape(-1)].set(tok_ids.reshape(-1))
    nsubs = padded // MOE_SUB
    items = (nsubs + MOE_NSUB - 1) // MOE_NSUB
    item_end = jnp.cumsum(items)
    total = item_end[-1]
    w = jnp.arange(n_wi, dtype=I32)
    w_eff = jnp.minimum(w, total - 1)
    wi_e = jnp.minimum(jnp.searchsorted(item_end, w_eff, side='right'), N_EXPERTS - 1).astype(I32)
    local = w_eff - (item_end - items)[wi_e]
    wi_row = (pad_start[wi_e] + local * MOE_TM).astype(I32)
    wi_nsub = jnp.where(w < total, jnp.clip(nsubs[wi_e] - local * MOE_NSUB, 0, MOE_NSUB), 0).astype(I32)
    tok_pos = jnp.minimum(wi_row[:, None] + jnp.arange(MOE_TM, dtype=I32)[None, :], n_slots - 1)
    wi_tok = slot_tok[tok_pos].reshape(n_wi, 1, MOE_TM)
    return dest.astype(I32), wi_e, wi_row, wi_nsub, wi_tok


def kernel(x_prompt, x_sample, cache_k, cache_v, cache_logf, state_pool, page_table,
           norm1_w, w_in, b_forget, q_norm_w, k_norm_w, w_pool, pool_scale,
           w_branch_a, w_branch_b, w_out, norm2_w, router_w, router_b,
           w_gu, b_gu, w_dn, b_dn):
    depth = w_in.shape[0]
    assert depth == 1
    bp, lp, _ = x_prompt.shape
    bs, ls, _ = x_sample.shape
    assert ls == 8 and N_HEADS == 8
    n_p, n_s = bp * lp, bs * ls
    n_all = n_p + n_s
    n_past = page_table.shape[1] * PAGE_SIZE

    wi = w_in[0]
    f_off = 3 * ATTN_WIDTH
    w_main = jnp.concatenate([wi[:, :f_off], wi[:, f_off + N_HEADS:]], axis=1).astype(BF16)
    w_f = jnp.pad(wi[:, f_off:f_off + N_HEADS], ((0, 0), (0, FORGET_LANES - N_HEADS))).astype(BF16)
    b_f = jnp.pad(b_forget[0], (0, FORGET_LANES - N_HEADS)).reshape(1, FORGET_LANES)
    n1w = norm1_w[0].reshape(1, D_MODEL)
    n2w = norm2_w[0].reshape(1, D_MODEL)
    qnw = jnp.tile(q_norm_w[0], N_HEADS).reshape(1, ATTN_WIDTH)
    knw = jnp.tile(k_norm_w[0], N_HEADS).reshape(1, ATTN_WIDTH)
    wp = w_pool[0].astype(BF16)
    ps = pool_scale[0].reshape(1, POOL_WIDTH)
    wa = w_branch_a[0].astype(BF16)
    wb = w_branch_b[0].astype(BF16)
    wo = w_out[0].astype(BF16)
    rw = jnp.pad(router_w[0], ((0, 0), (0, 128 - N_EXPERTS)))
    rb = jnp.pad(router_b[0], (0, 128 - N_EXPERTS), constant_values=NEG_BIG).reshape(1, 128)

    xp = x_prompt.reshape(n_p, D_MODEL)
    zp, lfp = _inproj(xp, n1w, w_main, w_f, b_f, qnw, knw, tm=1024)
    cp = _cumsum_rows(lfp, bl=256, seg_len=lp)
    zp3 = zp.reshape(bp, lp, Z_WIDTH)
    cp3 = cp.reshape(bp, lp, FORGET_LANES)
    cpt3 = jnp.transpose(cp3[:, :, :N_HEADS], (0, 2, 1))
    attn_p = _prompt_attention(zp3, cp3, cpt3)
    halo_p = pl.BlockSpec((1, POOL_HALO, POOL_WIDTH),
                          lambda bi, i: (bi, jnp.maximum(i * (512 // POOL_HALO) - 1, 0), ZU))
    pool_p = _pool_mix(zp3, halo_p, zp3, ZU, wp, ps, tm=512, base_pos=0, zero_first_halo=True)
    h1_p, xn_all, ti_p, gt_p = _merge(
        attn_p.reshape(n_p, ATTN_WIDTH), pool_p.reshape(n_p, POOL_WIDTH), zp, xp,
        wa, wb, wo, n2w, rw, rb, tm=256, xn_all=None, xn_row0=0, n_all=n_all)

    xs = x_sample.reshape(n_s, D_MODEL)
    zs, lfs = _inproj(xs, n1w, w_main, w_f, b_f, qnw, knw, tm=n_s)
    cs = _cumsum_rows(lfs, bl=n_s, seg_len=ls)
    zs3 = zs.reshape(bs, ls, Z_WIDTH)
    q_s = zs3[:, :, 0:ATTN_WIDTH].reshape(bs, ls, N_HEADS, HEAD_DIM)
    qbd = jnp.einsum('bthd,hg->bhtgd', q_s, jnp.eye(N_HEADS, dtype=F32))
    qbd = qbd.reshape(bs, N_HEADS * ls, ATTN_WIDTH).astype(BF16)
    cn = jnp.transpose(cs.reshape(bs, ls, FORGET_LANES)[:, :, :N_HEADS], (0, 2, 1))
    cncol = cn.reshape(bs, N_HEADS * ls, 1)
    cnkeys = jnp.broadcast_to(cn[:, :, None, :], (bs, N_HEADS, ls, ls)).reshape(bs, N_HEADS * ls, ls)
    cnkeys = jnp.pad(cnkeys, ((0, 0), (0, 0), (0, PAGE_SIZE - ls)))
    knew = jnp.pad(zs3[:, :, ATTN_WIDTH:2 * ATTN_WIDTH], ((0, 0), (0, PAGE_SIZE - ls), (0, 0)))
    vnew = jnp.pad(zs3[:, :, 2 * ATTN_WIDTH:3 * ATTN_WIDTH], ((0, 0), (0, PAGE_SIZE - ls), (0, 0)))
    n_phys = cache_k.shape[1]
    ck = cache_k[0].reshape(n_phys, PAGE_SIZE * N_HEADS, HEAD_DIM)
    cv = cache_v[0].reshape(n_phys, PAGE_SIZE * N_HEADS, HEAD_DIM)
    clf_t = jnp.transpose(cache_logf[0], (0, 2, 1))
    attn_s = _sample_attention(page_table, qbd, cncol, cnkeys, knew, vnew, ck, cv, clf_t)
    halo_src = jnp.pad(state_pool[0], ((0, 0), (POOL_HALO - POOL_HIST, 0), (0, 0)))
    halo_s = pl.BlockSpec((1, POOL_HALO, POOL_WIDTH), lambda bi, i: (bi, 0, 0))
    pool_s = _pool_mix(halo_src, halo_s, zs3, ZU, wp, ps, tm=ls, base_pos=n_past, zero_first_halo=False)
    h1_s, xn_all, ti_s, gt_s = _merge(
        attn_s.reshape(n_s, ATTN_WIDTH), pool_s.reshape(n_s, POOL_WIDTH), zs, xs,
        wa, wb, wo, n2w, rw, rb, tm=n_s, xn_all=xn_all, xn_row0=n_p, n_all=n_all)

    n_rank = pl.cdiv(n_all, 128) * 128
    ti_all = jnp.concatenate([ti_p, ti_s, jnp.full((n_rank - n_all, 128), -1, I32)], axis=0)
    rank, cnt = _expert_rank(ti_all)
    top_idx = ti_all[:n_all, :TOP_K]
    counts = cnt[0, :N_EXPERTS].astype(I32)
    n_blocks = pl.cdiv(n_all * TOP_K, MOE_SUB) + N_EXPERTS
    n_slots = n_blocks * MOE_SUB
    n_wi = n_blocks // MOE_NSUB + N_EXPERTS
    dest, wi_e, wi_row, wi_nsub, wi_tok = _routing_tables(
        top_idx, rank[:n_all, :TOP_K], counts, n_wi, n_slots)
    ys = _moe_experts(wi_e, wi_row, wi_nsub, wi_tok, xn_all, w_gu[0], b_gu[0], w_dn[0], b_dn[0], n_slots)
    y_p = _combine(dest[:n_p], gt_p, h1_p, ys, tt=128)
    y_s = _combine(dest[n_p:], gt_s, h1_s, ys, tt=n_s)

    heads = (N_HEADS, HEAD_DIM)
    k_p = zp3[:, :, ATTN_WIDTH:2 * ATTN_WIDTH].reshape(1, bp, lp, *heads)
    v_p = zp3[:, :, 2 * ATTN_WIDTH:3 * ATTN_WIDTH].reshape(1, bp, lp, *heads)
    f_p = lfp.reshape(bp, lp, FORGET_LANES)[None, :, :, :N_HEADS]
    u_p = zp3[:, lp - POOL_HIST:, 3 * ATTN_WIDTH:3 * ATTN_WIDTH + POOL_WIDTH][None]
    k_s = zs3[:, :, ATTN_WIDTH:2 * ATTN_WIDTH].reshape(1, bs, ls, *heads)
    v_s = zs3[:, :, 2 * ATTN_WIDTH:3 * ATTN_WIDTH].reshape(1, bs, ls, *heads)
    f_s = lfs.reshape(bs, ls, FORGET_LANES)[None, :, :, :N_HEADS]
    u_s = zs3[:, :, 3 * ATTN_WIDTH:3 * ATTN_WIDTH + POOL_WIDTH]
    pool_state = jnp.concatenate([state_pool[0], u_s], axis=1)[:, -POOL_HIST:][None]
    return (y_p.reshape(bp, lp, D_MODEL), y_s.reshape(bs, ls, D_MODEL),
            k_p, v_p, f_p, u_p, k_s, v_s, f_s, pool_state)
```

```python
import functools

import jax
import jax.numpy as jnp
from jax import lax
from jax.experimental import pallas as pl
from jax.experimental.pallas import tpu as pltpu

F32 = jnp.float32
BF16 = jnp.bfloat16
I32 = jnp.int32

D_MODEL = 2048
N_HEADS = 8
HEAD_DIM = 128
ATTN_WIDTH = N_HEADS * HEAD_DIM
ATTN_SCALE = HEAD_DIM ** -0.5
FORGET_LANES = 128
POOL_WINDOWS = (2, 4, 8, 16)
POOL_WIDTH = D_MODEL // 2
GW = POOL_WIDTH // len(POOL_WINDOWS)
POOL_HIST = max(POOL_WINDOWS) - 1
POOL_HALO = 16
N_EXPERTS = 32
TOP_K = 4
D_EXPERT = D_MODEL
SWIGLU_LIMIT = 7.0
SWIGLU_ALPHA = 1.702
NORM_EPS = 1e-6
PAGE_SIZE = 128
NEG_BIG = -1e30

ZQ, ZK, ZV, ZU = 0, 1, 2, 3
Z_WIDTH = 3 * ATTN_WIDTH + POOL_WIDTH + 2 * D_MODEL

MOE_SUB = 128
MOE_NSUB = 9
MOE_TM = MOE_SUB * MOE_NSUB
MOE_TN = 512
VMEM_LIMIT = 56 * 1024 * 1024

HIGHEST = lax.Precision.HIGHEST


def _cparams(sem, vmem=VMEM_LIMIT):
    return pltpu.CompilerParams(dimension_semantics=sem, vmem_limit_bytes=vmem)


def _log_sigmoid(x):
    return jnp.minimum(x, 0.0) - jnp.log1p(jnp.exp(-jnp.abs(x)))


def _inproj_kernel(x_ref, n1w_ref, w_ref, wf_ref, bf_ref, qnw_ref, knw_ref,
                   z_ref, lf_ref, xn_ref):
    j = pl.program_id(1)

    @pl.when(j == 0)
    def _():
        x = x_ref[...]
        ms = jnp.mean(x * x, axis=-1, keepdims=True)
        xn = x * lax.rsqrt(ms + NORM_EPS) * n1w_ref[...]
        xn_ref[...] = xn.astype(BF16)
        zf = jnp.dot(xn_ref[...], wf_ref[...], preferred_element_type=F32) + bf_ref[...]
        lf_ref[...] = _log_sigmoid(zf)

    z = jnp.dot(xn_ref[...], w_ref[...], preferred_element_type=F32)

    @pl.when(j <= ZK)
    def _():
        nw = jnp.where(j == ZQ, qnw_ref[...], knw_ref[...])
        for h in range(N_HEADS):
            sl = slice(h * HEAD_DIM, (h + 1) * HEAD_DIM)
            zh = z[:, sl]
            ms = jnp.mean(zh * zh, axis=-1, keepdims=True)
            z_ref[:, sl] = zh * lax.rsqrt(ms + NORM_EPS) * nw[:, sl]

    @pl.when((j == ZV) | (j == ZU))
    def _():
        z_ref[...] = z

    @pl.when(j > ZU)
    def _():
        z_ref[...] = jax.nn.sigmoid(z)


def _inproj(x, n1w, w_main, w_f, b_f, qnw, knw, tm):
    m = x.shape[0]
    tn = ATTN_WIDTH
    return pl.pallas_call(
        _inproj_kernel,
        grid=(m // tm, Z_WIDTH // tn),
        in_specs=[
            pl.BlockSpec((tm, D_MODEL), lambda i, j: (i, 0)),
            pl.BlockSpec((1, D_MODEL), lambda i, j: (0, 0)),
            pl.BlockSpec((D_MODEL, tn), lambda i, j: (0, j)),
            pl.BlockSpec((D_MODEL, FORGET_LANES), lambda i, j: (0, 0)),
            pl.BlockSpec((1, FORGET_LANES), lambda i, j: (0, 0)),
            pl.BlockSpec((1, tn), lambda i, j: (0, 0)),
            pl.BlockSpec((1, tn), lambda i, j: (0, 0)),
        ],
        out_specs=[
            pl.BlockSpec((tm, tn), lambda i, j: (i, j)),
            pl.BlockSpec((tm, FORGET_LANES), lambda i, j: (i, 0)),
        ],
        out_shape=[
            jax.ShapeDtypeStruct((m, Z_WIDTH), F32),
            jax.ShapeDtypeStruct((m, FORGET_LANES), F32),
        ],
        scratch_shapes=[pltpu.VMEM((tm, D_MODEL), BF16)],
        compiler_params=_cparams(("arbitrary", "arbitrary")),
        name="inproj",
    )(x, n1w, w_main, w_f, b_f, qnw, knw)


def _cumsum_kernel(lf_ref, c_ref, carry_ref, *, bl, seg_len):
    i = pl.program_id(0)
    r = lax.broadcasted_iota(I32, (bl, bl), 0)
    c = lax.broadcasted_iota(I32, (bl, bl), 1)
    mask = c <= r
    if seg_len < bl:
        mask = mask & ((r // seg_len) == (c // seg_len))
    tri = mask.astype(F32)
    cs = jnp.dot(tri, lf_ref[...], preferred_element_type=F32, precision=HIGHEST)
    if seg_len > bl:
        @pl.when((i * bl) % seg_len == 0)
        def _():
            carry_ref[...] = jnp.zeros_like(carry_ref)
        cs = cs + carry_ref[...]
        carry_ref[...] = cs[bl - 1:bl, :]
    c_ref[...] = cs


def _cumsum_rows(lf, bl, seg_len):
    m = lf.shape[0]
    return pl.pallas_call(
        functools.partial(_cumsum_kernel, bl=bl, seg_len=seg_len),
        grid=(m // bl,),
        in_specs=[pl.BlockSpec((bl, FORGET_LANES), lambda i: (i, 0))],
        out_specs=pl.BlockSpec((bl, FORGET_LANES), lambda i: (i, 0)),
        out_shape=jax.ShapeDtypeStruct((m, FORGET_LANES), F32),
        scratch_shapes=[pltpu.VMEM((1, FORGET_LANES), F32)],
        compiler_params=_cparams(("arbitrary",)),
        name="logf_cumsum",
    )(lf)


def _pattn_kernel(q_ref, k_ref, v_ref, c_ref, ct_ref, o_ref, *scratch, tq, ch):
    m_refs = scratch[0:N_HEADS]
    l_refs = scratch[N_HEADS:2 * N_HEADS]
    acc_refs = scratch[2 * N_HEADS:3 * N_HEADS]
    kb_ref, vb_ref = scratch[3 * N_HEADS:]
    qi = pl.program_id(1)
    ki = pl.program_id(2)
    log2e = 1.4426950408889634

    @pl.when(ki == 0)
    def _():
        for h in range(N_HEADS):
            m_refs[h][...] = jnp.full_like(m_refs[h], -jnp.inf)
            l_refs[h][...] = jnp.zeros_like(l_refs[h])
            acc_refs[h][...] = jnp.zeros_like(acc_refs[h])

    @pl.when(ki <= qi)
    def _():
        kb_ref[...] = k_ref[0].astype(BF16)
        vb_ref[...] = v_ref[0].astype(BF16)

    def block(diag):
        ct2 = ct_ref[0] * log2e
        c2 = c_ref[0] * log2e
        below = (lax.broadcasted_iota(I32, (ch, ch), 1) <= lax.broadcasted_iota(I32, (ch, ch), 0))
        for h in range(N_HEADS):
            sl = slice(h * HEAD_DIM, (h + 1) * HEAD_DIM)
            for r in range(tq // ch):
                rows = slice(r * ch, (r + 1) * ch)
                ncols = (r + 1) * ch if diag else tq
                q = q_ref[0, rows, sl].astype(BF16)
                u = lax.dot_general(q, kb_ref[0:ncols, sl], (((1,), (1,)), ((), ())),
                                    preferred_element_type=F32)
                u = u * (ATTN_SCALE * log2e) - ct2[h:h + 1, 0:ncols]
                if diag:
                    last = jnp.where(below, u[:, r * ch:], -jnp.inf)
                    u = last if r == 0 else jnp.concatenate([u[:, :r * ch], last], axis=1)
                c2t = c2[rows, h:h + 1]
                m_prev = m_refs[h][rows, :]
                m_new = jnp.maximum(m_prev, jnp.max(u, axis=-1, keepdims=True) + c2t)
                alpha = jnp.exp2(m_prev - m_new)
                p = jnp.exp2(u + (c2t - m_new[:, 0:1]))
                l_refs[h][rows, :] = alpha * l_refs[h][rows, :] + jnp.sum(p, axis=-1, keepdims=True)
                pv = jnp.dot(p.astype(BF16), vb_ref[0:ncols, sl], preferred_element_type=F32)
                acc_refs[h][rows, :] = alpha * acc_refs[h][rows, :] + pv
                m_refs[h][rows, :] = m_new

    @pl.when(ki < qi)
    def _():
        block(False)

    @pl.when(ki == qi)
    def _():
        block(True)
        for h in range(N_HEADS):
            sl = slice(h * HEAD_DIM, (h + 1) * HEAD_DIM)
            o_ref[0, :, sl] = (acc_refs[h][...] / l_refs[h][...]).astype(o_ref.dtype)


def _prompt_attention(z3, c3, ct3, tq=512, ch=512):
    b, l, _ = z3.shape
    tk = tq
    nq = l // tq
    kv_idx = lambda col: (lambda bi, qi, ki: (bi, jnp.minimum(ki, qi), col))
    return pl.pallas_call(
        functools.partial(_pattn_kernel, tq=tq, ch=ch),
        grid=(b, nq, nq),
        in_specs=[
            pl.BlockSpec((1, tq, ATTN_WIDTH), lambda bi, qi, ki: (bi, qi, ZQ)),
            pl.BlockSpec((1, tk, ATTN_WIDTH), kv_idx(ZK)),
            pl.BlockSpec((1, tk, ATTN_WIDTH), kv_idx(ZV)),
            pl.BlockSpec((1, tq, FORGET_LANES), lambda bi, qi, ki: (bi, qi, 0)),
            pl.BlockSpec((1, N_HEADS, tk), lambda bi, qi, ki: (bi, 0, jnp.minimum(ki, qi))),
        ],
        out_specs=pl.BlockSpec((1, tq, ATTN_WIDTH), lambda bi, qi, ki: (bi, qi, 0)),
        out_shape=jax.ShapeDtypeStruct((b, l, ATTN_WIDTH), BF16),
        scratch_shapes=(
            [pltpu.VMEM((tq, HEAD_DIM), F32) for _ in range(3 * N_HEADS)]
            + [pltpu.VMEM((tk, ATTN_WIDTH), BF16), pltpu.VMEM((tk, ATTN_WIDTH), BF16)]),
        compiler_params=_cparams(("arbitrary", "arbitrary", "arbitrary")),
        name="prompt_attention",
    )(z3, z3, z3, c3, ct3)


def _heads_to_lanes(ref):
    parts = [ref[0, pl.ds(h, PAGE_SIZE, stride=N_HEADS), :] for h in range(N_HEADS)]
    return jnp.concatenate(parts, axis=1).astype(BF16)


def _rows_per_head(x):
    return jnp.concatenate(
        [jnp.broadcast_to(x[h:h + 1, :], (8, x.shape[1])) for h in range(N_HEADS)], axis=0)


def _sattn_kernel(pt_ref, qbd_ref, cncol_ref, cnkeys_ref, knew_ref, vnew_ref, *rest, g_pages):
    del pt_ref
    k_refs = rest[0:g_pages]
    v_refs = rest[g_pages:2 * g_pages]
    lf_refs = rest[2 * g_pages:3 * g_pages]
    o_ref = rest[3 * g_pages]
    m_ref, l_ref, acc_ref, carry_ref = rest[3 * g_pages + 1:]
    jj = pl.program_id(1)
    nj = pl.num_programs(1)
    rows = N_HEADS * 8

    @pl.when(jj == 0)
    def _():
        m_ref[...] = jnp.full_like(m_ref, -jnp.inf)
        l_ref[...] = jnp.zeros_like(l_ref)
        acc_ref[...] = jnp.zeros_like(acc_ref)
        carry_ref[...] = jnp.zeros_like(carry_ref)

    qbd = qbd_ref[0]
    cncol = cncol_ref[0]
    later = (lax.broadcasted_iota(I32, (PAGE_SIZE, PAGE_SIZE), 0)
             > lax.broadcasted_iota(I32, (PAGE_SIZE, PAGE_SIZE), 1)).astype(F32)

    def scores(kcat):
        return lax.dot_general(qbd, kcat, (((1,), (1,)), ((), ())),
                               preferred_element_type=F32) * ATTN_SCALE

    def update(s, vcat, m, l, acc):
        m_new = jnp.maximum(m, jnp.max(s, axis=-1, keepdims=True))
        alpha = jnp.exp(m - m_new)
        p = jnp.exp(s - m_new)
        l = alpha * l + jnp.sum(p, axis=-1, keepdims=True)
        pv = jnp.dot(p.astype(BF16), vcat, preferred_element_type=F32)
        acc = jnp.concatenate([alpha] * N_HEADS, axis=1) * acc + pv
        return m_new, l, acc

    m, l, acc, carry = m_ref[...], l_ref[...], acc_ref[...], carry_ref[...]
    for g in range(g_pages):
        lf = _rows_per_head(lf_refs[g][0])
        suffix = jnp.dot(lf, later, preferred_element_type=F32, precision=HIGHEST)
        s = scores(_heads_to_lanes(k_refs[g])) + cncol + suffix + carry
        m, l, acc = update(s, _heads_to_lanes(v_refs[g]), m, l, acc)
        carry = carry + jnp.sum(lf, axis=-1, keepdims=True)

    @pl.when(jj < nj - 1)
    def _():
        m_ref[...] = m
        l_ref[...] = l
        acc_ref[...] = acc
        carry_ref[...] = carry

    @pl.when(jj == nj - 1)
    def _():
        s = scores(knew_ref[0].astype(BF16)) + cncol - cnkeys_ref[0]
        t_row = lax.broadcasted_iota(I32, (rows, PAGE_SIZE), 0) % 8
        s_col = lax.broadcasted_iota(I32, (rows, PAGE_SIZE), 1)
        s = jnp.where(s_col <= t_row, s, -jnp.inf)
        _, l2, acc2 = update(s, vnew_ref[0].astype(BF16), m, l, acc)
        o = acc2 / jnp.concatenate([l2] * N_HEADS, axis=1)
        for h in range(N_HEADS):
            sl = slice(h * HEAD_DIM, (h + 1) * HEAD_DIM)
            o_ref[0, :, sl] = o[h * 8:(h + 1) * 8, sl].astype(o_ref.dtype)


def _sample_attention(page_table, qbd, cncol, cnkeys, knew, vnew, ck, cv, clf_t, g_pages=4):
    bd, n_pages = page_table.shape
    rows = N_HEADS * 8
    pt_flat = page_table.reshape(-1)

    def page_idx(g):
        return lambda b, jj, pt: (pt[b * n_pages + n_pages - 1 - (jj * g_pages + g)], 0, 0)

    per_b = lambda b, jj, pt: (b, 0, 0)
    in_specs = [
        pl.BlockSpec((1, rows, ATTN_WIDTH), per_b),
        pl.BlockSpec((1, rows, 1), per_b),
        pl.BlockSpec((1, rows, PAGE_SIZE), per_b),
        pl.BlockSpec((1, PAGE_SIZE, ATTN_WIDTH), per_b),
        pl.BlockSpec((1, PAGE_SIZE, ATTN_WIDTH), per_b),
    ]
    in_specs += [pl.BlockSpec((1, PAGE_SIZE * N_HEADS, HEAD_DIM), page_idx(g)) for g in range(g_pages)]
    in_specs += [pl.BlockSpec((1, PAGE_SIZE * N_HEADS, HEAD_DIM), page_idx(g)) for g in range(g_pages)]
    in_specs += [pl.BlockSpec((1, N_HEADS, PAGE_SIZE), page_idx(g)) for g in range(g_pages)]
    grid_spec = pltpu.PrefetchScalarGridSpec(
        num_scalar_prefetch=1,
        grid=(bd, n_pages // g_pages),
        in_specs=in_specs,
        out_specs=pl.BlockSpec((1, 8, ATTN_WIDTH), per_b),
        scratch_shapes=[
            pltpu.VMEM((rows, PAGE_SIZE), F32),
            pltpu.VMEM((rows, PAGE_SIZE), F32),
            pltpu.VMEM((rows, ATTN_WIDTH), F32),
            pltpu.VMEM((rows, PAGE_SIZE), F32),
        ],
    )
    return pl.pallas_call(
        functools.partial(_sattn_kernel, g_pages=g_pages),
        grid_spec=grid_spec,
        out_shape=jax.ShapeDtypeStruct((bd, 8, ATTN_WIDTH), BF16),
        compiler_params=_cparams(("arbitrary", "arbitrary")),
        name="sample_attention",
    )(pt_flat, qbd, cncol, cnkeys, knew, vnew,
      *([ck] * g_pages), *([cv] * g_pages), *([clf_t] * g_pages))


def _pool_kernel(halo_ref, u_ref, wp_ref, ps_ref, o_ref, *, tm, base_pos, zero_first_halo):
    i = pl.program_id(1)
    halo = halo_ref[0]
    if zero_first_halo:
        halo = jnp.where(i == 0, 0.0, halo)
    u = u_ref[0]
    ext = jnp.concatenate([halo, u], axis=0)
    pos = base_pos + i * tm + lax.broadcasted_iota(I32, (tm, 1), 0)
    for g, w in enumerate(POOL_WINDOWS):
        sl = slice(g * GW, (g + 1) * GW)
        s = ext[:, sl]
        k = 1
        while k < w:
            s = s + pltpu.roll(s, shift=k, axis=0)
            k *= 2
        wsum = s[POOL_HALO:, :]
        count = jnp.minimum(pos + 1, w).astype(F32)
        mixed = wsum / count - u[:, sl]
        y = jnp.dot(mixed.astype(BF16), wp_ref[g], preferred_element_type=F32)
        o_ref[0, :, sl] = (y * ps_ref[:, sl]).astype(o_ref.dtype)


def _pool_mix(halo_src, halo_spec, u_src, u_col, wp, ps, tm, base_pos, zero_first_halo):
    b, l, _ = u_src.shape
    return pl.pallas_call(
        functools.partial(_pool_kernel, tm=tm, base_pos=base_pos, zero_first_halo=zero_first_halo),
        grid=(b, l // tm),
        in_specs=[
            halo_spec,
            pl.BlockSpec((1, tm, POOL_WIDTH), lambda bi, i: (bi, i, u_col)),
            pl.BlockSpec((len(POOL_WINDOWS), GW, GW), lambda bi, i: (0, 0, 0)),
            pl.BlockSpec((1, POOL_WIDTH), lambda bi, i: (0, 0)),
        ],
        out_specs=pl.BlockSpec((1, tm, POOL_WIDTH), lambda bi, i: (bi, i, 0)),
        out_shape=jax.ShapeDtypeStruct((b, l, POOL_WIDTH), BF16),
        compiler_params=_cparams(("arbitrary", "arbitrary")),
        name="pool_mix",
    )(halo_src, u_src, wp, ps)


def _merge_kernel(attn_ref, pool_ref, ga_ref, gb_ref, x_ref, wa_ref, wb_ref, wo_ref,
                  n2w_ref, rw_ref, rb_ref, *rest, tm, aliased):
    h_ref, xn_ref, ti_ref, gt_ref = rest[-4:]
    a = jnp.dot(attn_ref[...], wa_ref[...], preferred_element_type=F32)
    p = jnp.dot(pool_ref[...], wb_ref[...], preferred_element_type=F32)
    merged = ga_ref[...] * a + gb_ref[...] * p
    o = jnp.dot(merged.astype(BF16), wo_ref[...], preferred_element_type=F32)
    h1 = x_ref[...] + o
    h_ref[...] = h1
    ms = jnp.mean(h1 * h1, axis=-1, keepdims=True)
    xn = h1 * lax.rsqrt(ms + NORM_EPS) * n2w_ref[...]
    xn_ref[...] = xn
    logits = jnp.dot(xn, rw_ref[...], preferred_element_type=F32, precision=HIGHEST) + rb_ref[...]
    lane = lax.broadcasted_iota(I32, (tm, 128), 1).astype(F32)
    vals, idxs = [], []
    work = logits
    for _ in range(TOP_K):
        mv = jnp.max(work, axis=-1, keepdims=True)
        ix = jnp.min(jnp.where(work == mv, lane, 128.0), axis=-1, keepdims=True)
        vals.append(mv)
        idxs.append(ix)
        work = jnp.where(lane == ix, NEG_BIG, work)
    es = [jnp.exp(v - vals[0]) for v in vals]
    den = es[0] + es[1] + es[2] + es[3]
    ti = jnp.zeros((tm, 128), F32)
    gt = jnp.zeros((tm, 128), F32)
    for k in range(TOP_K):
        ti = jnp.where(lane == float(k), idxs[k], ti)
        gt = jnp.where(lane == float(k), es[k] / den, gt)
    ti_ref[...] = ti.astype(I32)
    gt_ref[...] = gt


def _merge(attn, pool, z, x, wa, wb, wo, n2w, rw, rb, tm, xn_all, xn_row0, n_all):
    m = x.shape[0]
    aliased = xn_all is not None
    const = lambda i: (0, 0)
    single = dict(pipeline_mode=pl.Buffered(1))
    xn_blk0 = xn_row0 // tm
    in_specs = [
        pl.BlockSpec((tm, ATTN_WIDTH), lambda i: (i, 0)),
        pl.BlockSpec((tm, POOL_WIDTH), lambda i: (i, 0)),
        pl.BlockSpec((tm, D_MODEL), lambda i: (i, 2)),
        pl.BlockSpec((tm, D_MODEL), lambda i: (i, 3)),
        pl.BlockSpec((tm, D_MODEL), lambda i: (i, 0)),
        pl.BlockSpec((ATTN_WIDTH, D_MODEL), const, **single),
        pl.BlockSpec((POOL_WIDTH, D_MODEL), const, **single),
        pl.BlockSpec((D_MODEL, D_MODEL), const, **single),
        pl.BlockSpec((1, D_MODEL), const),
        pl.BlockSpec((D_MODEL, 128), const),
        pl.BlockSpec((1, 128), const),
    ]
    args = [attn, pool, z, z, x, wa, wb, wo, n2w, rw, rb]
    io_alias = {}
    if aliased:
        in_specs.append(pl.BlockSpec(memory_space=pl.ANY))
        args.append(xn_all)
        io_alias = {len(args) - 1: 1}
    return pl.pallas_call(
        functools.partial(_merge_kernel, tm=tm, aliased=aliased),
        grid=(m // tm,),
        in_specs=in_specs,
        out_specs=[
            pl.BlockSpec((tm, D_MODEL), lambda i: (i, 0)),
            pl.BlockSpec((tm, D_MODEL), lambda i: (xn_blk0 + i, 0)),
            pl.BlockSpec((tm, 128), lambda i: (i, 0)),
            pl.BlockSpec((tm, 128), lambda i: (i, 0)),
        ],
        out_shape=[
            jax.ShapeDtypeStruct((m, D_MODEL), F32),
            jax.ShapeDtypeStruct((n_all, D_MODEL), F32),
            jax.ShapeDtypeStruct((m, 128), I32),
            jax.ShapeDtypeStruct((m, 128), F32),
        ],
        input_output_aliases=io_alias,
        compiler_params=_cparams(("arbitrary",)),
        name="merge_router",
    )(*args)


def _rank_kernel(idx_ref, rank_ref, cnt_ref, carry_ref, *, tr):
    i = pl.program_id(0)

    @pl.when(i == 0)
    def _():
        carry_ref[...] = jnp.zeros_like(carry_ref)

    idx = idx_ref[...]
    lane = lax.broadcasted_iota(I32, (tr, 128), 1)
    hits = [idx[:, k:k + 1] == lane for k in range(TOP_K)]
    onehot = jnp.zeros((tr, 128), F32)
    for hk in hits:
        onehot = onehot + hk.astype(F32)
    earlier = (lax.broadcasted_iota(I32, (tr, tr), 1)
               < lax.broadcasted_iota(I32, (tr, tr), 0)).astype(BF16)
    before = jnp.dot(earlier, onehot.astype(BF16), preferred_element_type=F32) + carry_ref[...]
    out = jnp.zeros((tr, 128), I32)
    for k, hk in enumerate(hits):
        rk = jnp.sum(jnp.where(hk, before, 0.0), axis=-1, keepdims=True)
        out = jnp.where(lane == k, rk.astype(I32), out)
    rank_ref[...] = out
    carry_ref[...] = carry_ref[...] + jnp.sum(onehot, axis=0, keepdims=True)
    cnt_ref[...] = carry_ref[...]


def _expert_rank(top_idx_padded, tr=128):
    n = top_idx_padded.shape[0]
    return pl.pallas_call(
        functools.partial(_rank_kernel, tr=tr),
        grid=(n // tr,),
        in_specs=[pl.BlockSpec((tr, 128), lambda i: (i, 0))],
        out_specs=[
            pl.BlockSpec((tr, 128), lambda i: (i, 0)),
            pl.BlockSpec((1, 128), lambda i: (0, 0)),
        ],
        out_shape=[
            jax.ShapeDtypeStruct((n, 128), I32),
            jax.ShapeDtypeStruct((1, 128), F32),
        ],
        scratch_shapes=[pltpu.VMEM((1, 128), F32)],
        compiler_params=_cparams(("arbitrary",)),
        name="expert_rank",
    )(top_idx_padded)


def _moe_row_block_copy(src, dst, sub, sem):
    return pltpu.make_async_copy(src.at[pl.ds(0, MOE_SUB)], dst.at[pl.ds(sub * MOE_SUB, MOE_SUB)], sem)


def _moe_kernel(wi_e_ref, wi_row_ref, wi_nsub_ref, tok_ref, xn_hbm,
                wg_ref, wu_ref, bg_ref, bu_ref, wd_ref, bd_ref, ys_hbm,
                x_scr, acc_scr, wg_bf, wu_bf, wd_bf, sem_in, sem_out):
    del wi_e_ref
    w = pl.program_id(0)
    j = pl.program_id(1)
    nj = pl.num_programs(1)
    nsub = wi_nsub_ref[w]
    row0 = wi_row_ref[w]

    @pl.when((j == 0) & (nsub > 0))
    def _():
        def issue(r, carry):
            tok = tok_ref[0, 0, r]
            pltpu.make_async_copy(xn_hbm.at[pl.ds(tok, 1)], x_scr.at[pl.ds(r, 1)], sem_in).start()
            return carry
        lax.fori_loop(0, nsub * MOE_SUB, issue, 0)
        for s in range(MOE_NSUB):
            @pl.when(s < nsub)
            def _():
                acc_scr[s * MOE_SUB:(s + 1) * MOE_SUB, :] = jnp.broadcast_to(
                    bd_ref[0], (MOE_SUB, D_MODEL))
        for s in range(MOE_NSUB):
            @pl.when(s < nsub)
            def _():
                _moe_row_block_copy(xn_hbm, x_scr, s, sem_in).wait()

    def expert_rows(rows):
        xs = x_scr[rows, :].astype(BF16)
        gate = jnp.dot(xs, wg_bf[...], preferred_element_type=F32) + bg_ref[0]
        up = jnp.dot(xs, wu_bf[...], preferred_element_type=F32) + bu_ref[0]
        gate = jnp.minimum(gate, SWIGLU_LIMIT)
        up = jnp.clip(up, -SWIGLU_LIMIT, SWIGLU_LIMIT)
        act = (up + 1.0) * (gate * jax.nn.sigmoid(SWIGLU_ALPHA * gate))
        acc_scr[rows, :] += jnp.dot(act.astype(BF16), wd_bf[...], preferred_element_type=F32)

    @pl.when(nsub > 0)
    def _():
        wg_bf[...] = wg_ref[0].astype(BF16)
        wu_bf[...] = wu_ref[0].astype(BF16)
        wd_bf[...] = wd_ref[0].astype(BF16)
        for c in range(MOE_NSUB // 4):
            @pl.when(4 * c + 4 <= nsub)
            def _():
                expert_rows(pl.ds(4 * c * MOE_SUB, 4 * MOE_SUB))

        @pl.when(nsub % 4 >= 2)
        def _():
            expert_rows(pl.ds(pl.multiple_of((nsub // 4) * 4 * MOE_SUB, MOE_SUB), 2 * MOE_SUB))

        @pl.when(nsub % 2 == 1)
        def _():
            expert_rows(pl.ds(pl.multiple_of((nsub - 1) * MOE_SUB, MOE_SUB), MOE_SUB))

    @pl.when((j == nj - 1) & (nsub > 0))
    def _():
        for s in range(MOE_NSUB):
            @pl.when(s < nsub)
            def _():
                pltpu.make_async_copy(
                    acc_scr.at[pl.ds(s * MOE_SUB, MOE_SUB)],
                    ys_hbm.at[pl.ds(pl.multiple_of(row0 + s * MOE_SUB, MOE_SUB), MOE_SUB)],
                    sem_out).start()
        for s in range(MOE_NSUB):
            @pl.when(s < nsub)
            def _():
                pltpu.make_async_copy(
                    acc_scr.at[pl.ds(s * MOE_SUB, MOE_SUB)],
                    ys_hbm.at[pl.ds(pl.multiple_of(row0 + s * MOE_SUB, MOE_SUB), MOE_SUB)],
                    sem_out).wait()


def _moe_experts(wi_e, wi_row, wi_nsub, wi_tok, xn_all, w_gu, b_gu, w_dn, b_dn, n_slots):
    n_wi = wi_e.shape[0]
    nj = D_EXPERT // MOE_TN
    def col(w, j, n):
        return jnp.where(n[w] > 0, j, nj - 1)

    grid_spec = pltpu.PrefetchScalarGridSpec(
        num_scalar_prefetch=3,
        grid=(n_wi, nj),
        in_specs=[
            pl.BlockSpec((1, 1, MOE_TM), lambda w, j, e, r, n: (w, 0, 0), memory_space=pltpu.SMEM),
            pl.BlockSpec(memory_space=pl.ANY),
            pl.BlockSpec((1, D_MODEL, MOE_TN), lambda w, j, e, r, n: (e[w], 0, col(w, j, n))),
            pl.BlockSpec((1, D_MODEL, MOE_TN), lambda w, j, e, r, n: (e[w], 0, nj + col(w, j, n))),
            pl.BlockSpec((1, 1, MOE_TN), lambda w, j, e, r, n: (e[w], 0, col(w, j, n))),
            pl.BlockSpec((1, 1, MOE_TN), lambda w, j, e, r, n: (e[w], 0, nj + col(w, j, n))),
            pl.BlockSpec((1, MOE_TN, D_MODEL), lambda w, j, e, r, n: (e[w], col(w, j, n), 0)),
            pl.BlockSpec((1, 1, D_MODEL), lambda w, j, e, r, n: (e[w], 0, 0)),
        ],
        out_specs=pl.BlockSpec(memory_space=pl.ANY),
        scratch_shapes=[
            pltpu.VMEM((MOE_TM, D_MODEL), F32),
            pltpu.VMEM((MOE_TM, D_MODEL), F32),
            pltpu.VMEM((D_MODEL, MOE_TN), BF16),
            pltpu.VMEM((D_MODEL, MOE_TN), BF16),
            pltpu.VMEM((MOE_TN, D_MODEL), BF16),
            pltpu.SemaphoreType.DMA(()),
            pltpu.SemaphoreType.DMA(()),
        ],
    )
    return pl.pallas_call(
        _moe_kernel,
        grid_spec=grid_spec,
        out_shape=jax.ShapeDtypeStruct((n_slots, D_MODEL), F32),
        compiler_params=_cparams(("arbitrary", "arbitrary"), vmem=58 * 1024 * 1024),
        name="moe_experts",
    )(wi_e, wi_row, wi_nsub, wi_tok, xn_all, w_gu, w_gu,
      b_gu.reshape(N_EXPERTS, 1, 2 * D_EXPERT), b_gu.reshape(N_EXPERTS, 1, 2 * D_EXPERT),
      w_dn, b_dn.reshape(N_EXPERTS, 1, D_MODEL))


def _combine_kernel(dest_ref, gt_ref, h_ref, ys_hbm, o_ref, buf, sem, *, tt):
    def issue(r, carry):
        for k in range(TOP_K):
            slot = dest_ref[0, 0, r * TOP_K + k]
            pltpu.make_async_copy(ys_hbm.at[pl.ds(slot, 1)], buf.at[k, pl.ds(r, 1)], sem).start()
        return carry
    lax.fori_loop(0, tt, issue, 0)
    for k in range(TOP_K):
        pltpu.make_async_copy(ys_hbm.at[pl.ds(0, tt)], buf.at[k], sem).wait()
    gt = gt_ref[...]
    y = gt[:, 0:1] * buf[0]
    for k in range(1, TOP_K):
        y = y + gt[:, k:k + 1] * buf[k]
    o_ref[...] = h_ref[...] + y


def _combine(dest, gates, h1, ys, tt):
    m = h1.shape[0]
    nt = m // tt
    return pl.pallas_call(
        functools.partial(_combine_kernel, tt=tt),
        grid=(nt,),
        in_specs=[
            pl.BlockSpec((1, 1, tt * TOP_K), lambda i: (i, 0, 0), memory_space=pltpu.SMEM),
            pl.BlockSpec((tt, 128), lambda i: (i, 0)),
            pl.BlockSpec((tt, D_MODEL), lambda i: (i, 0)),
            pl.BlockSpec(memory_space=pl.ANY),
        ],
        out_specs=pl.BlockSpec((tt, D_MODEL), lambda i: (i, 0)),
        out_shape=jax.ShapeDtypeStruct((m, D_MODEL), F32),
        scratch_shapes=[pltpu.VMEM((TOP_K, tt, D_MODEL), F32), pltpu.SemaphoreType.DMA(())],
        compiler_params=_cparams(("arbitrary",)),
        name="moe_combine",
    )(dest.reshape(nt, 1, tt * TOP_K), gates, h1, ys)


def _routing_tables(top_idx, rank, counts, n_wi, n_slots):
    n_tok = top_idx.shape[0]
    padded = (counts + MOE_SUB - 1) // MOE_SUB * MOE_SUB
    pad_start = jnp.cumsum(padded) - padded
    dest = pad_start[top_idx] + rank
    tok_ids = jnp.broadcast_to(jnp.arange(n_tok, dtype=I32)[:, None], (n_tok, TOP_K))
    slot_tok = jnp.zeros((n_slots,), I32).at[dest.reshape(-1)].set(tok_ids.reshape(-1))
    nsubs = padded // MOE_SUB
    items = (nsubs + MOE_NSUB - 1) // MOE_NSUB
    item_end = jnp.cumsum(items)
    total = item_end[-1]
    w = jnp.arange(n_wi, dtype=I32)
    w_eff = jnp.minimum(w, total - 1)
    wi_e = jnp.sum((item_end[None, :] <= w_eff[:, None]).astype(I32), axis=1)
    wi_e = jnp.minimum(wi_e, N_EXPERTS - 1)
    local = w_eff - (item_end - items)[wi_e]
    wi_row = (pad_start[wi_e] + local * MOE_TM).astype(I32)
    wi_nsub = jnp.where(w < total, jnp.clip(nsubs[wi_e] - local * MOE_NSUB, 0, MOE_NSUB), 0).astype(I32)
    tok_pos = jnp.minimum(wi_row[:, None] + jnp.arange(MOE_TM, dtype=I32)[None, :], n_slots - 1)
    wi_tok = slot_tok[tok_pos].reshape(n_wi, 1, MOE_TM)
    return dest.astype(I32), wi_e, wi_row, wi_nsub, wi_tok


def kernel(x_prompt, x_sample, cache_k, cache_v, cache_logf, state_pool, page_table,
           norm1_w, w_in, b_forget, q_norm_w, k_norm_w, w_pool, pool_scale,
           w_branch_a, w_branch_b, w_out, norm2_w, router_w, router_b,
           w_gu, b_gu, w_dn, b_dn):
    depth = w_in.shape[0]
    assert depth == 1
    bp, lp, _ = x_prompt.shape
    bs, ls, _ = x_sample.shape
    assert ls == 8 and N_HEADS == 8
    n_p, n_s = bp * lp, bs * ls
    n_all = n_p + n_s
    n_past = page_table.shape[1] * PAGE_SIZE

    wi = w_in[0]
    f_off = 3 * ATTN_WIDTH
    w_main = jnp.concatenate([wi[:, :f_off], wi[:, f_off + N_HEADS:]], axis=1).astype(BF16)
    w_f = jnp.pad(wi[:, f_off:f_off + N_HEADS], ((0, 0), (0, FORGET_LANES - N_HEADS))).astype(BF16)
    b_f = jnp.pad(b_forget[0], (0, FORGET_LANES - N_HEADS)).reshape(1, FORGET_LANES)
    n1w = norm1_w[0].reshape(1, D_MODEL)
    n2w = norm2_w[0].reshape(1, D_MODEL)
    qnw = jnp.tile(q_norm_w[0], N_HEADS).reshape(1, ATTN_WIDTH)
    knw = jnp.tile(k_norm_w[0], N_HEADS).reshape(1, ATTN_WIDTH)
    wp = w_pool[0].astype(BF16)
    ps = pool_scale[0].reshape(1, POOL_WIDTH)
    wa = w_branch_a[0].astype(BF16)
    wb = w_branch_b[0].astype(BF16)
    wo = w_out[0].astype(BF16)
    rw = jnp.pad(router_w[0], ((0, 0), (0, 128 - N_EXPERTS)))
    rb = jnp.pad(router_b[0], (0, 128 - N_EXPERTS), constant_values=NEG_BIG).reshape(1, 128)

    xp = x_prompt.reshape(n_p, D_MODEL)
    zp, lfp = _inproj(xp, n1w, w_main, w_f, b_f, qnw, knw, tm=1024)
    cp = _cumsum_rows(lfp, bl=256, seg_len=lp)
    zp3 = zp.reshape(bp, lp, Z_WIDTH)
    cp3 = cp.reshape(bp, lp, FORGET_LANES)
    cpt3 = jnp.transpose(cp3[:, :, :N_HEADS], (0, 2, 1))
    attn_p = _prompt_attention(zp3, cp3, cpt3)
    halo_p = pl.BlockSpec((1, POOL_HALO, POOL_WIDTH),
                          lambda bi, i: (bi, jnp.maximum(i * (512 // POOL_HALO) - 1, 0), ZU))
    pool_p = _pool_mix(zp3, halo_p, zp3, ZU, wp, ps, tm=512, base_pos=0, zero_first_halo=True)
    h1_p, xn_all, ti_p, gt_p = _merge(
        attn_p.reshape(n_p, ATTN_WIDTH), pool_p.reshape(n_p, POOL_WIDTH), zp, xp,
        wa, wb, wo, n2w, rw, rb, tm=256, xn_all=None, xn_row0=0, n_all=n_all)

    xs = x_sample.reshape(n_s, D_MODEL)
    zs, lfs = _inproj(xs, n1w, w_main, w_f, b_f, qnw, knw, tm=n_s)
    cs = _cumsum_rows(lfs, bl=n_s, seg_len=ls)
    zs3 = zs.reshape(bs, ls, Z_WIDTH)
    q_s = zs3[:, :, 0:ATTN_WIDTH].reshape(bs, ls, N_HEADS, HEAD_DIM)
    qbd = jnp.einsum('bthd,hg->bhtgd', q_s, jnp.eye(N_HEADS, dtype=F32))
    qbd = qbd.reshape(bs, N_HEADS * ls, ATTN_WIDTH).astype(BF16)
    cn = jnp.transpose(cs.reshape(bs, ls, FORGET_LANES)[:, :, :N_HEADS], (0, 2, 1))
    cncol = cn.reshape(bs, N_HEADS * ls, 1)
    cnkeys = jnp.broadcast_to(cn[:, :, None, :], (bs, N_HEADS, ls, ls)).reshape(bs, N_HEADS * ls, ls)
    cnkeys = jnp.pad(cnkeys, ((0, 0), (0, 0), (0, PAGE_SIZE - ls)))
    knew = jnp.pad(zs3[:, :, ATTN_WIDTH:2 * ATTN_WIDTH], ((0, 0), (0, PAGE_SIZE - ls), (0, 0)))
    vnew = jnp.pad(zs3[:, :, 2 * ATTN_WIDTH:3 * ATTN_WIDTH], ((0, 0), (0, PAGE_SIZE - ls), (0, 0)))
    n_phys = cache_k.shape[1]
    ck = cache_k[0].reshape(n_phys, PAGE_SIZE * N_HEADS, HEAD_DIM)
    cv = cache_v[0].reshape(n_phys, PAGE_SIZE * N_HEADS, HEAD_DIM)
    clf_t = jnp.transpose(cache_logf[0], (0, 2, 1))
    attn_s = _sample_attention(page_table, qbd, cncol, cnkeys, knew, vnew, ck, cv, clf_t)
    halo_src = jnp.pad(state_pool[0], ((0, 0), (POOL_HALO - POOL_HIST, 0), (0, 0)))
    halo_s = pl.BlockSpec((1, POOL_HALO, POOL_WIDTH), lambda bi, i: (bi, 0, 0))
    pool_s = _pool_mix(halo_src, halo_s, zs3, ZU, wp, ps, tm=ls, base_pos=n_past, zero_first_halo=False)
    h1_s, xn_all, ti_s, gt_s = _merge(
        attn_s.reshape(n_s, ATTN_WIDTH), pool_s.reshape(n_s, POOL_WIDTH), zs, xs,
        wa, wb, wo, n2w, rw, rb, tm=n_s, xn_all=xn_all, xn_row0=n_p, n_all=n_all)

    n_rank = pl.cdiv(n_all, 128) * 128
    ti_all = jnp.concatenate([ti_p, ti_s, jnp.full((n_rank - n_all, 128), -1, I32)], axis=0)
    rank, cnt = _expert_rank(ti_all)
    top_idx = ti_all[:n_all, :TOP_K]
    counts = cnt[0, :N_EXPERTS].astype(I32)
    n_blocks = pl.cdiv(n_all * TOP_K, MOE_SUB) + N_EXPERTS
    n_slots = n_blocks * MOE_SUB
    n_wi = n_blocks // MOE_NSUB + N_EXPERTS
    dest, wi_e, wi_row, wi_nsub, wi_tok = _routing_tables(
        top_idx, rank[:n_all, :TOP_K], counts, n_wi, n_slots)
    ys = _moe_experts(wi_e, wi_row, wi_nsub, wi_tok, xn_all, w_gu[0], b_gu[0], w_dn[0], b_dn[0], n_slots)
    y_p = _combine(dest[:n_p], gt_p, h1_p, ys, tt=128)
    y_s = _combine(dest[n_p:], gt_s, h1_s, ys, tt=n_s)

    heads = (N_HEADS, HEAD_DIM)
    k_p = zp3[:, :, ATTN_WIDTH:2 * ATTN_WIDTH].reshape(1, bp, lp, *heads)
    v_p = zp3[:, :, 2 * ATTN_WIDTH:3 * ATTN_WIDTH].reshape(1, bp, lp, *heads)
    f_p = lfp.reshape(bp, lp, FORGET_LANES)[None, :, :, :N_HEADS]
    u_p = zp3[:, lp - POOL_HIST:, 3 * ATTN_WIDTH:3 * ATTN_WIDTH + POOL_WIDTH][None]
    k_s = zs3[:, :, ATTN_WIDTH:2 * ATTN_WIDTH].reshape(1, bs, ls, *heads)
    v_s = zs3[:, :, 2 * ATTN_WIDTH:3 * ATTN_WIDTH].reshape(1, bs, ls, *heads)
    f_s = lfs.reshape(bs, ls, FORGET_LANES)[None, :, :, :N_HEADS]
    u_s = zs3[:, :, 3 * ATTN_WIDTH:3 * ATTN_WIDTH + POOL_WIDTH]
    pool_state = jnp.concatenate([state_pool[0], u_s], axis=1)[:, -POOL_HIST:][None]
    return (y_p.reshape(bp, lp, D_MODEL), y_s.reshape(bs, ls, D_MODEL),
            k_p, v_p, f_p, u_p, k_s, v_s, f_s, pool_state)
```

```python
import functools

import jax
import jax.numpy as jnp
from jax import lax
from jax.experimental import pallas as pl
from jax.experimental.pallas import tpu as pltpu

F32 = jnp.float32
BF16 = jnp.bfloat16
I32 = jnp.int32

D_MODEL = 2048
N_HEADS = 8
HEAD_DIM = 128
ATTN_WIDTH = N_HEADS * HEAD_DIM
ATTN_SCALE = HEAD_DIM ** -0.5
FORGET_LANES = 128
POOL_WINDOWS = (2, 4, 8, 16)
POOL_WIDTH = D_MODEL // 2
GW = POOL_WIDTH // len(POOL_WINDOWS)
POOL_HIST = max(POOL_WINDOWS) - 1
POOL_HALO = 16
N_EXPERTS = 32
TOP_K = 4
D_EXPERT = D_MODEL
SWIGLU_LIMIT = 7.0
SWIGLU_ALPHA = 1.702
NORM_EPS = 1e-6
PAGE_SIZE = 128
NEG_BIG = -1e30

ZQ, ZK, ZV, ZU = 0, 1, 2, 3
Z_WIDTH = 3 * ATTN_WIDTH + POOL_WIDTH + 2 * D_MODEL

MOE_SUB = 128
MOE_NSUB = 9
MOE_TM = MOE_SUB * MOE_NSUB
MOE_TN = 512
VMEM_LIMIT = 56 * 1024 * 1024

HIGHEST = lax.Precision.HIGHEST


def _cparams(sem, vmem=VMEM_LIMIT):
    return pltpu.CompilerParams(dimension_semantics=sem, vmem_limit_bytes=vmem)


def _log_sigmoid(x):
    return jnp.minimum(x, 0.0) - jnp.log1p(jnp.exp(-jnp.abs(x)))


def _inproj_kernel(x_ref, n1w_ref, w_ref, wf_ref, bf_ref, qnw_ref, knw_ref,
                   z_ref, lf_ref, xn_ref):
    j = pl.program_id(1)

    @pl.when(j == 0)
    def _():
        x = x_ref[...]
        ms = jnp.mean(x * x, axis=-1, keepdims=True)
        xn = x * lax.rsqrt(ms + NORM_EPS) * n1w_ref[...]
        xn_ref[...] = xn.astype(BF16)
        zf = jnp.dot(xn_ref[...], wf_ref[...], preferred_element_type=F32) + bf_ref[...]
        lf_ref[...] = _log_sigmoid(zf)

    z = jnp.dot(xn_ref[...], w_ref[...], preferred_element_type=F32)

    @pl.when(j <= ZK)
    def _():
        nw = jnp.where(j == ZQ, qnw_ref[...], knw_ref[...])
        for h in range(N_HEADS):
            sl = slice(h * HEAD_DIM, (h + 1) * HEAD_DIM)
            zh = z[:, sl]
            ms = jnp.mean(zh * zh, axis=-1, keepdims=True)
            z_ref[:, sl] = zh * lax.rsqrt(ms + NORM_EPS) * nw[:, sl]

    @pl.when((j == ZV) | (j == ZU))
    def _():
        z_ref[...] = z

    @pl.when(j > ZU)
    def _():
        z_ref[...] = jax.nn.sigmoid(z)


def _inproj(x, n1w, w_main, w_f, b_f, qnw, knw, tm):
    m = x.shape[0]
    tn = ATTN_WIDTH
    return pl.pallas_call(
        _inproj_kernel,
        grid=(m // tm, Z_WIDTH // tn),
        in_specs=[
            pl.BlockSpec((tm, D_MODEL), lambda i, j: (i, 0)),
            pl.BlockSpec((1, D_MODEL), lambda i, j: (0, 0)),
            pl.BlockSpec((D_MODEL, tn), lambda i, j: (0, j)),
            pl.BlockSpec((D_MODEL, FORGET_LANES), lambda i, j: (0, 0)),
            pl.BlockSpec((1, FORGET_LANES), lambda i, j: (0, 0)),
            pl.BlockSpec((1, tn), lambda i, j: (0, 0)),
            pl.BlockSpec((1, tn), lambda i, j: (0, 0)),
        ],
        out_specs=[
            pl.BlockSpec((tm, tn), lambda i, j: (i, j)),
            pl.BlockSpec((tm, FORGET_LANES), lambda i, j: (i, 0)),
        ],
        out_shape=[
            jax.ShapeDtypeStruct((m, Z_WIDTH), F32),
            jax.ShapeDtypeStruct((m, FORGET_LANES), F32),
        ],
        scratch_shapes=[pltpu.VMEM((tm, D_MODEL), BF16)],
        compiler_params=_cparams(("arbitrary", "arbitrary")),
        name="inproj",
    )(x, n1w, w_main, w_f, b_f, qnw, knw)


def _cumsum_kernel(lf_ref, c_ref, carry_ref, *, bl, seg_len):
    i = pl.program_id(0)
    r = lax.broadcasted_iota(I32, (bl, bl), 0)
    c = lax.broadcasted_iota(I32, (bl, bl), 1)
    mask = c <= r
    if seg_len < bl:
        mask = mask & ((r // seg_len) == (c // seg_len))
    tri = mask.astype(F32)
    cs = jnp.dot(tri, lf_ref[...], preferred_element_type=F32, precision=HIGHEST)
    if seg_len > bl:
        @pl.when((i * bl) % seg_len == 0)
        def _():
            carry_ref[...] = jnp.zeros_like(carry_ref)
        cs = cs + carry_ref[...]
        carry_ref[...] = cs[bl - 1:bl, :]
    c_ref[...] = cs


def _cumsum_rows(lf, bl, seg_len):
    m = lf.shape[0]
    return pl.pallas_call(
        functools.partial(_cumsum_kernel, bl=bl, seg_len=seg_len),
        grid=(m // bl,),
        in_specs=[pl.BlockSpec((bl, FORGET_LANES), lambda i: (i, 0))],
        out_specs=pl.BlockSpec((bl, FORGET_LANES), lambda i: (i, 0)),
        out_shape=jax.ShapeDtypeStruct((m, FORGET_LANES), F32),
        scratch_shapes=[pltpu.VMEM((1, FORGET_LANES), F32)],
        compiler_params=_cparams(("arbitrary",)),
        name="logf_cumsum",
    )(lf)


def _pattn_kernel(q_ref, k_ref, v_ref, c_ref, ct_ref, o_ref, *scratch, tq, ch):
    m_refs = scratch[0:N_HEADS]
    l_refs = scratch[N_HEADS:2 * N_HEADS]
    acc_refs = scratch[2 * N_HEADS:3 * N_HEADS]
    kb_ref, vb_ref = scratch[3 * N_HEADS:]
    qi = pl.program_id(1)
    ki = pl.program_id(2)
    log2e = 1.4426950408889634

    @pl.when(ki == 0)
    def _():
        for h in range(N_HEADS):
            m_refs[h][...] = jnp.full_like(m_refs[h], -jnp.inf)
            l_refs[h][...] = jnp.zeros_like(l_refs[h])
            acc_refs[h][...] = jnp.zeros_like(acc_refs[h])

    @pl.when(ki <= qi)
    def _():
        kb_ref[...] = k_ref[0].astype(BF16)
        vb_ref[...] = v_ref[0].astype(BF16)

    def block(diag):
        ct2 = ct_ref[0] * log2e
        c2 = c_ref[0] * log2e
        below = (lax.broadcasted_iota(I32, (ch, ch), 1) <= lax.broadcasted_iota(I32, (ch, ch), 0))
        for h in range(N_HEADS):
            sl = slice(h * HEAD_DIM, (h + 1) * HEAD_DIM)
            for r in range(tq // ch):
                rows = slice(r * ch, (r + 1) * ch)
                ncols = (r + 1) * ch if diag else tq
                q = q_ref[0, rows, sl].astype(BF16)
                u = lax.dot_general(q, kb_ref[0:ncols, sl], (((1,), (1,)), ((), ())),
                                    preferred_element_type=F32)
                u = u * (ATTN_SCALE * log2e) - ct2[h:h + 1, 0:ncols]
                if diag:
                    last = jnp.where(below, u[:, r * ch:], -jnp.inf)
                    u = last if r == 0 else jnp.concatenate([u[:, :r * ch], last], axis=1)
                c2t = c2[rows, h:h + 1]
                m_prev = m_refs[h][rows, :]
                m_new = jnp.maximum(m_prev, jnp.max(u, axis=-1, keepdims=True) + c2t)
                alpha = jnp.exp2(m_prev - m_new)
                p = jnp.exp2(u + (c2t - m_new[:, 0:1]))
                l_refs[h][rows, :] = alpha * l_refs[h][rows, :] + jnp.sum(p, axis=-1, keepdims=True)
                pv = jnp.dot(p.astype(BF16), vb_ref[0:ncols, sl], preferred_element_type=F32)
                acc_refs[h][rows, :] = alpha * acc_refs[h][rows, :] + pv
                m_refs[h][rows, :] = m_new

    @pl.when(ki < qi)
    def _():
        block(False)

    @pl.when(ki == qi)
    def _():
        block(True)
        for h in range(N_HEADS):
            sl = slice(h * HEAD_DIM, (h + 1) * HEAD_DIM)
            o_ref[0, :, sl] = (acc_refs[h][...] / l_refs[h][...]).astype(o_ref.dtype)


def _prompt_attention(z3, c3, ct3, tq=512, ch=512):
    b, l, _ = z3.shape
    tk = tq
    nq = l // tq
    kv_idx = lambda col: (lambda bi, qi, ki: (bi, jnp.minimum(ki, qi), col))
    return pl.pallas_call(
        functools.partial(_pattn_kernel, tq=tq, ch=ch),
        grid=(b, nq, nq),
        in_specs=[
            pl.BlockSpec((1, tq, ATTN_WIDTH), lambda bi, qi, ki: (bi, qi, ZQ)),
            pl.BlockSpec((1, tk, ATTN_WIDTH), kv_idx(ZK)),
            pl.BlockSpec((1, tk, ATTN_WIDTH), kv_idx(ZV)),
            pl.BlockSpec((1, tq, FORGET_LANES), lambda bi, qi, ki: (bi, qi, 0)),
            pl.BlockSpec((1, N_HEADS, tk), lambda bi, qi, ki: (bi, 0, jnp.minimum(ki, qi))),
        ],
        out_specs=pl.BlockSpec((1, tq, ATTN_WIDTH), lambda bi, qi, ki: (bi, qi, 0)),
        out_shape=jax.ShapeDtypeStruct((b, l, ATTN_WIDTH), BF16),
        scratch_shapes=(
            [pltpu.VMEM((tq, HEAD_DIM), F32) for _ in range(3 * N_HEADS)]
            + [pltpu.VMEM((tk, ATTN_WIDTH), BF16), pltpu.VMEM((tk, ATTN_WIDTH), BF16)]),
        compiler_params=_cparams(("arbitrary", "arbitrary", "arbitrary")),
        name="prompt_attention",
    )(z3, z3, z3, c3, ct3)


def _heads_to_lanes(ref):
    parts = [ref[0, pl.ds(h, PAGE_SIZE, stride=N_HEADS), :] for h in range(N_HEADS)]
    return jnp.concatenate(parts, axis=1).astype(BF16)


def _rows_per_head(x):
    return jnp.concatenate(
        [jnp.broadcast_to(x[h:h + 1, :], (8, x.shape[1])) for h in range(N_HEADS)], axis=0)


def _sattn_kernel(pt_ref, qbd_ref, cncol_ref, cnkeys_ref, knew_ref, vnew_ref, *rest, g_pages):
    del pt_ref
    k_refs = rest[0:g_pages]
    v_refs = rest[g_pages:2 * g_pages]
    lf_refs = rest[2 * g_pages:3 * g_pages]
    o_ref = rest[3 * g_pages]
    m_ref, l_ref, acc_ref, carry_ref = rest[3 * g_pages + 1:]
    jj = pl.program_id(1)
    nj = pl.num_programs(1)
    rows = N_HEADS * 8

    @pl.when(jj == 0)
    def _():
        m_ref[...] = jnp.full_like(m_ref, -jnp.inf)
        l_ref[...] = jnp.zeros_like(l_ref)
        acc_ref[...] = jnp.zeros_like(acc_ref)
        carry_ref[...] = jnp.zeros_like(carry_ref)

    qbd = qbd_ref[0]
    cncol = cncol_ref[0]
    later = (lax.broadcasted_iota(I32, (PAGE_SIZE, PAGE_SIZE), 0)
             > lax.broadcasted_iota(I32, (PAGE_SIZE, PAGE_SIZE), 1)).astype(F32)

    def scores(kcat):
        return lax.dot_general(qbd, kcat, (((1,), (1,)), ((), ())),
                               preferred_element_type=F32) * ATTN_SCALE

    def update(s, vcat, m, l, acc):
        m_new = jnp.maximum(m, jnp.max(s, axis=-1, keepdims=True))
        alpha = jnp.exp(m - m_new)
        p = jnp.exp(s - jnp.concatenate([m_new] * (s.shape[1] // PAGE_SIZE), axis=1))
        l = alpha * l + jnp.sum(p, axis=-1, keepdims=True)
        pv = jnp.dot(p.astype(BF16), vcat, preferred_element_type=F32)
        acc = jnp.concatenate([alpha] * N_HEADS, axis=1) * acc + pv
        return m_new, l, acc

    m, l, acc, carry = m_ref[...], l_ref[...], acc_ref[...], carry_ref[...]
    lf_all = jnp.concatenate([_rows_per_head(lf_refs[g][0]) for g in range(g_pages)], axis=0)
    suffix_all = jnp.dot(lf_all, later, preferred_element_type=F32, precision=HIGHEST)
    bias = []
    for g in range(g_pages):
        bias.append(suffix_all[g * rows:(g + 1) * rows, :] + carry)
        carry = carry + jnp.sum(lf_all[g * rows:(g + 1) * rows, :], axis=-1, keepdims=True)
    kcat = jnp.concatenate([_heads_to_lanes(k_refs[g]) for g in range(g_pages)], axis=0)
    vcat = jnp.concatenate([_heads_to_lanes(v_refs[g]) for g in range(g_pages)], axis=0)
    s = scores(kcat) + cncol + jnp.concatenate(bias, axis=1)
    m, l, acc = update(s, vcat, m, l, acc)

    @pl.when(jj < nj - 1)
    def _():
        m_ref[...] = m
        l_ref[...] = l
        acc_ref[...] = acc
        carry_ref[...] = carry

    @pl.when(jj == nj - 1)
    def _():
        s = scores(knew_ref[0].astype(BF16)) + cncol - cnkeys_ref[0]
        t_row = lax.broadcasted_iota(I32, (rows, PAGE_SIZE), 0) % 8
        s_col = lax.broadcasted_iota(I32, (rows, PAGE_SIZE), 1)
        s = jnp.where(s_col <= t_row, s, -jnp.inf)
        _, l2, acc2 = update(s, vnew_ref[0].astype(BF16), m, l, acc)
        o = acc2 / jnp.concatenate([l2] * N_HEADS, axis=1)
        for h in range(N_HEADS):
            sl = slice(h * HEAD_DIM, (h + 1) * HEAD_DIM)
            o_ref[0, :, sl] = o[h * 8:(h + 1) * 8, sl].astype(o_ref.dtype)


def _sample_attention(page_table, qbd, cncol, cnkeys, knew, vnew, ck, cv, clf_t, g_pages=8):
    bd, n_pages = page_table.shape
    rows = N_HEADS * 8
    pt_flat = page_table.reshape(-1)

    def page_idx(g):
        return lambda b, jj, pt: (pt[b * n_pages + n_pages - 1 - (jj * g_pages + g)], 0, 0)

    per_b = lambda b, jj, pt: (b, 0, 0)
    in_specs = [
        pl.BlockSpec((1, rows, ATTN_WIDTH), per_b),
        pl.BlockSpec((1, rows, 1), per_b),
        pl.BlockSpec((1, rows, PAGE_SIZE), per_b),
        pl.BlockSpec((1, PAGE_SIZE, ATTN_WIDTH), per_b),
        pl.BlockSpec((1, PAGE_SIZE, ATTN_WIDTH), per_b),
    ]
    in_specs += [pl.BlockSpec((1, PAGE_SIZE * N_HEADS, HEAD_DIM), page_idx(g)) for g in range(g_pages)]
    in_specs += [pl.BlockSpec((1, PAGE_SIZE * N_HEADS, HEAD_DIM), page_idx(g)) for g in range(g_pages)]
    in_specs += [pl.BlockSpec((1, N_HEADS, PAGE_SIZE), page_idx(g)) for g in range(g_pages)]
    grid_spec = pltpu.PrefetchScalarGridSpec(
        num_scalar_prefetch=1,
        grid=(bd, n_pages // g_pages),
        in_specs=in_specs,
        out_specs=pl.BlockSpec((1, 8, ATTN_WIDTH), per_b),
        scratch_shapes=[
            pltpu.VMEM((rows, PAGE_SIZE), F32),
            pltpu.VMEM((rows, PAGE_SIZE), F32),
            pltpu.VMEM((rows, ATTN_WIDTH), F32),
            pltpu.VMEM((rows, PAGE_SIZE), F32),
        ],
    )
    return pl.pallas_call(
        functools.partial(_sattn_kernel, g_pages=g_pages),
        grid_spec=grid_spec,
        out_shape=jax.ShapeDtypeStruct((bd, 8, ATTN_WIDTH), BF16),
        compiler_params=_cparams(("arbitrary", "arbitrary")),
        name="sample_attention",
    )(pt_flat, qbd, cncol, cnkeys, knew, vnew,
      *([ck] * g_pages), *([cv] * g_pages), *([clf_t] * g_pages))


def _pool_kernel(halo_ref, u_ref, wp_ref, ps_ref, o_ref, *, tm, base_pos, zero_first_halo):
    i = pl.program_id(1)
    halo = halo_ref[0]
    if zero_first_halo:
        halo = jnp.where(i == 0, 0.0, halo)
    u = u_ref[0]
    ext = jnp.concatenate([halo, u], axis=0)
    pos = base_pos + i * tm + lax.broadcasted_iota(I32, (tm, 1), 0)
    for g, w in enumerate(POOL_WINDOWS):
        sl = slice(g * GW, (g + 1) * GW)
        s = ext[:, sl]
        k = 1
        while k < w:
            s = s + pltpu.roll(s, shift=k, axis=0)
            k *= 2
        wsum = s[POOL_HALO:, :]
        count = jnp.minimum(pos + 1, w).astype(F32)
        mixed = wsum / count - u[:, sl]
        y = jnp.dot(mixed.astype(BF16), wp_ref[g], preferred_element_type=F32)
        o_ref[0, :, sl] = (y * ps_ref[:, sl]).astype(o_ref.dtype)


def _pool_mix(halo_src, halo_spec, u_src, u_col, wp, ps, tm, base_pos, zero_first_halo):
    b, l, _ = u_src.shape
    return pl.pallas_call(
        functools.partial(_pool_kernel, tm=tm, base_pos=base_pos, zero_first_halo=zero_first_halo),
        grid=(b, l // tm),
        in_specs=[
            halo_spec,
            pl.BlockSpec((1, tm, POOL_WIDTH), lambda bi, i: (bi, i, u_col)),
            pl.BlockSpec((len(POOL_WINDOWS), GW, GW), lambda bi, i: (0, 0, 0)),
            pl.BlockSpec((1, POOL_WIDTH), lambda bi, i: (0, 0)),
        ],
        out_specs=pl.BlockSpec((1, tm, POOL_WIDTH), lambda bi, i: (bi, i, 0)),
        out_shape=jax.ShapeDtypeStruct((b, l, POOL_WIDTH), BF16),
        compiler_params=_cparams(("arbitrary", "arbitrary")),
        name="pool_mix",
    )(halo_src, u_src, wp, ps)


def _merge_kernel(attn_ref, pool_ref, ga_ref, gb_ref, x_ref, wa_ref, wb_ref, wo_ref,
                  n2w_ref, rw_ref, rb_ref, *rest, tm, aliased):
    h_ref, xn_ref, ti_ref, gt_ref = rest[-4:]
    a = jnp.dot(attn_ref[...], wa_ref[...], preferred_element_type=F32)
    p = jnp.dot(pool_ref[...], wb_ref[...], preferred_element_type=F32)
    merged = ga_ref[...] * a + gb_ref[...] * p
    o = jnp.dot(merged.astype(BF16), wo_ref[...], preferred_element_type=F32)
    h1 = x_ref[...] + o
    h_ref[...] = h1
    ms = jnp.mean(h1 * h1, axis=-1, keepdims=True)
    xn = h1 * lax.rsqrt(ms + NORM_EPS) * n2w_ref[...]
    xn_ref[...] = xn
    logits = jnp.dot(xn, rw_ref[...], preferred_element_type=F32, precision=HIGHEST) + rb_ref[...]
    lane = lax.broadcasted_iota(I32, (tm, 128), 1).astype(F32)
    vals, idxs = [], []
    work = logits
    for _ in range(TOP_K):
        mv = jnp.max(work, axis=-1, keepdims=True)
        ix = jnp.min(jnp.where(work == mv, lane, 128.0), axis=-1, keepdims=True)
        vals.append(mv)
        idxs.append(ix)
        work = jnp.where(lane == ix, NEG_BIG, work)
    es = [jnp.exp(v - vals[0]) for v in vals]
    den = es[0] + es[1] + es[2] + es[3]
    ti = jnp.zeros((tm, 128), F32)
    gt = jnp.zeros((tm, 128), F32)
    for k in range(TOP_K):
        ti = jnp.where(lane == float(k), idxs[k], ti)
        gt = jnp.where(lane == float(k), es[k] / den, gt)
    ti_ref[...] = ti.astype(I32)
    gt_ref[...] = gt


def _merge(attn, pool, z, x, wa, wb, wo, n2w, rw, rb, tm, xn_all, xn_row0, n_all):
    m = x.shape[0]
    aliased = xn_all is not None
    const = lambda i: (0, 0)
    single = dict(pipeline_mode=pl.Buffered(1))
    xn_blk0 = xn_row0 // tm
    in_specs = [
        pl.BlockSpec((tm, ATTN_WIDTH), lambda i: (i, 0)),
        pl.BlockSpec((tm, POOL_WIDTH), lambda i: (i, 0)),
        pl.BlockSpec((tm, D_MODEL), lambda i: (i, 2)),
        pl.BlockSpec((tm, D_MODEL), lambda i: (i, 3)),
        pl.BlockSpec((tm, D_MODEL), lambda i: (i, 0)),
        pl.BlockSpec((ATTN_WIDTH, D_MODEL), const, **single),
        pl.BlockSpec((POOL_WIDTH, D_MODEL), const, **single),
        pl.BlockSpec((D_MODEL, D_MODEL), const, **single),
        pl.BlockSpec((1, D_MODEL), const),
        pl.BlockSpec((D_MODEL, 128), const),
        pl.BlockSpec((1, 128), const),
    ]
    args = [attn, pool, z, z, x, wa, wb, wo, n2w, rw, rb]
    io_alias = {}
    if aliased:
        in_specs.append(pl.BlockSpec(memory_space=pl.ANY))
        args.append(xn_all)
        io_alias = {len(args) - 1: 1}
    return pl.pallas_call(
        functools.partial(_merge_kernel, tm=tm, aliased=aliased),
        grid=(m // tm,),
        in_specs=in_specs,
        out_specs=[
            pl.BlockSpec((tm, D_MODEL), lambda i: (i, 0)),
            pl.BlockSpec((tm, D_MODEL), lambda i: (xn_blk0 + i, 0)),
            pl.BlockSpec((tm, 128), lambda i: (i, 0)),
            pl.BlockSpec((tm, 128), lambda i: (i, 0)),
        ],
        out_shape=[
            jax.ShapeDtypeStruct((m, D_MODEL), F32),
            jax.ShapeDtypeStruct((n_all, D_MODEL), F32),
            jax.ShapeDtypeStruct((m, 128), I32),
            jax.ShapeDtypeStruct((m, 128), F32),
        ],
        input_output_aliases=io_alias,
        compiler_params=_cparams(("arbitrary",)),
        name="merge_router",
    )(*args)


def _rank_kernel(idx_ref, rank_ref, cnt_ref, carry_ref, *, tr):
    i = pl.program_id(0)

    @pl.when(i == 0)
    def _():
        carry_ref[...] = jnp.zeros_like(carry_ref)

    idx = idx_ref[...]
    lane = lax.broadcasted_iota(I32, (tr, 128), 1)
    hits = [idx[:, k:k + 1] == lane for k in range(TOP_K)]
    onehot = jnp.zeros((tr, 128), F32)
    for hk in hits:
        onehot = onehot + hk.astype(F32)
    earlier = (lax.broadcasted_iota(I32, (tr, tr), 1)
               < lax.broadcasted_iota(I32, (tr, tr), 0)).astype(BF16)
    before = jnp.dot(earlier, onehot.astype(BF16), preferred_element_type=F32) + carry_ref[...]
    out = jnp.zeros((tr, 128), I32)
    for k, hk in enumerate(hits):
        rk = jnp.sum(jnp.where(hk, before, 0.0), axis=-1, keepdims=True)
        out = jnp.where(lane == k, rk.astype(I32), out)
    rank_ref[...] = out
    carry_ref[...] = carry_ref[...] + jnp.sum(onehot, axis=0, keepdims=True)
    cnt_ref[...] = carry_ref[...]


def _expert_rank(top_idx_padded, tr=128):
    n = top_idx_padded.shape[0]
    return pl.pallas_call(
        functools.partial(_rank_kernel, tr=tr),
        grid=(n // tr,),
        in_specs=[pl.BlockSpec((tr, 128), lambda i: (i, 0))],
        out_specs=[
            pl.BlockSpec((tr, 128), lambda i: (i, 0)),
            pl.BlockSpec((1, 128), lambda i: (0, 0)),
        ],
        out_shape=[
            jax.ShapeDtypeStruct((n, 128), I32),
            jax.ShapeDtypeStruct((1, 128), F32),
        ],
        scratch_shapes=[pltpu.VMEM((1, 128), F32)],
        compiler_params=_cparams(("arbitrary",)),
        name="expert_rank",
    )(top_idx_padded)


def _moe_row_block_copy(src, dst, sub, sem):
    return pltpu.make_async_copy(src.at[pl.ds(0, MOE_SUB)], dst.at[pl.ds(sub * MOE_SUB, MOE_SUB)], sem)


def _moe_out_block_copy(acc_scr, ys_hbm, sub, row, sem):
    return pltpu.make_async_copy(acc_scr.at[pl.ds(sub * MOE_SUB, MOE_SUB)],
                                 ys_hbm.at[pl.ds(row, MOE_SUB)], sem)


def _moe_kernel(wi_e_ref, wi_row_ref, wi_nsub_ref, tok_ref, xn_hbm,
                wg_ref, wu_ref, bg_ref, bu_ref, wd_ref, bd_ref, ys_hbm,
                x_scr, acc_scr, wg_bf, wu_bf, wd_bf, pend_ref, sem_in, sem_out):
    del wi_e_ref
    w = pl.program_id(0)
    j = pl.program_id(1)
    nj = pl.num_programs(1)
    nsub = wi_nsub_ref[w]
    row0 = wi_row_ref[w]
    issue_unroll = 8

    @pl.when((w == 0) & (j == 0))
    def _():
        pend_ref[0] = 0

    @pl.when((j == 0) & (nsub > 0))
    def _():
        def issue(blk, carry):
            for i in range(issue_unroll):
                r = blk * issue_unroll + i
                tok = tok_ref[0, 0, r]
                pltpu.make_async_copy(xn_hbm.at[pl.ds(tok, 1)], x_scr.at[pl.ds(r, 1)], sem_in).start()
            return carry
        lax.fori_loop(0, nsub * (MOE_SUB // issue_unroll), issue, 0)

    def drain_output():
        for s in range(MOE_NSUB):
            @pl.when(s < pend_ref[0])
            def _():
                _moe_out_block_copy(acc_scr, ys_hbm, s, 0, sem_out).wait()
        pend_ref[0] = 0

    @pl.when(j == 0)
    def _():
        drain_output()

    @pl.when((j == 0) & (nsub > 0))
    def _():
        for s in range(MOE_NSUB):
            @pl.when(s < nsub)
            def _():
                acc_scr[s * MOE_SUB:(s + 1) * MOE_SUB, :] = jnp.broadcast_to(
                    bd_ref[0], (MOE_SUB, D_MODEL))
        for s in range(MOE_NSUB):
            @pl.when(s < nsub)
            def _():
                _moe_row_block_copy(xn_hbm, x_scr, s, sem_in).wait()

    def expert_rows(rows):
        xs = x_scr[rows, :].astype(BF16)
        gate = jnp.dot(xs, wg_bf[...], preferred_element_type=F32) + bg_ref[0]
        up = jnp.dot(xs, wu_bf[...], preferred_element_type=F32) + bu_ref[0]
        gate = jnp.minimum(gate, SWIGLU_LIMIT)
        up = jnp.clip(up, -SWIGLU_LIMIT, SWIGLU_LIMIT)
        act = (up + 1.0) * (gate * jax.nn.sigmoid(SWIGLU_ALPHA * gate))
        acc_scr[rows, :] += jnp.dot(act.astype(BF16), wd_bf[...], preferred_element_type=F32)

    @pl.when(nsub > 0)
    def _():
        wg_bf[...] = wg_ref[0].astype(BF16)
        wu_bf[...] = wu_ref[0].astype(BF16)
        wd_bf[...] = wd_ref[0].astype(BF16)
        for c in range(MOE_NSUB // 4):
            @pl.when(4 * c + 4 <= nsub)
            def _():
                expert_rows(pl.ds(4 * c * MOE_SUB, 4 * MOE_SUB))

        @pl.when(nsub % 4 >= 2)
        def _():
            expert_rows(pl.ds(pl.multiple_of((nsub // 4) * 4 * MOE_SUB, MOE_SUB), 2 * MOE_SUB))

        @pl.when(nsub % 2 == 1)
        def _():
            expert_rows(pl.ds(pl.multiple_of((nsub - 1) * MOE_SUB, MOE_SUB), MOE_SUB))

    @pl.when((j == nj - 1) & (nsub > 0))
    def _():
        for s in range(MOE_NSUB):
            @pl.when(s < nsub)
            def _():
                _moe_out_block_copy(acc_scr, ys_hbm, s,
                                    pl.multiple_of(row0 + s * MOE_SUB, MOE_SUB), sem_out).start()
        pend_ref[0] = nsub

    @pl.when((w == pl.num_programs(0) - 1) & (j == nj - 1))
    def _():
        drain_output()


def _moe_experts(wi_e, wi_row, wi_nsub, wi_tok, xn_all, w_gu, b_gu, w_dn, b_dn, n_slots):
    n_wi = wi_e.shape[0]
    nj = D_EXPERT // MOE_TN
    def col(w, j, n):
        return jnp.where(n[w] > 0, j, nj - 1)

    grid_spec = pltpu.PrefetchScalarGridSpec(
        num_scalar_prefetch=3,
        grid=(n_wi, nj),
        in_specs=[
            pl.BlockSpec((1, 1, MOE_TM), lambda w, j, e, r, n: (w, 0, 0), memory_space=pltpu.SMEM),
            pl.BlockSpec(memory_space=pl.ANY),
            pl.BlockSpec((1, D_MODEL, MOE_TN), lambda w, j, e, r, n: (e[w], 0, col(w, j, n))),
            pl.BlockSpec((1, D_MODEL, MOE_TN), lambda w, j, e, r, n: (e[w], 0, nj + col(w, j, n))),
            pl.BlockSpec((1, 1, MOE_TN), lambda w, j, e, r, n: (e[w], 0, col(w, j, n))),
            pl.BlockSpec((1, 1, MOE_TN), lambda w, j, e, r, n: (e[w], 0, nj + col(w, j, n))),
            pl.BlockSpec((1, MOE_TN, D_MODEL), lambda w, j, e, r, n: (e[w], col(w, j, n), 0)),
            pl.BlockSpec((1, 1, D_MODEL), lambda w, j, e, r, n: (e[w], 0, 0)),
        ],
        out_specs=pl.BlockSpec(memory_space=pl.ANY),
        scratch_shapes=[
            pltpu.VMEM((MOE_TM, D_MODEL), F32),
            pltpu.VMEM((MOE_TM, D_MODEL), F32),
            pltpu.VMEM((D_MODEL, MOE_TN), BF16),
            pltpu.VMEM((D_MODEL, MOE_TN), BF16),
            pltpu.VMEM((MOE_TN, D_MODEL), BF16),
            pltpu.SMEM((1,), I32),
            pltpu.SemaphoreType.DMA(()),
            pltpu.SemaphoreType.DMA(()),
        ],
    )
    return pl.pallas_call(
        _moe_kernel,
        grid_spec=grid_spec,
        out_shape=jax.ShapeDtypeStruct((n_slots, D_MODEL), F32),
        compiler_params=_cparams(("arbitrary", "arbitrary"), vmem=58 * 1024 * 1024),
        name="moe_experts",
    )(wi_e, wi_row, wi_nsub, wi_tok, xn_all, w_gu, w_gu,
      b_gu.reshape(N_EXPERTS, 1, 2 * D_EXPERT), b_gu.reshape(N_EXPERTS, 1, 2 * D_EXPERT),
      w_dn, b_dn.reshape(N_EXPERTS, 1, D_MODEL))


def _combine_kernel(dest_ref, gt_ref, h_ref, ys_hbm, o_ref, buf, sem, *, tt):
    def issue(r, carry):
        for k in range(TOP_K):
            slot = dest_ref[0, 0, r * TOP_K + k]
            pltpu.make_async_copy(ys_hbm.at[pl.ds(slot, 1)], buf.at[k, pl.ds(r, 1)], sem).start()
        return carry
    lax.fori_loop(0, tt, issue, 0, unroll=4)
    for k in range(TOP_K):
        pltpu.make_async_copy(ys_hbm.at[pl.ds(0, tt)], buf.at[k], sem).wait()
    gt = gt_ref[...]
    y = gt[:, 0:1] * buf[0]
    for k in range(1, TOP_K):
        y = y + gt[:, k:k + 1] * buf[k]
    o_ref[...] = h_ref[...] + y


def _combine(dest, gates, h1, ys, tt):
    m = h1.shape[0]
    nt = m // tt
    return pl.pallas_call(
        functools.partial(_combine_kernel, tt=tt),
        grid=(nt,),
        in_specs=[
            pl.BlockSpec((1, 1, tt * TOP_K), lambda i: (i, 0, 0), memory_space=pltpu.SMEM),
            pl.BlockSpec((tt, 128), lambda i: (i, 0)),
            pl.BlockSpec((tt, D_MODEL), lambda i: (i, 0)),
            pl.BlockSpec(memory_space=pl.ANY),
        ],
        out_specs=pl.BlockSpec((tt, D_MODEL), lambda i: (i, 0)),
        out_shape=jax.ShapeDtypeStruct((m, D_MODEL), F32),
        scratch_shapes=[pltpu.VMEM((TOP_K, tt, D_MODEL), F32), pltpu.SemaphoreType.DMA(())],
        compiler_params=_cparams(("arbitrary",)),
        name="moe_combine",
    )(dest.reshape(nt, 1, tt * TOP_K), gates, h1, ys)


def _routing_tables(top_idx, rank, counts, n_wi, n_slots):
    n_tok = top_idx.shape[0]
    padded = (counts + MOE_SUB - 1) // MOE_SUB * MOE_SUB
    pad_start = jnp.cumsum(padded) - padded
    dest = pad_start[top_idx] + rank
    tok_ids = jnp.broadcast_to(jnp.arange(n_tok, dtype=I32)[:, None], (n_tok, TOP_K))
    slot_tok = jnp.zeros((n_slots,), I32).at[dest.reshape(-1)].set(
        tok_ids.reshape(-1), unique_indices=True)
    nsubs = padded // MOE_SUB
    items = (nsubs + MOE_NSUB - 1) // MOE_NSUB
    item_end = jnp.cumsum(items)
    total = item_end[-1]
    w = jnp.arange(n_wi, dtype=I32)
    w_eff = jnp.minimum(w, total - 1)
    wi_e = jnp.sum((item_end[None, :] <= w_eff[:, None]).astype(I32), axis=1)
    wi_e = jnp.minimum(wi_e, N_EXPERTS - 1)
    local = w_eff - (item_end - items)[wi_e]
    wi_row = (pad_start[wi_e] + local * MOE_TM).astype(I32)
    wi_nsub = jnp.where(w < total, jnp.clip(nsubs[wi_e] - local * MOE_NSUB, 0, MOE_NSUB), 0).astype(I32)
    tok_pos = jnp.minimum(wi_row[:, None] + jnp.arange(MOE_TM, dtype=I32)[None, :], n_slots - 1)
    wi_tok = slot_tok[tok_pos].reshape(n_wi, 1, MOE_TM)
    return dest.astype(I32), wi_e, wi_row, wi_nsub, wi_tok


def kernel(x_prompt, x_sample, cache_k, cache_v, cache_logf, state_pool, page_table,
           norm1_w, w_in, b_forget, q_norm_w, k_norm_w, w_pool, pool_scale,
           w_branch_a, w_branch_b, w_out, norm2_w, router_w, router_b,
           w_gu, b_gu, w_dn, b_dn):
    depth = w_in.shape[0]
    assert depth == 1
    bp, lp, _ = x_prompt.shape
    bs, ls, _ = x_sample.shape
    assert ls == 8 and N_HEADS == 8
    n_p, n_s = bp * lp, bs * ls
    n_all = n_p + n_s
    n_past = page_table.shape[1] * PAGE_SIZE

    wi = w_in[0]
    f_off = 3 * ATTN_WIDTH
    w_main = jnp.concatenate([wi[:, :f_off], wi[:, f_off + N_HEADS:]], axis=1).astype(BF16)
    w_f = jnp.pad(wi[:, f_off:f_off + N_HEADS], ((0, 0), (0, FORGET_LANES - N_HEADS))).astype(BF16)
    b_f = jnp.pad(b_forget[0], (0, FORGET_LANES - N_HEADS)).reshape(1, FORGET_LANES)
    n1w = norm1_w[0].reshape(1, D_MODEL)
    n2w = norm2_w[0].reshape(1, D_MODEL)
    qnw = jnp.tile(q_norm_w[0], N_HEADS).reshape(1, ATTN_WIDTH)
    knw = jnp.tile(k_norm_w[0], N_HEADS).reshape(1, ATTN_WIDTH)
    wp = w_pool[0].astype(BF16)
    ps = pool_scale[0].reshape(1, POOL_WIDTH)
    wa = w_branch_a[0].astype(BF16)
    wb = w_branch_b[0].astype(BF16)
    wo = w_out[0].astype(BF16)
    rw = jnp.pad(router_w[0], ((0, 0), (0, 128 - N_EXPERTS)))
    rb = jnp.pad(router_b[0], (0, 128 - N_EXPERTS), constant_values=NEG_BIG).reshape(1, 128)

    xp = x_prompt.reshape(n_p, D_MODEL)
    zp, lfp = _inproj(xp, n1w, w_main, w_f, b_f, qnw, knw, tm=1024)
    cp = _cumsum_rows(lfp, bl=256, seg_len=lp)
    zp3 = zp.reshape(bp, lp, Z_WIDTH)
    cp3 = cp.reshape(bp, lp, FORGET_LANES)
    cpt3 = jnp.transpose(cp3[:, :, :N_HEADS], (0, 2, 1))
    attn_p = _prompt_attention(zp3, cp3, cpt3)
    halo_p = pl.BlockSpec((1, POOL_HALO, POOL_WIDTH),
                          lambda bi, i: (bi, jnp.maximum(i * (512 // POOL_HALO) - 1, 0), ZU))
    pool_p = _pool_mix(zp3, halo_p, zp3, ZU, wp, ps, tm=512, base_pos=0, zero_first_halo=True)
    h1_p, xn_all, ti_p, gt_p = _merge(
        attn_p.reshape(n_p, ATTN_WIDTH), pool_p.reshape(n_p, POOL_WIDTH), zp, xp,
        wa, wb, wo, n2w, rw, rb, tm=256, xn_all=None, xn_row0=0, n_all=n_all)

    xs = x_sample.reshape(n_s, D_MODEL)
    zs, lfs = _inproj(xs, n1w, w_main, w_f, b_f, qnw, knw, tm=n_s)
    cs = _cumsum_rows(lfs, bl=n_s, seg_len=ls)
    zs3 = zs.reshape(bs, ls, Z_WIDTH)
    q_s = zs3[:, :, 0:ATTN_WIDTH].reshape(bs, ls, N_HEADS, HEAD_DIM)
    qbd = jnp.einsum('bthd,hg->bhtgd', q_s, jnp.eye(N_HEADS, dtype=F32))
    qbd = qbd.reshape(bs, N_HEADS * ls, ATTN_WIDTH).astype(BF16)
    cn = jnp.transpose(cs.reshape(bs, ls, FORGET_LANES)[:, :, :N_HEADS], (0, 2, 1))
    cncol = cn.reshape(bs, N_HEADS * ls, 1)
    cnkeys = jnp.broadcast_to(cn[:, :, None, :], (bs, N_HEADS, ls, ls)).reshape(bs, N_HEADS * ls, ls)
    cnkeys = jnp.pad(cnkeys, ((0, 0), (0, 0), (0, PAGE_SIZE - ls)))
    knew = jnp.pad(zs3[:, :, ATTN_WIDTH:2 * ATTN_WIDTH], ((0, 0), (0, PAGE_SIZE - ls), (0, 0)))
    vnew = jnp.pad(zs3[:, :, 2 * ATTN_WIDTH:3 * ATTN_WIDTH], ((0, 0), (0, PAGE_SIZE - ls), (0, 0)))
    n_phys = cache_k.shape[1]
    ck = cache_k[0].reshape(n_phys, PAGE_SIZE * N_HEADS, HEAD_DIM)
    cv = cache_v[0].reshape(n_phys, PAGE_SIZE * N_HEADS, HEAD_DIM)
    clf_t = jnp.transpose(cache_logf[0], (0, 2, 1))
    attn_s = _sample_attention(page_table, qbd, cncol, cnkeys, knew, vnew, ck, cv, clf_t)
    halo_src = jnp.pad(state_pool[0], ((0, 0), (POOL_HALO - POOL_HIST, 0), (0, 0)))
    halo_s = pl.BlockSpec((1, POOL_HALO, POOL_WIDTH), lambda bi, i: (bi, 0, 0))
    pool_s = _pool_mix(halo_src, halo_s, zs3, ZU, wp, ps, tm=ls, base_pos=n_past, zero_first_halo=False)
    h1_s, xn_all, ti_s, gt_s = _merge(
        attn_s.reshape(n_s, ATTN_WIDTH), pool_s.reshape(n_s, POOL_WIDTH), zs, xs,
        wa, wb, wo, n2w, rw, rb, tm=n_s, xn_all=xn_all, xn_row0=n_p, n_all=n_all)

    n_rank = pl.cdiv(n_all, 128) * 128
    ti_all = jnp.concatenate([ti_p, ti_s, jnp.full((n_rank - n_all, 128), -1, I32)], axis=0)
    rank, cnt = _expert_rank(ti_all)
    top_idx = ti_all[:n_all, :TOP_K]
    counts = cnt[0, :N_EXPERTS].astype(I32)
    n_blocks = pl.cdiv(n_all * TOP_K, MOE_SUB) + N_EXPERTS
    n_slots = n_blocks * MOE_SUB
    n_wi = n_blocks // MOE_NSUB + N_EXPERTS
    dest, wi_e, wi_row, wi_nsub, wi_tok = _routing_tables(
        top_idx, rank[:n_all, :TOP_K], counts, n_wi, n_slots)
    ys = _moe_experts(wi_e, wi_row, wi_nsub, wi_tok, xn_all, w_gu[0], b_gu[0], w_dn[0], b_dn[0], n_slots)
    y_p = _combine(dest[:n_p], gt_p, h1_p, ys, tt=128)
    y_s = _combine(dest[n_p:], gt_s, h1_s, ys, tt=n_s)

    heads = (N_HEADS, HEAD_DIM)
    k_p = zp3[:, :, ATTN_WIDTH:2 * ATTN_WIDTH].reshape(1, bp, lp, *heads)
    v_p = zp3[:, :, 2 * ATTN_WIDTH:3 * ATTN_WIDTH].reshape(1, bp, lp, *heads)
    f_p = lfp.reshape(bp, lp, FORGET_LANES)[None, :, :, :N_HEADS]
    u_p = zp3[:, lp - POOL_HIST:, 3 * ATTN_WIDTH:3 * ATTN_WIDTH + POOL_WIDTH][None]
    k_s = zs3[:, :, ATTN_WIDTH:2 * ATTN_WIDTH].reshape(1, bs, ls, *heads)
    v_s = zs3[:, :, 2 * ATTN_WIDTH:3 * ATTN_WIDTH].reshape(1, bs, ls, *heads)
    f_s = lfs.reshape(bs, ls, FORGET_LANES)[None, :, :, :N_HEADS]
    u_s = zs3[:, :, 3 * ATTN_WIDTH:3 * ATTN_WIDTH + POOL_WIDTH]
    pool_state = jnp.concatenate([state_pool[0], u_s], axis=1)[:, -POOL_HIST:][None]
    return (y_p.reshape(bp, lp, D_MODEL), y_s.reshape(bs, ls, D_MODEL),
            k_p, v_p, f_p, u_p, k_s, v_s, f_s, pool_state)
```

```python
import functools

import jax
import jax.numpy as jnp
from jax import lax
from jax.experimental import pallas as pl
from jax.experimental.pallas import tpu as pltpu

F32 = jnp.float32
BF16 = jnp.bfloat16
I32 = jnp.int32

D_MODEL = 2048
N_HEADS = 8
HEAD_DIM = 128
ATTN_WIDTH = N_HEADS * HEAD_DIM
ATTN_SCALE = HEAD_DIM ** -0.5
FORGET_LANES = 128
POOL_WINDOWS = (2, 4, 8, 16)
POOL_WIDTH = D_MODEL // 2
GW = POOL_WIDTH // len(POOL_WINDOWS)
POOL_HIST = max(POOL_WINDOWS) - 1
POOL_HALO = 16
N_EXPERTS = 32
TOP_K = 4
D_EXPERT = D_MODEL
SWIGLU_LIMIT = 7.0
SWIGLU_ALPHA = 1.702
NORM_EPS = 1e-6
PAGE_SIZE = 128
NEG_BIG = -1e30

ZQ, ZK, ZV, ZU = 0, 1, 2, 3
Z_WIDTH = 3 * ATTN_WIDTH + POOL_WIDTH + 2 * D_MODEL

MOE_SUB = 128
MOE_NSUB = 9
MOE_TM = MOE_SUB * MOE_NSUB
MOE_TN = 512
VMEM_LIMIT = 56 * 1024 * 1024

HIGHEST = lax.Precision.HIGHEST


def _cparams(sem, vmem=VMEM_LIMIT, flags=None):
    return pltpu.CompilerParams(dimension_semantics=sem, vmem_limit_bytes=vmem, flags=flags)


def _log_sigmoid(x):
    return jnp.minimum(x, 0.0) - jnp.log1p(jnp.exp(-jnp.abs(x)))


_NT = (((1,), (1,)), ((), ()))


def _inproj_kernel(x_ref, n1w_ref, w_ref, wf_ref, bf_ref, qnw_ref, knw_ref,
                   q_ref, k_ref, v_ref, u_ref, g_ref, lf_ref, xn_ref, *, q_scale):
    j = pl.program_id(1)

    @pl.when(j == 0)
    def _():
        x = x_ref[...]
        ms = jnp.mean(x * x, axis=-1, keepdims=True)
        xn = x * lax.rsqrt(ms + NORM_EPS) * n1w_ref[...]
        xn_ref[...] = xn.astype(BF16)
        zf = lax.dot_general(xn_ref[...], wf_ref[...], _NT, preferred_element_type=F32) + bf_ref[...]
        lf_ref[...] = _log_sigmoid(zf)

    z = lax.dot_general(xn_ref[...], w_ref[...], _NT, preferred_element_type=F32)

    def head_norm(nw_ref, dst_ref, scale):
        for h in range(N_HEADS):
            sl = slice(h * HEAD_DIM, (h + 1) * HEAD_DIM)
            zh = z[:, sl]
            ms = jnp.mean(zh * zh, axis=-1, keepdims=True)
            y = zh * lax.rsqrt(ms + NORM_EPS) * nw_ref[:, sl]
            if scale != 1.0:
                y = y * scale
            dst_ref[:, sl] = y.astype(dst_ref.dtype)

    @pl.when(j == ZQ)
    def _():
        head_norm(qnw_ref, q_ref, q_scale)

    @pl.when(j == ZK)
    def _():
        head_norm(knw_ref, k_ref, 1.0)

    @pl.when(j == ZV)
    def _():
        v_ref[...] = z

    @pl.when(j == ZU)
    def _():
        u_ref[...] = z

    @pl.when(j > ZU)
    def _():
        g_ref[...] = jax.nn.sigmoid(z)


def _inproj(x, n1w, w_main_t, w_f_t, b_f, qnw, knw, tm, q_scale):
    m = x.shape[0]
    tn = ATTN_WIDTH
    row = lambda i, j: (i, 0)
    const = lambda i, j: (0, 0)
    return pl.pallas_call(
        functools.partial(_inproj_kernel, q_scale=q_scale),
        grid=(m // tm, Z_WIDTH // tn),
        in_specs=[
            pl.BlockSpec((tm, D_MODEL), row),
            pl.BlockSpec((1, D_MODEL), const),
            pl.BlockSpec((tn, D_MODEL), lambda i, j: (j, 0)),
            pl.BlockSpec((FORGET_LANES, D_MODEL), const),
            pl.BlockSpec((1, FORGET_LANES), const),
            pl.BlockSpec((1, tn), const),
            pl.BlockSpec((1, tn), const),
        ],
        out_specs=[
            pl.BlockSpec((tm, tn), row),
            pl.BlockSpec((tm, tn), row),
            pl.BlockSpec((tm, tn), row),
            pl.BlockSpec((tm, tn), row),
            pl.BlockSpec((tm, tn), lambda i, j: (i, jnp.maximum(j - (ZU + 1), 0))),
            pl.BlockSpec((tm, FORGET_LANES), row),
        ],
        out_shape=[
            jax.ShapeDtypeStruct((m, ATTN_WIDTH), BF16),
            jax.ShapeDtypeStruct((m, ATTN_WIDTH), F32),
            jax.ShapeDtypeStruct((m, ATTN_WIDTH), F32),
            jax.ShapeDtypeStruct((m, POOL_WIDTH), F32),
            jax.ShapeDtypeStruct((m, 2 * D_MODEL), F32),
            jax.ShapeDtypeStruct((m, FORGET_LANES), F32),
        ],
        scratch_shapes=[pltpu.VMEM((tm, D_MODEL), BF16)],
        compiler_params=_cparams(("arbitrary", "arbitrary")),
        name="inproj",
    )(x, n1w, w_main_t, w_f_t, b_f, qnw, knw)


def _cumsum_kernel(lf_ref, c_ref, carry_ref, *, bl, seg_len):
    i = pl.program_id(0)
    r = lax.broadcasted_iota(I32, (bl, bl), 0)
    c = lax.broadcasted_iota(I32, (bl, bl), 1)
    mask = c <= r
    if seg_len < bl:
        mask = mask & ((r // seg_len) == (c // seg_len))
    tri = mask.astype(F32)
    cs = jnp.dot(tri, lf_ref[...], preferred_element_type=F32, precision=HIGHEST)
    if seg_len > bl:
        @pl.when((i * bl) % seg_len == 0)
        def _():
            carry_ref[...] = jnp.zeros_like(carry_ref)
        cs = cs + carry_ref[...]
        carry_ref[...] = cs[bl - 1:bl, :]
    c_ref[...] = cs


def _cumsum_rows(lf, bl, seg_len):
    m = lf.shape[0]
    return pl.pallas_call(
        functools.partial(_cumsum_kernel, bl=bl, seg_len=seg_len),
        grid=(m // bl,),
        in_specs=[pl.BlockSpec((bl, FORGET_LANES), lambda i: (i, 0))],
        out_specs=pl.BlockSpec((bl, FORGET_LANES), lambda i: (i, 0)),
        out_shape=jax.ShapeDtypeStruct((m, FORGET_LANES), F32),
        scratch_shapes=[pltpu.VMEM((1, FORGET_LANES), F32)],
        compiler_params=_cparams(("arbitrary",)),
        name="logf_cumsum",
    )(lf)


LOG2E = 1.4426950408889634


def _pattn_kernel(q_ref, k_ref, v_ref, c_ref, kbias_ref, o_ref, *scratch, tq, ch):
    m_refs = scratch[0:N_HEADS]
    accl_refs = scratch[N_HEADS:2 * N_HEADS]
    kb_ref, vb_ref = scratch[2 * N_HEADS:]
    qi = pl.program_id(1)
    ki = pl.program_id(2)
    tk = tq

    @pl.when(ki == 0)
    def _():
        for h in range(N_HEADS):
            m_refs[h][...] = jnp.full_like(m_refs[h], -jnp.inf)
            accl_refs[h][...] = jnp.zeros_like(accl_refs[h])

    @pl.when(ki <= qi)
    def _():
        kb_ref[...] = k_ref[0].astype(BF16)
        vb_ref[...] = v_ref[0].astype(BF16)

    def block(diag):
        c2 = c_ref[0] * LOG2E
        q_ones = (lax.broadcasted_iota(I32, (ch, HEAD_DIM), 1) < 3).astype(BF16)
        v_ones = jnp.ones((tk, HEAD_DIM), BF16)
        below = (lax.broadcasted_iota(I32, (ch, ch), 1) <= lax.broadcasted_iota(I32, (ch, ch), 0))

        def where(h, r):
            rows = slice(r * ch, (r + 1) * ch)
            cols = slice(0, (r + 1) * ch if diag else tk)
            return rows, cols, slice(h * HEAD_DIM, (h + 1) * HEAD_DIM)

        def stage_scores(h, r):
            rows, cols, sl = where(h, r)
            q_cat = jnp.concatenate([q_ref[0, rows, sl], q_ones], axis=1)
            k_cat = jnp.concatenate([kb_ref[cols, sl], kbias_ref[0, h, cols, :]], axis=1)
            u = lax.dot_general(q_cat, k_cat, _NT, preferred_element_type=F32)
            if diag:
                last = jnp.where(below, u[:, r * ch:], -jnp.inf)
                u = last if r == 0 else jnp.concatenate([u[:, :r * ch], last], axis=1)
            return u

        def stage_max(h, r, u):
            rows, _, _ = where(h, r)
            c2t = c2[rows, h:h + 1]
            m_prev = m_refs[h][rows, :]
            m_new = jnp.maximum(m_prev, jnp.max(u, axis=-1, keepdims=True) + c2t)
            m_refs[h][rows, :] = m_new
            return jnp.exp2(m_prev - m_new), c2t - m_new[:, 0:1]

        def stage_accumulate(h, r, p, alpha):
            rows, cols, sl = where(h, r)
            v_cat = jnp.concatenate([vb_ref[cols, sl], v_ones[cols, :]], axis=1)
            pv = jnp.dot(p, v_cat, preferred_element_type=F32)
            accl_refs[h][rows, :] = jnp.concatenate([alpha, alpha], axis=1) * accl_refs[h][rows, :] + pv

        units = [(h, r) for r in range(tq // ch) for h in range(N_HEADS)]
        u_of, ab_of, p_of = {}, {}, {}
        for t in range(len(units) + 3):
            if t < len(units):
                u_of[t] = stage_scores(*units[t])
            if 0 <= t - 1 < len(units):
                ab_of[t - 1] = stage_max(*units[t - 1], u_of[t - 1])
            if 0 <= t - 2 < len(units):
                p_of[t - 2] = jnp.exp2(u_of.pop(t - 2) + ab_of[t - 2][1]).astype(BF16)
            if 0 <= t - 3 < len(units):
                stage_accumulate(*units[t - 3], p_of.pop(t - 3), ab_of.pop(t - 3)[0])

    @pl.when(ki < qi)
    def _():
        block(False)

    @pl.when(ki == qi)
    def _():
        block(True)
        for h in range(N_HEADS):
            sl = slice(h * HEAD_DIM, (h + 1) * HEAD_DIM)
            accl = accl_refs[h][...]
            o_ref[0, :, sl] = (accl[:, :HEAD_DIM] / accl[:, HEAD_DIM:]).astype(o_ref.dtype)


def _prompt_attention(q3, k3, v3, c3, kbias, tq=512, ch=128):
    b, l, _ = q3.shape
    tk = tq
    nq = l // tq
    kv_idx = lambda bi, qi, ki: (bi, jnp.minimum(ki, qi), 0)
    return pl.pallas_call(
        functools.partial(_pattn_kernel, tq=tq, ch=ch),
        grid=(b, nq, nq),
        in_specs=[
            pl.BlockSpec((1, tq, ATTN_WIDTH), lambda bi, qi, ki: (bi, qi, 0)),
            pl.BlockSpec((1, tk, ATTN_WIDTH), kv_idx),
            pl.BlockSpec((1, tk, ATTN_WIDTH), kv_idx),
            pl.BlockSpec((1, tq, FORGET_LANES), lambda bi, qi, ki: (bi, qi, 0)),
            pl.BlockSpec((1, N_HEADS, tk, HEAD_DIM),
                         lambda bi, qi, ki: (bi, 0, jnp.minimum(ki, qi), 0)),
        ],
        out_specs=pl.BlockSpec((1, tq, ATTN_WIDTH), lambda bi, qi, ki: (bi, qi, 0)),
        out_shape=jax.ShapeDtypeStruct((b, l, ATTN_WIDTH), BF16),
        scratch_shapes=(
            [pltpu.VMEM((tq, HEAD_DIM), F32) for _ in range(N_HEADS)]
            + [pltpu.VMEM((tq, 2 * HEAD_DIM), F32) for _ in range(N_HEADS)]
            + [pltpu.VMEM((tk, ATTN_WIDTH), BF16), pltpu.VMEM((tk, ATTN_WIDTH), BF16)]),
        compiler_params=_cparams(("arbitrary", "arbitrary", "arbitrary")),
        name="prompt_attention",
    )(q3, k3, v3, c3, kbias)


def _bf16_pieces(x, n):
    out = []
    for _ in range(n):
        piece = x.astype(BF16).astype(F32)
        out.append(piece)
        x = x - piece
    return out


def _kbias_kernel(c_ref, o_ref):
    c = c_ref[0] * (-LOG2E)
    lane = lax.broadcasted_iota(I32, c.shape, 1)
    for h in range(N_HEADS):
        hi, mid, lo = _bf16_pieces(c[:, h:h + 1], 3)
        tile = jnp.where(lane == 0, hi, jnp.where(lane == 1, mid, jnp.where(lane == 2, lo, 0.0)))
        o_ref[0, h] = tile.astype(BF16)


def _key_bias_pieces(c3, tl=512):
    b, l, _ = c3.shape
    return pl.pallas_call(
        _kbias_kernel,
        grid=(b, l // tl),
        in_specs=[pl.BlockSpec((1, tl, FORGET_LANES), lambda bi, i: (bi, i, 0))],
        out_specs=pl.BlockSpec((1, N_HEADS, tl, HEAD_DIM), lambda bi, i: (bi, 0, i, 0)),
        out_shape=jax.ShapeDtypeStruct((b, N_HEADS, l, HEAD_DIM), BF16),
        compiler_params=_cparams(("arbitrary", "arbitrary")),
        name="key_bias",
    )(c3)


def _heads_to_lanes(ref):
    parts = [ref[0, pl.ds(h, PAGE_SIZE, stride=N_HEADS), :] for h in range(N_HEADS)]
    return jnp.concatenate(parts, axis=1).astype(BF16)


def _rows_per_head(x):
    return jnp.concatenate(
        [jnp.broadcast_to(x[h:h + 1, :], (8, x.shape[1])) for h in range(N_HEADS)], axis=0)


def _sattn_kernel(pt_ref, qbd_ref, cncol_ref, cnkeys_ref, knew_ref, vnew_ref, *rest, g_pages):
    del pt_ref
    k_refs = rest[0:g_pages]
    v_refs = rest[g_pages:2 * g_pages]
    lf_refs = rest[2 * g_pages:3 * g_pages]
    o_ref = rest[3 * g_pages]
    m_ref, l_ref, acc_ref, carry_ref = rest[3 * g_pages + 1:]
    jj = pl.program_id(1)
    nj = pl.num_programs(1)
    rows = N_HEADS * 8

    @pl.when(jj == 0)
    def _():
        m_ref[...] = jnp.full_like(m_ref, -jnp.inf)
        l_ref[...] = jnp.zeros_like(l_ref)
        acc_ref[...] = jnp.zeros_like(acc_ref)
        carry_ref[...] = jnp.zeros_like(carry_ref)

    qbd = qbd_ref[0]
    cncol = cncol_ref[0]
    later = (lax.broadcasted_iota(I32, (PAGE_SIZE, PAGE_SIZE), 0)
             > lax.broadcasted_iota(I32, (PAGE_SIZE, PAGE_SIZE), 1)).astype(F32)

    def scores(kcat):
        return lax.dot_general(qbd, kcat, (((1,), (1,)), ((), ())),
                               preferred_element_type=F32) * ATTN_SCALE

    def update(s, vcat, m, l, acc):
        m_new = jnp.maximum(m, jnp.max(s, axis=-1, keepdims=True))
        alpha = jnp.exp(m - m_new)
        p = jnp.exp(s - jnp.concatenate([m_new] * (s.shape[1] // PAGE_SIZE), axis=1))
        l = alpha * l + jnp.sum(p, axis=-1, keepdims=True)
        pv = jnp.dot(p.astype(BF16), vcat, preferred_element_type=F32)
        acc = jnp.concatenate([alpha] * N_HEADS, axis=1) * acc + pv
        return m_new, l, acc

    m, l, acc, carry = m_ref[...], l_ref[...], acc_ref[...], carry_ref[...]
    lf_all = jnp.concatenate([_rows_per_head(lf_refs[g][0]) for g in range(g_pages)], axis=0)
    suffix_all = jnp.dot(lf_all, later, preferred_element_type=F32, precision=HIGHEST)
    bias = []
    for g in range(g_pages):
        bias.append(suffix_all[g * rows:(g + 1) * rows, :] + carry)
        carry = carry + jnp.sum(lf_all[g * rows:(g + 1) * rows, :], axis=-1, keepdims=True)
    kcat = jnp.concatenate([_heads_to_lanes(k_refs[g]) for g in range(g_pages)], axis=0)
    vcat = jnp.concatenate([_heads_to_lanes(v_refs[g]) for g in range(g_pages)], axis=0)
    s = scores(kcat) + cncol + jnp.concatenate(bias, axis=1)
    m, l, acc = update(s, vcat, m, l, acc)

    @pl.when(jj < nj - 1)
    def _():
        m_ref[...] = m
        l_ref[...] = l
        acc_ref[...] = acc
        carry_ref[...] = carry

    @pl.when(jj == nj - 1)
    def _():
        s = scores(knew_ref[0].astype(BF16)) + cncol - cnkeys_ref[0]
        t_row = lax.broadcasted_iota(I32, (rows, PAGE_SIZE), 0) % 8
        s_col = lax.broadcasted_iota(I32, (rows, PAGE_SIZE), 1)
        s = jnp.where(s_col <= t_row, s, -jnp.inf)
        _, l2, acc2 = update(s, vnew_ref[0].astype(BF16), m, l, acc)
        o = acc2 / jnp.concatenate([l2] * N_HEADS, axis=1)
        for h in range(N_HEADS):
            sl = slice(h * HEAD_DIM, (h + 1) * HEAD_DIM)
            o_ref[0, :, sl] = o[h * 8:(h + 1) * 8, sl].astype(o_ref.dtype)


def _sample_attention(page_table, qbd, cncol, cnkeys, knew, vnew, ck, cv, clf_t, g_pages=8):
    bd, n_pages = page_table.shape
    rows = N_HEADS * 8
    pt_flat = page_table.reshape(-1)

    def page_idx(g):
        return lambda b, jj, pt: (pt[b * n_pages + n_pages - 1 - (jj * g_pages + g)], 0, 0)

    per_b = lambda b, jj, pt: (b, 0, 0)
    in_specs = [
        pl.BlockSpec((1, rows, ATTN_WIDTH), per_b),
        pl.BlockSpec((1, rows, 1), per_b),
        pl.BlockSpec((1, rows, PAGE_SIZE), per_b),
        pl.BlockSpec((1, PAGE_SIZE, ATTN_WIDTH), per_b),
        pl.BlockSpec((1, PAGE_SIZE, ATTN_WIDTH), per_b),
    ]
    in_specs += [pl.BlockSpec((1, PAGE_SIZE * N_HEADS, HEAD_DIM), page_idx(g)) for g in range(g_pages)]
    in_specs += [pl.BlockSpec((1, PAGE_SIZE * N_HEADS, HEAD_DIM), page_idx(g)) for g in range(g_pages)]
    in_specs += [pl.BlockSpec((1, N_HEADS, PAGE_SIZE), page_idx(g)) for g in range(g_pages)]
    grid_spec = pltpu.PrefetchScalarGridSpec(
        num_scalar_prefetch=1,
        grid=(bd, n_pages // g_pages),
        in_specs=in_specs,
        out_specs=pl.BlockSpec((1, 8, ATTN_WIDTH), per_b),
        scratch_shapes=[
            pltpu.VMEM((rows, PAGE_SIZE), F32),
            pltpu.VMEM((rows, PAGE_SIZE), F32),
            pltpu.VMEM((rows, ATTN_WIDTH), F32),
            pltpu.VMEM((rows, PAGE_SIZE), F32),
        ],
    )
    return pl.pallas_call(
        functools.partial(_sattn_kernel, g_pages=g_pages),
        grid_spec=grid_spec,
        out_shape=jax.ShapeDtypeStruct((bd, 8, ATTN_WIDTH), BF16),
        compiler_params=_cparams(("arbitrary", "arbitrary")),
        name="sample_attention",
    )(pt_flat, qbd, cncol, cnkeys, knew, vnew,
      *([ck] * g_pages), *([cv] * g_pages), *([clf_t] * g_pages))


def _pool_kernel(halo_ref, u_ref, wp_ref, ps_ref, o_ref, *, tm, base_pos, zero_first_halo):
    i = pl.program_id(1)
    halo = halo_ref[0]
    if zero_first_halo:
        halo = jnp.where(i == 0, 0.0, halo)
    u = u_ref[0]
    ext = jnp.concatenate([halo, u], axis=0)
    pos = base_pos + i * tm + lax.broadcasted_iota(I32, (tm, 1), 0)
    for g, w in enumerate(POOL_WINDOWS):
        sl = slice(g * GW, (g + 1) * GW)
        s = ext[:, sl]
        k = 1
        while k < w:
            s = s + pltpu.roll(s, shift=k, axis=0)
            k *= 2
        wsum = s[POOL_HALO:, :]
        count = jnp.minimum(pos + 1, w).astype(F32)
        mixed = wsum / count - u[:, sl]
        y = jnp.dot(mixed.astype(BF16), wp_ref[g], preferred_element_type=F32)
        o_ref[0, :, sl] = (y * ps_ref[:, sl]).astype(o_ref.dtype)


def _pool_mix(halo_src, halo_spec, u_src, wp, ps, tm, base_pos, zero_first_halo):
    b, l, _ = u_src.shape
    return pl.pallas_call(
        functools.partial(_pool_kernel, tm=tm, base_pos=base_pos, zero_first_halo=zero_first_halo),
        grid=(b, l // tm),
        in_specs=[
            halo_spec,
            pl.BlockSpec((1, tm, POOL_WIDTH), lambda bi, i: (bi, i, 0)),
            pl.BlockSpec((len(POOL_WINDOWS), GW, GW), lambda bi, i: (0, 0, 0)),
            pl.BlockSpec((1, POOL_WIDTH), lambda bi, i: (0, 0)),
        ],
        out_specs=pl.BlockSpec((1, tm, POOL_WIDTH), lambda bi, i: (bi, i, 0)),
        out_shape=jax.ShapeDtypeStruct((b, l, POOL_WIDTH), BF16),
        compiler_params=_cparams(("arbitrary", "arbitrary")),
        name="pool_mix",
    )(halo_src, u_src, wp, ps)


def _merge_kernel(attn_ref, pool_ref, ga_ref, gb_ref, x_ref, wa_ref, wb_ref, wo_ref,
                  n2w_ref, rw_ref, rb_ref, *rest, tm, aliased):
    h_ref, xn_ref, ti_ref, gt_ref, rwp_ref = rest[-5:]

    @pl.when(pl.program_id(0) == 0)
    def _():
        for n, piece in enumerate(_bf16_pieces(rw_ref[...], 3)):
            rwp_ref[n] = piece.astype(BF16)

    a = jnp.dot(attn_ref[...], wa_ref[...], preferred_element_type=F32)
    p = jnp.dot(pool_ref[...], wb_ref[...], preferred_element_type=F32)
    merged = ga_ref[...] * a + gb_ref[...] * p
    o = jnp.dot(merged.astype(BF16), wo_ref[...], preferred_element_type=F32)
    h1 = x_ref[...] + o
    h_ref[...] = h1
    ms = jnp.mean(h1 * h1, axis=-1, keepdims=True)
    xn = h1 * lax.rsqrt(ms + NORM_EPS) * n2w_ref[...]
    xn_ref[...] = xn
    x_hi, x_lo = [v.astype(BF16) for v in _bf16_pieces(xn, 2)]
    logits = rb_ref[...]
    for xp, wn in ((x_hi, 0), (x_hi, 1), (x_lo, 0), (x_hi, 2), (x_lo, 1)):
        logits = logits + jnp.dot(xp, rwp_ref[wn], preferred_element_type=F32)
    lane = lax.broadcasted_iota(I32, (tm, 128), 1).astype(F32)
    vals, idxs = [], []
    work = logits
    for _ in range(TOP_K):
        mv = jnp.max(work, axis=-1, keepdims=True)
        ix = jnp.min(jnp.where(work == mv, lane, 128.0), axis=-1, keepdims=True)
        vals.append(mv)
        idxs.append(ix)
        work = jnp.where(lane == ix, NEG_BIG, work)
    es = [jnp.exp(v - vals[0]) for v in vals]
    den = es[0] + es[1] + es[2] + es[3]
    ti = jnp.zeros((tm, 128), F32)
    gt = jnp.zeros((tm, 128), F32)
    for k in range(TOP_K):
        ti = jnp.where(lane == float(k), idxs[k], ti)
        gt = jnp.where(lane == float(k), es[k] / den, gt)
    ti_ref[...] = ti.astype(I32)
    gt_ref[...] = gt


def _merge(attn, pool, z, x, wa, wb, wo, n2w, rw, rb, tm, xn_all, xn_row0, n_all):
    m = x.shape[0]
    aliased = xn_all is not None
    const = lambda i: (0, 0)
    single = dict(pipeline_mode=pl.Buffered(1))
    xn_blk0 = xn_row0 // tm
    in_specs = [
        pl.BlockSpec((tm, ATTN_WIDTH), lambda i: (i, 0)),
        pl.BlockSpec((tm, POOL_WIDTH), lambda i: (i, 0)),
        pl.BlockSpec((tm, D_MODEL), lambda i: (i, 0)),
        pl.BlockSpec((tm, D_MODEL), lambda i: (i, 1)),
        pl.BlockSpec((tm, D_MODEL), lambda i: (i, 0)),
        pl.BlockSpec((ATTN_WIDTH, D_MODEL), const, **single),
        pl.BlockSpec((POOL_WIDTH, D_MODEL), const, **single),
        pl.BlockSpec((D_MODEL, D_MODEL), const, **single),
        pl.BlockSpec((1, D_MODEL), const),
        pl.BlockSpec((D_MODEL, 128), const),
        pl.BlockSpec((1, 128), const),
    ]
    args = [attn, pool, z, z, x, wa, wb, wo, n2w, rw, rb]
    io_alias = {}
    if aliased:
        in_specs.append(pl.BlockSpec(memory_space=pl.ANY))
        args.append(xn_all)
        io_alias = {len(args) - 1: 1}
    return pl.pallas_call(
        functools.partial(_merge_kernel, tm=tm, aliased=aliased),
        grid=(m // tm,),
        in_specs=in_specs,
        out_specs=[
            pl.BlockSpec((tm, D_MODEL), lambda i: (i, 0)),
            pl.BlockSpec((tm, D_MODEL), lambda i: (xn_blk0 + i, 0)),
            pl.BlockSpec((tm, 128), lambda i: (i, 0)),
            pl.BlockSpec((tm, 128), lambda i: (i, 0)),
        ],
        out_shape=[
            jax.ShapeDtypeStruct((m, D_MODEL), F32),
            jax.ShapeDtypeStruct((n_all, D_MODEL), F32),
            jax.ShapeDtypeStruct((m, 128), I32),
            jax.ShapeDtypeStruct((m, 128), F32),
        ],
        input_output_aliases=io_alias,
        scratch_shapes=[pltpu.VMEM((3, D_MODEL, 128), BF16)],
        compiler_params=_cparams(("arbitrary",)),
        name="merge_router",
    )(*args)


def _rank_kernel(idx_ref, rank_ref, cnt_ref, carry_ref, *, tr):
    i = pl.program_id(0)

    @pl.when(i == 0)
    def _():
        carry_ref[...] = jnp.zeros_like(carry_ref)

    idx = idx_ref[...]
    lane = lax.broadcasted_iota(I32, (tr, 128), 1)
    hits = [idx[:, k:k + 1] == lane for k in range(TOP_K)]
    onehot = jnp.zeros((tr, 128), F32)
    for hk in hits:
        onehot = onehot + hk.astype(F32)
    earlier = (lax.broadcasted_iota(I32, (tr, tr), 1)
               < lax.broadcasted_iota(I32, (tr, tr), 0)).astype(BF16)
    before = jnp.dot(earlier, onehot.astype(BF16), preferred_element_type=F32) + carry_ref[...]
    out = jnp.zeros((tr, 128), I32)
    for k, hk in enumerate(hits):
        rk = jnp.sum(jnp.where(hk, before, 0.0), axis=-1, keepdims=True)
        out = jnp.where(lane == k, rk.astype(I32), out)
    rank_ref[...] = out
    carry_ref[...] = carry_ref[...] + jnp.sum(onehot, axis=0, keepdims=True)
    cnt_ref[...] = carry_ref[...]


def _expert_rank(top_idx_padded, tr=128):
    n = top_idx_padded.shape[0]
    return pl.pallas_call(
        functools.partial(_rank_kernel, tr=tr),
        grid=(n // tr,),
        in_specs=[pl.BlockSpec((tr, 128), lambda i: (i, 0))],
        out_specs=[
            pl.BlockSpec((tr, 128), lambda i: (i, 0)),
            pl.BlockSpec((1, 128), lambda i: (0, 0)),
        ],
        out_shape=[
            jax.ShapeDtypeStruct((n, 128), I32),
            jax.ShapeDtypeStruct((1, 128), F32),
        ],
        scratch_shapes=[pltpu.VMEM((1, 128), F32)],
        compiler_params=_cparams(("arbitrary",)),
        name="expert_rank",
    )(top_idx_padded)


def _moe_row_block_copy(src, dst, sub, sem):
    return pltpu.make_async_copy(src.at[pl.ds(0, MOE_SUB)], dst.at[pl.ds(sub * MOE_SUB, MOE_SUB)], sem)


def _moe_out_block_copy(acc_scr, ys_hbm, sub, row, sem):
    return pltpu.make_async_copy(acc_scr.at[pl.ds(sub * MOE_SUB, MOE_SUB)],
                                 ys_hbm.at[pl.ds(row, MOE_SUB)], sem)


def _moe_kernel(wi_e_ref, wi_row_ref, wi_nsub_ref, tok_ref, xn_hbm,
                wg_ref, wu_ref, bg_ref, bu_ref, wd_ref, bd_ref, ys_hbm,
                x_scr, acc_scr, wg_bf, wu_bf, wd_bf, pend_ref, sem_in, sem_out):
    del wi_e_ref
    w = pl.program_id(0)
    j = pl.program_id(1)
    nj = pl.num_programs(1)
    nsub = wi_nsub_ref[w]
    row0 = wi_row_ref[w]
    issue_unroll = 8

    @pl.when((w == 0) & (j == 0))
    def _():
        pend_ref[0] = 0

    @pl.when((j == 0) & (nsub > 0))
    def _():
        def issue(blk, carry):
            for i in range(issue_unroll):
                r = blk * issue_unroll + i
                tok = tok_ref[0, 0, r]
                pltpu.make_async_copy(xn_hbm.at[pl.ds(tok, 1)], x_scr.at[pl.ds(r, 1)], sem_in).start()
            return carry
        lax.fori_loop(0, nsub * (MOE_SUB // issue_unroll), issue, 0)

    def drain_output():
        for s in range(MOE_NSUB):
            @pl.when(s < pend_ref[0])
            def _():
                _moe_out_block_copy(acc_scr, ys_hbm, s, 0, sem_out).wait()
        pend_ref[0] = 0

    @pl.when(j == 0)
    def _():
        drain_output()

    @pl.when((j == 0) & (nsub > 0))
    def _():
        for s in range(MOE_NSUB):
            @pl.when(s < nsub)
            def _():
                acc_scr[s * MOE_SUB:(s + 1) * MOE_SUB, :] = jnp.broadcast_to(
                    bd_ref[0], (MOE_SUB, D_MODEL))
        for s in range(MOE_NSUB):
            @pl.when(s < nsub)
            def _():
                _moe_row_block_copy(xn_hbm, x_scr, s, sem_in).wait()

    def expert_rows(rows):
        xs = x_scr[rows, :].astype(BF16)
        gate = jnp.dot(xs, wg_bf[...], preferred_element_type=F32) + bg_ref[0]
        up = jnp.dot(xs, wu_bf[...], preferred_element_type=F32) + bu_ref[0]
        gate = jnp.minimum(gate, SWIGLU_LIMIT)
        up = jnp.clip(up, -SWIGLU_LIMIT, SWIGLU_LIMIT)
        act = (up + 1.0) * (gate * jax.nn.sigmoid(SWIGLU_ALPHA * gate))
        acc_scr[rows, :] += jnp.dot(act.astype(BF16), wd_bf[...], preferred_element_type=F32)

    @pl.when(nsub > 0)
    def _():
        wg_bf[...] = wg_ref[0].astype(BF16)
        wu_bf[...] = wu_ref[0].astype(BF16)
        wd_bf[...] = wd_ref[0].astype(BF16)
        for c in range(MOE_NSUB // 4):
            @pl.when(4 * c + 4 <= nsub)
            def _():
                expert_rows(pl.ds(4 * c * MOE_SUB, 4 * MOE_SUB))

        @pl.when(nsub % 4 >= 2)
        def _():
            expert_rows(pl.ds(pl.multiple_of((nsub // 4) * 4 * MOE_SUB, MOE_SUB), 2 * MOE_SUB))

        @pl.when(nsub % 2 == 1)
        def _():
            expert_rows(pl.ds(pl.multiple_of((nsub - 1) * MOE_SUB, MOE_SUB), MOE_SUB))

    @pl.when((j == nj - 1) & (nsub > 0))
    def _():
        for s in range(MOE_NSUB):
            @pl.when(s < nsub)
            def _():
                _moe_out_block_copy(acc_scr, ys_hbm, s,
                                    pl.multiple_of(row0 + s * MOE_SUB, MOE_SUB), sem_out).start()
        pend_ref[0] = nsub

    @pl.when((w == pl.num_programs(0) - 1) & (j == nj - 1))
    def _():
        drain_output()


def _moe_experts(wi_e, wi_row, wi_nsub, wi_tok, xn_all, w_gu, b_gu, w_dn, b_dn, n_slots):
    n_wi = wi_e.shape[0]
    nj = D_EXPERT // MOE_TN
    def col(w, j, n):
        return jnp.where(n[w] > 0, j, nj - 1)

    grid_spec = pltpu.PrefetchScalarGridSpec(
        num_scalar_prefetch=3,
        grid=(n_wi, nj),
        in_specs=[
            pl.BlockSpec((1, 1, MOE_TM), lambda w, j, e, r, n: (w, 0, 0), memory_space=pltpu.SMEM),
            pl.BlockSpec(memory_space=pl.ANY),
            pl.BlockSpec((1, D_MODEL, MOE_TN), lambda w, j, e, r, n: (e[w], 0, col(w, j, n))),
            pl.BlockSpec((1, D_MODEL, MOE_TN), lambda w, j, e, r, n: (e[w], 0, nj + col(w, j, n))),
            pl.BlockSpec((1, 1, MOE_TN), lambda w, j, e, r, n: (e[w], 0, col(w, j, n))),
            pl.BlockSpec((1, 1, MOE_TN), lambda w, j, e, r, n: (e[w], 0, nj + col(w, j, n))),
            pl.BlockSpec((1, MOE_TN, D_MODEL), lambda w, j, e, r, n: (e[w], col(w, j, n), 0)),
            pl.BlockSpec((1, 1, D_MODEL), lambda w, j, e, r, n: (e[w], 0, 0)),
        ],
        out_specs=pl.BlockSpec(memory_space=pl.ANY),
        scratch_shapes=[
            pltpu.VMEM((MOE_TM, D_MODEL), F32),
            pltpu.VMEM((MOE_TM, D_MODEL), F32),
            pltpu.VMEM((D_MODEL, MOE_TN), BF16),
            pltpu.VMEM((D_MODEL, MOE_TN), BF16),
            pltpu.VMEM((MOE_TN, D_MODEL), BF16),
            pltpu.SMEM((1,), I32),
            pltpu.SemaphoreType.DMA(()),
            pltpu.SemaphoreType.DMA(()),
        ],
    )
    return pl.pallas_call(
        _moe_kernel,
        grid_spec=grid_spec,
        out_shape=jax.ShapeDtypeStruct((n_slots, D_MODEL), F32),
        compiler_params=_cparams(("arbitrary", "arbitrary"), vmem=58 * 1024 * 1024),
        name="moe_experts",
    )(wi_e, wi_row, wi_nsub, wi_tok, xn_all, w_gu, w_gu,
      b_gu.reshape(N_EXPERTS, 1, 2 * D_EXPERT), b_gu.reshape(N_EXPERTS, 1, 2 * D_EXPERT),
      w_dn, b_dn.reshape(N_EXPERTS, 1, D_MODEL))


def _combine_kernel(dest_ref, gt_ref, h_ref, ys_hbm, o_ref, buf, sem, *, tt):
    def issue(r, carry):
        for k in range(TOP_K):
            slot = dest_ref[0, 0, r * TOP_K + k]
            pltpu.make_async_copy(ys_hbm.at[pl.ds(slot, 1)], buf.at[k, pl.ds(r, 1)], sem).start()
        return carry
    lax.fori_loop(0, tt, issue, 0, unroll=4)
    for k in range(TOP_K):
        pltpu.make_async_copy(ys_hbm.at[pl.ds(0, tt)], buf.at[k], sem).wait()
    gt = gt_ref[...]
    y = gt[:, 0:1] * buf[0]
    for k in range(1, TOP_K):
        y = y + gt[:, k:k + 1] * buf[k]
    o_ref[...] = h_ref[...] + y


def _combine(dest, gates, h1, ys, tt):
    m = h1.shape[0]
    nt = m // tt
    return pl.pallas_call(
        functools.partial(_combine_kernel, tt=tt),
        grid=(nt,),
        in_specs=[
            pl.BlockSpec((1, 1, tt * TOP_K), lambda i: (i, 0, 0), memory_space=pltpu.SMEM),
            pl.BlockSpec((tt, 128), lambda i: (i, 0)),
            pl.BlockSpec((tt, D_MODEL), lambda i: (i, 0)),
            pl.BlockSpec(memory_space=pl.ANY),
        ],
        out_specs=pl.BlockSpec((tt, D_MODEL), lambda i: (i, 0)),
        out_shape=jax.ShapeDtypeStruct((m, D_MODEL), F32),
        scratch_shapes=[pltpu.VMEM((TOP_K, tt, D_MODEL), F32), pltpu.SemaphoreType.DMA(())],
        compiler_params=_cparams(("arbitrary",)),
        name="moe_combine",
    )(dest.reshape(nt, 1, tt * TOP_K), gates, h1, ys)


def _routing_tables(top_idx, rank, counts, n_wi, n_slots):
    n_tok = top_idx.shape[0]
    padded = (counts + MOE_SUB - 1) // MOE_SUB * MOE_SUB
    pad_start = jnp.cumsum(padded) - padded
    dest = pad_start[top_idx] + rank
    tok_ids = jnp.broadcast_to(jnp.arange(n_tok, dtype=I32)[:, None], (n_tok, TOP_K))
    slot_tok = jnp.zeros((n_slots,), I32).at[dest.reshape(-1)].set(
        tok_ids.reshape(-1), unique_indices=True)
    nsubs = padded // MOE_SUB
    items = (nsubs + MOE_NSUB - 1) // MOE_NSUB
    item_end = jnp.cumsum(items)
    total = item_end[-1]
    w = jnp.arange(n_wi, dtype=I32)
    w_eff = jnp.minimum(w, total - 1)
    wi_e = jnp.sum((item_end[None, :] <= w_eff[:, None]).astype(I32), axis=1)
    wi_e = jnp.minimum(wi_e, N_EXPERTS - 1)
    local = w_eff - (item_end - items)[wi_e]
    wi_row = (pad_start[wi_e] + local * MOE_TM).astype(I32)
    wi_nsub = jnp.where(w < total, jnp.clip(nsubs[wi_e] - local * MOE_NSUB, 0, MOE_NSUB), 0).astype(I32)
    tok_pos = jnp.minimum(wi_row[:, None] + jnp.arange(MOE_TM, dtype=I32)[None, :], n_slots - 1)
    wi_tok = slot_tok[tok_pos].reshape(n_wi, 1, MOE_TM)
    return dest.astype(I32), wi_e, wi_row, wi_nsub, wi_tok


def kernel(x_prompt, x_sample, cache_k, cache_v, cache_logf, state_pool, page_table,
           norm1_w, w_in, b_forget, q_norm_w, k_norm_w, w_pool, pool_scale,
           w_branch_a, w_branch_b, w_out, norm2_w, router_w, router_b,
           w_gu, b_gu, w_dn, b_dn):
    depth = w_in.shape[0]
    assert depth == 1
    bp, lp, _ = x_prompt.shape
    bs, ls, _ = x_sample.shape
    assert ls == 8 and N_HEADS == 8
    n_p, n_s = bp * lp, bs * ls
    n_all = n_p + n_s
    n_past = page_table.shape[1] * PAGE_SIZE

    wt = jnp.transpose(w_in[0])
    f_off = 3 * ATTN_WIDTH
    w_main = jnp.concatenate([wt[:f_off], wt[f_off + N_HEADS:]], axis=0).astype(BF16)
    w_f = jnp.pad(wt[f_off:f_off + N_HEADS], ((0, FORGET_LANES - N_HEADS), (0, 0))).astype(BF16)
    b_f = jnp.pad(b_forget[0], (0, FORGET_LANES - N_HEADS)).reshape(1, FORGET_LANES)
    n1w = norm1_w[0].reshape(1, D_MODEL)
    n2w = norm2_w[0].reshape(1, D_MODEL)
    qnw = jnp.tile(q_norm_w[0], N_HEADS).reshape(1, ATTN_WIDTH)
    knw = jnp.tile(k_norm_w[0], N_HEADS).reshape(1, ATTN_WIDTH)
    wp = w_pool[0].astype(BF16)
    ps = pool_scale[0].reshape(1, POOL_WIDTH)
    wa = w_branch_a[0].astype(BF16)
    wb = w_branch_b[0].astype(BF16)
    wo = w_out[0].astype(BF16)
    rw = jnp.pad(router_w[0], ((0, 0), (0, 128 - N_EXPERTS)))
    rb = jnp.pad(router_b[0], (0, 128 - N_EXPERTS), constant_values=NEG_BIG).reshape(1, 128)

    xp = x_prompt.reshape(n_p, D_MODEL)
    qp, kp, vp, up, gp, lfp = _inproj(xp, n1w, w_main, w_f, b_f, qnw, knw, tm=512,
                                      q_scale=ATTN_SCALE * LOG2E)
    cp = _cumsum_rows(lfp, bl=256, seg_len=lp)
    cp3 = cp.reshape(bp, lp, FORGET_LANES)
    attn_p = _prompt_attention(qp.reshape(bp, lp, ATTN_WIDTH), kp.reshape(bp, lp, ATTN_WIDTH),
                               vp.reshape(bp, lp, ATTN_WIDTH), cp3, _key_bias_pieces(cp3))
    up3 = up.reshape(bp, lp, POOL_WIDTH)
    halo_p = pl.BlockSpec((1, POOL_HALO, POOL_WIDTH),
                          lambda bi, i: (bi, jnp.maximum(i * (512 // POOL_HALO) - 1, 0), 0))
    pool_p = _pool_mix(up3, halo_p, up3, wp, ps, tm=512, base_pos=0, zero_first_halo=True)
    h1_p, xn_all, ti_p, gt_p = _merge(
        attn_p.reshape(n_p, ATTN_WIDTH), pool_p.reshape(n_p, POOL_WIDTH), gp, xp,
        wa, wb, wo, n2w, rw, rb, tm=256, xn_all=None, xn_row0=0, n_all=n_all)

    xs = x_sample.reshape(n_s, D_MODEL)
    qs, ks, vs, us, gs, lfs = _inproj(xs, n1w, w_main, w_f, b_f, qnw, knw, tm=n_s, q_scale=1.0)
    cs = _cumsum_rows(lfs, bl=n_s, seg_len=ls)
    q_s = qs.astype(F32).reshape(bs, ls, N_HEADS, HEAD_DIM)
    qbd = jnp.einsum('bthd,hg->bhtgd', q_s, jnp.eye(N_HEADS, dtype=F32))
    qbd = qbd.reshape(bs, N_HEADS * ls, ATTN_WIDTH).astype(BF16)
    cn = jnp.transpose(cs.reshape(bs, ls, FORGET_LANES)[:, :, :N_HEADS], (0, 2, 1))
    cncol = cn.reshape(bs, N_HEADS * ls, 1)
    cnkeys = jnp.broadcast_to(cn[:, :, None, :], (bs, N_HEADS, ls, ls)).reshape(bs, N_HEADS * ls, ls)
    cnkeys = jnp.pad(cnkeys, ((0, 0), (0, 0), (0, PAGE_SIZE - ls)))
    knew = jnp.pad(ks.reshape(bs, ls, ATTN_WIDTH), ((0, 0), (0, PAGE_SIZE - ls), (0, 0)))
    vnew = jnp.pad(vs.reshape(bs, ls, ATTN_WIDTH), ((0, 0), (0, PAGE_SIZE - ls), (0, 0)))
    n_phys = cache_k.shape[1]
    ck = cache_k[0].reshape(n_phys, PAGE_SIZE * N_HEADS, HEAD_DIM)
    cv = cache_v[0].reshape(n_phys, PAGE_SIZE * N_HEADS, HEAD_DIM)
    clf_t = jnp.transpose(cache_logf[0], (0, 2, 1))
    attn_s = _sample_attention(page_table, qbd, cncol, cnkeys, knew, vnew, ck, cv, clf_t)
    halo_src = jnp.pad(state_pool[0], ((0, 0), (POOL_HALO - POOL_HIST, 0), (0, 0)))
    halo_s = pl.BlockSpec((1, POOL_HALO, POOL_WIDTH), lambda bi, i: (bi, 0, 0))
    us3 = us.reshape(bs, ls, POOL_WIDTH)
    pool_s = _pool_mix(halo_src, halo_s, us3, wp, ps, tm=ls, base_pos=n_past, zero_first_halo=False)
    h1_s, xn_all, ti_s, gt_s = _merge(
        attn_s.reshape(n_s, ATTN_WIDTH), pool_s.reshape(n_s, POOL_WIDTH), gs, xs,
        wa, wb, wo, n2w, rw, rb, tm=n_s, xn_all=xn_all, xn_row0=n_p, n_all=n_all)

    n_rank = pl.cdiv(n_all, 128) * 128
    ti_all = jnp.concatenate([ti_p, ti_s, jnp.full((n_rank - n_all, 128), -1, I32)], axis=0)
    rank, cnt = _expert_rank(ti_all)
    top_idx = ti_all[:n_all, :TOP_K]
    counts = cnt[0, :N_EXPERTS].astype(I32)
    n_blocks = pl.cdiv(n_all * TOP_K, MOE_SUB) + N_EXPERTS
    n_slots = n_blocks * MOE_SUB
    n_wi = n_blocks // MOE_NSUB + N_EXPERTS
    dest, wi_e, wi_row, wi_nsub, wi_tok = _routing_tables(
        top_idx, rank[:n_all, :TOP_K], counts, n_wi, n_slots)
    ys = _moe_experts(wi_e, wi_row, wi_nsub, wi_tok, xn_all, w_gu[0], b_gu[0], w_dn[0], b_dn[0], n_slots)
    y_p = _combine(dest[:n_p], gt_p, h1_p, ys, tt=128)
    y_s = _combine(dest[n_p:], gt_s, h1_s, ys, tt=n_s)

    heads = (N_HEADS, HEAD_DIM)
    k_p = kp.reshape(1, bp, lp, *heads)
    v_p = vp.reshape(1, bp, lp, *heads)
    f_p = lfp.reshape(bp, lp, FORGET_LANES)[None, :, :, :N_HEADS]
    u_p = up3[:, lp - POOL_HIST:][None]
    k_s = ks.reshape(1, bs, ls, *heads)
    v_s = vs.reshape(1, bs, ls, *heads)
    f_s = lfs.reshape(bs, ls, FORGET_LANES)[None, :, :, :N_HEADS]
    pool_state = jnp.concatenate([state_pool[0], us3], axis=1)[:, -POOL_HIST:][None]
    return (y_p.reshape(bp, lp, D_MODEL), y_s.reshape(bs, ls, D_MODEL),
            k_p, v_p, f_p, u_p, k_s, v_s, f_s, pool_state)
```

```python
import functools

import jax
import jax.numpy as jnp
from jax import lax
from jax.experimental import pallas as pl
from jax.experimental.pallas import tpu as pltpu

F32 = jnp.float32
BF16 = jnp.bfloat16
I32 = jnp.int32

D_MODEL = 2048
N_HEADS = 8
HEAD_DIM = 128
ATTN_WIDTH = N_HEADS * HEAD_DIM
ATTN_SCALE = HEAD_DIM ** -0.5
FORGET_LANES = 128
POOL_WINDOWS = (2, 4, 8, 16)
POOL_WIDTH = D_MODEL // 2
GW = POOL_WIDTH // len(POOL_WINDOWS)
POOL_HIST = max(POOL_WINDOWS) - 1
POOL_HALO = 16
N_EXPERTS = 32
TOP_K = 4
D_EXPERT = D_MODEL
SWIGLU_LIMIT = 7.0
SWIGLU_ALPHA = 1.702
NORM_EPS = 1e-6
PAGE_SIZE = 128
NEG_BIG = -1e30

ZQ, ZK, ZV, ZU = 0, 1, 2, 3
Z_WIDTH = 3 * ATTN_WIDTH + POOL_WIDTH + 2 * D_MODEL

MOE_SUB = 128
MOE_NSUB = 9
MOE_TM = MOE_SUB * MOE_NSUB
MOE_TN = 256
MOE_NJ = D_EXPERT // MOE_TN
MOE_ROW_GROUPS = ((0, 512), (512, 512), (1024, 128))
assert sum(n for _, n in MOE_ROW_GROUPS) == MOE_TM and MOE_TM % (MOE_NJ * len(MOE_ROW_GROUPS)) == 0
VMEM_LIMIT = 56 * 1024 * 1024

HIGHEST = lax.Precision.HIGHEST


def _cparams(sem, vmem=VMEM_LIMIT, flags=None):
    return pltpu.CompilerParams(dimension_semantics=sem, vmem_limit_bytes=vmem, flags=flags)


def _log_sigmoid(x):
    return jnp.minimum(x, 0.0) - jnp.log1p(jnp.exp(-jnp.abs(x)))


_NT = (((1,), (1,)), ((), ()))


def _inproj_kernel(x_ref, n1w_ref, w_ref, wf_ref, bf_ref, qnw_ref, knw_ref,
                   q_ref, k_ref, v_ref, u_ref, g_ref, lf_ref, xn_ref, *, q_scale):
    j = pl.program_id(1)

    @pl.when(j == 0)
    def _():
        x = x_ref[...]
        ms = jnp.mean(x * x, axis=-1, keepdims=True)
        xn = x * lax.rsqrt(ms + NORM_EPS) * n1w_ref[...]
        xn_ref[...] = xn.astype(BF16)
        zf = lax.dot_general(xn_ref[...], wf_ref[...], _NT, preferred_element_type=F32) + bf_ref[...]
        lf_ref[...] = _log_sigmoid(zf)

    def proj():
        return lax.dot_general(xn_ref[...], w_ref[...], _NT, preferred_element_type=F32)

    def head_norm(nw_ref, dst_ref, scale):
        z = proj()
        for h in range(N_HEADS):
            sl = slice(h * HEAD_DIM, (h + 1) * HEAD_DIM)
            zh = z[:, sl]
            ms = jnp.mean(zh * zh, axis=-1, keepdims=True)
            y = zh * lax.rsqrt(ms + NORM_EPS) * nw_ref[:, sl]
            if scale != 1.0:
                y = y * scale
            dst_ref[:, sl] = y.astype(dst_ref.dtype)

    @pl.when(j == ZQ)
    def _():
        head_norm(qnw_ref, q_ref, q_scale)

    @pl.when(j == ZK)
    def _():
        head_norm(knw_ref, k_ref, 1.0)

    @pl.when(j == ZV)
    def _():
        v_ref[...] = proj()

    @pl.when(j == ZU)
    def _():
        u_ref[...] = proj()

    @pl.when(j > ZU)
    def _():
        g_ref[...] = jax.nn.sigmoid(proj())


def _inproj(x, n1w, w_main_t, w_f_t, b_f, qnw, knw, tm, q_scale):
    m = x.shape[0]
    tn = ATTN_WIDTH
    row = lambda i, j: (i, 0)
    const = lambda i, j: (0, 0)
    return pl.pallas_call(
        functools.partial(_inproj_kernel, q_scale=q_scale),
        grid=(m // tm, Z_WIDTH // tn),
        in_specs=[
            pl.BlockSpec((tm, D_MODEL), row),
            pl.BlockSpec((1, D_MODEL), const),
            pl.BlockSpec((tn, D_MODEL), lambda i, j: (j, 0)),
            pl.BlockSpec((FORGET_LANES, D_MODEL), const),
            pl.BlockSpec((1, FORGET_LANES), const),
            pl.BlockSpec((1, tn), const),
            pl.BlockSpec((1, tn), const),
        ],
        out_specs=[
            pl.BlockSpec((tm, tn), row),
            pl.BlockSpec((tm, tn), row),
            pl.BlockSpec((tm, tn), row),
            pl.BlockSpec((tm, tn), row),
            pl.BlockSpec((tm, tn), lambda i, j: (i, jnp.maximum(j - (ZU + 1), 0))),
            pl.BlockSpec((tm, FORGET_LANES), row),
        ],
        out_shape=[
            jax.ShapeDtypeStruct((m, ATTN_WIDTH), BF16),
            jax.ShapeDtypeStruct((m, ATTN_WIDTH), F32),
            jax.ShapeDtypeStruct((m, ATTN_WIDTH), F32),
            jax.ShapeDtypeStruct((m, POOL_WIDTH), F32),
            jax.ShapeDtypeStruct((m, 2 * D_MODEL), F32),
            jax.ShapeDtypeStruct((m, FORGET_LANES), F32),
        ],
        scratch_shapes=[pltpu.VMEM((tm, D_MODEL), BF16)],
        compiler_params=_cparams(("arbitrary", "arbitrary")),
        name="inproj",
    )(x, n1w, w_main_t, w_f_t, b_f, qnw, knw)


def _cumsum_kernel(lf_ref, c_ref, carry_ref, *, bl, seg_len):
    i = pl.program_id(0)
    r = lax.broadcasted_iota(I32, (bl, bl), 0)
    c = lax.broadcasted_iota(I32, (bl, bl), 1)
    mask = c <= r
    if seg_len < bl:
        mask = mask & ((r // seg_len) == (c // seg_len))
    tri = mask.astype(F32)
    cs = jnp.dot(tri, lf_ref[...], preferred_element_type=F32, precision=HIGHEST)
    if seg_len > bl:
        @pl.when((i * bl) % seg_len == 0)
        def _():
            carry_ref[...] = jnp.zeros_like(carry_ref)
        cs = cs + carry_ref[...]
        carry_ref[...] = cs[bl - 1:bl, :]
    c_ref[...] = cs


def _cumsum_rows(lf, bl, seg_len):
    m = lf.shape[0]
    return pl.pallas_call(
        functools.partial(_cumsum_kernel, bl=bl, seg_len=seg_len),
        grid=(m // bl,),
        in_specs=[pl.BlockSpec((bl, FORGET_LANES), lambda i: (i, 0))],
        out_specs=pl.BlockSpec((bl, FORGET_LANES), lambda i: (i, 0)),
        out_shape=jax.ShapeDtypeStruct((m, FORGET_LANES), F32),
        scratch_shapes=[pltpu.VMEM((1, FORGET_LANES), F32)],
        compiler_params=_cparams(("arbitrary",)),
        name="logf_cumsum",
    )(lf)


LOG2E = 1.4426950408889634


def _pattn_kernel(q_ref, k_ref, v_ref, c_ref, kbias_ref, o_ref, *scratch, tq, ch):
    m_refs = scratch[0:N_HEADS]
    accl_refs = scratch[N_HEADS:2 * N_HEADS]
    kb_ref, vb_ref = scratch[2 * N_HEADS:]
    qi = pl.program_id(1)
    ki = pl.program_id(2)
    tk = tq

    @pl.when(ki == 0)
    def _():
        for h in range(N_HEADS):
            m_refs[h][...] = jnp.full_like(m_refs[h], -jnp.inf)
            accl_refs[h][...] = jnp.zeros_like(accl_refs[h])

    @pl.when(ki <= qi)
    def _():
        kb_ref[...] = k_ref[0].astype(BF16)
        vb_ref[...] = v_ref[0].astype(BF16)

    def block(diag):
        c2 = c_ref[0] * LOG2E
        q_ones = (lax.broadcasted_iota(I32, (ch, HEAD_DIM), 1) < 3).astype(BF16)
        v_ones = jnp.ones((tk, HEAD_DIM), BF16)
        below = (lax.broadcasted_iota(I32, (ch, ch), 1) <= lax.broadcasted_iota(I32, (ch, ch), 0))

        def where(h, r):
            rows = slice(r * ch, (r + 1) * ch)
            cols = slice(0, (r + 1) * ch if diag else tk)
            return rows, cols, slice(h * HEAD_DIM, (h + 1) * HEAD_DIM)

        def stage_scores(h, r):
            rows, cols, sl = where(h, r)
            q_cat = jnp.concatenate([q_ref[0, rows, sl], q_ones], axis=1)
            k_cat = jnp.concatenate([kb_ref[cols, sl], kbias_ref[0, h, cols, :]], axis=1)
            u = lax.dot_general(q_cat, k_cat, _NT, preferred_element_type=F32)
            if diag:
                last = jnp.where(below, u[:, r * ch:], -jnp.inf)
                u = last if r == 0 else jnp.concatenate([u[:, :r * ch], last], axis=1)
            return u

        def stage_max(h, r, u):
            rows, _, _ = where(h, r)
            c2t = c2[rows, h:h + 1]
            m_prev = m_refs[h][rows, :]
            m_new = jnp.maximum(m_prev, jnp.max(u, axis=-1, keepdims=True) + c2t)
            m_refs[h][rows, :] = m_new
            return jnp.exp2(m_prev - m_new), c2t - m_new[:, 0:1]

        def stage_accumulate(h, r, p, alpha):
            rows, cols, sl = where(h, r)
            v_cat = jnp.concatenate([vb_ref[cols, sl], v_ones[cols, :]], axis=1)
            pv = jnp.dot(p, v_cat, preferred_element_type=F32)
            accl_refs[h][rows, :] = jnp.concatenate([alpha, alpha], axis=1) * accl_refs[h][rows, :] + pv

        units = [(h, r) for r in range(tq // ch) for h in range(N_HEADS)]
        u_of, ab_of, p_of = {}, {}, {}
        for t in range(len(units) + 3):
            if t < len(units):
                u_of[t] = stage_scores(*units[t])
            if 0 <= t - 1 < len(units):
                ab_of[t - 1] = stage_max(*units[t - 1], u_of[t - 1])
            if 0 <= t - 2 < len(units):
                p_of[t - 2] = jnp.exp2(u_of.pop(t - 2) + ab_of[t - 2][1]).astype(BF16)
            if 0 <= t - 3 < len(units):
                stage_accumulate(*units[t - 3], p_of.pop(t - 3), ab_of.pop(t - 3)[0])

    @pl.when(ki < qi)
    def _():
        block(False)

    @pl.when(ki == qi)
    def _():
        block(True)
        for h in range(N_HEADS):
            sl = slice(h * HEAD_DIM, (h + 1) * HEAD_DIM)
            accl = accl_refs[h][...]
            o_ref[0, :, sl] = (accl[:, :HEAD_DIM] / accl[:, HEAD_DIM:]).astype(o_ref.dtype)


def _prompt_attention(q3, k3, v3, c3, kbias, tq=512, ch=128):
    b, l, _ = q3.shape
    tk = tq
    nq = l // tq
    kv_idx = lambda bi, qi, ki: (bi, jnp.minimum(ki, qi), 0)
    return pl.pallas_call(
        functools.partial(_pattn_kernel, tq=tq, ch=ch),
        grid=(b, nq, nq),
        in_specs=[
            pl.BlockSpec((1, tq, ATTN_WIDTH), lambda bi, qi, ki: (bi, qi, 0)),
            pl.BlockSpec((1, tk, ATTN_WIDTH), kv_idx),
            pl.BlockSpec((1, tk, ATTN_WIDTH), kv_idx),
            pl.BlockSpec((1, tq, FORGET_LANES), lambda bi, qi, ki: (bi, qi, 0)),
            pl.BlockSpec((1, N_HEADS, tk, HEAD_DIM),
                         lambda bi, qi, ki: (bi, 0, jnp.minimum(ki, qi), 0)),
        ],
        out_specs=pl.BlockSpec((1, tq, ATTN_WIDTH), lambda bi, qi, ki: (bi, qi, 0)),
        out_shape=jax.ShapeDtypeStruct((b, l, ATTN_WIDTH), BF16),
        scratch_shapes=(
            [pltpu.VMEM((tq, HEAD_DIM), F32) for _ in range(N_HEADS)]
            + [pltpu.VMEM((tq, 2 * HEAD_DIM), F32) for _ in range(N_HEADS)]
            + [pltpu.VMEM((tk, ATTN_WIDTH), BF16), pltpu.VMEM((tk, ATTN_WIDTH), BF16)]),
        compiler_params=_cparams(("arbitrary", "arbitrary", "arbitrary")),
        name="prompt_attention",
    )(q3, k3, v3, c3, kbias)


def _bf16_pieces(x, n):
    out = []
    for _ in range(n):
        piece = x.astype(BF16).astype(F32)
        out.append(piece)
        x = x - piece
    return out


def _kbias_kernel(c_ref, o_ref):
    c = c_ref[0] * (-LOG2E)
    lane = lax.broadcasted_iota(I32, c.shape, 1)
    for h in range(N_HEADS):
        hi, mid, lo = _bf16_pieces(c[:, h:h + 1], 3)
        tile = jnp.where(lane == 0, hi, jnp.where(lane == 1, mid, jnp.where(lane == 2, lo, 0.0)))
        o_ref[0, h] = tile.astype(BF16)


def _key_bias_pieces(c3, tl=512):
    b, l, _ = c3.shape
    return pl.pallas_call(
        _kbias_kernel,
        grid=(b, l // tl),
        in_specs=[pl.BlockSpec((1, tl, FORGET_LANES), lambda bi, i: (bi, i, 0))],
        out_specs=pl.BlockSpec((1, N_HEADS, tl, HEAD_DIM), lambda bi, i: (bi, 0, i, 0)),
        out_shape=jax.ShapeDtypeStruct((b, N_HEADS, l, HEAD_DIM), BF16),
        compiler_params=_cparams(("arbitrary", "arbitrary")),
        name="key_bias",
    )(c3)


def _heads_to_lanes(ref):
    parts = [ref[0, pl.ds(h, PAGE_SIZE, stride=N_HEADS), :] for h in range(N_HEADS)]
    return jnp.concatenate(parts, axis=1).astype(BF16)


def _rows_per_head(x):
    return jnp.concatenate(
        [jnp.broadcast_to(x[h:h + 1, :], (8, x.shape[1])) for h in range(N_HEADS)], axis=0)


def _sattn_kernel(pt_ref, qbd_ref, cncol_ref, cnkeys_ref, knew_ref, vnew_ref, *rest, g_pages):
    del pt_ref
    k_refs = rest[0:g_pages]
    v_refs = rest[g_pages:2 * g_pages]
    lf_refs = rest[2 * g_pages:3 * g_pages]
    o_ref = rest[3 * g_pages]
    m_ref, l_ref, acc_ref, carry_ref = rest[3 * g_pages + 1:]
    jj = pl.program_id(1)
    nj = pl.num_programs(1)
    rows = N_HEADS * 8

    @pl.when(jj == 0)
    def _():
        m_ref[...] = jnp.full_like(m_ref, -jnp.inf)
        l_ref[...] = jnp.zeros_like(l_ref)
        acc_ref[...] = jnp.zeros_like(acc_ref)
        carry_ref[...] = jnp.zeros_like(carry_ref)

    qbd = qbd_ref[0]
    cncol = cncol_ref[0]
    later = (lax.broadcasted_iota(I32, (PAGE_SIZE, PAGE_SIZE), 0)
             > lax.broadcasted_iota(I32, (PAGE_SIZE, PAGE_SIZE), 1)).astype(F32)

    def scores(kcat):
        return lax.dot_general(qbd, kcat, (((1,), (1,)), ((), ())),
                               preferred_element_type=F32) * ATTN_SCALE

    def update(s, vcat, m, l, acc):
        m_new = jnp.maximum(m, jnp.max(s, axis=-1, keepdims=True))
        alpha = jnp.exp(m - m_new)
        p = jnp.exp(s - jnp.concatenate([m_new] * (s.shape[1] // PAGE_SIZE), axis=1))
        l = alpha * l + jnp.sum(p, axis=-1, keepdims=True)
        pv = jnp.dot(p.astype(BF16), vcat, preferred_element_type=F32)
        acc = jnp.concatenate([alpha] * N_HEADS, axis=1) * acc + pv
        return m_new, l, acc

    m, l, acc, carry = m_ref[...], l_ref[...], acc_ref[...], carry_ref[...]
    lf_all = jnp.concatenate([_rows_per_head(lf_refs[g][0]) for g in range(g_pages)], axis=0)
    suffix_all = jnp.dot(lf_all, later, preferred_element_type=F32, precision=HIGHEST)
    bias = []
    for g in range(g_pages):
        bias.append(suffix_all[g * rows:(g + 1) * rows, :] + carry)
        carry = carry + jnp.sum(lf_all[g * rows:(g + 1) * rows, :], axis=-1, keepdims=True)
    kcat = jnp.concatenate([_heads_to_lanes(k_refs[g]) for g in range(g_pages)], axis=0)
    vcat = jnp.concatenate([_heads_to_lanes(v_refs[g]) for g in range(g_pages)], axis=0)
    s = scores(kcat) + cncol + jnp.concatenate(bias, axis=1)
    m, l, acc = update(s, vcat, m, l, acc)

    @pl.when(jj < nj - 1)
    def _():
        m_ref[...] = m
        l_ref[...] = l
        acc_ref[...] = acc
        carry_ref[...] = carry

    @pl.when(jj == nj - 1)
    def _():
        s = scores(knew_ref[0].astype(BF16)) + cncol - cnkeys_ref[0]
        t_row = lax.broadcasted_iota(I32, (rows, PAGE_SIZE), 0) % 8
        s_col = lax.broadcasted_iota(I32, (rows, PAGE_SIZE), 1)
        s = jnp.where(s_col <= t_row, s, -jnp.inf)
        _, l2, acc2 = update(s, vnew_ref[0].astype(BF16), m, l, acc)
        o = acc2 / jnp.concatenate([l2] * N_HEADS, axis=1)
        for h in range(N_HEADS):
            sl = slice(h * HEAD_DIM, (h + 1) * HEAD_DIM)
            o_ref[0, :, sl] = o[h * 8:(h + 1) * 8, sl].astype(o_ref.dtype)


def _sample_attention(page_table, qbd, cncol, cnkeys, knew, vnew, ck, cv, clf_t, g_pages=8):
    bd, n_pages = page_table.shape
    rows = N_HEADS * 8
    pt_flat = page_table.reshape(-1)

    def page_idx(g):
        return lambda b, jj, pt: (pt[b * n_pages + n_pages - 1 - (jj * g_pages + g)], 0, 0)

    per_b = lambda b, jj, pt: (b, 0, 0)
    in_specs = [
        pl.BlockSpec((1, rows, ATTN_WIDTH), per_b),
        pl.BlockSpec((1, rows, 1), per_b),
        pl.BlockSpec((1, rows, PAGE_SIZE), per_b),
        pl.BlockSpec((1, PAGE_SIZE, ATTN_WIDTH), per_b),
        pl.BlockSpec((1, PAGE_SIZE, ATTN_WIDTH), per_b),
    ]
    in_specs += [pl.BlockSpec((1, PAGE_SIZE * N_HEADS, HEAD_DIM), page_idx(g)) for g in range(g_pages)]
    in_specs += [pl.BlockSpec((1, PAGE_SIZE * N_HEADS, HEAD_DIM), page_idx(g)) for g in range(g_pages)]
    in_specs += [pl.BlockSpec((1, N_HEADS, PAGE_SIZE), page_idx(g)) for g in range(g_pages)]
    grid_spec = pltpu.PrefetchScalarGridSpec(
        num_scalar_prefetch=1,
        grid=(bd, n_pages // g_pages),
        in_specs=in_specs,
        out_specs=pl.BlockSpec((1, 8, ATTN_WIDTH), per_b),
        scratch_shapes=[
            pltpu.VMEM((rows, PAGE_SIZE), F32),
            pltpu.VMEM((rows, PAGE_SIZE), F32),
            pltpu.VMEM((rows, ATTN_WIDTH), F32),
            pltpu.VMEM((rows, PAGE_SIZE), F32),
        ],
    )
    return pl.pallas_call(
        functools.partial(_sattn_kernel, g_pages=g_pages),
        grid_spec=grid_spec,
        out_shape=jax.ShapeDtypeStruct((bd, 8, ATTN_WIDTH), BF16),
        compiler_params=_cparams(("arbitrary", "arbitrary")),
        name="sample_attention",
    )(pt_flat, qbd, cncol, cnkeys, knew, vnew,
      *([ck] * g_pages), *([cv] * g_pages), *([clf_t] * g_pages))


def _pool_kernel(halo_ref, u_ref, wp_ref, ps_ref, o_ref, *, tm, base_pos, zero_first_halo):
    i = pl.program_id(1)
    halo = halo_ref[0]
    if zero_first_halo:
        halo = jnp.where(i == 0, 0.0, halo)
    u = u_ref[0]
    ext = jnp.concatenate([halo, u], axis=0)
    pos = base_pos + i * tm + lax.broadcasted_iota(I32, (tm, 1), 0)
    for g, w in enumerate(POOL_WINDOWS):
        sl = slice(g * GW, (g + 1) * GW)
        s = ext[:, sl]
        k = 1
        while k < w:
            s = s + pltpu.roll(s, shift=k, axis=0)
            k *= 2
        wsum = s[POOL_HALO:, :]
        count = jnp.minimum(pos + 1, w).astype(F32)
        mixed = wsum / count - u[:, sl]
        y = jnp.dot(mixed.astype(BF16), wp_ref[g], preferred_element_type=F32)
        o_ref[0, :, sl] = (y * ps_ref[:, sl]).astype(o_ref.dtype)


def _pool_mix(halo_src, halo_spec, u_src, wp, ps, tm, base_pos, zero_first_halo):
    b, l, _ = u_src.shape
    return pl.pallas_call(
        functools.partial(_pool_kernel, tm=tm, base_pos=base_pos, zero_first_halo=zero_first_halo),
        grid=(b, l // tm),
        in_specs=[
            halo_spec,
            pl.BlockSpec((1, tm, POOL_WIDTH), lambda bi, i: (bi, i, 0)),
            pl.BlockSpec((len(POOL_WINDOWS), GW, GW), lambda bi, i: (0, 0, 0)),
            pl.BlockSpec((1, POOL_WIDTH), lambda bi, i: (0, 0)),
        ],
        out_specs=pl.BlockSpec((1, tm, POOL_WIDTH), lambda bi, i: (bi, i, 0)),
        out_shape=jax.ShapeDtypeStruct((b, l, POOL_WIDTH), BF16),
        compiler_params=_cparams(("arbitrary", "arbitrary")),
        name="pool_mix",
    )(halo_src, u_src, wp, ps)


def _merge_kernel(attn_ref, pool_ref, ga_ref, gb_ref, x_ref, wa_ref, wb_ref, wo_ref,
                  n2w_ref, rw_ref, rb_ref, *rest, tm, aliased):
    h_ref, xn_ref, ti_ref, gt_ref, rwp_ref = rest[-5:]

    @pl.when(pl.program_id(0) == 0)
    def _():
        for n, piece in enumerate(_bf16_pieces(rw_ref[...], 3)):
            rwp_ref[n] = piece.astype(BF16)

    a = jnp.dot(attn_ref[...], wa_ref[...], preferred_element_type=F32)
    p = jnp.dot(pool_ref[...], wb_ref[...], preferred_element_type=F32)
    merged = ga_ref[...] * a + gb_ref[...] * p
    o = jnp.dot(merged.astype(BF16), wo_ref[...], preferred_element_type=F32)
    h1 = x_ref[...] + o
    h_ref[...] = h1
    ms = jnp.mean(h1 * h1, axis=-1, keepdims=True)
    xn = h1 * lax.rsqrt(ms + NORM_EPS) * n2w_ref[...]
    xn_ref[...] = xn
    x_hi, x_lo = [v.astype(BF16) for v in _bf16_pieces(xn, 2)]
    logits = rb_ref[...]
    for xp, wn in ((x_hi, 0), (x_hi, 1), (x_lo, 0), (x_hi, 2), (x_lo, 1)):
        logits = logits + jnp.dot(xp, rwp_ref[wn], preferred_element_type=F32)
    lane = lax.broadcasted_iota(I32, (tm, 128), 1).astype(F32)
    vals, idxs = [], []
    work = logits
    for _ in range(TOP_K):
        mv = jnp.max(work, axis=-1, keepdims=True)
        ix = jnp.min(jnp.where(work == mv, lane, 128.0), axis=-1, keepdims=True)
        vals.append(mv)
        idxs.append(ix)
        work = jnp.where(lane == ix, NEG_BIG, work)
    es = [jnp.exp(v - vals[0]) for v in vals]
    den = es[0] + es[1] + es[2] + es[3]
    ti = jnp.zeros((tm, 128), F32)
    gt = jnp.zeros((tm, 128), F32)
    for k in range(TOP_K):
        ti = jnp.where(lane == float(k), idxs[k], ti)
        gt = jnp.where(lane == float(k), es[k] / den, gt)
    ti_ref[...] = ti.astype(I32)
    gt_ref[...] = gt


def _merge(attn, pool, z, x, wa, wb, wo, n2w, rw, rb, tm, xn_all, xn_row0, n_all):
    m = x.shape[0]
    aliased = xn_all is not None
    const = lambda i: (0, 0)
    single = dict(pipeline_mode=pl.Buffered(1))
    xn_blk0 = xn_row0 // tm
    in_specs = [
        pl.BlockSpec((tm, ATTN_WIDTH), lambda i: (i, 0)),
        pl.BlockSpec((tm, POOL_WIDTH), lambda i: (i, 0)),
        pl.BlockSpec((tm, D_MODEL), lambda i: (i, 0)),
        pl.BlockSpec((tm, D_MODEL), lambda i: (i, 1)),
        pl.BlockSpec((tm, D_MODEL), lambda i: (i, 0)),
        pl.BlockSpec((ATTN_WIDTH, D_MODEL), const, **single),
        pl.BlockSpec((POOL_WIDTH, D_MODEL), const, **single),
        pl.BlockSpec((D_MODEL, D_MODEL), const, **single),
        pl.BlockSpec((1, D_MODEL), const),
        pl.BlockSpec((D_MODEL, 128), const),
        pl.BlockSpec((1, 128), const),
    ]
    args = [attn, pool, z, z, x, wa, wb, wo, n2w, rw, rb]
    io_alias = {}
    if aliased:
        in_specs.append(pl.BlockSpec(memory_space=pl.ANY))
        args.append(xn_all)
        io_alias = {len(args) - 1: 1}
    return pl.pallas_call(
        functools.partial(_merge_kernel, tm=tm, aliased=aliased),
        grid=(m // tm,),
        in_specs=in_specs,
        out_specs=[
            pl.BlockSpec((tm, D_MODEL), lambda i: (i, 0)),
            pl.BlockSpec((tm, D_MODEL), lambda i: (xn_blk0 + i, 0)),
            pl.BlockSpec((tm, 128), lambda i: (i, 0)),
            pl.BlockSpec((tm, 128), lambda i: (i, 0)),
        ],
        out_shape=[
            jax.ShapeDtypeStruct((m, D_MODEL), F32),
            jax.ShapeDtypeStruct((n_all, D_MODEL), F32),
            jax.ShapeDtypeStruct((m, 128), I32),
            jax.ShapeDtypeStruct((m, 128), F32),
        ],
        input_output_aliases=io_alias,
        scratch_shapes=[pltpu.VMEM((3, D_MODEL, 128), BF16)],
        compiler_params=_cparams(("arbitrary",)),
        name="merge_router",
    )(*args)


def _rank_kernel(idx_ref, rank_ref, cnt_ref, carry_ref, *, tr):
    i = pl.program_id(0)

    @pl.when(i == 0)
    def _():
        carry_ref[...] = jnp.zeros_like(carry_ref)

    idx = idx_ref[...]
    lane = lax.broadcasted_iota(I32, (tr, 128), 1)
    hits = [idx[:, k:k + 1] == lane for k in range(TOP_K)]
    onehot = jnp.zeros((tr, 128), F32)
    for hk in hits:
        onehot = onehot + hk.astype(F32)
    earlier = (lax.broadcasted_iota(I32, (tr, tr), 1)
               < lax.broadcasted_iota(I32, (tr, tr), 0)).astype(BF16)
    before = jnp.dot(earlier, onehot.astype(BF16), preferred_element_type=F32) + carry_ref[...]
    out = jnp.zeros((tr, 128), I32)
    for k, hk in enumerate(hits):
        rk = jnp.sum(jnp.where(hk, before, 0.0), axis=-1, keepdims=True)
        out = jnp.where(lane == k, rk.astype(I32), out)
    rank_ref[...] = out
    carry_ref[...] = carry_ref[...] + jnp.sum(onehot, axis=0, keepdims=True)
    cnt_ref[...] = carry_ref[...]


def _expert_rank(top_idx_padded, tr=128):
    n = top_idx_padded.shape[0]
    return pl.pallas_call(
        functools.partial(_rank_kernel, tr=tr),
        grid=(n // tr,),
        in_specs=[pl.BlockSpec((tr, 128), lambda i: (i, 0))],
        out_specs=[
            pl.BlockSpec((tr, 128), lambda i: (i, 0)),
            pl.BlockSpec((1, 128), lambda i: (0, 0)),
        ],
        out_shape=[
            jax.ShapeDtypeStruct((n, 128), I32),
            jax.ShapeDtypeStruct((1, 128), F32),
        ],
        scratch_shapes=[pltpu.VMEM((1, 128), F32)],
        compiler_params=_cparams(("arbitrary",)),
        name="expert_rank",
    )(top_idx_padded)


def _moe_out_block_copy(acc_scr, ys_hbm, sub, row, sem):
    return pltpu.make_async_copy(acc_scr.at[pl.ds(sub * MOE_SUB, MOE_SUB)],
                                 ys_hbm.at[pl.ds(row, MOE_SUB)], sem)


def _moe_kernel(wi_e_ref, wi_row_ref, wi_nsub_ref, tok_ref, tok_next_ref, xn_hbm,
                wg_ref, wu_ref, bg_ref, bu_ref, wd_ref, bd_ref, ys_hbm,
                x_scr, acc_scr, wg_bf, wu_bf, wd_bf, pend_ref, sem_in, sem_out):
    del wi_e_ref
    w = pl.program_id(0)
    j = pl.program_id(1)
    nj = pl.num_programs(1)
    nsub = wi_nsub_ref[w]
    row0 = wi_row_ref[w]
    cur = w % 2
    rows_per_step = MOE_TM // MOE_NJ

    def issue_rows(tok_tbl, slot, first, count):
        for i in range(count):
            r = first + i
            tok = tok_tbl[0, 0, r]
            pltpu.make_async_copy(xn_hbm.at[pl.ds(tok, 1)], x_scr.at[slot, pl.ds(r, 1)], sem_in).start()

    @pl.when((w == 0) & (j == 0))
    def _():
        pend_ref[0] = 0

    @pl.when((w == 0) & (j == 0) & (nsub > 0))
    def _():
        def issue(blk, carry):
            issue_rows(tok_ref, 0, blk * 8, 8)
            return carry
        lax.fori_loop(0, MOE_TM // 8, issue, 0)

    def drain_output():
        for s in range(MOE_NSUB):
            @pl.when(s < pend_ref[0])
            def _():
                _moe_out_block_copy(acc_scr, ys_hbm, s, 0, sem_out).wait()
        pend_ref[0] = 0

    @pl.when(j == 0)
    def _():
        drain_output()

    rows_in_flight = jnp.where(w == 0, nsub > 0, wi_nsub_ref[jnp.maximum(w - 1, 0)] > 0)

    @pl.when((j == 0) & rows_in_flight)
    def _():
        for s in range(MOE_NSUB):
            pltpu.make_async_copy(xn_hbm.at[pl.ds(0, MOE_SUB)],
                                  x_scr.at[cur, pl.ds(s * MOE_SUB, MOE_SUB)], sem_in).wait()

    @pl.when((j == 0) & (nsub > 0))
    def _():
        acc_scr[...] = jnp.broadcast_to(bd_ref[0], (MOE_TM, D_MODEL))

    def expert_rows(rows):
        xs = x_scr[cur, rows, :].astype(BF16)
        gate = jnp.dot(xs, wg_bf[...], preferred_element_type=F32) + bg_ref[0]
        up = jnp.dot(xs, wu_bf[...], preferred_element_type=F32) + bu_ref[0]
        gate = jnp.minimum(gate, SWIGLU_LIMIT)
        up = jnp.clip(up, -SWIGLU_LIMIT, SWIGLU_LIMIT)
        act = (up + 1.0) * (gate * jax.nn.sigmoid(SWIGLU_ALPHA * gate))
        acc_scr[rows, :] += jnp.dot(act.astype(BF16), wd_bf[...], preferred_element_type=F32)

    @pl.when(nsub > 0)
    def _():
        wg_bf[...] = wg_ref[0].astype(BF16)
        wu_bf[...] = wu_ref[0].astype(BF16)
        wd_bf[...] = wd_ref[0].astype(BF16)
        first = j * rows_per_step
        share = rows_per_step // len(MOE_ROW_GROUPS)
        for n, (start, size) in enumerate(MOE_ROW_GROUPS):
            issue_rows(tok_next_ref, 1 - cur, first + n * share, share)
            expert_rows(pl.ds(start, size))

    @pl.when((j == nj - 1) & (nsub > 0))
    def _():
        for s in range(MOE_NSUB):
            @pl.when(s < nsub)
            def _():
                _moe_out_block_copy(acc_scr, ys_hbm, s,
                                    pl.multiple_of(row0 + s * MOE_SUB, MOE_SUB), sem_out).start()
        pend_ref[0] = nsub

    @pl.when((w == pl.num_programs(0) - 1) & (j == nj - 1))
    def _():
        drain_output()


def _moe_experts(wi_e, wi_row, wi_nsub, wi_tok, xn_all, w_gu, b_gu, w_dn, b_dn, n_slots):
    n_wi = wi_e.shape[0]
    nj = D_EXPERT // MOE_TN
    def col(w, j, n):
        return jnp.where(n[w] > 0, j, nj - 1)

    grid_spec = pltpu.PrefetchScalarGridSpec(
        num_scalar_prefetch=3,
        grid=(n_wi, nj),
        in_specs=[
            pl.BlockSpec((1, 1, MOE_TM), lambda w, j, e, r, n: (w, 0, 0), memory_space=pltpu.SMEM),
            pl.BlockSpec((1, 1, MOE_TM), lambda w, j, e, r, n: (jnp.minimum(w + 1, n_wi - 1), 0, 0),
                         memory_space=pltpu.SMEM),
            pl.BlockSpec(memory_space=pl.ANY),
            pl.BlockSpec((1, D_MODEL, MOE_TN), lambda w, j, e, r, n: (e[w], 0, col(w, j, n))),
            pl.BlockSpec((1, D_MODEL, MOE_TN), lambda w, j, e, r, n: (e[w], 0, nj + col(w, j, n))),
            pl.BlockSpec((1, 1, MOE_TN), lambda w, j, e, r, n: (e[w], 0, col(w, j, n))),
            pl.BlockSpec((1, 1, MOE_TN), lambda w, j, e, r, n: (e[w], 0, nj + col(w, j, n))),
            pl.BlockSpec((1, MOE_TN, D_MODEL), lambda w, j, e, r, n: (e[w], col(w, j, n), 0)),
            pl.BlockSpec((1, 1, D_MODEL), lambda w, j, e, r, n: (e[w], 0, 0)),
        ],
        out_specs=pl.BlockSpec(memory_space=pl.ANY),
        scratch_shapes=[
            pltpu.VMEM((2, MOE_TM, D_MODEL), F32),
            pltpu.VMEM((MOE_TM, D_MODEL), F32),
            pltpu.VMEM((D_MODEL, MOE_TN), BF16),
            pltpu.VMEM((D_MODEL, MOE_TN), BF16),
            pltpu.VMEM((MOE_TN, D_MODEL), BF16),
            pltpu.SMEM((1,), I32),
            pltpu.SemaphoreType.DMA(()),
            pltpu.SemaphoreType.DMA(()),
        ],
    )
    return pl.pallas_call(
        _moe_kernel,
        grid_spec=grid_spec,
        out_shape=jax.ShapeDtypeStruct((n_slots, D_MODEL), F32),
        compiler_params=_cparams(("arbitrary", "arbitrary"), vmem=58 * 1024 * 1024),
        name="moe_experts",
    )(wi_e, wi_row, wi_nsub, wi_tok, wi_tok, xn_all, w_gu, w_gu,
      b_gu.reshape(N_EXPERTS, 1, 2 * D_EXPERT), b_gu.reshape(N_EXPERTS, 1, 2 * D_EXPERT),
      w_dn, b_dn.reshape(N_EXPERTS, 1, D_MODEL))


def _combine_kernel(dest_ref, gt_ref, h_ref, ys_hbm, o_ref, buf, sem, *, tt):
    def issue(r, carry):
        for k in range(TOP_K):
            slot = dest_ref[0, 0, r * TOP_K + k]
            pltpu.make_async_copy(ys_hbm.at[pl.ds(slot, 1)], buf.at[k, pl.ds(r, 1)], sem).start()
        return carry
    lax.fori_loop(0, tt, issue, 0, unroll=4)
    for k in range(TOP_K):
        pltpu.make_async_copy(ys_hbm.at[pl.ds(0, tt)], buf.at[k], sem).wait()
    gt = gt_ref[...]
    y = gt[:, 0:1] * buf[0]
    for k in range(1, TOP_K):
        y = y + gt[:, k:k + 1] * buf[k]
    o_ref[...] = h_ref[...] + y


def _combine(dest, gates, h1, ys, tt):
    m = h1.shape[0]
    nt = m // tt
    return pl.pallas_call(
        functools.partial(_combine_kernel, tt=tt),
        grid=(nt,),
        in_specs=[
            pl.BlockSpec((1, 1, tt * TOP_K), lambda i: (i, 0, 0), memory_space=pltpu.SMEM),
            pl.BlockSpec((tt, 128), lambda i: (i, 0)),
            pl.BlockSpec((tt, D_MODEL), lambda i: (i, 0)),
            pl.BlockSpec(memory_space=pl.ANY),
        ],
        out_specs=pl.BlockSpec((tt, D_MODEL), lambda i: (i, 0)),
        out_shape=jax.ShapeDtypeStruct((m, D_MODEL), F32),
        scratch_shapes=[pltpu.VMEM((TOP_K, tt, D_MODEL), F32), pltpu.SemaphoreType.DMA(())],
        compiler_params=_cparams(("arbitrary",)),
        name="moe_combine",
    )(dest.reshape(nt, 1, tt * TOP_K), gates, h1, ys)


def _routing_tables(top_idx, rank, counts, n_wi, n_slots):
    n_tok = top_idx.shape[0]
    padded = (counts + MOE_SUB - 1) // MOE_SUB * MOE_SUB
    pad_start = jnp.cumsum(padded) - padded
    dest = pad_start[top_idx] + rank
    tok_ids = jnp.broadcast_to(jnp.arange(n_tok, dtype=I32)[:, None], (n_tok, TOP_K))
    slot_tok = jnp.zeros((n_slots,), I32).at[dest.reshape(-1)].set(
        tok_ids.reshape(-1), unique_indices=True)
    nsubs = padded // MOE_SUB
    items = (nsubs + MOE_NSUB - 1) // MOE_NSUB
    item_end = jnp.cumsum(items)
    total = item_end[-1]
    w = jnp.arange(n_wi, dtype=I32)
    w_eff = jnp.minimum(w, total - 1)
    wi_e = jnp.sum((item_end[None, :] <= w_eff[:, None]).astype(I32), axis=1)
    wi_e = jnp.minimum(wi_e, N_EXPERTS - 1)
    local = w_eff - (item_end - items)[wi_e]
    wi_row = (pad_start[wi_e] + local * MOE_TM).astype(I32)
    wi_nsub = jnp.where(w < total, jnp.clip(nsubs[wi_e] - local * MOE_NSUB, 0, MOE_NSUB), 0).astype(I32)
    tok_pos = jnp.minimum(wi_row[:, None] + jnp.arange(MOE_TM, dtype=I32)[None, :], n_slots - 1)
    wi_tok = slot_tok[tok_pos].reshape(n_wi, 1, MOE_TM)
    return dest.astype(I32), wi_e, wi_row, wi_nsub, wi_tok


def kernel(x_prompt, x_sample, cache_k, cache_v, cache_logf, state_pool, page_table,
           norm1_w, w_in, b_forget, q_norm_w, k_norm_w, w_pool, pool_scale,
           w_branch_a, w_branch_b, w_out, norm2_w, router_w, router_b,
           w_gu, b_gu, w_dn, b_dn):
    depth = w_in.shape[0]
    assert depth == 1
    bp, lp, _ = x_prompt.shape
    bs, ls, _ = x_sample.shape
    assert ls == 8 and N_HEADS == 8
    n_p, n_s = bp * lp, bs * ls
    n_all = n_p + n_s
    n_past = page_table.shape[1] * PAGE_SIZE

    wt = jnp.transpose(w_in[0])
    f_off = 3 * ATTN_WIDTH
    w_main = jnp.concatenate([wt[:f_off], wt[f_off + N_HEADS:]], axis=0).astype(BF16)
    w_f = jnp.pad(wt[f_off:f_off + N_HEADS], ((0, FORGET_LANES - N_HEADS), (0, 0))).astype(BF16)
    b_f = jnp.pad(b_forget[0], (0, FORGET_LANES - N_HEADS)).reshape(1, FORGET_LANES)
    n1w = norm1_w[0].reshape(1, D_MODEL)
    n2w = norm2_w[0].reshape(1, D_MODEL)
    qnw = jnp.tile(q_norm_w[0], N_HEADS).reshape(1, ATTN_WIDTH)
    knw = jnp.tile(k_norm_w[0], N_HEADS).reshape(1, ATTN_WIDTH)
    wp = w_pool[0].astype(BF16)
    ps = pool_scale[0].reshape(1, POOL_WIDTH)
    wa = w_branch_a[0].astype(BF16)
    wb = w_branch_b[0].astype(BF16)
    wo = w_out[0].astype(BF16)
    rw = jnp.pad(router_w[0], ((0, 0), (0, 128 - N_EXPERTS)))
    rb = jnp.pad(router_b[0], (0, 128 - N_EXPERTS), constant_values=NEG_BIG).reshape(1, 128)

    xp = x_prompt.reshape(n_p, D_MODEL)
    qp, kp, vp, up, gp, lfp = _inproj(xp, n1w, w_main, w_f, b_f, qnw, knw, tm=512,
                                      q_scale=ATTN_SCALE * LOG2E)
    cp = _cumsum_rows(lfp, bl=256, seg_len=lp)
    cp3 = cp.reshape(bp, lp, FORGET_LANES)
    attn_p = _prompt_attention(qp.reshape(bp, lp, ATTN_WIDTH), kp.reshape(bp, lp, ATTN_WIDTH),
                               vp.reshape(bp, lp, ATTN_WIDTH), cp3, _key_bias_pieces(cp3))
    up3 = up.reshape(bp, lp, POOL_WIDTH)
    halo_p = pl.BlockSpec((1, POOL_HALO, POOL_WIDTH),
                          lambda bi, i: (bi, jnp.maximum(i * (512 // POOL_HALO) - 1, 0), 0))
    pool_p = _pool_mix(up3, halo_p, up3, wp, ps, tm=512, base_pos=0, zero_first_halo=True)
    h1_p, xn_all, ti_p, gt_p = _merge(
        attn_p.reshape(n_p, ATTN_WIDTH), pool_p.reshape(n_p, POOL_WIDTH), gp, xp,
        wa, wb, wo, n2w, rw, rb, tm=256, xn_all=None, xn_row0=0, n_all=n_all)

    xs = x_sample.reshape(n_s, D_MODEL)
    qs, ks, vs, us, gs, lfs = _inproj(xs, n1w, w_main, w_f, b_f, qnw, knw, tm=n_s, q_scale=1.0)
    cs = _cumsum_rows(lfs, bl=n_s, seg_len=ls)
    q_s = qs.astype(F32).reshape(bs, ls, N_HEADS, HEAD_DIM)
    qbd = jnp.einsum('bthd,hg->bhtgd', q_s, jnp.eye(N_HEADS, dtype=F32))
    qbd = qbd.reshape(bs, N_HEADS * ls, ATTN_WIDTH).astype(BF16)
    cn = jnp.transpose(cs.reshape(bs, ls, FORGET_LANES)[:, :, :N_HEADS], (0, 2, 1))
    cncol = cn.reshape(bs, N_HEADS * ls, 1)
    cnkeys = jnp.broadcast_to(cn[:, :, None, :], (bs, N_HEADS, ls, ls)).reshape(bs, N_HEADS * ls, ls)
    cnkeys = jnp.pad(cnkeys, ((0, 0), (0, 0), (0, PAGE_SIZE - ls)))
    knew = jnp.pad(ks.reshape(bs, ls, ATTN_WIDTH), ((0, 0), (0, PAGE_SIZE - ls), (0, 0)))
    vnew = jnp.pad(vs.reshape(bs, ls, ATTN_WIDTH), ((0, 0), (0, PAGE_SIZE - ls), (0, 0)))
    n_phys = cache_k.shape[1]
    ck = cache_k[0].reshape(n_phys, PAGE_SIZE * N_HEADS, HEAD_DIM)
    cv = cache_v[0].reshape(n_phys, PAGE_SIZE * N_HEADS, HEAD_DIM)
    clf_t = jnp.transpose(cache_logf[0], (0, 2, 1))
    attn_s = _sample_attention(page_table, qbd, cncol, cnkeys, knew, vnew, ck, cv, clf_t)
    halo_src = jnp.pad(state_pool[0], ((0, 0), (POOL_HALO - POOL_HIST, 0), (0, 0)))
    halo_s = pl.BlockSpec((1, POOL_HALO, POOL_WIDTH), lambda bi, i: (bi, 0, 0))
    us3 = us.reshape(bs, ls, POOL_WIDTH)
    pool_s = _pool_mix(halo_src, halo_s, us3, wp, ps, tm=ls, base_pos=n_past, zero_first_halo=False)
    h1_s, xn_all, ti_s, gt_s = _merge(
        attn_s.reshape(n_s, ATTN_WIDTH), pool_s.reshape(n_s, POOL_WIDTH), gs, xs,
        wa, wb, wo, n2w, rw, rb, tm=n_s, xn_all=xn_all, xn_row0=n_p, n_all=n_all)

    n_rank = pl.cdiv(n_all, 128) * 128
    ti_all = jnp.concatenate([ti_p, ti_s, jnp.full((n_rank - n_all, 128), -1, I32)], axis=0)
    rank, cnt = _expert_rank(ti_all)
    top_idx = ti_all[:n_all, :TOP_K]
    counts = cnt[0, :N_EXPERTS].astype(I32)
    n_blocks = pl.cdiv(n_all * TOP_K, MOE_SUB) + N_EXPERTS
    n_slots = n_blocks * MOE_SUB
    n_wi = n_blocks // MOE_NSUB + N_EXPERTS
    assert (n_blocks + (MOE_NSUB - 1) * N_EXPERTS) // MOE_NSUB < n_wi
    dest, wi_e, wi_row, wi_nsub, wi_tok = _routing_tables(
        top_idx, rank[:n_all, :TOP_K], counts, n_wi, n_slots)
    ys = _moe_experts(wi_e, wi_row, wi_nsub, wi_tok, xn_all, w_gu[0], b_gu[0], w_dn[0], b_dn[0], n_slots)
    y_p = _combine(dest[:n_p], gt_p, h1_p, ys, tt=128)
    y_s = _combine(dest[n_p:], gt_s, h1_s, ys, tt=n_s)

    heads = (N_HEADS, HEAD_DIM)
    k_p = kp.reshape(1, bp, lp, *heads)
    v_p = vp.reshape(1, bp, lp, *heads)
    f_p = lfp.reshape(bp, lp, FORGET_LANES)[None, :, :, :N_HEADS]
    u_p = up3[:, lp - POOL_HIST:][None]
    k_s = ks.reshape(1, bs, ls, *heads)
    v_s = vs.reshape(1, bs, ls, *heads)
    f_s = lfs.reshape(bs, ls, FORGET_LANES)[None, :, :, :N_HEADS]
    pool_state = jnp.concatenate([state_pool[0], us3], axis=1)[:, -POOL_HIST:][None]
    return (y_p.reshape(bp, lp, D_MODEL), y_s.reshape(bs, ls, D_MODEL),
            k_p, v_p, f_p, u_p, k_s, v_s, f_s, pool_state)
```

```python
import functools

import jax
import jax.numpy as jnp
from jax import lax
from jax.experimental import pallas as pl
from jax.experimental.pallas import tpu as pltpu

F32 = jnp.float32
BF16 = jnp.bfloat16
I32 = jnp.int32

D_MODEL = 2048
N_HEADS = 8
HEAD_DIM = 128
ATTN_WIDTH = N_HEADS * HEAD_DIM
ATTN_SCALE = HEAD_DIM ** -0.5
FORGET_LANES = 128
POOL_WINDOWS = (2, 4, 8, 16)
POOL_WIDTH = D_MODEL // 2
GW = POOL_WIDTH // len(POOL_WINDOWS)
POOL_HIST = max(POOL_WINDOWS) - 1
POOL_HALO = 16
N_EXPERTS = 32
TOP_K = 4
D_EXPERT = D_MODEL
SWIGLU_LIMIT = 7.0
SWIGLU_ALPHA = 1.702
NORM_EPS = 1e-6
PAGE_SIZE = 128
NEG_BIG = -1e30

ZQ, ZK, ZV, ZU = 0, 1, 2, 3
Z_WIDTH = 3 * ATTN_WIDTH + POOL_WIDTH + 2 * D_MODEL

MOE_SUB = 128
MOE_NSUB = 9
MOE_TM = MOE_SUB * MOE_NSUB
MOE_TK = 256
MOE_TN = 256
MOE_KS = D_MODEL // MOE_TK
MOE_NS = D_EXPERT // MOE_TN
MOE_UP_GROUPS = 4
MOE_DOWN_GROUPS = 2
assert MOE_TM % ((MOE_KS + MOE_NS) * MOE_UP_GROUPS) == 0
VMEM_LIMIT = 56 * 1024 * 1024

HIGHEST = lax.Precision.HIGHEST


def _cparams(sem, vmem=VMEM_LIMIT, flags=None):
    return pltpu.CompilerParams(dimension_semantics=sem, vmem_limit_bytes=vmem, flags=flags)


def _log_sigmoid(x):
    return jnp.minimum(x, 0.0) - jnp.log1p(jnp.exp(-jnp.abs(x)))


_NT = (((1,), (1,)), ((), ()))


def _inproj_kernel(x_ref, n1w_ref, w_ref, wf_ref, bf_ref, qnw_ref, knw_ref,
                   q_ref, k_ref, v_ref, u_ref, g_ref, lf_ref, xn_ref, *, q_scale):
    j = pl.program_id(1)

    @pl.when(j == 0)
    def _():
        x = x_ref[...]
        ms = jnp.mean(x * x, axis=-1, keepdims=True)
        xn = x * lax.rsqrt(ms + NORM_EPS) * n1w_ref[...]
        xn_ref[...] = xn.astype(BF16)
        zf = lax.dot_general(xn_ref[...], wf_ref[...], _NT, preferred_element_type=F32) + bf_ref[...]
        lf_ref[...] = _log_sigmoid(zf)

    def proj():
        return lax.dot_general(xn_ref[...], w_ref[...], _NT, preferred_element_type=F32)

    def head_norm(nw_ref, dst_ref, scale):
        z = proj()
        for h in range(N_HEADS):
            sl = slice(h * HEAD_DIM, (h + 1) * HEAD_DIM)
            zh = z[:, sl]
            ms = jnp.mean(zh * zh, axis=-1, keepdims=True)
            y = zh * lax.rsqrt(ms + NORM_EPS) * nw_ref[:, sl]
            if scale != 1.0:
                y = y * scale
            dst_ref[:, sl] = y.astype(dst_ref.dtype)

    @pl.when(j == ZQ)
    def _():
        head_norm(qnw_ref, q_ref, q_scale)

    @pl.when(j == ZK)
    def _():
        head_norm(knw_ref, k_ref, 1.0)

    @pl.when(j == ZV)
    def _():
        v_ref[...] = proj()

    @pl.when(j == ZU)
    def _():
        u_ref[...] = proj()

    @pl.when(j > ZU)
    def _():
        g_ref[...] = jax.nn.sigmoid(proj())


def _inproj(x, n1w, w_main_t, w_f_t, b_f, qnw, knw, tm, q_scale):
    m = x.shape[0]
    tn = ATTN_WIDTH
    row = lambda i, j: (i, 0)
    const = lambda i, j: (0, 0)
    return pl.pallas_call(
        functools.partial(_inproj_kernel, q_scale=q_scale),
        grid=(m // tm, Z_WIDTH // tn),
        in_specs=[
            pl.BlockSpec((tm, D_MODEL), row),
            pl.BlockSpec((1, D_MODEL), const),
            pl.BlockSpec((tn, D_MODEL), lambda i, j: (j, 0)),
            pl.BlockSpec((FORGET_LANES, D_MODEL), const),
            pl.BlockSpec((1, FORGET_LANES), const),
            pl.BlockSpec((1, tn), const),
            pl.BlockSpec((1, tn), const),
        ],
        out_specs=[
            pl.BlockSpec((tm, tn), row),
            pl.BlockSpec((tm, tn), row),
            pl.BlockSpec((tm, tn), row),
            pl.BlockSpec((tm, tn), row),
            pl.BlockSpec((tm, tn), lambda i, j: (i, jnp.maximum(j - (ZU + 1), 0))),
            pl.BlockSpec((tm, FORGET_LANES), row),
        ],
        out_shape=[
            jax.ShapeDtypeStruct((m, ATTN_WIDTH), BF16),
            jax.ShapeDtypeStruct((m, ATTN_WIDTH), F32),
            jax.ShapeDtypeStruct((m, ATTN_WIDTH), F32),
            jax.ShapeDtypeStruct((m, POOL_WIDTH), F32),
            jax.ShapeDtypeStruct((m, 2 * D_MODEL), F32),
            jax.ShapeDtypeStruct((m, FORGET_LANES), F32),
        ],
        scratch_shapes=[pltpu.VMEM((tm, D_MODEL), BF16)],
        compiler_params=_cparams(("arbitrary", "arbitrary")),
        name="inproj",
    )(x, n1w, w_main_t, w_f_t, b_f, qnw, knw)


def _cumsum_kernel(lf_ref, c_ref, carry_ref, *, bl, seg_len):
    i = pl.program_id(0)
    r = lax.broadcasted_iota(I32, (bl, bl), 0)
    c = lax.broadcasted_iota(I32, (bl, bl), 1)
    mask = c <= r
    if seg_len < bl:
        mask = mask & ((r // seg_len) == (c // seg_len))
    tri = mask.astype(F32)
    cs = jnp.dot(tri, lf_ref[...], preferred_element_type=F32, precision=HIGHEST)
    if seg_len > bl:
        @pl.when((i * bl) % seg_len == 0)
        def _():
            carry_ref[...] = jnp.zeros_like(carry_ref)
        cs = cs + carry_ref[...]
        carry_ref[...] = cs[bl - 1:bl, :]
    c_ref[...] = cs


def _cumsum_rows(lf, bl, seg_len):
    m = lf.shape[0]
    return pl.pallas_call(
        functools.partial(_cumsum_kernel, bl=bl, seg_len=seg_len),
        grid=(m // bl,),
        in_specs=[pl.BlockSpec((bl, FORGET_LANES), lambda i: (i, 0))],
        out_specs=pl.BlockSpec((bl, FORGET_LANES), lambda i: (i, 0)),
        out_shape=jax.ShapeDtypeStruct((m, FORGET_LANES), F32),
        scratch_shapes=[pltpu.VMEM((1, FORGET_LANES), F32)],
        compiler_params=_cparams(("arbitrary",)),
        name="logf_cumsum",
    )(lf)


LOG2E = 1.4426950408889634


def _pattn_kernel(q_ref, k_ref, v_ref, c_ref, kbias_ref, o_ref, *scratch, tq, ch):
    m_refs = scratch[0:N_HEADS]
    accl_refs = scratch[N_HEADS:2 * N_HEADS]
    kb_ref, vb_ref = scratch[2 * N_HEADS:]
    qi = pl.program_id(1)
    ki = pl.program_id(2)
    tk = tq

    @pl.when(ki == 0)
    def _():
        for h in range(N_HEADS):
            m_refs[h][...] = jnp.full_like(m_refs[h], -jnp.inf)
            accl_refs[h][...] = jnp.zeros_like(accl_refs[h])

    @pl.when(ki <= qi)
    def _():
        kb_ref[...] = k_ref[0].astype(BF16)
        vb_ref[...] = v_ref[0].astype(BF16)

    def block(diag):
        c2 = c_ref[0] * LOG2E
        q_ones = (lax.broadcasted_iota(I32, (ch, HEAD_DIM), 1) < 3).astype(BF16)
        v_ones = jnp.ones((tk, HEAD_DIM), BF16)
        below = (lax.broadcasted_iota(I32, (ch, ch), 1) <= lax.broadcasted_iota(I32, (ch, ch), 0))

        def where(h, r):
            rows = slice(r * ch, (r + 1) * ch)
            cols = slice(0, (r + 1) * ch if diag else tk)
            return rows, cols, slice(h * HEAD_DIM, (h + 1) * HEAD_DIM)

        def stage_scores(h, r):
            rows, cols, sl = where(h, r)
            q_cat = jnp.concatenate([q_ref[0, rows, sl], q_ones], axis=1)
            k_cat = jnp.concatenate([kb_ref[cols, sl], kbias_ref[0, h, cols, :]], axis=1)
            u = lax.dot_general(q_cat, k_cat, _NT, preferred_element_type=F32)
            if diag:
                last = jnp.where(below, u[:, r * ch:], -jnp.inf)
                u = last if r == 0 else jnp.concatenate([u[:, :r * ch], last], axis=1)
            return u

        def stage_max(h, r, u):
            rows, _, _ = where(h, r)
            c2t = c2[rows, h:h + 1]
            m_prev = m_refs[h][rows, :]
            m_new = jnp.maximum(m_prev, jnp.max(u, axis=-1, keepdims=True) + c2t)
            m_refs[h][rows, :] = m_new
            return jnp.exp2(m_prev - m_new), c2t - m_new[:, 0:1]

        def stage_accumulate(h, r, p, alpha):
            rows, cols, sl = where(h, r)
            v_cat = jnp.concatenate([vb_ref[cols, sl], v_ones[cols, :]], axis=1)
            pv = jnp.dot(p, v_cat, preferred_element_type=F32)
            accl_refs[h][rows, :] = jnp.concatenate([alpha, alpha], axis=1) * accl_refs[h][rows, :] + pv

        units = [(h, r) for r in range(tq // ch) for h in range(N_HEADS)]
        u_of, ab_of, p_of = {}, {}, {}
        for t in range(len(units) + 3):
            if t < len(units):
                u_of[t] = stage_scores(*units[t])
            if 0 <= t - 1 < len(units):
                ab_of[t - 1] = stage_max(*units[t - 1], u_of[t - 1])
            if 0 <= t - 2 < len(units):
                p_of[t - 2] = jnp.exp2(u_of.pop(t - 2) + ab_of[t - 2][1]).astype(BF16)
            if 0 <= t - 3 < len(units):
                stage_accumulate(*units[t - 3], p_of.pop(t - 3), ab_of.pop(t - 3)[0])

    @pl.when(ki < qi)
    def _():
        block(False)

    @pl.when(ki == qi)
    def _():
        block(True)
        for h in range(N_HEADS):
            sl = slice(h * HEAD_DIM, (h + 1) * HEAD_DIM)
            accl = accl_refs[h][...]
            o_ref[0, :, sl] = (accl[:, :HEAD_DIM] / accl[:, HEAD_DIM:]).astype(o_ref.dtype)


def _prompt_attention(q3, k3, v3, c3, kbias, tq=512, ch=128):
    b, l, _ = q3.shape
    tk = tq
    nq = l // tq
    kv_idx = lambda bi, qi, ki: (bi, jnp.minimum(ki, qi), 0)
    return pl.pallas_call(
        functools.partial(_pattn_kernel, tq=tq, ch=ch),
        grid=(b, nq, nq),
        in_specs=[
            pl.BlockSpec((1, tq, ATTN_WIDTH), lambda bi, qi, ki: (bi, qi, 0)),
            pl.BlockSpec((1, tk, ATTN_WIDTH), kv_idx),
            pl.BlockSpec((1, tk, ATTN_WIDTH), kv_idx),
            pl.BlockSpec((1, tq, FORGET_LANES), lambda bi, qi, ki: (bi, qi, 0)),
            pl.BlockSpec((1, N_HEADS, tk, HEAD_DIM),
                         lambda bi, qi, ki: (bi, 0, jnp.minimum(ki, qi), 0)),
        ],
        out_specs=pl.BlockSpec((1, tq, ATTN_WIDTH), lambda bi, qi, ki: (bi, qi, 0)),
        out_shape=jax.ShapeDtypeStruct((b, l, ATTN_WIDTH), BF16),
        scratch_shapes=(
            [pltpu.VMEM((tq, HEAD_DIM), F32) for _ in range(N_HEADS)]
            + [pltpu.VMEM((tq, 2 * HEAD_DIM), F32) for _ in range(N_HEADS)]
            + [pltpu.VMEM((tk, ATTN_WIDTH), BF16), pltpu.VMEM((tk, ATTN_WIDTH), BF16)]),
        compiler_params=_cparams(("arbitrary", "arbitrary", "arbitrary")),
        name="prompt_attention",
    )(q3, k3, v3, c3, kbias)


def _bf16_pieces(x, n):
    out = []
    for _ in range(n):
        piece = x.astype(BF16).astype(F32)
        out.append(piece)
        x = x - piece
    return out


def _kbias_kernel(c_ref, o_ref):
    c = c_ref[0] * (-LOG2E)
    lane = lax.broadcasted_iota(I32, c.shape, 1)
    for h in range(N_HEADS):
        hi, mid, lo = _bf16_pieces(c[:, h:h + 1], 3)
        tile = jnp.where(lane == 0, hi, jnp.where(lane == 1, mid, jnp.where(lane == 2, lo, 0.0)))
        o_ref[0, h] = tile.astype(BF16)


def _key_bias_pieces(c3, tl=512):
    b, l, _ = c3.shape
    return pl.pallas_call(
        _kbias_kernel,
        grid=(b, l // tl),
        in_specs=[pl.BlockSpec((1, tl, FORGET_LANES), lambda bi, i: (bi, i, 0))],
        out_specs=pl.BlockSpec((1, N_HEADS, tl, HEAD_DIM), lambda bi, i: (bi, 0, i, 0)),
        out_shape=jax.ShapeDtypeStruct((b, N_HEADS, l, HEAD_DIM), BF16),
        compiler_params=_cparams(("arbitrary", "arbitrary")),
        name="key_bias",
    )(c3)


def _heads_to_lanes(ref):
    parts = [ref[0, pl.ds(h, PAGE_SIZE, stride=N_HEADS), :] for h in range(N_HEADS)]
    return jnp.concatenate(parts, axis=1).astype(BF16)


def _rows_per_head(x):
    return jnp.concatenate(
        [jnp.broadcast_to(x[h:h + 1, :], (8, x.shape[1])) for h in range(N_HEADS)], axis=0)


def _sattn_kernel(pt_ref, qbd_ref, cncol_ref, cnkeys_ref, knew_ref, vnew_ref, *rest, g_pages):
    del pt_ref
    k_refs = rest[0:g_pages]
    v_refs = rest[g_pages:2 * g_pages]
    lf_refs = rest[2 * g_pages:3 * g_pages]
    o_ref = rest[3 * g_pages]
    m_ref, l_ref, acc_ref, carry_ref = rest[3 * g_pages + 1:]
    jj = pl.program_id(1)
    nj = pl.num_programs(1)
    rows = N_HEADS * 8

    @pl.when(jj == 0)
    def _():
        m_ref[...] = jnp.full_like(m_ref, -jnp.inf)
        l_ref[...] = jnp.zeros_like(l_ref)
        acc_ref[...] = jnp.zeros_like(acc_ref)
        carry_ref[...] = jnp.zeros_like(carry_ref)

    qbd = qbd_ref[0]
    cncol = cncol_ref[0]
    later = (lax.broadcasted_iota(I32, (PAGE_SIZE, PAGE_SIZE), 0)
             > lax.broadcasted_iota(I32, (PAGE_SIZE, PAGE_SIZE), 1)).astype(F32)

    def scores(kcat):
        return lax.dot_general(qbd, kcat, (((1,), (1,)), ((), ())),
                               preferred_element_type=F32) * ATTN_SCALE

    def update(s, vcat, m, l, acc):
        m_new = jnp.maximum(m, jnp.max(s, axis=-1, keepdims=True))
        alpha = jnp.exp(m - m_new)
        p = jnp.exp(s - jnp.concatenate([m_new] * (s.shape[1] // PAGE_SIZE), axis=1))
        l = alpha * l + jnp.sum(p, axis=-1, keepdims=True)
        pv = jnp.dot(p.astype(BF16), vcat, preferred_element_type=F32)
        acc = jnp.concatenate([alpha] * N_HEADS, axis=1) * acc + pv
        return m_new, l, acc

    m, l, acc, carry = m_ref[...], l_ref[...], acc_ref[...], carry_ref[...]
    lf_all = jnp.concatenate([_rows_per_head(lf_refs[g][0]) for g in range(g_pages)], axis=0)
    suffix_all = jnp.dot(lf_all, later, preferred_element_type=F32, precision=HIGHEST)
    bias = []
    for g in range(g_pages):
        bias.append(suffix_all[g * rows:(g + 1) * rows, :] + carry)
        carry = carry + jnp.sum(lf_all[g * rows:(g + 1) * rows, :], axis=-1, keepdims=True)
    kcat = jnp.concatenate([_heads_to_lanes(k_refs[g]) for g in range(g_pages)], axis=0)
    vcat = jnp.concatenate([_heads_to_lanes(v_refs[g]) for g in range(g_pages)], axis=0)
    s = scores(kcat) + cncol + jnp.concatenate(bias, axis=1)
    m, l, acc = update(s, vcat, m, l, acc)

    @pl.when(jj < nj - 1)
    def _():
        m_ref[...] = m
        l_ref[...] = l
        acc_ref[...] = acc
        carry_ref[...] = carry

    @pl.when(jj == nj - 1)
    def _():
        s = scores(knew_ref[0].astype(BF16)) + cncol - cnkeys_ref[0]
        t_row = lax.broadcasted_iota(I32, (rows, PAGE_SIZE), 0) % 8
        s_col = lax.broadcasted_iota(I32, (rows, PAGE_SIZE), 1)
        s = jnp.where(s_col <= t_row, s, -jnp.inf)
        _, l2, acc2 = update(s, vnew_ref[0].astype(BF16), m, l, acc)
        o = acc2 / jnp.concatenate([l2] * N_HEADS, axis=1)
        for h in range(N_HEADS):
            sl = slice(h * HEAD_DIM, (h + 1) * HEAD_DIM)
            o_ref[0, :, sl] = o[h * 8:(h + 1) * 8, sl].astype(o_ref.dtype)


def _sample_attention(page_table, qbd, cncol, cnkeys, knew, vnew, ck, cv, clf_t, g_pages=8):
    bd, n_pages = page_table.shape
    rows = N_HEADS * 8
    pt_flat = page_table.reshape(-1)

    def page_idx(g):
        return lambda b, jj, pt: (pt[b * n_pages + n_pages - 1 - (jj * g_pages + g)], 0, 0)

    per_b = lambda b, jj, pt: (b, 0, 0)
    in_specs = [
        pl.BlockSpec((1, rows, ATTN_WIDTH), per_b),
        pl.BlockSpec((1, rows, 1), per_b),
        pl.BlockSpec((1, rows, PAGE_SIZE), per_b),
        pl.BlockSpec((1, PAGE_SIZE, ATTN_WIDTH), per_b),
        pl.BlockSpec((1, PAGE_SIZE, ATTN_WIDTH), per_b),
    ]
    in_specs += [pl.BlockSpec((1, PAGE_SIZE * N_HEADS, HEAD_DIM), page_idx(g)) for g in range(g_pages)]
    in_specs += [pl.BlockSpec((1, PAGE_SIZE * N_HEADS, HEAD_DIM), page_idx(g)) for g in range(g_pages)]
    in_specs += [pl.BlockSpec((1, N_HEADS, PAGE_SIZE), page_idx(g)) for g in range(g_pages)]
    grid_spec = pltpu.PrefetchScalarGridSpec(
        num_scalar_prefetch=1,
        grid=(bd, n_pages // g_pages),
        in_specs=in_specs,
        out_specs=pl.BlockSpec((1, 8, ATTN_WIDTH), per_b),
        scratch_shapes=[
            pltpu.VMEM((rows, PAGE_SIZE), F32),
            pltpu.VMEM((rows, PAGE_SIZE), F32),
            pltpu.VMEM((rows, ATTN_WIDTH), F32),
            pltpu.VMEM((rows, PAGE_SIZE), F32),
        ],
    )
    return pl.pallas_call(
        functools.partial(_sattn_kernel, g_pages=g_pages),
        grid_spec=grid_spec,
        out_shape=jax.ShapeDtypeStruct((bd, 8, ATTN_WIDTH), BF16),
        compiler_params=_cparams(("arbitrary", "arbitrary")),
        name="sample_attention",
    )(pt_flat, qbd, cncol, cnkeys, knew, vnew,
      *([ck] * g_pages), *([cv] * g_pages), *([clf_t] * g_pages))


def _pool_kernel(halo_ref, u_ref, wp_ref, ps_ref, o_ref, *, tm, base_pos, zero_first_halo):
    i = pl.program_id(1)
    halo = halo_ref[0]
    if zero_first_halo:
        halo = jnp.where(i == 0, 0.0, halo)
    u = u_ref[0]
    ext = jnp.concatenate([halo, u], axis=0)
    pos = base_pos + i * tm + lax.broadcasted_iota(I32, (tm, 1), 0)
    for g, w in enumerate(POOL_WINDOWS):
        sl = slice(g * GW, (g + 1) * GW)
        s = ext[:, sl]
        k = 1
        while k < w:
            s = s + pltpu.roll(s, shift=k, axis=0)
            k *= 2
        wsum = s[POOL_HALO:, :]
        count = jnp.minimum(pos + 1, w).astype(F32)
        mixed = wsum / count - u[:, sl]
        y = jnp.dot(mixed.astype(BF16), wp_ref[g], preferred_element_type=F32)
        o_ref[0, :, sl] = (y * ps_ref[:, sl]).astype(o_ref.dtype)


def _pool_mix(halo_src, halo_spec, u_src, wp, ps, tm, base_pos, zero_first_halo):
    b, l, _ = u_src.shape
    return pl.pallas_call(
        functools.partial(_pool_kernel, tm=tm, base_pos=base_pos, zero_first_halo=zero_first_halo),
        grid=(b, l // tm),
        in_specs=[
            halo_spec,
            pl.BlockSpec((1, tm, POOL_WIDTH), lambda bi, i: (bi, i, 0)),
            pl.BlockSpec((len(POOL_WINDOWS), GW, GW), lambda bi, i: (0, 0, 0)),
            pl.BlockSpec((1, POOL_WIDTH), lambda bi, i: (0, 0)),
        ],
        out_specs=pl.BlockSpec((1, tm, POOL_WIDTH), lambda bi, i: (bi, i, 0)),
        out_shape=jax.ShapeDtypeStruct((b, l, POOL_WIDTH), BF16),
        compiler_params=_cparams(("arbitrary", "arbitrary")),
        name="pool_mix",
    )(halo_src, u_src, wp, ps)


def _merge_kernel(attn_ref, pool_ref, ga_ref, gb_ref, x_ref, wa_ref, wb_ref, wo_ref,
                  n2w_ref, rw_ref, rb_ref, *rest, tm, aliased):
    h_ref, xn_ref, ti_ref, gt_ref, rwp_ref = rest[-5:]

    @pl.when(pl.program_id(0) == 0)
    def _():
        for n, piece in enumerate(_bf16_pieces(rw_ref[...], 3)):
            rwp_ref[n] = piece.astype(BF16)

    a = jnp.dot(attn_ref[...], wa_ref[...], preferred_element_type=F32)
    p = jnp.dot(pool_ref[...], wb_ref[...], preferred_element_type=F32)
    merged = ga_ref[...] * a + gb_ref[...] * p
    o = jnp.dot(merged.astype(BF16), wo_ref[...], preferred_element_type=F32)
    h1 = x_ref[...] + o
    h_ref[...] = h1
    ms = jnp.mean(h1 * h1, axis=-1, keepdims=True)
    xn = h1 * lax.rsqrt(ms + NORM_EPS) * n2w_ref[...]
    xn_ref[...] = xn
    x_hi, x_lo = [v.astype(BF16) for v in _bf16_pieces(xn, 2)]
    logits = rb_ref[...]
    for xp, wn in ((x_hi, 0), (x_hi, 1), (x_lo, 0), (x_hi, 2), (x_lo, 1)):
        logits = logits + jnp.dot(xp, rwp_ref[wn], preferred_element_type=F32)
    lane = lax.broadcasted_iota(I32, (tm, 128), 1).astype(F32)
    vals, idxs = [], []
    work = logits
    for _ in range(TOP_K):
        mv = jnp.max(work, axis=-1, keepdims=True)
        ix = jnp.min(jnp.where(work == mv, lane, 128.0), axis=-1, keepdims=True)
        vals.append(mv)
        idxs.append(ix)
        work = jnp.where(lane == ix, NEG_BIG, work)
    es = [jnp.exp(v - vals[0]) for v in vals]
    den = es[0] + es[1] + es[2] + es[3]
    ti = jnp.zeros((tm, 128), F32)
    gt = jnp.zeros((tm, 128), F32)
    for k in range(TOP_K):
        ti = jnp.where(lane == float(k), idxs[k], ti)
        gt = jnp.where(lane == float(k), es[k] / den, gt)
    ti_ref[...] = ti.astype(I32)
    gt_ref[...] = gt


def _merge(attn, pool, z, x, wa, wb, wo, n2w, rw, rb, tm, xn_all, xn_row0, n_all):
    m = x.shape[0]
    aliased = xn_all is not None
    const = lambda i: (0, 0)
    single = dict(pipeline_mode=pl.Buffered(1))
    xn_blk0 = xn_row0 // tm
    in_specs = [
        pl.BlockSpec((tm, ATTN_WIDTH), lambda i: (i, 0)),
        pl.BlockSpec((tm, POOL_WIDTH), lambda i: (i, 0)),
        pl.BlockSpec((tm, D_MODEL), lambda i: (i, 0)),
        pl.BlockSpec((tm, D_MODEL), lambda i: (i, 1)),
        pl.BlockSpec((tm, D_MODEL), lambda i: (i, 0)),
        pl.BlockSpec((ATTN_WIDTH, D_MODEL), const, **single),
        pl.BlockSpec((POOL_WIDTH, D_MODEL), const, **single),
        pl.BlockSpec((D_MODEL, D_MODEL), const, **single),
        pl.BlockSpec((1, D_MODEL), const),
        pl.BlockSpec((D_MODEL, 128), const),
        pl.BlockSpec((1, 128), const),
    ]
    args = [attn, pool, z, z, x, wa, wb, wo, n2w, rw, rb]
    io_alias = {}
    if aliased:
        in_specs.append(pl.BlockSpec(memory_space=pl.ANY))
        args.append(xn_all)
        io_alias = {len(args) - 1: 1}
    return pl.pallas_call(
        functools.partial(_merge_kernel, tm=tm, aliased=aliased),
        grid=(m // tm,),
        in_specs=in_specs,
        out_specs=[
            pl.BlockSpec((tm, D_MODEL), lambda i: (i, 0)),
            pl.BlockSpec((tm, D_MODEL), lambda i: (xn_blk0 + i, 0)),
            pl.BlockSpec((tm, 128), lambda i: (i, 0)),
            pl.BlockSpec((tm, 128), lambda i: (i, 0)),
        ],
        out_shape=[
            jax.ShapeDtypeStruct((m, D_MODEL), F32),
            jax.ShapeDtypeStruct((n_all, D_MODEL), F32),
            jax.ShapeDtypeStruct((m, 128), I32),
            jax.ShapeDtypeStruct((m, 128), F32),
        ],
        input_output_aliases=io_alias,
        scratch_shapes=[pltpu.VMEM((3, D_MODEL, 128), BF16)],
        compiler_params=_cparams(("arbitrary",)),
        name="merge_router",
    )(*args)


def _rank_kernel(idx_ref, rank_ref, cnt_ref, carry_ref, *, tr):
    i = pl.program_id(0)

    @pl.when(i == 0)
    def _():
        carry_ref[...] = jnp.zeros_like(carry_ref)

    idx = idx_ref[...]
    lane = lax.broadcasted_iota(I32, (tr, 128), 1)
    hits = [idx[:, k:k + 1] == lane for k in range(TOP_K)]
    onehot = jnp.zeros((tr, 128), F32)
    for hk in hits:
        onehot = onehot + hk.astype(F32)
    earlier = (lax.broadcasted_iota(I32, (tr, tr), 1)
               < lax.broadcasted_iota(I32, (tr, tr), 0)).astype(BF16)
    before = jnp.dot(earlier, onehot.astype(BF16), preferred_element_type=F32) + carry_ref[...]
    out = jnp.zeros((tr, 128), I32)
    for k, hk in enumerate(hits):
        rk = jnp.sum(jnp.where(hk, before, 0.0), axis=-1, keepdims=True)
        out = jnp.where(lane == k, rk.astype(I32), out)
    rank_ref[...] = out
    carry_ref[...] = carry_ref[...] + jnp.sum(onehot, axis=0, keepdims=True)
    cnt_ref[...] = carry_ref[...]


def _expert_rank(top_idx_padded, tr=128):
    n = top_idx_padded.shape[0]
    return pl.pallas_call(
        functools.partial(_rank_kernel, tr=tr),
        grid=(n // tr,),
        in_specs=[pl.BlockSpec((tr, 128), lambda i: (i, 0))],
        out_specs=[
            pl.BlockSpec((tr, 128), lambda i: (i, 0)),
            pl.BlockSpec((1, 128), lambda i: (0, 0)),
        ],
        out_shape=[
            jax.ShapeDtypeStruct((n, 128), I32),
            jax.ShapeDtypeStruct((1, 128), F32),
        ],
        scratch_shapes=[pltpu.VMEM((1, 128), F32)],
        compiler_params=_cparams(("arbitrary",)),
        name="expert_rank",
    )(top_idx_padded)


def _moe_out_block_copy(acc_scr, ys_hbm, sub, row, sem):
    return pltpu.make_async_copy(acc_scr.at[pl.ds(sub * MOE_SUB, MOE_SUB)],
                                 ys_hbm.at[pl.ds(row, MOE_SUB)], sem)


def _moe_kernel(wi_e_ref, wi_row_ref, wi_nsub_ref, tok_ref, tok_next_ref, xn_hbm,
                wgu_ref, bgu_ref, wd_ref, bd_ref, ys_hbm,
                x_scr, hdn_scr, pend_ref, sem_in, sem_out):
    del wi_e_ref
    w = pl.program_id(0)
    j = pl.program_id(1)
    nj = pl.num_programs(1)
    nsub = wi_nsub_ref[w]
    row0 = wi_row_ref[w]
    cur = w % 2
    acc_scr = x_scr.at[cur]
    rows_per_step = MOE_TM // (MOE_KS + MOE_NS)

    def issue_rows(tok_tbl, slot, first, count):
        for i in range(count):
            r = first + i
            tok = tok_tbl[0, 0, r]
            pltpu.make_async_copy(xn_hbm.at[pl.ds(tok, 1)], x_scr.at[slot, pl.ds(r, 1)], sem_in).start()

    @pl.when((w == 0) & (j == 0))
    def _():
        pend_ref[0] = 0

    @pl.when((w == 0) & (j == 0) & (nsub > 0))
    def _():
        def issue(blk, carry):
            issue_rows(tok_ref, 0, blk * 8, 8)
            return carry
        lax.fori_loop(0, MOE_TM // 8, issue, 0)

    def drain_output():
        for s in range(MOE_NSUB):
            @pl.when(s < pend_ref[0])
            def _():
                _moe_out_block_copy(acc_scr, ys_hbm, s, 0, sem_out).wait()
        pend_ref[0] = 0

    @pl.when(j == 0)
    def _():
        drain_output()

    rows_in_flight = jnp.where(w == 0, nsub > 0, wi_nsub_ref[jnp.maximum(w - 1, 0)] > 0)

    @pl.when((j == 0) & rows_in_flight)
    def _():
        for s in range(MOE_NSUB):
            pltpu.make_async_copy(xn_hbm.at[pl.ds(0, MOE_SUB)],
                                  x_scr.at[cur, pl.ds(s * MOE_SUB, MOE_SUB)], sem_in).wait()

    @pl.when((j == 0) & (nsub > 0))
    def _():
        hdn_scr[...] = jnp.broadcast_to(bgu_ref[0], (MOE_TM, 2 * D_EXPERT))

    def up_step(k):
        xk = x_scr[cur, :, k * MOE_TK:(k + 1) * MOE_TK].astype(BF16)
        share = rows_per_step // MOE_UP_GROUPS
        width = 2 * D_EXPERT // MOE_UP_GROUPS
        for c in range(MOE_UP_GROUPS):
            cols = slice(c * width, (c + 1) * width)
            issue_rows(tok_next_ref, 1 - cur, j * rows_per_step + c * share, share)
            hdn_scr[:, cols] += jnp.dot(xk, wgu_ref[0, :, cols].astype(BF16), preferred_element_type=F32)

    def down_step(n):
        gate = jnp.minimum(hdn_scr[:, n * MOE_TN:(n + 1) * MOE_TN], SWIGLU_LIMIT)
        up = jnp.clip(hdn_scr[:, D_EXPERT + n * MOE_TN:D_EXPERT + (n + 1) * MOE_TN],
                      -SWIGLU_LIMIT, SWIGLU_LIMIT)
        act = ((up + 1.0) * (gate * jax.nn.sigmoid(SWIGLU_ALPHA * gate))).astype(BF16)
        share = rows_per_step // MOE_DOWN_GROUPS
        width = D_MODEL // MOE_DOWN_GROUPS
        for c in range(MOE_DOWN_GROUPS):
            cols = slice(c * width, (c + 1) * width)
            issue_rows(tok_next_ref, 1 - cur, j * rows_per_step + c * share, share)
            y = jnp.dot(act, wd_ref[0, :, cols].astype(BF16), preferred_element_type=F32)
            if n == 0:
                acc_scr[:, cols] = y + bd_ref[0, :, cols]
            else:
                acc_scr[:, cols] += y

    for k in range(MOE_KS):
        @pl.when((j == k) & (nsub > 0))
        def _():
            up_step(k)

    for n in range(MOE_NS):
        @pl.when((j == MOE_KS + n) & (nsub > 0))
        def _():
            down_step(n)

    @pl.when((j == nj - 1) & (nsub > 0))
    def _():
        for s in range(MOE_NSUB):
            @pl.when(s < nsub)
            def _():
                _moe_out_block_copy(acc_scr, ys_hbm, s,
                                    pl.multiple_of(row0 + s * MOE_SUB, MOE_SUB), sem_out).start()
        pend_ref[0] = nsub

    @pl.when((w == pl.num_programs(0) - 1) & (j == nj - 1))
    def _():
        drain_output()


def _moe_experts(wi_e, wi_row, wi_nsub, wi_tok, xn_all, w_gu, b_gu, w_dn, b_dn, n_slots):
    n_wi = wi_e.shape[0]

    def up_blk(w, j, n):
        return jnp.where(n[w] > 0, jnp.minimum(j, MOE_KS - 1), MOE_KS - 1)

    def down_blk(w, j, n):
        return jnp.where(n[w] > 0, jnp.maximum(j - MOE_KS, 0), MOE_NS - 1)

    grid_spec = pltpu.PrefetchScalarGridSpec(
        num_scalar_prefetch=3,
        grid=(n_wi, MOE_KS + MOE_NS),
        in_specs=[
            pl.BlockSpec((1, 1, MOE_TM), lambda w, j, e, r, n: (w, 0, 0), memory_space=pltpu.SMEM),
            pl.BlockSpec((1, 1, MOE_TM), lambda w, j, e, r, n: (jnp.minimum(w + 1, n_wi - 1), 0, 0),
                         memory_space=pltpu.SMEM),
            pl.BlockSpec(memory_space=pl.ANY),
            pl.BlockSpec((1, MOE_TK, 2 * D_EXPERT), lambda w, j, e, r, n: (e[w], up_blk(w, j, n), 0)),
            pl.BlockSpec((1, 1, 2 * D_EXPERT), lambda w, j, e, r, n: (e[w], 0, 0)),
            pl.BlockSpec((1, MOE_TN, D_MODEL), lambda w, j, e, r, n: (e[w], down_blk(w, j, n), 0)),
            pl.BlockSpec((1, 1, D_MODEL), lambda w, j, e, r, n: (e[w], 0, 0)),
        ],
        out_specs=pl.BlockSpec(memory_space=pl.ANY),
        scratch_shapes=[
            pltpu.VMEM((2, MOE_TM, D_MODEL), F32),
            pltpu.VMEM((MOE_TM, 2 * D_EXPERT), F32),
            pltpu.SMEM((1,), I32),
            pltpu.SemaphoreType.DMA(()),
            pltpu.SemaphoreType.DMA(()),
        ],
    )
    return pl.pallas_call(
        _moe_kernel,
        grid_spec=grid_spec,
        out_shape=jax.ShapeDtypeStruct((n_slots, D_MODEL), F32),
        compiler_params=_cparams(("arbitrary", "arbitrary"), vmem=60 * 1024 * 1024),
        name="moe_experts",
    )(wi_e, wi_row, wi_nsub, wi_tok, wi_tok, xn_all, w_gu,
      b_gu.reshape(N_EXPERTS, 1, 2 * D_EXPERT), w_dn, b_dn.reshape(N_EXPERTS, 1, D_MODEL))


def _combine_kernel(dest_ref, gt_ref, h_ref, ys_hbm, o_ref, buf, sem, *, tt):
    def issue(r, carry):
        for k in range(TOP_K):
            slot = dest_ref[0, 0, r * TOP_K + k]
            pltpu.make_async_copy(ys_hbm.at[pl.ds(slot, 1)], buf.at[k, pl.ds(r, 1)], sem).start()
        return carry
    lax.fori_loop(0, tt, issue, 0, unroll=4)
    for k in range(TOP_K):
        pltpu.make_async_copy(ys_hbm.at[pl.ds(0, tt)], buf.at[k], sem).wait()
    gt = gt_ref[...]
    y = gt[:, 0:1] * buf[0]
    for k in range(1, TOP_K):
        y = y + gt[:, k:k + 1] * buf[k]
    o_ref[...] = h_ref[...] + y


def _combine(dest, gates, h1, ys, tt):
    m = h1.shape[0]
    nt = m // tt
    return pl.pallas_call(
        functools.partial(_combine_kernel, tt=tt),
        grid=(nt,),
        in_specs=[
            pl.BlockSpec((1, 1, tt * TOP_K), lambda i: (i, 0, 0), memory_space=pltpu.SMEM),
            pl.BlockSpec((tt, 128), lambda i: (i, 0)),
            pl.BlockSpec((tt, D_MODEL), lambda i: (i, 0)),
            pl.BlockSpec(memory_space=pl.ANY),
        ],
        out_specs=pl.BlockSpec((tt, D_MODEL), lambda i: (i, 0)),
        out_shape=jax.ShapeDtypeStruct((m, D_MODEL), F32),
        scratch_shapes=[pltpu.VMEM((TOP_K, tt, D_MODEL), F32), pltpu.SemaphoreType.DMA(())],
        compiler_params=_cparams(("arbitrary",)),
        name="moe_combine",
    )(dest.reshape(nt, 1, tt * TOP_K), gates, h1, ys)


def _routing_tables(top_idx, rank, counts, n_wi, n_slots):
    n_tok = top_idx.shape[0]
    padded = (counts + MOE_SUB - 1) // MOE_SUB * MOE_SUB
    pad_start = jnp.cumsum(padded) - padded
    dest = pad_start[top_idx] + rank
    tok_ids = jnp.broadcast_to(jnp.arange(n_tok, dtype=I32)[:, None], (n_tok, TOP_K))
    slot_tok = jnp.zeros((n_slots,), I32).at[dest.reshape(-1)].set(
        tok_ids.reshape(-1), unique_indices=True)
    nsubs = padded // MOE_SUB
    items = (nsubs + MOE_NSUB - 1) // MOE_NSUB
    item_end = jnp.cumsum(items)
    total = item_end[-1]
    w = jnp.arange(n_wi, dtype=I32)
    w_eff = jnp.minimum(w, total - 1)
    wi_e = jnp.sum((item_end[None, :] <= w_eff[:, None]).astype(I32), axis=1)
    wi_e = jnp.minimum(wi_e, N_EXPERTS - 1)
    local = w_eff - (item_end - items)[wi_e]
    wi_row = (pad_start[wi_e] + local * MOE_TM).astype(I32)
    wi_nsub = jnp.where(w < total, jnp.clip(nsubs[wi_e] - local * MOE_NSUB, 0, MOE_NSUB), 0).astype(I32)
    tok_pos = jnp.minimum(wi_row[:, None] + jnp.arange(MOE_TM, dtype=I32)[None, :], n_slots - 1)
    wi_tok = slot_tok[tok_pos].reshape(n_wi, 1, MOE_TM)
    return dest.astype(I32), wi_e, wi_row, wi_nsub, wi_tok


def kernel(x_prompt, x_sample, cache_k, cache_v, cache_logf, state_pool, page_table,
           norm1_w, w_in, b_forget, q_norm_w, k_norm_w, w_pool, pool_scale,
           w_branch_a, w_branch_b, w_out, norm2_w, router_w, router_b,
           w_gu, b_gu, w_dn, b_dn):
    depth = w_in.shape[0]
    assert depth == 1
    bp, lp, _ = x_prompt.shape
    bs, ls, _ = x_sample.shape
    assert ls == 8 and N_HEADS == 8
    n_p, n_s = bp * lp, bs * ls
    n_all = n_p + n_s
    n_past = page_table.shape[1] * PAGE_SIZE

    wt = jnp.transpose(w_in[0])
    f_off = 3 * ATTN_WIDTH
    w_main = jnp.concatenate([wt[:f_off], wt[f_off + N_HEADS:]], axis=0).astype(BF16)
    w_f = jnp.pad(wt[f_off:f_off + N_HEADS], ((0, FORGET_LANES - N_HEADS), (0, 0))).astype(BF16)
    b_f = jnp.pad(b_forget[0], (0, FORGET_LANES - N_HEADS)).reshape(1, FORGET_LANES)
    n1w = norm1_w[0].reshape(1, D_MODEL)
    n2w = norm2_w[0].reshape(1, D_MODEL)
    qnw = jnp.tile(q_norm_w[0], N_HEADS).reshape(1, ATTN_WIDTH)
    knw = jnp.tile(k_norm_w[0], N_HEADS).reshape(1, ATTN_WIDTH)
    wp = w_pool[0].astype(BF16)
    ps = pool_scale[0].reshape(1, POOL_WIDTH)
    wa = w_branch_a[0].astype(BF16)
    wb = w_branch_b[0].astype(BF16)
    wo = w_out[0].astype(BF16)
    rw = jnp.pad(router_w[0], ((0, 0), (0, 128 - N_EXPERTS)))
    rb = jnp.pad(router_b[0], (0, 128 - N_EXPERTS), constant_values=NEG_BIG).reshape(1, 128)

    xp = x_prompt.reshape(n_p, D_MODEL)
    qp, kp, vp, up, gp, lfp = _inproj(xp, n1w, w_main, w_f, b_f, qnw, knw, tm=512,
                                      q_scale=ATTN_SCALE * LOG2E)
    cp = _cumsum_rows(lfp, bl=256, seg_len=lp)
    cp3 = cp.reshape(bp, lp, FORGET_LANES)
    attn_p = _prompt_attention(qp.reshape(bp, lp, ATTN_WIDTH), kp.reshape(bp, lp, ATTN_WIDTH),
                               vp.reshape(bp, lp, ATTN_WIDTH), cp3, _key_bias_pieces(cp3))
    up3 = up.reshape(bp, lp, POOL_WIDTH)
    halo_p = pl.BlockSpec((1, POOL_HALO, POOL_WIDTH),
                          lambda bi, i: (bi, jnp.maximum(i * (512 // POOL_HALO) - 1, 0), 0))
    pool_p = _pool_mix(up3, halo_p, up3, wp, ps, tm=512, base_pos=0, zero_first_halo=True)
    h1_p, xn_all, ti_p, gt_p = _merge(
        attn_p.reshape(n_p, ATTN_WIDTH), pool_p.reshape(n_p, POOL_WIDTH), gp, xp,
        wa, wb, wo, n2w, rw, rb, tm=256, xn_all=None, xn_row0=0, n_all=n_all)

    xs = x_sample.reshape(n_s, D_MODEL)
    qs, ks, vs, us, gs, lfs = _inproj(xs, n1w, w_main, w_f, b_f, qnw, knw, tm=n_s, q_scale=1.0)
    cs = _cumsum_rows(lfs, bl=n_s, seg_len=ls)
    q_s = qs.astype(F32).reshape(bs, ls, N_HEADS, HEAD_DIM)
    qbd = jnp.einsum('bthd,hg->bhtgd', q_s, jnp.eye(N_HEADS, dtype=F32))
    qbd = qbd.reshape(bs, N_HEADS * ls, ATTN_WIDTH).astype(BF16)
    cn = jnp.transpose(cs.reshape(bs, ls, FORGET_LANES)[:, :, :N_HEADS], (0, 2, 1))
    cncol = cn.reshape(bs, N_HEADS * ls, 1)
    cnkeys = jnp.broadcast_to(cn[:, :, None, :], (bs, N_HEADS, ls, ls)).reshape(bs, N_HEADS * ls, ls)
    cnkeys = jnp.pad(cnkeys, ((0, 0), (0, 0), (0, PAGE_SIZE - ls)))
    knew = jnp.pad(ks.reshape(bs, ls, ATTN_WIDTH), ((0, 0), (0, PAGE_SIZE - ls), (0, 0)))
    vnew = jnp.pad(vs.reshape(bs, ls, ATTN_WIDTH), ((0, 0), (0, PAGE_SIZE - ls), (0, 0)))
    n_phys = cache_k.shape[1]
    ck = cache_k[0].reshape(n_phys, PAGE_SIZE * N_HEADS, HEAD_DIM)
    cv = cache_v[0].reshape(n_phys, PAGE_SIZE * N_HEADS, HEAD_DIM)
    clf_t = jnp.transpose(cache_logf[0], (0, 2, 1))
    attn_s = _sample_attention(page_table, qbd, cncol, cnkeys, knew, vnew, ck, cv, clf_t)
    halo_src = jnp.pad(state_pool[0], ((0, 0), (POOL_HALO - POOL_HIST, 0), (0, 0)))
    halo_s = pl.BlockSpec((1, POOL_HALO, POOL_WIDTH), lambda bi, i: (bi, 0, 0))
    us3 = us.reshape(bs, ls, POOL_WIDTH)
    pool_s = _pool_mix(halo_src, halo_s, us3, wp, ps, tm=ls, base_pos=n_past, zero_first_halo=False)
    h1_s, xn_all, ti_s, gt_s = _merge(
        attn_s.reshape(n_s, ATTN_WIDTH), pool_s.reshape(n_s, POOL_WIDTH), gs, xs,
        wa, wb, wo, n2w, rw, rb, tm=n_s, xn_all=xn_all, xn_row0=n_p, n_all=n_all)

    n_rank = pl.cdiv(n_all, 128) * 128
    ti_all = jnp.concatenate([ti_p, ti_s, jnp.full((n_rank - n_all, 128), -1, I32)], axis=0)
    rank, cnt = _expert_rank(ti_all)
    top_idx = ti_all[:n_all, :TOP_K]
    counts = cnt[0, :N_EXPERTS].astype(I32)
    n_blocks = pl.cdiv(n_all * TOP_K, MOE_SUB) + N_EXPERTS
    n_slots = n_blocks * MOE_SUB
    n_wi = n_blocks // MOE_NSUB + N_EXPERTS
    assert (n_blocks + (MOE_NSUB - 1) * N_EXPERTS) // MOE_NSUB < n_wi
    dest, wi_e, wi_row, wi_nsub, wi_tok = _routing_tables(
        top_idx, rank[:n_all, :TOP_K], counts, n_wi, n_slots)
    ys = _moe_experts(wi_e, wi_row, wi_nsub, wi_tok, xn_all, w_gu[0], b_gu[0], w_dn[0], b_dn[0], n_slots)
    y_p = _combine(dest[:n_p], gt_p, h1_p, ys, tt=128)
    y_s = _combine(dest[n_p:], gt_s, h1_s, ys, tt=n_s)

    heads = (N_HEADS, HEAD_DIM)
    k_p = kp.reshape(1, bp, lp, *heads)
    v_p = vp.reshape(1, bp, lp, *heads)
    f_p = lfp.reshape(bp, lp, FORGET_LANES)[None, :, :, :N_HEADS]
    u_p = up3[:, lp - POOL_HIST:][None]
    k_s = ks.reshape(1, bs, ls, *heads)
    v_s = vs.reshape(1, bs, ls, *heads)
    f_s = lfs.reshape(bs, ls, FORGET_LANES)[None, :, :, :N_HEADS]
    pool_state = jnp.concatenate([state_pool[0], us3], axis=1)[:, -POOL_HIST:][None]
    return (y_p.reshape(bp, lp, D_MODEL), y_s.reshape(bs, ls, D_MODEL),
            k_p, v_p, f_p, u_p, k_s, v_s, f_s, pool_state)
```

```python
import functools

import jax
import jax.numpy as jnp
from jax import lax
from jax.experimental import pallas as pl
from jax.experimental.pallas import tpu as pltpu

F32 = jnp.float32
BF16 = jnp.bfloat16
I32 = jnp.int32

D_MODEL = 2048
N_HEADS = 8
HEAD_DIM = 128
ATTN_WIDTH = N_HEADS * HEAD_DIM
ATTN_SCALE = HEAD_DIM ** -0.5
FORGET_LANES = 128
POOL_WINDOWS = (2, 4, 8, 16)
POOL_WIDTH = D_MODEL // 2
GW = POOL_WIDTH // len(POOL_WINDOWS)
POOL_HIST = max(POOL_WINDOWS) - 1
POOL_HALO = 16
N_EXPERTS = 32
TOP_K = 4
D_EXPERT = D_MODEL
SWIGLU_LIMIT = 7.0
SWIGLU_ALPHA = 1.702
NORM_EPS = 1e-6
PAGE_SIZE = 128
NEG_BIG = -1e30

ZQ, ZK, ZV, ZU = 0, 1, 2, 3
Z_WIDTH = 3 * ATTN_WIDTH + POOL_WIDTH + 2 * D_MODEL

MOE_SUB = 128
MOE_NSUB = 9
MOE_TM = MOE_SUB * MOE_NSUB
MOE_TK = 256
MOE_TN = 256
MOE_KS = D_MODEL // MOE_TK
MOE_NS = D_EXPERT // MOE_TN
MOE_UP_GROUPS = 4
MOE_DOWN_GROUPS = 2
assert MOE_TM % ((MOE_KS + MOE_NS) * MOE_UP_GROUPS) == 0
VMEM_LIMIT = 56 * 1024 * 1024

HIGHEST = lax.Precision.HIGHEST


def _cparams(sem, vmem=VMEM_LIMIT, flags=None):
    return pltpu.CompilerParams(dimension_semantics=sem, vmem_limit_bytes=vmem, flags=flags)


def _log_sigmoid(x):
    return jnp.minimum(x, 0.0) - jnp.log1p(jnp.exp(-jnp.abs(x)))


_NT = (((1,), (1,)), ((), ()))


def _inproj_kernel(x_ref, n1w_ref, w_ref, wf_ref, bf_ref, qnw_ref, knw_ref,
                   q_ref, k_ref, v_ref, u_ref, g_ref, lf_ref, xn_ref, *, q_scale):
    j = pl.program_id(1)

    @pl.when(j == 0)
    def _():
        x = x_ref[...]
        ms = jnp.mean(x * x, axis=-1, keepdims=True)
        xn = x * lax.rsqrt(ms + NORM_EPS) * n1w_ref[...]
        xn_ref[...] = xn.astype(BF16)
        zf = lax.dot_general(xn_ref[...], wf_ref[...], _NT, preferred_element_type=F32) + bf_ref[...]
        lf_ref[...] = _log_sigmoid(zf)

    def proj():
        return lax.dot_general(xn_ref[...], w_ref[...], _NT, preferred_element_type=F32)

    def head_norm(nw_ref, dst_ref, scale):
        z = proj()
        for h in range(N_HEADS):
            sl = slice(h * HEAD_DIM, (h + 1) * HEAD_DIM)
            zh = z[:, sl]
            ms = jnp.mean(zh * zh, axis=-1, keepdims=True)
            y = zh * lax.rsqrt(ms + NORM_EPS) * nw_ref[:, sl]
            if scale != 1.0:
                y = y * scale
            dst_ref[:, sl] = y.astype(dst_ref.dtype)

    @pl.when(j == ZQ)
    def _():
        head_norm(qnw_ref, q_ref, q_scale)

    @pl.when(j == ZK)
    def _():
        head_norm(knw_ref, k_ref, 1.0)

    @pl.when(j == ZV)
    def _():
        v_ref[...] = proj()

    @pl.when(j == ZU)
    def _():
        u_ref[...] = proj()

    @pl.when(j > ZU)
    def _():
        g_ref[...] = jax.nn.sigmoid(proj())


def _inproj(x, n1w, w_main_t, w_f_t, b_f, qnw, knw, tm, q_scale):
    m = x.shape[0]
    tn = ATTN_WIDTH
    row = lambda i, j: (i, 0)
    const = lambda i, j: (0, 0)
    return pl.pallas_call(
        functools.partial(_inproj_kernel, q_scale=q_scale),
        grid=(m // tm, Z_WIDTH // tn),
        in_specs=[
            pl.BlockSpec((tm, D_MODEL), row),
            pl.BlockSpec((1, D_MODEL), const),
            pl.BlockSpec((tn, D_MODEL), lambda i, j: (j, 0)),
            pl.BlockSpec((FORGET_LANES, D_MODEL), const),
            pl.BlockSpec((1, FORGET_LANES), const),
            pl.BlockSpec((1, tn), const),
            pl.BlockSpec((1, tn), const),
        ],
        out_specs=[
            pl.BlockSpec((tm, tn), row),
            pl.BlockSpec((tm, tn), row),
            pl.BlockSpec((tm, tn), row),
            pl.BlockSpec((tm, tn), row),
            pl.BlockSpec((tm, tn), lambda i, j: (i, jnp.maximum(j - (ZU + 1), 0))),
            pl.BlockSpec((tm, FORGET_LANES), row),
        ],
        out_shape=[
            jax.ShapeDtypeStruct((m, ATTN_WIDTH), BF16),
            jax.ShapeDtypeStruct((m, ATTN_WIDTH), F32),
            jax.ShapeDtypeStruct((m, ATTN_WIDTH), F32),
            jax.ShapeDtypeStruct((m, POOL_WIDTH), F32),
            jax.ShapeDtypeStruct((m, 2 * D_MODEL), F32),
            jax.ShapeDtypeStruct((m, FORGET_LANES), F32),
        ],
        scratch_shapes=[pltpu.VMEM((tm, D_MODEL), BF16)],
        compiler_params=_cparams(("arbitrary", "arbitrary")),
        name="inproj",
    )(x, n1w, w_main_t, w_f_t, b_f, qnw, knw)


def _cumsum_kernel(lf_ref, c_ref, carry_ref, *, bl, seg_len):
    i = pl.program_id(0)
    r = lax.broadcasted_iota(I32, (bl, bl), 0)
    c = lax.broadcasted_iota(I32, (bl, bl), 1)
    mask = c <= r
    if seg_len < bl:
        mask = mask & ((r // seg_len) == (c // seg_len))
    tri = mask.astype(F32)
    cs = jnp.dot(tri, lf_ref[...], preferred_element_type=F32, precision=HIGHEST)
    if seg_len > bl:
        @pl.when((i * bl) % seg_len == 0)
        def _():
            carry_ref[...] = jnp.zeros_like(carry_ref)
        cs = cs + carry_ref[...]
        carry_ref[...] = cs[bl - 1:bl, :]
    c_ref[...] = cs


def _cumsum_rows(lf, bl, seg_len):
    m = lf.shape[0]
    return pl.pallas_call(
        functools.partial(_cumsum_kernel, bl=bl, seg_len=seg_len),
        grid=(m // bl,),
        in_specs=[pl.BlockSpec((bl, FORGET_LANES), lambda i: (i, 0))],
        out_specs=pl.BlockSpec((bl, FORGET_LANES), lambda i: (i, 0)),
        out_shape=jax.ShapeDtypeStruct((m, FORGET_LANES), F32),
        scratch_shapes=[pltpu.VMEM((1, FORGET_LANES), F32)],
        compiler_params=_cparams(("arbitrary",)),
        name="logf_cumsum",
    )(lf)


LOG2E = 1.4426950408889634


def _pattn_kernel(q_ref, k_ref, v_ref, c_ref, kbias_ref, o_ref, *scratch, tq, ch):
    m_refs = scratch[0:N_HEADS]
    accl_refs = scratch[N_HEADS:2 * N_HEADS]
    kb_ref, vb_ref = scratch[2 * N_HEADS:]
    qi = pl.program_id(1)
    ki = pl.program_id(2)
    tk = tq

    @pl.when(ki == 0)
    def _():
        for h in range(N_HEADS):
            m_refs[h][...] = jnp.full_like(m_refs[h], -jnp.inf)
            accl_refs[h][...] = jnp.zeros_like(accl_refs[h])

    @pl.when(ki <= qi)
    def _():
        kb_ref[...] = k_ref[0].astype(BF16)
        vb_ref[...] = v_ref[0].astype(BF16)

    def block(diag):
        c2 = c_ref[0] * LOG2E
        q_ones = (lax.broadcasted_iota(I32, (ch, HEAD_DIM), 1) < 3).astype(BF16)
        v_ones = jnp.ones((tk, HEAD_DIM), BF16)
        below = (lax.broadcasted_iota(I32, (ch, ch), 1) <= lax.broadcasted_iota(I32, (ch, ch), 0))

        def where(h, r):
            rows = slice(r * ch, (r + 1) * ch)
            cols = slice(0, (r + 1) * ch if diag else tk)
            return rows, cols, slice(h * HEAD_DIM, (h + 1) * HEAD_DIM)

        def stage_scores(h, r):
            rows, cols, sl = where(h, r)
            q_cat = jnp.concatenate([q_ref[0, rows, sl], q_ones], axis=1)
            k_cat = jnp.concatenate([kb_ref[cols, sl], kbias_ref[0, h, cols, :]], axis=1)
            u = lax.dot_general(q_cat, k_cat, _NT, preferred_element_type=F32)
            if diag:
                last = jnp.where(below, u[:, r * ch:], -jnp.inf)
                u = last if r == 0 else jnp.concatenate([u[:, :r * ch], last], axis=1)
            return u

        def stage_max(h, r, u):
            rows, _, _ = where(h, r)
            c2t = c2[rows, h:h + 1]
            m_prev = m_refs[h][rows, :]
            m_new = jnp.maximum(m_prev, jnp.max(u, axis=-1, keepdims=True) + c2t)
            m_refs[h][rows, :] = m_new
            return jnp.exp2(m_prev - m_new), c2t - m_new[:, 0:1]

        def stage_accumulate(h, r, p, alpha):
            rows, cols, sl = where(h, r)
            v_cat = jnp.concatenate([vb_ref[cols, sl], v_ones[cols, :]], axis=1)
            pv = jnp.dot(p, v_cat, preferred_element_type=F32)
            accl_refs[h][rows, :] = jnp.concatenate([alpha, alpha], axis=1) * accl_refs[h][rows, :] + pv

        units = [(h, r) for r in range(tq // ch) for h in range(N_HEADS)]
        u_of, ab_of, p_of = {}, {}, {}
        for t in range(len(units) + 3):
            if t < len(units):
                u_of[t] = stage_scores(*units[t])
            if 0 <= t - 1 < len(units):
                ab_of[t - 1] = stage_max(*units[t - 1], u_of[t - 1])
            if 0 <= t - 2 < len(units):
                p_of[t - 2] = jnp.exp2(u_of.pop(t - 2) + ab_of[t - 2][1]).astype(BF16)
            if 0 <= t - 3 < len(units):
                stage_accumulate(*units[t - 3], p_of.pop(t - 3), ab_of.pop(t - 3)[0])

    @pl.when(ki < qi)
    def _():
        block(False)

    @pl.when(ki == qi)
    def _():
        block(True)
        for h in range(N_HEADS):
            sl = slice(h * HEAD_DIM, (h + 1) * HEAD_DIM)
            accl = accl_refs[h][...]
            o_ref[0, :, sl] = (accl[:, :HEAD_DIM] / accl[:, HEAD_DIM:]).astype(o_ref.dtype)


def _prompt_attention(q3, k3, v3, c3, kbias, tq=512, ch=128):
    b, l, _ = q3.shape
    tk = tq
    nq = l // tq
    kv_idx = lambda bi, qi, ki: (bi, jnp.minimum(ki, qi), 0)
    return pl.pallas_call(
        functools.partial(_pattn_kernel, tq=tq, ch=ch),
        grid=(b, nq, nq),
        in_specs=[
            pl.BlockSpec((1, tq, ATTN_WIDTH), lambda bi, qi, ki: (bi, qi, 0)),
            pl.BlockSpec((1, tk, ATTN_WIDTH), kv_idx),
            pl.BlockSpec((1, tk, ATTN_WIDTH), kv_idx),
            pl.BlockSpec((1, tq, FORGET_LANES), lambda bi, qi, ki: (bi, qi, 0)),
            pl.BlockSpec((1, N_HEADS, tk, HEAD_DIM),
                         lambda bi, qi, ki: (bi, 0, jnp.minimum(ki, qi), 0)),
        ],
        out_specs=pl.BlockSpec((1, tq, ATTN_WIDTH), lambda bi, qi, ki: (bi, qi, 0)),
        out_shape=jax.ShapeDtypeStruct((b, l, ATTN_WIDTH), BF16),
        scratch_shapes=(
            [pltpu.VMEM((tq, HEAD_DIM), F32) for _ in range(N_HEADS)]
            + [pltpu.VMEM((tq, 2 * HEAD_DIM), F32) for _ in range(N_HEADS)]
            + [pltpu.VMEM((tk, ATTN_WIDTH), BF16), pltpu.VMEM((tk, ATTN_WIDTH), BF16)]),
        compiler_params=_cparams(("arbitrary", "arbitrary", "arbitrary")),
        name="prompt_attention",
    )(q3, k3, v3, c3, kbias)


def _bf16_pieces(x, n):
    out = []
    for _ in range(n):
        piece = x.astype(BF16).astype(F32)
        out.append(piece)
        x = x - piece
    return out


def _kbias_kernel(c_ref, o_ref):
    c = c_ref[0] * (-LOG2E)
    lane = lax.broadcasted_iota(I32, c.shape, 1)
    for h in range(N_HEADS):
        hi, mid, lo = _bf16_pieces(c[:, h:h + 1], 3)
        tile = jnp.where(lane == 0, hi, jnp.where(lane == 1, mid, jnp.where(lane == 2, lo, 0.0)))
        o_ref[0, h] = tile.astype(BF16)


def _key_bias_pieces(c3, tl=512):
    b, l, _ = c3.shape
    return pl.pallas_call(
        _kbias_kernel,
        grid=(b, l // tl),
        in_specs=[pl.BlockSpec((1, tl, FORGET_LANES), lambda bi, i: (bi, i, 0))],
        out_specs=pl.BlockSpec((1, N_HEADS, tl, HEAD_DIM), lambda bi, i: (bi, 0, i, 0)),
        out_shape=jax.ShapeDtypeStruct((b, N_HEADS, l, HEAD_DIM), BF16),
        compiler_params=_cparams(("arbitrary", "arbitrary")),
        name="key_bias",
    )(c3)


def _heads_to_lanes(ref):
    parts = [ref[0, pl.ds(h, PAGE_SIZE, stride=N_HEADS), :] for h in range(N_HEADS)]
    return jnp.concatenate(parts, axis=1).astype(BF16)


def _rows_per_head(x):
    return jnp.concatenate(
        [jnp.broadcast_to(x[h:h + 1, :], (8, x.shape[1])) for h in range(N_HEADS)], axis=0)


def _sattn_kernel(pt_ref, qbd_ref, cncol_ref, cnkeys_ref, knew_ref, vnew_ref, *rest, g_pages):
    del pt_ref
    k_refs = rest[0:g_pages]
    v_refs = rest[g_pages:2 * g_pages]
    lf_refs = rest[2 * g_pages:3 * g_pages]
    o_ref = rest[3 * g_pages]
    m_ref, l_ref, acc_ref, carry_ref = rest[3 * g_pages + 1:]
    jj = pl.program_id(1)
    nj = pl.num_programs(1)
    rows = N_HEADS * 8

    @pl.when(jj == 0)
    def _():
        m_ref[...] = jnp.full_like(m_ref, -jnp.inf)
        l_ref[...] = jnp.zeros_like(l_ref)
        acc_ref[...] = jnp.zeros_like(acc_ref)
        carry_ref[...] = jnp.zeros_like(carry_ref)

    qbd = qbd_ref[0]
    cncol = cncol_ref[0]
    later = (lax.broadcasted_iota(I32, (PAGE_SIZE, PAGE_SIZE), 0)
             > lax.broadcasted_iota(I32, (PAGE_SIZE, PAGE_SIZE), 1)).astype(F32)

    def scores(kcat):
        return lax.dot_general(qbd, kcat, (((1,), (1,)), ((), ())),
                               preferred_element_type=F32) * ATTN_SCALE

    def update(s, vcat, m, l, acc):
        m_new = jnp.maximum(m, jnp.max(s, axis=-1, keepdims=True))
        alpha = jnp.exp(m - m_new)
        p = jnp.exp(s - jnp.concatenate([m_new] * (s.shape[1] // PAGE_SIZE), axis=1))
        l = alpha * l + jnp.sum(p, axis=-1, keepdims=True)
        pv = jnp.dot(p.astype(BF16), vcat, preferred_element_type=F32)
        acc = jnp.concatenate([alpha] * N_HEADS, axis=1) * acc + pv
        return m_new, l, acc

    m, l, acc, carry = m_ref[...], l_ref[...], acc_ref[...], carry_ref[...]
    lf_all = jnp.concatenate([_rows_per_head(lf_refs[g][0]) for g in range(g_pages)], axis=0)
    suffix_all = jnp.dot(lf_all, later, preferred_element_type=F32, precision=HIGHEST)
    bias = []
    for g in range(g_pages):
        bias.append(suffix_all[g * rows:(g + 1) * rows, :] + carry)
        carry = carry + jnp.sum(lf_all[g * rows:(g + 1) * rows, :], axis=-1, keepdims=True)
    kcat = jnp.concatenate([_heads_to_lanes(k_refs[g]) for g in range(g_pages)], axis=0)
    vcat = jnp.concatenate([_heads_to_lanes(v_refs[g]) for g in range(g_pages)], axis=0)
    s = scores(kcat) + cncol + jnp.concatenate(bias, axis=1)
    m, l, acc = update(s, vcat, m, l, acc)

    @pl.when(jj < nj - 1)
    def _():
        m_ref[...] = m
        l_ref[...] = l
        acc_ref[...] = acc
        carry_ref[...] = carry

    @pl.when(jj == nj - 1)
    def _():
        s = scores(knew_ref[0].astype(BF16)) + cncol - cnkeys_ref[0]
        t_row = lax.broadcasted_iota(I32, (rows, PAGE_SIZE), 0) % 8
        s_col = lax.broadcasted_iota(I32, (rows, PAGE_SIZE), 1)
        s = jnp.where(s_col <= t_row, s, -jnp.inf)
        _, l2, acc2 = update(s, vnew_ref[0].astype(BF16), m, l, acc)
        o = acc2 / jnp.concatenate([l2] * N_HEADS, axis=1)
        for h in range(N_HEADS):
            sl = slice(h * HEAD_DIM, (h + 1) * HEAD_DIM)
            o_ref[0, :, sl] = o[h * 8:(h + 1) * 8, sl].astype(o_ref.dtype)


def _sample_attention(page_table, qbd, cncol, cnkeys, knew, vnew, ck, cv, clf_t, g_pages=8):
    bd, n_pages = page_table.shape
    rows = N_HEADS * 8
    pt_flat = page_table.reshape(-1)

    def page_idx(g):
        return lambda b, jj, pt: (pt[b * n_pages + n_pages - 1 - (jj * g_pages + g)], 0, 0)

    per_b = lambda b, jj, pt: (b, 0, 0)
    in_specs = [
        pl.BlockSpec((1, rows, ATTN_WIDTH), per_b),
        pl.BlockSpec((1, rows, 1), per_b),
        pl.BlockSpec((1, rows, PAGE_SIZE), per_b),
        pl.BlockSpec((1, PAGE_SIZE, ATTN_WIDTH), per_b),
        pl.BlockSpec((1, PAGE_SIZE, ATTN_WIDTH), per_b),
    ]
    in_specs += [pl.BlockSpec((1, PAGE_SIZE * N_HEADS, HEAD_DIM), page_idx(g)) for g in range(g_pages)]
    in_specs += [pl.BlockSpec((1, PAGE_SIZE * N_HEADS, HEAD_DIM), page_idx(g)) for g in range(g_pages)]
    in_specs += [pl.BlockSpec((1, N_HEADS, PAGE_SIZE), page_idx(g)) for g in range(g_pages)]
    grid_spec = pltpu.PrefetchScalarGridSpec(
        num_scalar_prefetch=1,
        grid=(bd, n_pages // g_pages),
        in_specs=in_specs,
        out_specs=pl.BlockSpec((1, 8, ATTN_WIDTH), per_b),
        scratch_shapes=[
            pltpu.VMEM((rows, PAGE_SIZE), F32),
            pltpu.VMEM((rows, PAGE_SIZE), F32),
            pltpu.VMEM((rows, ATTN_WIDTH), F32),
            pltpu.VMEM((rows, PAGE_SIZE), F32),
        ],
    )
    return pl.pallas_call(
        functools.partial(_sattn_kernel, g_pages=g_pages),
        grid_spec=grid_spec,
        out_shape=jax.ShapeDtypeStruct((bd, 8, ATTN_WIDTH), BF16),
        compiler_params=_cparams(("arbitrary", "arbitrary")),
        name="sample_attention",
    )(pt_flat, qbd, cncol, cnkeys, knew, vnew,
      *([ck] * g_pages), *([cv] * g_pages), *([clf_t] * g_pages))


def _pool_kernel(halo_ref, u_ref, wp_ref, ps_ref, o_ref, *, tm, base_pos, zero_first_halo):
    i = pl.program_id(1)
    halo = halo_ref[0]
    if zero_first_halo:
        halo = jnp.where(i == 0, 0.0, halo)
    u = u_ref[0]
    ext = jnp.concatenate([halo, u], axis=0)
    pos = base_pos + i * tm + lax.broadcasted_iota(I32, (tm, 1), 0)
    for g, w in enumerate(POOL_WINDOWS):
        sl = slice(g * GW, (g + 1) * GW)
        s = ext[:, sl]
        k = 1
        while k < w:
            s = s + pltpu.roll(s, shift=k, axis=0)
            k *= 2
        wsum = s[POOL_HALO:, :]
        count = jnp.minimum(pos + 1, w).astype(F32)
        mixed = wsum / count - u[:, sl]
        y = jnp.dot(mixed.astype(BF16), wp_ref[g], preferred_element_type=F32)
        o_ref[0, :, sl] = (y * ps_ref[:, sl]).astype(o_ref.dtype)


def _pool_mix(halo_src, halo_spec, u_src, wp, ps, tm, base_pos, zero_first_halo):
    b, l, _ = u_src.shape
    return pl.pallas_call(
        functools.partial(_pool_kernel, tm=tm, base_pos=base_pos, zero_first_halo=zero_first_halo),
        grid=(b, l // tm),
        in_specs=[
            halo_spec,
            pl.BlockSpec((1, tm, POOL_WIDTH), lambda bi, i: (bi, i, 0)),
            pl.BlockSpec((len(POOL_WINDOWS), GW, GW), lambda bi, i: (0, 0, 0)),
            pl.BlockSpec((1, POOL_WIDTH), lambda bi, i: (0, 0)),
        ],
        out_specs=pl.BlockSpec((1, tm, POOL_WIDTH), lambda bi, i: (bi, i, 0)),
        out_shape=jax.ShapeDtypeStruct((b, l, POOL_WIDTH), BF16),
        compiler_params=_cparams(("arbitrary", "arbitrary")),
        name="pool_mix",
    )(halo_src, u_src, wp, ps)


def _merge_kernel(attn_ref, pool_ref, ga_ref, gb_ref, x_ref, wa_ref, wb_ref, wo_ref,
                  n2w_ref, rw_ref, rb_ref, *rest, tm, aliased):
    h_ref, xn_ref, ti_ref, gt_ref, rwp_ref = rest[-5:]

    @pl.when(pl.program_id(0) == 0)
    def _():
        for n, piece in enumerate(_bf16_pieces(rw_ref[...], 3)):
            rwp_ref[n] = piece.astype(BF16)

    a = jnp.dot(attn_ref[...], wa_ref[...], preferred_element_type=F32)
    p = jnp.dot(pool_ref[...], wb_ref[...], preferred_element_type=F32)
    merged = ga_ref[...] * a + gb_ref[...] * p
    o = jnp.dot(merged.astype(BF16), wo_ref[...], preferred_element_type=F32)
    h1 = x_ref[...] + o
    h_ref[...] = h1
    ms = jnp.mean(h1 * h1, axis=-1, keepdims=True)
    xn = h1 * lax.rsqrt(ms + NORM_EPS) * n2w_ref[...]
    xn_ref[...] = xn
    x_hi, x_lo = [v.astype(BF16) for v in _bf16_pieces(xn, 2)]
    logits = rb_ref[...]
    for xp, wn in ((x_hi, 0), (x_hi, 1), (x_lo, 0), (x_hi, 2), (x_lo, 1)):
        logits = logits + jnp.dot(xp, rwp_ref[wn], preferred_element_type=F32)
    lane = lax.broadcasted_iota(I32, (tm, 128), 1).astype(F32)
    vals, idxs = [], []
    work = logits
    for _ in range(TOP_K):
        mv = jnp.max(work, axis=-1, keepdims=True)
        ix = jnp.min(jnp.where(work == mv, lane, 128.0), axis=-1, keepdims=True)
        vals.append(mv)
        idxs.append(ix)
        work = jnp.where(lane == ix, NEG_BIG, work)
    es = [jnp.exp(v - vals[0]) for v in vals]
    den = es[0] + es[1] + es[2] + es[3]
    ti = jnp.zeros((tm, 128), F32)
    gt = jnp.zeros((tm, 128), F32)
    for k in range(TOP_K):
        ti = jnp.where(lane == float(k), idxs[k], ti)
        gt = jnp.where(lane == float(k), es[k] / den, gt)
    ti_ref[...] = ti.astype(I32)
    gt_ref[...] = gt


def _merge(attn, pool, z, x, wa, wb, wo, n2w, rw, rb, tm, xn_all, xn_row0, n_all):
    m = x.shape[0]
    aliased = xn_all is not None
    const = lambda i: (0, 0)
    single = dict(pipeline_mode=pl.Buffered(1))
    xn_blk0 = xn_row0 // tm
    in_specs = [
        pl.BlockSpec((tm, ATTN_WIDTH), lambda i: (i, 0)),
        pl.BlockSpec((tm, POOL_WIDTH), lambda i: (i, 0)),
        pl.BlockSpec((tm, D_MODEL), lambda i: (i, 0)),
        pl.BlockSpec((tm, D_MODEL), lambda i: (i, 1)),
        pl.BlockSpec((tm, D_MODEL), lambda i: (i, 0)),
        pl.BlockSpec((ATTN_WIDTH, D_MODEL), const, **single),
        pl.BlockSpec((POOL_WIDTH, D_MODEL), const, **single),
        pl.BlockSpec((D_MODEL, D_MODEL), const, **single),
        pl.BlockSpec((1, D_MODEL), const),
        pl.BlockSpec((D_MODEL, 128), const),
        pl.BlockSpec((1, 128), const),
    ]
    args = [attn, pool, z, z, x, wa, wb, wo, n2w, rw, rb]
    io_alias = {}
    if aliased:
        in_specs.append(pl.BlockSpec(memory_space=pl.ANY))
        args.append(xn_all)
        io_alias = {len(args) - 1: 1}
    return pl.pallas_call(
        functools.partial(_merge_kernel, tm=tm, aliased=aliased),
        grid=(m // tm,),
        in_specs=in_specs,
        out_specs=[
            pl.BlockSpec((tm, D_MODEL), lambda i: (i, 0)),
            pl.BlockSpec((tm, D_MODEL), lambda i: (xn_blk0 + i, 0)),
            pl.BlockSpec((tm, 128), lambda i: (i, 0)),
            pl.BlockSpec((tm, 128), lambda i: (i, 0)),
        ],
        out_shape=[
            jax.ShapeDtypeStruct((m, D_MODEL), F32),
            jax.ShapeDtypeStruct((n_all, D_MODEL), F32),
            jax.ShapeDtypeStruct((m, 128), I32),
            jax.ShapeDtypeStruct((m, 128), F32),
        ],
        input_output_aliases=io_alias,
        scratch_shapes=[pltpu.VMEM((3, D_MODEL, 128), BF16)],
        compiler_params=_cparams(("arbitrary",)),
        name="merge_router",
    )(*args)


def _rank_kernel(idx_ref, rank_ref, cnt_ref, carry_ref, *, tr):
    i = pl.program_id(0)

    @pl.when(i == 0)
    def _():
        carry_ref[...] = jnp.zeros_like(carry_ref)

    idx = idx_ref[...]
    lane = lax.broadcasted_iota(I32, (tr, 128), 1)
    hits = [idx[:, k:k + 1] == lane for k in range(TOP_K)]
    onehot = jnp.zeros((tr, 128), F32)
    for hk in hits:
        onehot = onehot + hk.astype(F32)
    earlier = (lax.broadcasted_iota(I32, (tr, tr), 1)
               < lax.broadcasted_iota(I32, (tr, tr), 0)).astype(BF16)
    before = jnp.dot(earlier, onehot.astype(BF16), preferred_element_type=F32) + carry_ref[...]
    out = jnp.zeros((tr, 128), I32)
    for k, hk in enumerate(hits):
        rk = jnp.sum(jnp.where(hk, before, 0.0), axis=-1, keepdims=True)
        out = jnp.where(lane == k, rk.astype(I32), out)
    rank_ref[...] = out
    carry_ref[...] = carry_ref[...] + jnp.sum(onehot, axis=0, keepdims=True)
    cnt_ref[...] = carry_ref[...]


def _expert_rank(top_idx_padded, tr=128):
    n = top_idx_padded.shape[0]
    return pl.pallas_call(
        functools.partial(_rank_kernel, tr=tr),
        grid=(n // tr,),
        in_specs=[pl.BlockSpec((tr, 128), lambda i: (i, 0))],
        out_specs=[
            pl.BlockSpec((tr, 128), lambda i: (i, 0)),
            pl.BlockSpec((1, 128), lambda i: (0, 0)),
        ],
        out_shape=[
            jax.ShapeDtypeStruct((n, 128), I32),
            jax.ShapeDtypeStruct((1, 128), F32),
        ],
        scratch_shapes=[pltpu.VMEM((1, 128), F32)],
        compiler_params=_cparams(("arbitrary",)),
        name="expert_rank",
    )(top_idx_padded)


def _moe_out_block_copy(acc_scr, ys_hbm, sub, row, sem):
    return pltpu.make_async_copy(acc_scr.at[pl.ds(sub * MOE_SUB, MOE_SUB)],
                                 ys_hbm.at[pl.ds(row, MOE_SUB)], sem)


def _moe_kernel(wi_e_ref, wi_row_ref, wi_nsub_ref, tok_ref, tok_next_ref, xn_hbm,
                wgu_ref, bgu_ref, wd_ref, bd_ref, ys_hbm,
                x_scr, hdn_scr, pend_ref, sem_in, sem_out):
    del wi_e_ref
    w = pl.program_id(0)
    j = pl.program_id(1)
    nj = pl.num_programs(1)
    nsub = wi_nsub_ref[w]
    row0 = wi_row_ref[w]
    cur = w % 2
    acc_scr = x_scr.at[cur]
    rows_per_step = MOE_TM // (MOE_KS + MOE_NS)

    def issue_rows(tok_tbl, slot, first, count):
        for i in range(count):
            r = first + i
            tok = tok_tbl[0, 0, r]
            pltpu.make_async_copy(xn_hbm.at[pl.ds(tok, 1)], x_scr.at[slot, pl.ds(r, 1)], sem_in).start()

    @pl.when((w == 0) & (j == 0))
    def _():
        pend_ref[0] = 0

    @pl.when((w == 0) & (j == 0) & (nsub > 0))
    def _():
        def issue(blk, carry):
            issue_rows(tok_ref, 0, blk * 8, 8)
            return carry
        lax.fori_loop(0, MOE_TM // 8, issue, 0)

    def drain_output():
        for s in range(MOE_NSUB):
            @pl.when(s < pend_ref[0])
            def _():
                _moe_out_block_copy(acc_scr, ys_hbm, s, 0, sem_out).wait()
        pend_ref[0] = 0

    @pl.when(j == 0)
    def _():
        drain_output()

    rows_in_flight = jnp.where(w == 0, nsub > 0, wi_nsub_ref[jnp.maximum(w - 1, 0)] > 0)

    @pl.when((j == 0) & rows_in_flight)
    def _():
        for s in range(MOE_NSUB):
            pltpu.make_async_copy(xn_hbm.at[pl.ds(0, MOE_SUB)],
                                  x_scr.at[cur, pl.ds(s * MOE_SUB, MOE_SUB)], sem_in).wait()

    @pl.when((j == 0) & (nsub > 0))
    def _():
        hdn_scr[...] = jnp.broadcast_to(bgu_ref[0], (MOE_TM, 2 * D_EXPERT))

    def up_step():
        xk = x_scr[cur, :, pl.ds(pl.multiple_of(j * MOE_TK, MOE_TK), MOE_TK)].astype(BF16)
        share = rows_per_step // MOE_UP_GROUPS
        width = 2 * D_EXPERT // MOE_UP_GROUPS
        for c in range(MOE_UP_GROUPS):
            cols = slice(c * width, (c + 1) * width)
            issue_rows(tok_next_ref, 1 - cur, j * rows_per_step + c * share, share)
            hdn_scr[:, cols] += jnp.dot(xk, wgu_ref[0, :, cols].astype(BF16), preferred_element_type=F32)

    def down_step():
        n0 = pl.multiple_of((j - MOE_KS) * MOE_TN, MOE_TN)
        gate = jnp.minimum(hdn_scr[:, pl.ds(n0, MOE_TN)], SWIGLU_LIMIT)
        up = jnp.clip(hdn_scr[:, pl.ds(D_EXPERT + n0, MOE_TN)], -SWIGLU_LIMIT, SWIGLU_LIMIT)
        act = ((up + 1.0) * (gate * jax.nn.sigmoid(SWIGLU_ALPHA * gate))).astype(BF16)
        share = rows_per_step // MOE_DOWN_GROUPS
        width = D_MODEL // MOE_DOWN_GROUPS
        for c in range(MOE_DOWN_GROUPS):
            cols = slice(c * width, (c + 1) * width)
            issue_rows(tok_next_ref, 1 - cur, j * rows_per_step + c * share, share)
            acc_scr[:, cols] += jnp.dot(act, wd_ref[0, :, cols].astype(BF16), preferred_element_type=F32)

    @pl.when((j < MOE_KS) & (nsub > 0))
    def _():
        up_step()

    @pl.when((j == MOE_KS) & (nsub > 0))
    def _():
        acc_scr[...] = jnp.broadcast_to(bd_ref[0], (MOE_TM, D_MODEL))

    @pl.when((j >= MOE_KS) & (nsub > 0))
    def _():
        down_step()

    @pl.when((j == nj - 1) & (nsub > 0))
    def _():
        for s in range(MOE_NSUB):
            @pl.when(s < nsub)
            def _():
                _moe_out_block_copy(acc_scr, ys_hbm, s,
                                    pl.multiple_of(row0 + s * MOE_SUB, MOE_SUB), sem_out).start()
        pend_ref[0] = nsub

    @pl.when((w == pl.num_programs(0) - 1) & (j == nj - 1))
    def _():
        drain_output()


def _moe_experts(wi_e, wi_row, wi_nsub, wi_tok, xn_all, w_gu, b_gu, w_dn, b_dn, n_slots):
    n_wi = wi_e.shape[0]

    def up_blk(w, j, n):
        return jnp.where(n[w] > 0, jnp.minimum(j, MOE_KS - 1), MOE_KS - 1)

    def down_blk(w, j, n):
        return jnp.where(n[w] > 0, jnp.maximum(j - MOE_KS, 0), MOE_NS - 1)

    grid_spec = pltpu.PrefetchScalarGridSpec(
        num_scalar_prefetch=3,
        grid=(n_wi, MOE_KS + MOE_NS),
        in_specs=[
            pl.BlockSpec((1, 1, MOE_TM), lambda w, j, e, r, n: (w, 0, 0), memory_space=pltpu.SMEM),
            pl.BlockSpec((1, 1, MOE_TM), lambda w, j, e, r, n: (jnp.minimum(w + 1, n_wi - 1), 0, 0),
                         memory_space=pltpu.SMEM),
            pl.BlockSpec(memory_space=pl.ANY),
            pl.BlockSpec((1, MOE_TK, 2 * D_EXPERT), lambda w, j, e, r, n: (e[w], up_blk(w, j, n), 0)),
            pl.BlockSpec((1, 1, 2 * D_EXPERT), lambda w, j, e, r, n: (e[w], 0, 0)),
            pl.BlockSpec((1, MOE_TN, D_MODEL), lambda w, j, e, r, n: (e[w], down_blk(w, j, n), 0)),
            pl.BlockSpec((1, 1, D_MODEL), lambda w, j, e, r, n: (e[w], 0, 0)),
        ],
        out_specs=pl.BlockSpec(memory_space=pl.ANY),
        scratch_shapes=[
            pltpu.VMEM((2, MOE_TM, D_MODEL), F32),
            pltpu.VMEM((MOE_TM, 2 * D_EXPERT), F32),
            pltpu.SMEM((1,), I32),
            pltpu.SemaphoreType.DMA(()),
            pltpu.SemaphoreType.DMA(()),
        ],
    )
    return pl.pallas_call(
        _moe_kernel,
        grid_spec=grid_spec,
        out_shape=jax.ShapeDtypeStruct((n_slots, D_MODEL), F32),
        compiler_params=_cparams(("arbitrary", "arbitrary"), vmem=60 * 1024 * 1024),
        name="moe_experts",
    )(wi_e, wi_row, wi_nsub, wi_tok, wi_tok, xn_all, w_gu,
      b_gu.reshape(N_EXPERTS, 1, 2 * D_EXPERT), w_dn, b_dn.reshape(N_EXPERTS, 1, D_MODEL))


def _combine_kernel(dest_ref, gt_ref, h_ref, ys_hbm, o_ref, buf, sem, *, tt):
    def issue(r, carry):
        for k in range(TOP_K):
            slot = dest_ref[0, 0, r * TOP_K + k]
            pltpu.make_async_copy(ys_hbm.at[pl.ds(slot, 1)], buf.at[k, pl.ds(r, 1)], sem).start()
        return carry
    lax.fori_loop(0, tt, issue, 0, unroll=4)
    for k in range(TOP_K):
        pltpu.make_async_copy(ys_hbm.at[pl.ds(0, tt)], buf.at[k], sem).wait()
    gt = gt_ref[...]
    y = gt[:, 0:1] * buf[0]
    for k in range(1, TOP_K):
        y = y + gt[:, k:k + 1] * buf[k]
    o_ref[...] = h_ref[...] + y


def _combine(dest, gates, h1, ys, tt):
    m = h1.shape[0]
    nt = m // tt
    return pl.pallas_call(
        functools.partial(_combine_kernel, tt=tt),
        grid=(nt,),
        in_specs=[
            pl.BlockSpec((1, 1, tt * TOP_K), lambda i: (i, 0, 0), memory_space=pltpu.SMEM),
            pl.BlockSpec((tt, 128), lambda i: (i, 0)),
            pl.BlockSpec((tt, D_MODEL), lambda i: (i, 0)),
            pl.BlockSpec(memory_space=pl.ANY),
        ],
        out_specs=pl.BlockSpec((tt, D_MODEL), lambda i: (i, 0)),
        out_shape=jax.ShapeDtypeStruct((m, D_MODEL), F32),
        scratch_shapes=[pltpu.VMEM((TOP_K, tt, D_MODEL), F32), pltpu.SemaphoreType.DMA(())],
        compiler_params=_cparams(("arbitrary",)),
        name="moe_combine",
    )(dest.reshape(nt, 1, tt * TOP_K), gates, h1, ys)


def _routing_tables(top_idx, rank, counts, n_wi, n_slots):
    n_tok = top_idx.shape[0]
    padded = (counts + MOE_SUB - 1) // MOE_SUB * MOE_SUB
    pad_start = jnp.cumsum(padded) - padded
    dest = pad_start[top_idx] + rank
    tok_ids = jnp.broadcast_to(jnp.arange(n_tok, dtype=I32)[:, None], (n_tok, TOP_K))
    slot_tok = jnp.zeros((n_slots,), I32).at[dest.reshape(-1)].set(
        tok_ids.reshape(-1), unique_indices=True)
    nsubs = padded // MOE_SUB
    items = (nsubs + MOE_NSUB - 1) // MOE_NSUB
    item_end = jnp.cumsum(items)
    total = item_end[-1]
    w = jnp.arange(n_wi, dtype=I32)
    w_eff = jnp.minimum(w, total - 1)
    wi_e = jnp.sum((item_end[None, :] <= w_eff[:, None]).astype(I32), axis=1)
    wi_e = jnp.minimum(wi_e, N_EXPERTS - 1)
    local = w_eff - (item_end - items)[wi_e]
    wi_row = (pad_start[wi_e] + local * MOE_TM).astype(I32)
    wi_nsub = jnp.where(w < total, jnp.clip(nsubs[wi_e] - local * MOE_NSUB, 0, MOE_NSUB), 0).astype(I32)
    tok_pos = jnp.minimum(wi_row[:, None] + jnp.arange(MOE_TM, dtype=I32)[None, :], n_slots - 1)
    wi_tok = slot_tok[tok_pos].reshape(n_wi, 1, MOE_TM)
    return dest.astype(I32), wi_e, wi_row, wi_nsub, wi_tok


def kernel(x_prompt, x_sample, cache_k, cache_v, cache_logf, state_pool, page_table,
           norm1_w, w_in, b_forget, q_norm_w, k_norm_w, w_pool, pool_scale,
           w_branch_a, w_branch_b, w_out, norm2_w, router_w, router_b,
           w_gu, b_gu, w_dn, b_dn):
    depth = w_in.shape[0]
    assert depth == 1
    bp, lp, _ = x_prompt.shape
    bs, ls, _ = x_sample.shape
    assert ls == 8 and N_HEADS == 8
    n_p, n_s = bp * lp, bs * ls
    n_all = n_p + n_s
    n_past = page_table.shape[1] * PAGE_SIZE

    wt = jnp.transpose(w_in[0])
    f_off = 3 * ATTN_WIDTH
    w_main = jnp.concatenate([wt[:f_off], wt[f_off + N_HEADS:]], axis=0).astype(BF16)
    w_f = jnp.pad(wt[f_off:f_off + N_HEADS], ((0, FORGET_LANES - N_HEADS), (0, 0))).astype(BF16)
    b_f = jnp.pad(b_forget[0], (0, FORGET_LANES - N_HEADS)).reshape(1, FORGET_LANES)
    n1w = norm1_w[0].reshape(1, D_MODEL)
    n2w = norm2_w[0].reshape(1, D_MODEL)
    qnw = jnp.tile(q_norm_w[0], N_HEADS).reshape(1, ATTN_WIDTH)
    knw = jnp.tile(k_norm_w[0], N_HEADS).reshape(1, ATTN_WIDTH)
    wp = w_pool[0].astype(BF16)
    ps = pool_scale[0].reshape(1, POOL_WIDTH)
    wa = w_branch_a[0].astype(BF16)
    wb = w_branch_b[0].astype(BF16)
    wo = w_out[0].astype(BF16)
    rw = jnp.pad(router_w[0], ((0, 0), (0, 128 - N_EXPERTS)))
    rb = jnp.pad(router_b[0], (0, 128 - N_EXPERTS), constant_values=NEG_BIG).reshape(1, 128)

    xp = x_prompt.reshape(n_p, D_MODEL)
    qp, kp, vp, up, gp, lfp = _inproj(xp, n1w, w_main, w_f, b_f, qnw, knw, tm=512,
                                      q_scale=ATTN_SCALE * LOG2E)
    cp = _cumsum_rows(lfp, bl=256, seg_len=lp)
    cp3 = cp.reshape(bp, lp, FORGET_LANES)
    attn_p = _prompt_attention(qp.reshape(bp, lp, ATTN_WIDTH), kp.reshape(bp, lp, ATTN_WIDTH),
                               vp.reshape(bp, lp, ATTN_WIDTH), cp3, _key_bias_pieces(cp3))
    up3 = up.reshape(bp, lp, POOL_WIDTH)
    halo_p = pl.BlockSpec((1, POOL_HALO, POOL_WIDTH),
                          lambda bi, i: (bi, jnp.maximum(i * (512 // POOL_HALO) - 1, 0), 0))
    pool_p = _pool_mix(up3, halo_p, up3, wp, ps, tm=512, base_pos=0, zero_first_halo=True)
    h1_p, xn_all, ti_p, gt_p = _merge(
        attn_p.reshape(n_p, ATTN_WIDTH), pool_p.reshape(n_p, POOL_WIDTH), gp, xp,
        wa, wb, wo, n2w, rw, rb, tm=256, xn_all=None, xn_row0=0, n_all=n_all)

    xs = x_sample.reshape(n_s, D_MODEL)
    qs, ks, vs, us, gs, lfs = _inproj(xs, n1w, w_main, w_f, b_f, qnw, knw, tm=n_s, q_scale=1.0)
    cs = _cumsum_rows(lfs, bl=n_s, seg_len=ls)
    q_s = qs.astype(F32).reshape(bs, ls, N_HEADS, HEAD_DIM)
    qbd = jnp.einsum('bthd,hg->bhtgd', q_s, jnp.eye(N_HEADS, dtype=F32))
    qbd = qbd.reshape(bs, N_HEADS * ls, ATTN_WIDTH).astype(BF16)
    cn = jnp.transpose(cs.reshape(bs, ls, FORGET_LANES)[:, :, :N_HEADS], (0, 2, 1))
    cncol = cn.reshape(bs, N_HEADS * ls, 1)
    cnkeys = jnp.broadcast_to(cn[:, :, None, :], (bs, N_HEADS, ls, ls)).reshape(bs, N_HEADS * ls, ls)
    cnkeys = jnp.pad(cnkeys, ((0, 0), (0, 0), (0, PAGE_SIZE - ls)))
    knew = jnp.pad(ks.reshape(bs, ls, ATTN_WIDTH), ((0, 0), (0, PAGE_SIZE - ls), (0, 0)))
    vnew = jnp.pad(vs.reshape(bs, ls, ATTN_WIDTH), ((0, 0), (0, PAGE_SIZE - ls), (0, 0)))
    n_phys = cache_k.shape[1]
    ck = cache_k[0].reshape(n_phys, PAGE_SIZE * N_HEADS, HEAD_DIM)
    cv = cache_v[0].reshape(n_phys, PAGE_SIZE * N_HEADS, HEAD_DIM)
    clf_t = jnp.transpose(cache_logf[0], (0, 2, 1))
    attn_s = _sample_attention(page_table, qbd, cncol, cnkeys, knew, vnew, ck, cv, clf_t)
    halo_src = jnp.pad(state_pool[0], ((0, 0), (POOL_HALO - POOL_HIST, 0), (0, 0)))
    halo_s = pl.BlockSpec((1, POOL_HALO, POOL_WIDTH), lambda bi, i: (bi, 0, 0))
    us3 = us.reshape(bs, ls, POOL_WIDTH)
    pool_s = _pool_mix(halo_src, halo_s, us3, wp, ps, tm=ls, base_pos=n_past, zero_first_halo=False)
    h1_s, xn_all, ti_s, gt_s = _merge(
        attn_s.reshape(n_s, ATTN_WIDTH), pool_s.reshape(n_s, POOL_WIDTH), gs, xs,
        wa, wb, wo, n2w, rw, rb, tm=n_s, xn_all=xn_all, xn_row0=n_p, n_all=n_all)

    n_rank = pl.cdiv(n_all, 128) * 128
    ti_all = jnp.concatenate([ti_p, ti_s, jnp.full((n_rank - n_all, 128), -1, I32)], axis=0)
    rank, cnt = _expert_rank(ti_all)
    top_idx = ti_all[:n_all, :TOP_K]
    counts = cnt[0, :N_EXPERTS].astype(I32)
    n_blocks = pl.cdiv(n_all * TOP_K, MOE_SUB) + N_EXPERTS
    n_slots = n_blocks * MOE_SUB
    n_wi = n_blocks // MOE_NSUB + N_EXPERTS
    assert (n_blocks + (MOE_NSUB - 1) * N_EXPERTS) // MOE_NSUB < n_wi
    dest, wi_e, wi_row, wi_nsub, wi_tok = _routing_tables(
        top_idx, rank[:n_all, :TOP_K], counts, n_wi, n_slots)
    ys = _moe_experts(wi_e, wi_row, wi_nsub, wi_tok, xn_all, w_gu[0], b_gu[0], w_dn[0], b_dn[0], n_slots)
    y_p = _combine(dest[:n_p], gt_p, h1_p, ys, tt=128)
    y_s = _combine(dest[n_p:], gt_s, h1_s, ys, tt=n_s)

    heads = (N_HEADS, HEAD_DIM)
    k_p = kp.reshape(1, bp, lp, *heads)
    v_p = vp.reshape(1, bp, lp, *heads)
    f_p = lfp.reshape(bp, lp, FORGET_LANES)[None, :, :, :N_HEADS]
    u_p = up3[:, lp - POOL_HIST:][None]
    k_s = ks.reshape(1, bs, ls, *heads)
    v_s = vs.reshape(1, bs, ls, *heads)
    f_s = lfs.reshape(bs, ls, FORGET_LANES)[None, :, :, :N_HEADS]
    pool_state = jnp.concatenate([state_pool[0], us3], axis=1)[:, -POOL_HIST:][None]
    return (y_p.reshape(bp, lp, D_MODEL), y_s.reshape(bs, ls, D_MODEL),
            k_p, v_p, f_p, u_p, k_s, v_s, f_s, pool_state)
```

```python
import functools

import jax
import jax.numpy as jnp
from jax import lax
from jax.experimental import pallas as pl
from jax.experimental.pallas import tpu as pltpu

F32 = jnp.float32
BF16 = jnp.bfloat16
I32 = jnp.int32

D_MODEL = 2048
N_HEADS = 8
HEAD_DIM = 128
ATTN_WIDTH = N_HEADS * HEAD_DIM
ATTN_SCALE = HEAD_DIM ** -0.5
FORGET_LANES = 128
POOL_WINDOWS = (2, 4, 8, 16)
POOL_WIDTH = D_MODEL // 2
GW = POOL_WIDTH // len(POOL_WINDOWS)
POOL_HIST = max(POOL_WINDOWS) - 1
POOL_HALO = 16
N_EXPERTS = 32
TOP_K = 4
D_EXPERT = D_MODEL
SWIGLU_LIMIT = 7.0
SWIGLU_ALPHA = 1.702
NORM_EPS = 1e-6
PAGE_SIZE = 128
NEG_BIG = -1e30

ZQ, ZK, ZV, ZU = 0, 1, 2, 3
Z_WIDTH = 3 * ATTN_WIDTH + POOL_WIDTH + 2 * D_MODEL

MOE_SUB = 128
MOE_NSUB = 9
MOE_TM = MOE_SUB * MOE_NSUB
MOE_TN = 512
RANK_TILE = 512
VMEM_LIMIT = 56 * 1024 * 1024

HIGHEST = lax.Precision.HIGHEST


def _cparams(sem, vmem=VMEM_LIMIT, flags=None):
    return pltpu.CompilerParams(dimension_semantics=sem, vmem_limit_bytes=vmem, flags=flags)


def _log_sigmoid(x):
    return jnp.minimum(x, 0.0) - jnp.log1p(jnp.exp(-jnp.abs(x)))


_NT = (((1,), (1,)), ((), ()))


def _inproj_kernel(x_ref, n1w_ref, w_ref, wf_ref, bf_ref, qnw_ref, knw_ref,
                   q_ref, k_ref, v_ref, u_ref, g_ref, lf_ref, xn_ref, *, q_scale):
    j = pl.program_id(1)

    @pl.when(j == 0)
    def _():
        x = x_ref[...]
        ms = jnp.mean(x * x, axis=-1, keepdims=True)
        xn = x * lax.rsqrt(ms + NORM_EPS) * n1w_ref[...]
        xn_ref[...] = xn.astype(BF16)
        zf = lax.dot_general(xn_ref[...], wf_ref[...], _NT, preferred_element_type=F32) + bf_ref[...]
        lf_ref[...] = _log_sigmoid(zf)

    def proj():
        return lax.dot_general(xn_ref[...], w_ref[...], _NT, preferred_element_type=F32)

    def head_norm(nw_ref, dst_ref, scale):
        z = proj()
        for h in range(N_HEADS):
            sl = slice(h * HEAD_DIM, (h + 1) * HEAD_DIM)
            zh = z[:, sl]
            ms = jnp.mean(zh * zh, axis=-1, keepdims=True)
            y = zh * lax.rsqrt(ms + NORM_EPS) * nw_ref[:, sl]
            if scale != 1.0:
                y = y * scale
            dst_ref[:, sl] = y.astype(dst_ref.dtype)

    @pl.when(j == ZQ)
    def _():
        head_norm(qnw_ref, q_ref, q_scale)

    @pl.when(j == ZK)
    def _():
        head_norm(knw_ref, k_ref, 1.0)

    @pl.when(j == ZV)
    def _():
        v_ref[...] = proj()

    @pl.when(j == ZU)
    def _():
        u_ref[...] = proj()

    @pl.when(j > ZU)
    def _():
        g_ref[...] = jax.nn.sigmoid(proj())


def _inproj(x, n1w, w_main_t, w_f_t, b_f, qnw, knw, tm, q_scale):
    m = x.shape[0]
    tn = ATTN_WIDTH
    row = lambda i, j: (i, 0)
    const = lambda i, j: (0, 0)
    return pl.pallas_call(
        functools.partial(_inproj_kernel, q_scale=q_scale),
        grid=(m // tm, Z_WIDTH // tn),
        in_specs=[
            pl.BlockSpec((tm, D_MODEL), row),
            pl.BlockSpec((1, D_MODEL), const),
            pl.BlockSpec((tn, D_MODEL), lambda i, j: (j, 0)),
            pl.BlockSpec((FORGET_LANES, D_MODEL), const),
            pl.BlockSpec((1, FORGET_LANES), const),
            pl.BlockSpec((1, tn), const),
            pl.BlockSpec((1, tn), const),
        ],
        out_specs=[
            pl.BlockSpec((tm, tn), row),
            pl.BlockSpec((tm, tn), row),
            pl.BlockSpec((tm, tn), row),
            pl.BlockSpec((tm, tn), row),
            pl.BlockSpec((tm, tn), lambda i, j: (i, jnp.maximum(j - (ZU + 1), 0))),
            pl.BlockSpec((tm, FORGET_LANES), row),
        ],
        out_shape=[
            jax.ShapeDtypeStruct((m, ATTN_WIDTH), BF16),
            jax.ShapeDtypeStruct((m, ATTN_WIDTH), F32),
            jax.ShapeDtypeStruct((m, ATTN_WIDTH), F32),
            jax.ShapeDtypeStruct((m, POOL_WIDTH), F32),
            jax.ShapeDtypeStruct((m, 2 * D_MODEL), F32),
            jax.ShapeDtypeStruct((m, FORGET_LANES), F32),
        ],
        scratch_shapes=[pltpu.VMEM((tm, D_MODEL), BF16)],
        compiler_params=_cparams(("arbitrary", "arbitrary")),
        name="inproj",
    )(x, n1w, w_main_t, w_f_t, b_f, qnw, knw)


def _cumsum_kernel(lf_ref, c_ref, carry_ref, *, bl, seg_len):
    i = pl.program_id(0)
    r = lax.broadcasted_iota(I32, (bl, bl), 0)
    c = lax.broadcasted_iota(I32, (bl, bl), 1)
    mask = c <= r
    if seg_len < bl:
        mask = mask & ((r // seg_len) == (c // seg_len))
    tri = mask.astype(F32)
    cs = jnp.dot(tri, lf_ref[...], preferred_element_type=F32, precision=HIGHEST)
    if seg_len > bl:
        @pl.when((i * bl) % seg_len == 0)
        def _():
            carry_ref[...] = jnp.zeros_like(carry_ref)
        cs = cs + carry_ref[...]
        carry_ref[...] = cs[bl - 1:bl, :]
    c_ref[...] = cs


def _cumsum_rows(lf, bl, seg_len):
    m = lf.shape[0]
    return pl.pallas_call(
        functools.partial(_cumsum_kernel, bl=bl, seg_len=seg_len),
        grid=(m // bl,),
        in_specs=[pl.BlockSpec((bl, FORGET_LANES), lambda i: (i, 0))],
        out_specs=pl.BlockSpec((bl, FORGET_LANES), lambda i: (i, 0)),
        out_shape=jax.ShapeDtypeStruct((m, FORGET_LANES), F32),
        scratch_shapes=[pltpu.VMEM((1, FORGET_LANES), F32)],
        compiler_params=_cparams(("arbitrary",)),
        name="logf_cumsum",
    )(lf)


LOG2E = 1.4426950408889634


def _pattn_kernel(q_ref, k_ref, v_ref, c_ref, kbias_ref, o_ref, *scratch, tq, ch):
    m_refs = scratch[0:N_HEADS]
    accl_refs = scratch[N_HEADS:2 * N_HEADS]
    kb_ref, vb_ref = scratch[2 * N_HEADS:]
    qi = pl.program_id(1)
    ki = pl.program_id(2)
    tk = tq

    @pl.when(ki == 0)
    def _():
        for h in range(N_HEADS):
            m_refs[h][...] = jnp.full_like(m_refs[h], -jnp.inf)
            accl_refs[h][...] = jnp.zeros_like(accl_refs[h])

    @pl.when(ki <= qi)
    def _():
        kb_ref[...] = k_ref[0].astype(BF16)
        vb_ref[...] = v_ref[0].astype(BF16)

    def block(diag):
        c2 = c_ref[0] * LOG2E
        q_ones = (lax.broadcasted_iota(I32, (ch, HEAD_DIM), 1) < 3).astype(BF16)
        v_ones = jnp.ones((tk, HEAD_DIM), BF16)
        below = (lax.broadcasted_iota(I32, (ch, ch), 1) <= lax.broadcasted_iota(I32, (ch, ch), 0))

        def where(h, r):
            rows = slice(r * ch, (r + 1) * ch)
            cols = slice(0, (r + 1) * ch if diag else tk)
            return rows, cols, slice(h * HEAD_DIM, (h + 1) * HEAD_DIM)

        def stage_scores(h, r):
            rows, cols, sl = where(h, r)
            q_cat = jnp.concatenate([q_ref[0, rows, sl], q_ones], axis=1)
            k_cat = jnp.concatenate([kb_ref[cols, sl], kbias_ref[0, h, cols, :]], axis=1)
            u = lax.dot_general(q_cat, k_cat, _NT, preferred_element_type=F32)
            if diag:
                last = jnp.where(below, u[:, r * ch:], -jnp.inf)
                u = last if r == 0 else jnp.concatenate([u[:, :r * ch], last], axis=1)
            return u

        def stage_max(h, r, u):
            rows, _, _ = where(h, r)
            c2t = c2[rows, h:h + 1]
            m_prev = m_refs[h][rows, :]
            m_new = jnp.maximum(m_prev, jnp.max(u, axis=-1, keepdims=True) + c2t)
            m_refs[h][rows, :] = m_new
            return jnp.exp2(m_prev - m_new), c2t - m_new[:, 0:1]

        def stage_accumulate(h, r, p, alpha):
            rows, cols, sl = where(h, r)
            v_cat = jnp.concatenate([vb_ref[cols, sl], v_ones[cols, :]], axis=1)
            pv = jnp.dot(p, v_cat, preferred_element_type=F32)
            accl_refs[h][rows, :] = jnp.concatenate([alpha, alpha], axis=1) * accl_refs[h][rows, :] + pv

        units = [(h, r) for r in range(tq // ch) for h in range(N_HEADS)]
        u_of, ab_of, p_of = {}, {}, {}
        for t in range(len(units) + 3):
            if t < len(units):
                u_of[t] = stage_scores(*units[t])
            if 0 <= t - 1 < len(units):
                ab_of[t - 1] = stage_max(*units[t - 1], u_of[t - 1])
            if 0 <= t - 2 < len(units):
                p_of[t - 2] = jnp.exp2(u_of.pop(t - 2) + ab_of[t - 2][1]).astype(BF16)
            if 0 <= t - 3 < len(units):
                stage_accumulate(*units[t - 3], p_of.pop(t - 3), ab_of.pop(t - 3)[0])

    @pl.when(ki < qi)
    def _():
        block(False)

    @pl.when(ki == qi)
    def _():
        block(True)
        for h in range(N_HEADS):
            sl = slice(h * HEAD_DIM, (h + 1) * HEAD_DIM)
            accl = accl_refs[h][...]
            o_ref[0, :, sl] = (accl[:, :HEAD_DIM] / accl[:, HEAD_DIM:]).astype(o_ref.dtype)


def _prompt_attention(q3, k3, v3, c3, kbias, tq=512, ch=128):
    b, l, _ = q3.shape
    tk = tq
    nq = l // tq
    kv_idx = lambda bi, qi, ki: (bi, jnp.minimum(ki, qi), 0)
    return pl.pallas_call(
        functools.partial(_pattn_kernel, tq=tq, ch=ch),
        grid=(b, nq, nq),
        in_specs=[
            pl.BlockSpec((1, tq, ATTN_WIDTH), lambda bi, qi, ki: (bi, qi, 0)),
            pl.BlockSpec((1, tk, ATTN_WIDTH), kv_idx),
            pl.BlockSpec((1, tk, ATTN_WIDTH), kv_idx),
            pl.BlockSpec((1, tq, FORGET_LANES), lambda bi, qi, ki: (bi, qi, 0)),
            pl.BlockSpec((1, N_HEADS, tk, HEAD_DIM),
                         lambda bi, qi, ki: (bi, 0, jnp.minimum(ki, qi), 0)),
        ],
        out_specs=pl.BlockSpec((1, tq, ATTN_WIDTH), lambda bi, qi, ki: (bi, qi, 0)),
        out_shape=jax.ShapeDtypeStruct((b, l, ATTN_WIDTH), BF16),
        scratch_shapes=(
            [pltpu.VMEM((tq, HEAD_DIM), F32) for _ in range(N_HEADS)]
            + [pltpu.VMEM((tq, 2 * HEAD_DIM), F32) for _ in range(N_HEADS)]
            + [pltpu.VMEM((tk, ATTN_WIDTH), BF16), pltpu.VMEM((tk, ATTN_WIDTH), BF16)]),
        compiler_params=_cparams(("arbitrary", "arbitrary", "arbitrary")),
        name="prompt_attention",
    )(q3, k3, v3, c3, kbias)


def _bf16_pieces(x, n):
    out = []
    for _ in range(n):
        piece = x.astype(BF16).astype(F32)
        out.append(piece)
        x = x - piece
    return out


def _kbias_kernel(c_ref, o_ref):
    c = c_ref[0] * (-LOG2E)
    lane = lax.broadcasted_iota(I32, c.shape, 1)
    for h in range(N_HEADS):
        hi, mid, lo = _bf16_pieces(c[:, h:h + 1], 3)
        tile = jnp.where(lane == 0, hi, jnp.where(lane == 1, mid, jnp.where(lane == 2, lo, 0.0)))
        o_ref[0, h] = tile.astype(BF16)


def _key_bias_pieces(c3, tl=512):
    b, l, _ = c3.shape
    return pl.pallas_call(
        _kbias_kernel,
        grid=(b, l // tl),
        in_specs=[pl.BlockSpec((1, tl, FORGET_LANES), lambda bi, i: (bi, i, 0))],
        out_specs=pl.BlockSpec((1, N_HEADS, tl, HEAD_DIM), lambda bi, i: (bi, 0, i, 0)),
        out_shape=jax.ShapeDtypeStruct((b, N_HEADS, l, HEAD_DIM), BF16),
        compiler_params=_cparams(("arbitrary", "arbitrary")),
        name="key_bias",
    )(c3)


def _heads_to_lanes(ref):
    parts = [ref[0, pl.ds(h, PAGE_SIZE, stride=N_HEADS), :] for h in range(N_HEADS)]
    return jnp.concatenate(parts, axis=1).astype(BF16)


def _rows_per_head(x):
    return jnp.concatenate(
        [jnp.broadcast_to(x[h:h + 1, :], (8, x.shape[1])) for h in range(N_HEADS)], axis=0)


def _sattn_kernel(pt_ref, qbd_ref, cncol_ref, cnkeys_ref, knew_ref, vnew_ref, *rest, g_pages):
    del pt_ref
    k_refs = rest[0:g_pages]
    v_refs = rest[g_pages:2 * g_pages]
    lf_refs = rest[2 * g_pages:3 * g_pages]
    o_ref = rest[3 * g_pages]
    m_ref, l_ref, acc_ref, carry_ref = rest[3 * g_pages + 1:]
    jj = pl.program_id(1)
    nj = pl.num_programs(1)
    rows = N_HEADS * 8

    @pl.when(jj == 0)
    def _():
        m_ref[...] = jnp.full_like(m_ref, -jnp.inf)
        l_ref[...] = jnp.zeros_like(l_ref)
        acc_ref[...] = jnp.zeros_like(acc_ref)
        carry_ref[...] = jnp.zeros_like(carry_ref)

    qbd = qbd_ref[0]
    cncol = cncol_ref[0]
    later = (lax.broadcasted_iota(I32, (PAGE_SIZE, PAGE_SIZE), 0)
             > lax.broadcasted_iota(I32, (PAGE_SIZE, PAGE_SIZE), 1)).astype(F32)

    def scores(kcat):
        return lax.dot_general(qbd, kcat, (((1,), (1,)), ((), ())),
                               preferred_element_type=F32) * ATTN_SCALE

    def update(s, vcat, m, l, acc):
        m_new = jnp.maximum(m, jnp.max(s, axis=-1, keepdims=True))
        alpha = jnp.exp(m - m_new)
        p = jnp.exp(s - jnp.concatenate([m_new] * (s.shape[1] // PAGE_SIZE), axis=1))
        l = alpha * l + jnp.sum(p, axis=-1, keepdims=True)
        pv = jnp.dot(p.astype(BF16), vcat, preferred_element_type=F32)
        acc = jnp.concatenate([alpha] * N_HEADS, axis=1) * acc + pv
        return m_new, l, acc

    m, l, acc, carry = m_ref[...], l_ref[...], acc_ref[...], carry_ref[...]
    lf_all = jnp.concatenate([_rows_per_head(lf_refs[g][0]) for g in range(g_pages)], axis=0)
    suffix_all = jnp.dot(lf_all, later, preferred_element_type=F32, precision=HIGHEST)
    bias = []
    for g in range(g_pages):
        bias.append(suffix_all[g * rows:(g + 1) * rows, :] + carry)
        carry = carry + jnp.sum(lf_all[g * rows:(g + 1) * rows, :], axis=-1, keepdims=True)
    kcat = jnp.concatenate([_heads_to_lanes(k_refs[g]) for g in range(g_pages)], axis=0)
    vcat = jnp.concatenate([_heads_to_lanes(v_refs[g]) for g in range(g_pages)], axis=0)
    s = scores(kcat) + cncol + jnp.concatenate(bias, axis=1)
    m, l, acc = update(s, vcat, m, l, acc)

    @pl.when(jj < nj - 1)
    def _():
        m_ref[...] = m
        l_ref[...] = l
        acc_ref[...] = acc
        carry_ref[...] = carry

    @pl.when(jj == nj - 1)
    def _():
        s = scores(knew_ref[0].astype(BF16)) + cncol - cnkeys_ref[0]
        t_row = lax.broadcasted_iota(I32, (rows, PAGE_SIZE), 0) % 8
        s_col = lax.broadcasted_iota(I32, (rows, PAGE_SIZE), 1)
        s = jnp.where(s_col <= t_row, s, -jnp.inf)
        _, l2, acc2 = update(s, vnew_ref[0].astype(BF16), m, l, acc)
        o = acc2 / jnp.concatenate([l2] * N_HEADS, axis=1)
        for h in range(N_HEADS):
            sl = slice(h * HEAD_DIM, (h + 1) * HEAD_DIM)
            o_ref[0, :, sl] = o[h * 8:(h + 1) * 8, sl].astype(o_ref.dtype)


def _sample_attention(page_table, qbd, cncol, cnkeys, knew, vnew, ck, cv, clf_t, g_pages=16):
    bd, n_pages = page_table.shape
    rows = N_HEADS * 8
    pt_flat = page_table.reshape(-1)

    def page_idx(g):
        return lambda b, jj, pt: (pt[b * n_pages + n_pages - 1 - (jj * g_pages + g)], 0, 0)

    per_b = lambda b, jj, pt: (b, 0, 0)
    in_specs = [
        pl.BlockSpec((1, rows, ATTN_WIDTH), per_b),
        pl.BlockSpec((1, rows, 1), per_b),
        pl.BlockSpec((1, rows, PAGE_SIZE), per_b),
        pl.BlockSpec((1, PAGE_SIZE, ATTN_WIDTH), per_b),
        pl.BlockSpec((1, PAGE_SIZE, ATTN_WIDTH), per_b),
    ]
    in_specs += [pl.BlockSpec((1, PAGE_SIZE * N_HEADS, HEAD_DIM), page_idx(g)) for g in range(g_pages)]
    in_specs += [pl.BlockSpec((1, PAGE_SIZE * N_HEADS, HEAD_DIM), page_idx(g)) for g in range(g_pages)]
    in_specs += [pl.BlockSpec((1, N_HEADS, PAGE_SIZE), page_idx(g)) for g in range(g_pages)]
    grid_spec = pltpu.PrefetchScalarGridSpec(
        num_scalar_prefetch=1,
        grid=(bd, n_pages // g_pages),
        in_specs=in_specs,
        out_specs=pl.BlockSpec((1, 8, ATTN_WIDTH), per_b),
        scratch_shapes=[
            pltpu.VMEM((rows, PAGE_SIZE), F32),
            pltpu.VMEM((rows, PAGE_SIZE), F32),
            pltpu.VMEM((rows, ATTN_WIDTH), F32),
            pltpu.VMEM((rows, PAGE_SIZE), F32),
        ],
    )
    return pl.pallas_call(
        functools.partial(_sattn_kernel, g_pages=g_pages),
        grid_spec=grid_spec,
        out_shape=jax.ShapeDtypeStruct((bd, 8, ATTN_WIDTH), BF16),
        compiler_params=_cparams(("arbitrary", "arbitrary")),
        name="sample_attention",
    )(pt_flat, qbd, cncol, cnkeys, knew, vnew,
      *([ck] * g_pages), *([cv] * g_pages), *([clf_t] * g_pages))


def _pool_kernel(halo_ref, u_ref, wp_ref, ps_ref, o_ref, *, tm, base_pos, zero_first_halo):
    i = pl.program_id(1)
    halo = halo_ref[0]
    if zero_first_halo:
        halo = jnp.where(i == 0, 0.0, halo)
    u = u_ref[0]
    ext = jnp.concatenate([halo, u], axis=0)
    pos = base_pos + i * tm + lax.broadcasted_iota(I32, (tm, 1), 0)
    for g, w in enumerate(POOL_WINDOWS):
        sl = slice(g * GW, (g + 1) * GW)
        s = ext[:, sl]
        k = 1
        while k < w:
            s = s + pltpu.roll(s, shift=k, axis=0)
            k *= 2
        wsum = s[POOL_HALO:, :]
        count = jnp.minimum(pos + 1, w).astype(F32)
        mixed = wsum / count - u[:, sl]
        y = jnp.dot(mixed.astype(BF16), wp_ref[g], preferred_element_type=F32)
        o_ref[0, :, sl] = (y * ps_ref[:, sl]).astype(o_ref.dtype)


def _pool_mix(halo_src, halo_spec, u_src, wp, ps, tm, base_pos, zero_first_halo):
    b, l, _ = u_src.shape
    return pl.pallas_call(
        functools.partial(_pool_kernel, tm=tm, base_pos=base_pos, zero_first_halo=zero_first_halo),
        grid=(b, l // tm),
        in_specs=[
            halo_spec,
            pl.BlockSpec((1, tm, POOL_WIDTH), lambda bi, i: (bi, i, 0)),
            pl.BlockSpec((len(POOL_WINDOWS), GW, GW), lambda bi, i: (0, 0, 0)),
            pl.BlockSpec((1, POOL_WIDTH), lambda bi, i: (0, 0)),
        ],
        out_specs=pl.BlockSpec((1, tm, POOL_WIDTH), lambda bi, i: (bi, i, 0)),
        out_shape=jax.ShapeDtypeStruct((b, l, POOL_WIDTH), BF16),
        compiler_params=_cparams(("arbitrary", "arbitrary")),
        name="pool_mix",
    )(halo_src, u_src, wp, ps)


def _merge_kernel(attn_ref, pool_ref, ga_ref, gb_ref, x_ref, wa_ref, wb_ref, wo_ref,
                  n2w_ref, rw_ref, rb_ref, *rest, tm, aliased):
    h_ref, xn_ref, ti_ref, gt_ref, rwp_ref = rest[-5:]

    @pl.when(pl.program_id(0) == 0)
    def _():
        for n, piece in enumerate(_bf16_pieces(rw_ref[...], 3)):
            rwp_ref[n] = piece.astype(BF16)

    a = jnp.dot(attn_ref[...], wa_ref[...], preferred_element_type=F32)
    p = jnp.dot(pool_ref[...], wb_ref[...], preferred_element_type=F32)
    merged = ga_ref[...] * a + gb_ref[...] * p
    o = jnp.dot(merged.astype(BF16), wo_ref[...], preferred_element_type=F32)
    h1 = x_ref[...] + o
    h_ref[...] = h1
    ms = jnp.mean(h1 * h1, axis=-1, keepdims=True)
    xn = h1 * lax.rsqrt(ms + NORM_EPS) * n2w_ref[...]
    xn_ref[...] = xn
    x_hi, x_lo = [v.astype(BF16) for v in _bf16_pieces(xn, 2)]
    logits = rb_ref[...]
    for xp, wn in ((x_hi, 0), (x_hi, 1), (x_lo, 0), (x_hi, 2), (x_lo, 1)):
        logits = logits + jnp.dot(xp, rwp_ref[wn], preferred_element_type=F32)
    lane = lax.broadcasted_iota(I32, (tm, 128), 1).astype(F32)
    vals, idxs = [], []
    work = logits
    for _ in range(TOP_K):
        mv = jnp.max(work, axis=-1, keepdims=True)
        ix = jnp.min(jnp.where(work == mv, lane, 128.0), axis=-1, keepdims=True)
        vals.append(mv)
        idxs.append(ix)
        work = jnp.where(lane == ix, NEG_BIG, work)
    es = [jnp.exp(v - vals[0]) for v in vals]
    den = es[0] + es[1] + es[2] + es[3]
    ti = jnp.zeros((tm, 128), F32)
    gt = jnp.zeros((tm, 128), F32)
    for k in range(TOP_K):
        ti = jnp.where(lane == float(k), idxs[k], ti)
        gt = jnp.where(lane == float(k), es[k] / den, gt)
    ti_ref[...] = ti.astype(I32)
    gt_ref[...] = gt


def _merge(attn, pool, z, x, wa, wb, wo, n2w, rw, rb, tm, xn_all, xn_row0, n_all):
    m = x.shape[0]
    aliased = xn_all is not None
    const = lambda i: (0, 0)
    single = dict(pipeline_mode=pl.Buffered(1))
    xn_blk0 = xn_row0 // tm
    in_specs = [
        pl.BlockSpec((tm, ATTN_WIDTH), lambda i: (i, 0)),
        pl.BlockSpec((tm, POOL_WIDTH), lambda i: (i, 0)),
        pl.BlockSpec((tm, D_MODEL), lambda i: (i, 0)),
        pl.BlockSpec((tm, D_MODEL), lambda i: (i, 1)),
        pl.BlockSpec((tm, D_MODEL), lambda i: (i, 0)),
        pl.BlockSpec((ATTN_WIDTH, D_MODEL), const, **single),
        pl.BlockSpec((POOL_WIDTH, D_MODEL), const, **single),
        pl.BlockSpec((D_MODEL, D_MODEL), const, **single),
        pl.BlockSpec((1, D_MODEL), const),
        pl.BlockSpec((D_MODEL, 128), const),
        pl.BlockSpec((1, 128), const),
    ]
    args = [attn, pool, z, z, x, wa, wb, wo, n2w, rw, rb]
    io_alias = {}
    if aliased:
        in_specs.append(pl.BlockSpec(memory_space=pl.ANY))
        args.append(xn_all)
        io_alias = {len(args) - 1: 1}
    return pl.pallas_call(
        functools.partial(_merge_kernel, tm=tm, aliased=aliased),
        grid=(m // tm,),
        in_specs=in_specs,
        out_specs=[
            pl.BlockSpec((tm, D_MODEL), lambda i: (i, 0)),
            pl.BlockSpec((tm, D_MODEL), lambda i: (xn_blk0 + i, 0)),
            pl.BlockSpec((tm, 128), lambda i: (i, 0)),
            pl.BlockSpec((tm, 128), lambda i: (i, 0)),
        ],
        out_shape=[
            jax.ShapeDtypeStruct((m, D_MODEL), F32),
            jax.ShapeDtypeStruct((n_all, D_MODEL), F32),
            jax.ShapeDtypeStruct((m, 128), I32),
            jax.ShapeDtypeStruct((m, 128), F32),
        ],
        input_output_aliases=io_alias,
        scratch_shapes=[pltpu.VMEM((3, D_MODEL, 128), BF16)],
        compiler_params=_cparams(("arbitrary",)),
        name="merge_router",
    )(*args)


def _rank_kernel(idx_ref, rank_ref, cnt_ref, carry_ref, *, tr):
    i = pl.program_id(0)

    @pl.when(i == 0)
    def _():
        carry_ref[...] = jnp.zeros_like(carry_ref)

    idx = idx_ref[...]
    lane = lax.broadcasted_iota(I32, (tr, 128), 1)
    hits = [idx[:, k:k + 1] == lane for k in range(TOP_K)]
    onehot = jnp.zeros((tr, 128), F32)
    for hk in hits:
        onehot = onehot + hk.astype(F32)
    earlier = (lax.broadcasted_iota(I32, (tr, tr), 1)
               < lax.broadcasted_iota(I32, (tr, tr), 0)).astype(BF16)
    before = jnp.dot(earlier, onehot.astype(BF16), preferred_element_type=F32) + carry_ref[...]
    out = jnp.zeros((tr, 128), I32)
    for k, hk in enumerate(hits):
        rk = jnp.sum(jnp.where(hk, before, 0.0), axis=-1, keepdims=True)
        out = jnp.where(lane == k, rk.astype(I32), out)
    rank_ref[...] = out
    carry_ref[...] = carry_ref[...] + jnp.sum(onehot, axis=0, keepdims=True)
    cnt_ref[...] = carry_ref[...]


def _expert_rank(top_idx_padded, tr):
    n = top_idx_padded.shape[0]
    return pl.pallas_call(
        functools.partial(_rank_kernel, tr=tr),
        grid=(n // tr,),
        in_specs=[pl.BlockSpec((tr, 128), lambda i: (i, 0))],
        out_specs=[
            pl.BlockSpec((tr, 128), lambda i: (i, 0)),
            pl.BlockSpec((1, 128), lambda i: (0, 0)),
        ],
        out_shape=[
            jax.ShapeDtypeStruct((n, 128), I32),
            jax.ShapeDtypeStruct((1, 128), F32),
        ],
        scratch_shapes=[pltpu.VMEM((1, 128), F32)],
        compiler_params=_cparams(("arbitrary",)),
        name="expert_rank",
    )(top_idx_padded)


def _moe_out_block_copy(acc_scr, ys_hbm, sub, row, sem):
    return pltpu.make_async_copy(acc_scr.at[pl.ds(sub * MOE_SUB, MOE_SUB)],
                                 ys_hbm.at[pl.ds(row, MOE_SUB)], sem)


def _moe_kernel(wi_e_ref, wi_row_ref, wi_nsub_ref, tok_ref, xn_hbm,
                wg_ref, wu_ref, bg_ref, bu_ref, wd_ref, bd_ref, ys_hbm,
                x_scr, acc_scr, wg_bf, wu_bf, wd_bf, pend_ref, sem_in, sem_out):
    del wi_e_ref
    w = pl.program_id(0)
    j = pl.program_id(1)
    nj = pl.num_programs(1)
    nsub = wi_nsub_ref[w]
    row0 = wi_row_ref[w]
    issue_unroll = 8

    @pl.when((w == 0) & (j == 0))
    def _():
        pend_ref[0] = 0

    @pl.when((j == 0) & (nsub > 0))
    def _():
        def issue(blk, carry):
            for i in range(issue_unroll):
                r = blk * issue_unroll + i
                tok = tok_ref[0, 0, r]
                pltpu.make_async_copy(xn_hbm.at[pl.ds(tok, 1)], x_scr.at[pl.ds(r, 1)], sem_in).start()
            return carry
        lax.fori_loop(0, nsub * (MOE_SUB // issue_unroll), issue, 0)

    def drain_output():
        for s in range(MOE_NSUB):
            @pl.when(s < pend_ref[0])
            def _():
                _moe_out_block_copy(acc_scr, ys_hbm, s, 0, sem_out).wait()
        pend_ref[0] = 0

    @pl.when(j == 0)
    def _():
        drain_output()

    @pl.when((j == 0) & (nsub > 0))
    def _():
        for s in range(MOE_NSUB):
            @pl.when(s < nsub)
            def _():
                acc_scr[s * MOE_SUB:(s + 1) * MOE_SUB, :] = jnp.broadcast_to(
                    bd_ref[0], (MOE_SUB, D_MODEL))
        for s in range(MOE_NSUB):
            @pl.when(s < nsub)
            def _():
                pltpu.make_async_copy(xn_hbm.at[pl.ds(0, MOE_SUB)],
                                      x_scr.at[pl.ds(s * MOE_SUB, MOE_SUB)], sem_in).wait()

    def expert_rows(rows):
        xs = x_scr[rows, :].astype(BF16)
        gate = jnp.dot(xs, wg_bf[...], preferred_element_type=F32) + bg_ref[0]
        up = jnp.dot(xs, wu_bf[...], preferred_element_type=F32) + bu_ref[0]
        gate = jnp.minimum(gate, SWIGLU_LIMIT)
        up = jnp.clip(up, -SWIGLU_LIMIT, SWIGLU_LIMIT)
        act = (up + 1.0) * (gate * jax.nn.sigmoid(SWIGLU_ALPHA * gate))
        acc_scr[rows, :] += jnp.dot(act.astype(BF16), wd_bf[...], preferred_element_type=F32)

    @pl.when(nsub > 0)
    def _():
        wg_bf[...] = wg_ref[0].astype(BF16)
        wu_bf[...] = wu_ref[0].astype(BF16)
        wd_bf[...] = wd_ref[0].astype(BF16)
        for c in range(MOE_NSUB // 4):
            @pl.when(4 * c + 4 <= nsub)
            def _():
                expert_rows(pl.ds(4 * c * MOE_SUB, 4 * MOE_SUB))

        @pl.when(nsub % 4 >= 2)
        def _():
            expert_rows(pl.ds(pl.multiple_of((nsub // 4) * 4 * MOE_SUB, MOE_SUB), 2 * MOE_SUB))

        @pl.when(nsub % 2 == 1)
        def _():
            expert_rows(pl.ds(pl.multiple_of((nsub - 1) * MOE_SUB, MOE_SUB), MOE_SUB))

    @pl.when((j == nj - 1) & (nsub > 0))
    def _():
        for s in range(MOE_NSUB):
            @pl.when(s < nsub)
            def _():
                _moe_out_block_copy(acc_scr, ys_hbm, s,
                                    pl.multiple_of(row0 + s * MOE_SUB, MOE_SUB), sem_out).start()
        pend_ref[0] = nsub

    @pl.when((w == pl.num_programs(0) - 1) & (j == nj - 1))
    def _():
        drain_output()


def _moe_experts(wi_e, wi_row, wi_nsub, wi_tok, xn_all, w_gu, b_gu, w_dn, b_dn, n_slots):
    n_wi = wi_e.shape[0]
    nj = D_EXPERT // MOE_TN

    def col(w, j, n):
        return jnp.where(n[w] > 0, j, nj - 1)

    grid_spec = pltpu.PrefetchScalarGridSpec(
        num_scalar_prefetch=3,
        grid=(n_wi, nj),
        in_specs=[
            pl.BlockSpec((1, 1, MOE_TM), lambda w, j, e, r, n: (w, 0, 0), memory_space=pltpu.SMEM),
            pl.BlockSpec(memory_space=pl.ANY),
            pl.BlockSpec((1, D_MODEL, MOE_TN), lambda w, j, e, r, n: (e[w], 0, col(w, j, n))),
            pl.BlockSpec((1, D_MODEL, MOE_TN), lambda w, j, e, r, n: (e[w], 0, nj + col(w, j, n))),
            pl.BlockSpec((1, 1, MOE_TN), lambda w, j, e, r, n: (e[w], 0, col(w, j, n))),
            pl.BlockSpec((1, 1, MOE_TN), lambda w, j, e, r, n: (e[w], 0, nj + col(w, j, n))),
            pl.BlockSpec((1, MOE_TN, D_MODEL), lambda w, j, e, r, n: (e[w], col(w, j, n), 0)),
            pl.BlockSpec((1, 1, D_MODEL), lambda w, j, e, r, n: (e[w], 0, 0)),
        ],
        out_specs=pl.BlockSpec(memory_space=pl.ANY),
        scratch_shapes=[
            pltpu.VMEM((MOE_TM, D_MODEL), F32),
            pltpu.VMEM((MOE_TM, D_MODEL), F32),
            pltpu.VMEM((D_MODEL, MOE_TN), BF16),
            pltpu.VMEM((D_MODEL, MOE_TN), BF16),
            pltpu.VMEM((MOE_TN, D_MODEL), BF16),
            pltpu.SMEM((1,), I32),
            pltpu.SemaphoreType.DMA(()),
            pltpu.SemaphoreType.DMA(()),
        ],
    )
    return pl.pallas_call(
        _moe_kernel,
        grid_spec=grid_spec,
        out_shape=jax.ShapeDtypeStruct((n_slots, D_MODEL), F32),
        compiler_params=_cparams(("arbitrary", "arbitrary"), vmem=58 * 1024 * 1024),
        name="moe_experts",
    )(wi_e, wi_row, wi_nsub, wi_tok, xn_all, w_gu, w_gu,
      b_gu.reshape(N_EXPERTS, 1, 2 * D_EXPERT), b_gu.reshape(N_EXPERTS, 1, 2 * D_EXPERT),
      w_dn, b_dn.reshape(N_EXPERTS, 1, D_MODEL))


def _combine_kernel(dest_ref, gt_ref, h_ref, ys_hbm, o_ref, buf, sem, *, tt):
    def issue(r, carry):
        for k in range(TOP_K):
            slot = dest_ref[0, 0, r * TOP_K + k]
            pltpu.make_async_copy(ys_hbm.at[pl.ds(slot, 1)], buf.at[k, pl.ds(r, 1)], sem).start()
        return carry
    lax.fori_loop(0, tt, issue, 0, unroll=4)
    for k in range(TOP_K):
        pltpu.make_async_copy(ys_hbm.at[pl.ds(0, tt)], buf.at[k], sem).wait()
    gt = gt_ref[...]
    y = gt[:, 0:1] * buf[0]
    for k in range(1, TOP_K):
        y = y + gt[:, k:k + 1] * buf[k]
    o_ref[...] = h_ref[...] + y


def _combine(dest, gates, h1, ys, tt):
    m = h1.shape[0]
    nt = m // tt
    return pl.pallas_call(
        functools.partial(_combine_kernel, tt=tt),
        grid=(nt,),
        in_specs=[
            pl.BlockSpec((1, 1, tt * TOP_K), lambda i: (i, 0, 0), memory_space=pltpu.SMEM),
            pl.BlockSpec((tt, 128), lambda i: (i, 0)),
            pl.BlockSpec((tt, D_MODEL), lambda i: (i, 0)),
            pl.BlockSpec(memory_space=pl.ANY),
        ],
        out_specs=pl.BlockSpec((tt, D_MODEL), lambda i: (i, 0)),
        out_shape=jax.ShapeDtypeStruct((m, D_MODEL), F32),
        scratch_shapes=[pltpu.VMEM((TOP_K, tt, D_MODEL), F32), pltpu.SemaphoreType.DMA(())],
        compiler_params=_cparams(("arbitrary",)),
        name="moe_combine",
    )(dest.reshape(nt, 1, tt * TOP_K), gates, h1, ys)


def _routing_tables(top_idx, rank, counts, n_wi, n_slots):
    n_tok = top_idx.shape[0]
    padded = (counts + MOE_SUB - 1) // MOE_SUB * MOE_SUB
    pad_start = jnp.cumsum(padded) - padded
    dest = pad_start[top_idx] + rank
    tok_ids = jnp.broadcast_to(jnp.arange(n_tok, dtype=I32)[:, None], (n_tok, TOP_K))
    slot_tok = jnp.zeros((n_slots,), I32).at[dest.reshape(-1)].set(
        tok_ids.reshape(-1), unique_indices=True)
    nsubs = padded // MOE_SUB
    items = (nsubs + MOE_NSUB - 1) // MOE_NSUB
    item_end = jnp.cumsum(items)
    total = item_end[-1]
    w = jnp.arange(n_wi, dtype=I32)
    w_eff = jnp.minimum(w, total - 1)
    wi_e = jnp.sum((item_end[None, :] <= w_eff[:, None]).astype(I32), axis=1)
    wi_e = jnp.minimum(wi_e, N_EXPERTS - 1)
    local = w_eff - (item_end - items)[wi_e]
    wi_row = (pad_start[wi_e] + local * MOE_TM).astype(I32)
    wi_nsub = jnp.where(w < total, jnp.clip(nsubs[wi_e] - local * MOE_NSUB, 0, MOE_NSUB), 0).astype(I32)
    tok_pos = jnp.minimum(wi_row[:, None] + jnp.arange(MOE_TM, dtype=I32)[None, :], n_slots - 1)
    wi_tok = slot_tok[tok_pos].reshape(n_wi, 1, MOE_TM)
    return dest.astype(I32), wi_e, wi_row, wi_nsub, wi_tok


def kernel(x_prompt, x_sample, cache_k, cache_v, cache_logf, state_pool, page_table,
           norm1_w, w_in, b_forget, q_norm_w, k_norm_w, w_pool, pool_scale,
           w_branch_a, w_branch_b, w_out, norm2_w, router_w, router_b,
           w_gu, b_gu, w_dn, b_dn):
    depth = w_in.shape[0]
    assert depth == 1
    bp, lp, _ = x_prompt.shape
    bs, ls, _ = x_sample.shape
    assert ls == 8 and N_HEADS == 8
    n_p, n_s = bp * lp, bs * ls
    n_all = n_p + n_s
    n_past = page_table.shape[1] * PAGE_SIZE

    wt = jnp.transpose(w_in[0])
    f_off = 3 * ATTN_WIDTH
    w_main = jnp.concatenate([wt[:f_off], wt[f_off + N_HEADS:]], axis=0).astype(BF16)
    w_f = jnp.pad(wt[f_off:f_off + N_HEADS], ((0, FORGET_LANES - N_HEADS), (0, 0))).astype(BF16)
    b_f = jnp.pad(b_forget[0], (0, FORGET_LANES - N_HEADS)).reshape(1, FORGET_LANES)
    n1w = norm1_w[0].reshape(1, D_MODEL)
    n2w = norm2_w[0].reshape(1, D_MODEL)
    qnw = jnp.tile(q_norm_w[0], N_HEADS).reshape(1, ATTN_WIDTH)
    knw = jnp.tile(k_norm_w[0], N_HEADS).reshape(1, ATTN_WIDTH)
    wp = w_pool[0].astype(BF16)
    ps = pool_scale[0].reshape(1, POOL_WIDTH)
    wa = w_branch_a[0].astype(BF16)
    wb = w_branch_b[0].astype(BF16)
    wo = w_out[0].astype(BF16)
    rw = jnp.pad(router_w[0], ((0, 0), (0, 128 - N_EXPERTS)))
    rb = jnp.pad(router_b[0], (0, 128 - N_EXPERTS), constant_values=NEG_BIG).reshape(1, 128)

    xp = x_prompt.reshape(n_p, D_MODEL)
    qp, kp, vp, up, gp, lfp = _inproj(xp, n1w, w_main, w_f, b_f, qnw, knw, tm=512,
                                      q_scale=ATTN_SCALE * LOG2E)
    cp = _cumsum_rows(lfp, bl=512, seg_len=lp)
    cp3 = cp.reshape(bp, lp, FORGET_LANES)
    attn_p = _prompt_attention(qp.reshape(bp, lp, ATTN_WIDTH), kp.reshape(bp, lp, ATTN_WIDTH),
                               vp.reshape(bp, lp, ATTN_WIDTH), cp3, _key_bias_pieces(cp3))
    up3 = up.reshape(bp, lp, POOL_WIDTH)
    halo_p = pl.BlockSpec((1, POOL_HALO, POOL_WIDTH),
                          lambda bi, i: (bi, jnp.maximum(i * (512 // POOL_HALO) - 1, 0), 0))
    pool_p = _pool_mix(up3, halo_p, up3, wp, ps, tm=512, base_pos=0, zero_first_halo=True)
    h1_p, xn_all, ti_p, gt_p = _merge(
        attn_p.reshape(n_p, ATTN_WIDTH), pool_p.reshape(n_p, POOL_WIDTH), gp, xp,
        wa, wb, wo, n2w, rw, rb, tm=256, xn_all=None, xn_row0=0, n_all=n_all)

    xs = x_sample.reshape(n_s, D_MODEL)
    qs, ks, vs, us, gs, lfs = _inproj(xs, n1w, w_main, w_f, b_f, qnw, knw, tm=n_s, q_scale=1.0)
    cs = _cumsum_rows(lfs, bl=n_s, seg_len=ls)
    q_s = qs.astype(F32).reshape(bs, ls, N_HEADS, HEAD_DIM)
    qbd = jnp.einsum('bthd,hg->bhtgd', q_s, jnp.eye(N_HEADS, dtype=F32))
    qbd = qbd.reshape(bs, N_HEADS * ls, ATTN_WIDTH).astype(BF16)
    cn = jnp.transpose(cs.reshape(bs, ls, FORGET_LANES)[:, :, :N_HEADS], (0, 2, 1))
    cncol = cn.reshape(bs, N_HEADS * ls, 1)
    cnkeys = jnp.broadcast_to(cn[:, :, None, :], (bs, N_HEADS, ls, ls)).reshape(bs, N_HEADS * ls, ls)
    cnkeys = jnp.pad(cnkeys, ((0, 0), (0, 0), (0, PAGE_SIZE - ls)))
    knew = jnp.pad(ks.reshape(bs, ls, ATTN_WIDTH), ((0, 0), (0, PAGE_SIZE - ls), (0, 0)))
    vnew = jnp.pad(vs.reshape(bs, ls, ATTN_WIDTH), ((0, 0), (0, PAGE_SIZE - ls), (0, 0)))
    n_phys = cache_k.shape[1]
    ck = cache_k[0].reshape(n_phys, PAGE_SIZE * N_HEADS, HEAD_DIM)
    cv = cache_v[0].reshape(n_phys, PAGE_SIZE * N_HEADS, HEAD_DIM)
    clf_t = jnp.transpose(cache_logf[0], (0, 2, 1))
    attn_s = _sample_attention(page_table, qbd, cncol, cnkeys, knew, vnew, ck, cv, clf_t)
    halo_src = jnp.pad(state_pool[0], ((0, 0), (POOL_HALO - POOL_HIST, 0), (0, 0)))
    halo_s = pl.BlockSpec((1, POOL_HALO, POOL_WIDTH), lambda bi, i: (bi, 0, 0))
    us3 = us.reshape(bs, ls, POOL_WIDTH)
    pool_s = _pool_mix(halo_src, halo_s, us3, wp, ps, tm=ls, base_pos=n_past, zero_first_halo=False)
    h1_s, xn_all, ti_s, gt_s = _merge(
        attn_s.reshape(n_s, ATTN_WIDTH), pool_s.reshape(n_s, POOL_WIDTH), gs, xs,
        wa, wb, wo, n2w, rw, rb, tm=n_s, xn_all=xn_all, xn_row0=n_p, n_all=n_all)

    n_rank = pl.cdiv(n_all, RANK_TILE) * RANK_TILE
    ti_all = jnp.concatenate([ti_p, ti_s, jnp.full((n_rank - n_all, 128), -1, I32)], axis=0)
    rank, cnt = _expert_rank(ti_all, tr=RANK_TILE)
    top_idx = ti_all[:n_all, :TOP_K]
    counts = cnt[0, :N_EXPERTS].astype(I32)
    n_blocks = pl.cdiv(n_all * TOP_K, MOE_SUB) + N_EXPERTS
    n_slots = n_blocks * MOE_SUB
    n_wi = n_blocks // MOE_NSUB + N_EXPERTS
    assert (n_blocks + (MOE_NSUB - 1) * N_EXPERTS) // MOE_NSUB <= n_wi
    dest, wi_e, wi_row, wi_nsub, wi_tok = _routing_tables(
        top_idx, rank[:n_all, :TOP_K], counts, n_wi, n_slots)
    ys = _moe_experts(wi_e, wi_row, wi_nsub, wi_tok, xn_all, w_gu[0], b_gu[0], w_dn[0], b_dn[0], n_slots)
    y_p = _combine(dest[:n_p], gt_p, h1_p, ys, tt=128)
    y_s = _combine(dest[n_p:], gt_s, h1_s, ys, tt=n_s)

    heads = (N_HEADS, HEAD_DIM)
    k_p = kp.reshape(1, bp, lp, *heads)
    v_p = vp.reshape(1, bp, lp, *heads)
    f_p = lfp.reshape(bp, lp, FORGET_LANES)[None, :, :, :N_HEADS]
    u_p = up3[:, lp - POOL_HIST:][None]
    k_s = ks.reshape(1, bs, ls, *heads)
    v_s = vs.reshape(1, bs, ls, *heads)
    f_s = lfs.reshape(bs, ls, FORGET_LANES)[None, :, :, :N_HEADS]
    pool_state = jnp.concatenate([state_pool[0], us3], axis=1)[:, -POOL_HIST:][None]
    return (y_p.reshape(bp, lp, D_MODEL), y_s.reshape(bs, ls, D_MODEL),
            k_p, v_p, f_p, u_p, k_s, v_s, f_s, pool_state)
```

```python
import functools

import jax
import jax.numpy as jnp
from jax import lax
from jax.experimental import pallas as pl
from jax.experimental.pallas import tpu as pltpu

F32 = jnp.float32
BF16 = jnp.bfloat16
I32 = jnp.int32

D_MODEL = 2048
N_HEADS = 8
HEAD_DIM = 128
ATTN_WIDTH = N_HEADS * HEAD_DIM
ATTN_SCALE = HEAD_DIM ** -0.5
FORGET_LANES = 128
POOL_WINDOWS = (2, 4, 8, 16)
POOL_WIDTH = D_MODEL // 2
GW = POOL_WIDTH // len(POOL_WINDOWS)
POOL_HIST = max(POOL_WINDOWS) - 1
POOL_HALO = 16
N_EXPERTS = 32
TOP_K = 4
D_EXPERT = D_MODEL
SWIGLU_LIMIT = 7.0
SWIGLU_ALPHA = 1.702
NORM_EPS = 1e-6
PAGE_SIZE = 128
NEG_BIG = -1e30

ZQ, ZK, ZV, ZU = 0, 1, 2, 3
Z_WIDTH = 3 * ATTN_WIDTH + POOL_WIDTH + 2 * D_MODEL

MOE_SUB = 128
MOE_NSUB = 9
MOE_TM = MOE_SUB * MOE_NSUB
MOE_TN = 512
RANK_TILE = 512
VMEM_LIMIT = 56 * 1024 * 1024

HIGHEST = lax.Precision.HIGHEST


def _cparams(sem, vmem=VMEM_LIMIT, flags=None):
    return pltpu.CompilerParams(dimension_semantics=sem, vmem_limit_bytes=vmem, flags=flags)


def _log_sigmoid(x):
    return jnp.minimum(x, 0.0) - jnp.log1p(jnp.exp(-jnp.abs(x)))


_NT = (((1,), (1,)), ((), ()))


def _inproj_kernel(x_ref, n1w_ref, wa_ref, wb_ref, wf_ref, bf_ref, qnw_ref, knw_ref,
                   q_ref, k_ref, v_ref, u_ref, g_ref, lf_ref, xn_ref, *, q_scale):
    j = pl.program_id(1)

    @pl.when(j == 0)
    def _():
        x = x_ref[...]
        ms = jnp.mean(x * x, axis=-1, keepdims=True)
        xn = x * lax.rsqrt(ms + NORM_EPS) * n1w_ref[...]
        xn_ref[...] = xn.astype(BF16)
        zf = lax.dot_general(xn_ref[...], wf_ref[...], _NT, preferred_element_type=F32) + bf_ref[...]
        lf_ref[...] = _log_sigmoid(zf)

    def proj(w_ref):
        return lax.dot_general(xn_ref[...], w_ref[...], _NT, preferred_element_type=F32)

    def head_norm(nw_ref, dst_ref, scale):
        z = proj(wa_ref)
        for h in range(N_HEADS):
            sl = slice(h * HEAD_DIM, (h + 1) * HEAD_DIM)
            zh = z[:, sl]
            ms = jnp.mean(zh * zh, axis=-1, keepdims=True)
            y = zh * lax.rsqrt(ms + NORM_EPS) * nw_ref[:, sl]
            if scale != 1.0:
                y = y * scale
            dst_ref[:, sl] = y.astype(dst_ref.dtype)

    @pl.when(j == ZQ)
    def _():
        head_norm(qnw_ref, q_ref, q_scale)

    @pl.when(j == ZK)
    def _():
        head_norm(knw_ref, k_ref, 1.0)

    @pl.when(j == ZV)
    def _():
        v_ref[...] = proj(wa_ref)

    @pl.when(j == ZU)
    def _():
        u_ref[...] = proj(wb_ref)

    @pl.when(j > ZU)
    def _():
        g_ref[...] = jax.nn.sigmoid(proj(wb_ref))


def _inproj(x, n1w, w_qkv_t, w_rest_t, w_f_t, b_f, qnw, knw, tm, q_scale):
    m = x.shape[0]
    tn = ATTN_WIDTH
    row = lambda i, j: (i, 0)
    const = lambda i, j: (0, 0)
    return pl.pallas_call(
        functools.partial(_inproj_kernel, q_scale=q_scale),
        grid=(m // tm, Z_WIDTH // tn),
        in_specs=[
            pl.BlockSpec((tm, D_MODEL), row),
            pl.BlockSpec((1, D_MODEL), const),
            pl.BlockSpec((tn, D_MODEL), lambda i, j: (jnp.minimum(j, ZV), 0)),
            pl.BlockSpec((tn, D_MODEL), lambda i, j: (jnp.maximum(j - ZU, 0), 0)),
            pl.BlockSpec((FORGET_LANES, D_MODEL), const),
            pl.BlockSpec((1, FORGET_LANES), const),
            pl.BlockSpec((1, tn), const),
            pl.BlockSpec((1, tn), const),
        ],
        out_specs=[
            pl.BlockSpec((tm, tn), row),
            pl.BlockSpec((tm, tn), row),
            pl.BlockSpec((tm, tn), row),
            pl.BlockSpec((tm, tn), row),
            pl.BlockSpec((tm, tn), lambda i, j: (i, jnp.maximum(j - (ZU + 1), 0))),
            pl.BlockSpec((tm, FORGET_LANES), row),
        ],
        out_shape=[
            jax.ShapeDtypeStruct((m, ATTN_WIDTH), BF16),
            jax.ShapeDtypeStruct((m, ATTN_WIDTH), F32),
            jax.ShapeDtypeStruct((m, ATTN_WIDTH), F32),
            jax.ShapeDtypeStruct((m, POOL_WIDTH), F32),
            jax.ShapeDtypeStruct((m, 2 * D_MODEL), F32),
            jax.ShapeDtypeStruct((m, FORGET_LANES), F32),
        ],
        scratch_shapes=[pltpu.VMEM((tm, D_MODEL), BF16)],
        compiler_params=_cparams(("arbitrary", "arbitrary")),
        name="inproj",
    )(x, n1w, w_qkv_t, w_rest_t, w_f_t, b_f, qnw, knw)


def _cumsum_kernel(lf_ref, c_ref, carry_ref, *, bl, seg_len):
    i = pl.program_id(0)
    r = lax.broadcasted_iota(I32, (bl, bl), 0)
    c = lax.broadcasted_iota(I32, (bl, bl), 1)
    mask = c <= r
    if seg_len < bl:
        mask = mask & ((r // seg_len) == (c // seg_len))
    tri = mask.astype(F32)
    cs = jnp.dot(tri, lf_ref[...], preferred_element_type=F32, precision=HIGHEST)
    if seg_len > bl:
        @pl.when((i * bl) % seg_len == 0)
        def _():
            carry_ref[...] = jnp.zeros_like(carry_ref)
        cs = cs + carry_ref[...]
        carry_ref[...] = cs[bl - 1:bl, :]
    c_ref[...] = cs


def _cumsum_rows(lf, bl, seg_len):
    m = lf.shape[0]
    return pl.pallas_call(
        functools.partial(_cumsum_kernel, bl=bl, seg_len=seg_len),
        grid=(m // bl,),
        in_specs=[pl.BlockSpec((bl, FORGET_LANES), lambda i: (i, 0))],
        out_specs=pl.BlockSpec((bl, FORGET_LANES), lambda i: (i, 0)),
        out_shape=jax.ShapeDtypeStruct((m, FORGET_LANES), F32),
        scratch_shapes=[pltpu.VMEM((1, FORGET_LANES), F32)],
        compiler_params=_cparams(("arbitrary",)),
        name="logf_cumsum",
    )(lf)


LOG2E = 1.4426950408889634


def _pattn_kernel(q_ref, k_ref, v_ref, c_ref, kbias_ref, o_ref, *scratch, tq, ch):
    m_refs = scratch[0:N_HEADS]
    accl_refs = scratch[N_HEADS:2 * N_HEADS]
    kb_ref, vb_ref = scratch[2 * N_HEADS:]
    qi = pl.program_id(1)
    ki = pl.program_id(2)
    tk = tq

    @pl.when(ki == 0)
    def _():
        for h in range(N_HEADS):
            m_refs[h][...] = jnp.full_like(m_refs[h], -jnp.inf)
            accl_refs[h][...] = jnp.zeros_like(accl_refs[h])

    @pl.when(ki <= qi)
    def _():
        kb_ref[...] = k_ref[0].astype(BF16)
        vb_ref[...] = v_ref[0].astype(BF16)

    def block(diag):
        c2 = c_ref[0] * LOG2E
        q_ones = (lax.broadcasted_iota(I32, (ch, HEAD_DIM), 1) < 3).astype(BF16)
        v_ones = jnp.ones((tk, HEAD_DIM), BF16)
        below = (lax.broadcasted_iota(I32, (ch, ch), 1) <= lax.broadcasted_iota(I32, (ch, ch), 0))

        def where(h, r):
            rows = slice(r * ch, (r + 1) * ch)
            cols = slice(0, (r + 1) * ch if diag else tk)
            return rows, cols, slice(h * HEAD_DIM, (h + 1) * HEAD_DIM)

        def stage_scores(h, r):
            rows, cols, sl = where(h, r)
            q_cat = jnp.concatenate([q_ref[0, rows, sl], q_ones], axis=1)
            k_cat = jnp.concatenate([kb_ref[cols, sl], kbias_ref[0, h, cols, :]], axis=1)
            u = lax.dot_general(q_cat, k_cat, _NT, preferred_element_type=F32)
            if diag:
                last = jnp.where(below, u[:, r * ch:], -jnp.inf)
                u = last if r == 0 else jnp.concatenate([u[:, :r * ch], last], axis=1)
            return u

        def stage_max(h, r, u):
            rows, _, _ = where(h, r)
            c2t = c2[rows, h:h + 1]
            m_prev = m_refs[h][rows, :]
            m_new = jnp.maximum(m_prev, jnp.max(u, axis=-1, keepdims=True) + c2t)
            m_refs[h][rows, :] = m_new
            return jnp.exp2(m_prev - m_new), c2t - m_new[:, 0:1]

        def stage_accumulate(h, r, p, alpha):
            rows, cols, sl = where(h, r)
            v_cat = jnp.concatenate([vb_ref[cols, sl], v_ones[cols, :]], axis=1)
            pv = jnp.dot(p, v_cat, preferred_element_type=F32)
            accl_refs[h][rows, :] = jnp.concatenate([alpha, alpha], axis=1) * accl_refs[h][rows, :] + pv

        units = [(h, r) for r in range(tq // ch) for h in range(N_HEADS)]
        u_of, ab_of, p_of = {}, {}, {}
        for t in range(len(units) + 3):
            if t < len(units):
                u_of[t] = stage_scores(*units[t])
            if 0 <= t - 1 < len(units):
                ab_of[t - 1] = stage_max(*units[t - 1], u_of[t - 1])
            if 0 <= t - 2 < len(units):
                p_of[t - 2] = jnp.exp2(u_of.pop(t - 2) + ab_of[t - 2][1]).astype(BF16)
            if 0 <= t - 3 < len(units):
                stage_accumulate(*units[t - 3], p_of.pop(t - 3), ab_of.pop(t - 3)[0])

    @pl.when(ki < qi)
    def _():
        block(False)

    @pl.when(ki == qi)
    def _():
        block(True)
        for h in range(N_HEADS):
            sl = slice(h * HEAD_DIM, (h + 1) * HEAD_DIM)
            accl = accl_refs[h][...]
            o_ref[0, :, sl] = (accl[:, :HEAD_DIM] / accl[:, HEAD_DIM:]).astype(o_ref.dtype)


def _prompt_attention(q3, k3, v3, c3, kbias, tq=512, ch=128):
    b, l, _ = q3.shape
    tk = tq
    nq = l // tq
    kv_idx = lambda bi, qi, ki: (bi, jnp.minimum(ki, qi), 0)
    return pl.pallas_call(
        functools.partial(_pattn_kernel, tq=tq, ch=ch),
        grid=(b, nq, nq),
        in_specs=[
            pl.BlockSpec((1, tq, ATTN_WIDTH), lambda bi, qi, ki: (bi, qi, 0)),
            pl.BlockSpec((1, tk, ATTN_WIDTH), kv_idx),
            pl.BlockSpec((1, tk, ATTN_WIDTH), kv_idx),
            pl.BlockSpec((1, tq, FORGET_LANES), lambda bi, qi, ki: (bi, qi, 0)),
            pl.BlockSpec((1, N_HEADS, tk, HEAD_DIM),
                         lambda bi, qi, ki: (bi, 0, jnp.minimum(ki, qi), 0)),
        ],
        out_specs=pl.BlockSpec((1, tq, ATTN_WIDTH), lambda bi, qi, ki: (bi, qi, 0)),
        out_shape=jax.ShapeDtypeStruct((b, l, ATTN_WIDTH), BF16),
        scratch_shapes=(
            [pltpu.VMEM((tq, HEAD_DIM), F32) for _ in range(N_HEADS)]
            + [pltpu.VMEM((tq, 2 * HEAD_DIM), F32) for _ in range(N_HEADS)]
            + [pltpu.VMEM((tk, ATTN_WIDTH), BF16), pltpu.VMEM((tk, ATTN_WIDTH), BF16)]),
        compiler_params=_cparams(("arbitrary", "arbitrary", "arbitrary")),
        name="prompt_attention",
    )(q3, k3, v3, c3, kbias)


def _bf16_pieces(x, n):
    out = []
    for _ in range(n):
        piece = x.astype(BF16).astype(F32)
        out.append(piece)
        x = x - piece
    return out


def _kbias_kernel(c_ref, o_ref):
    c = c_ref[0] * (-LOG2E)
    lane = lax.broadcasted_iota(I32, c.shape, 1)
    for h in range(N_HEADS):
        hi, mid, lo = _bf16_pieces(c[:, h:h + 1], 3)
        tile = jnp.where(lane == 0, hi, jnp.where(lane == 1, mid, jnp.where(lane == 2, lo, 0.0)))
        o_ref[0, h] = tile.astype(BF16)


def _key_bias_pieces(c3, tl=512):
    b, l, _ = c3.shape
    return pl.pallas_call(
        _kbias_kernel,
        grid=(b, l // tl),
        in_specs=[pl.BlockSpec((1, tl, FORGET_LANES), lambda bi, i: (bi, i, 0))],
        out_specs=pl.BlockSpec((1, N_HEADS, tl, HEAD_DIM), lambda bi, i: (bi, 0, i, 0)),
        out_shape=jax.ShapeDtypeStruct((b, N_HEADS, l, HEAD_DIM), BF16),
        compiler_params=_cparams(("arbitrary", "arbitrary")),
        name="key_bias",
    )(c3)


def _heads_to_lanes(ref):
    parts = [ref[0, pl.ds(h, PAGE_SIZE, stride=N_HEADS), :] for h in range(N_HEADS)]
    return jnp.concatenate(parts, axis=1).astype(BF16)


def _rows_per_head(x):
    return jnp.concatenate(
        [jnp.broadcast_to(x[h:h + 1, :], (8, x.shape[1])) for h in range(N_HEADS)], axis=0)


def _sattn_kernel(pt_ref, qbd_ref, cncol_ref, cnkeys_ref, knew_ref, vnew_ref, *rest, g_pages):
    del pt_ref
    k_refs = rest[0:g_pages]
    v_refs = rest[g_pages:2 * g_pages]
    lf_refs = rest[2 * g_pages:3 * g_pages]
    o_ref = rest[3 * g_pages]
    m_ref, l_ref, acc_ref, carry_ref = rest[3 * g_pages + 1:]
    jj = pl.program_id(1)
    nj = pl.num_programs(1)
    rows = N_HEADS * 8

    @pl.when(jj == 0)
    def _():
        m_ref[...] = jnp.full_like(m_ref, -jnp.inf)
        l_ref[...] = jnp.zeros_like(l_ref)
        acc_ref[...] = jnp.zeros_like(acc_ref)
        carry_ref[...] = jnp.zeros_like(carry_ref)

    qbd = qbd_ref[0]
    cncol = cncol_ref[0]
    later = (lax.broadcasted_iota(I32, (PAGE_SIZE, PAGE_SIZE), 0)
             > lax.broadcasted_iota(I32, (PAGE_SIZE, PAGE_SIZE), 1)).astype(F32)

    def scores(kcat):
        return lax.dot_general(qbd, kcat, (((1,), (1,)), ((), ())),
                               preferred_element_type=F32) * ATTN_SCALE

    def update(s, vcat, m, l, acc):
        m_new = jnp.maximum(m, jnp.max(s, axis=-1, keepdims=True))
        alpha = jnp.exp(m - m_new)
        p = jnp.exp(s - jnp.concatenate([m_new] * (s.shape[1] // PAGE_SIZE), axis=1))
        l = alpha * l + jnp.sum(p, axis=-1, keepdims=True)
        pv = jnp.dot(p.astype(BF16), vcat, preferred_element_type=F32)
        acc = jnp.concatenate([alpha] * N_HEADS, axis=1) * acc + pv
        return m_new, l, acc

    m, l, acc, carry = m_ref[...], l_ref[...], acc_ref[...], carry_ref[...]
    lf_all = jnp.concatenate([_rows_per_head(lf_refs[g][0]) for g in range(g_pages)], axis=0)
    suffix_all = jnp.dot(lf_all, later, preferred_element_type=F32, precision=HIGHEST)
    bias = []
    for g in range(g_pages):
        bias.append(suffix_all[g * rows:(g + 1) * rows, :] + carry)
        carry = carry + jnp.sum(lf_all[g * rows:(g + 1) * rows, :], axis=-1, keepdims=True)
    kcat = jnp.concatenate([_heads_to_lanes(k_refs[g]) for g in range(g_pages)], axis=0)
    vcat = jnp.concatenate([_heads_to_lanes(v_refs[g]) for g in range(g_pages)], axis=0)
    s = scores(kcat) + cncol + jnp.concatenate(bias, axis=1)
    m, l, acc = update(s, vcat, m, l, acc)

    @pl.when(jj < nj - 1)
    def _():
        m_ref[...] = m
        l_ref[...] = l
        acc_ref[...] = acc
        carry_ref[...] = carry

    @pl.when(jj == nj - 1)
    def _():
        s = scores(knew_ref[0].astype(BF16)) + cncol - cnkeys_ref[0]
        t_row = lax.broadcasted_iota(I32, (rows, PAGE_SIZE), 0) % 8
        s_col = lax.broadcasted_iota(I32, (rows, PAGE_SIZE), 1)
        s = jnp.where(s_col <= t_row, s, -jnp.inf)
        _, l2, acc2 = update(s, vnew_ref[0].astype(BF16), m, l, acc)
        o = acc2 / jnp.concatenate([l2] * N_HEADS, axis=1)
        for h in range(N_HEADS):
            sl = slice(h * HEAD_DIM, (h + 1) * HEAD_DIM)
            o_ref[0, :, sl] = o[h * 8:(h + 1) * 8, sl].astype(o_ref.dtype)


def _sample_attention(page_table, qbd, cncol, cnkeys, knew, vnew, ck, cv, clf_t, g_pages=16):
    bd, n_pages = page_table.shape
    rows = N_HEADS * 8
    pt_flat = page_table.reshape(-1)

    def page_idx(g):
        return lambda b, jj, pt: (pt[b * n_pages + n_pages - 1 - (jj * g_pages + g)], 0, 0)

    per_b = lambda b, jj, pt: (b, 0, 0)
    in_specs = [
        pl.BlockSpec((1, rows, ATTN_WIDTH), per_b),
        pl.BlockSpec((1, rows, 1), per_b),
        pl.BlockSpec((1, rows, PAGE_SIZE), per_b),
        pl.BlockSpec((1, PAGE_SIZE, ATTN_WIDTH), per_b),
        pl.BlockSpec((1, PAGE_SIZE, ATTN_WIDTH), per_b),
    ]
    in_specs += [pl.BlockSpec((1, PAGE_SIZE * N_HEADS, HEAD_DIM), page_idx(g)) for g in range(g_pages)]
    in_specs += [pl.BlockSpec((1, PAGE_SIZE * N_HEADS, HEAD_DIM), page_idx(g)) for g in range(g_pages)]
    in_specs += [pl.BlockSpec((1, N_HEADS, PAGE_SIZE), page_idx(g)) for g in range(g_pages)]
    grid_spec = pltpu.PrefetchScalarGridSpec(
        num_scalar_prefetch=1,
        grid=(bd, n_pages // g_pages),
        in_specs=in_specs,
        out_specs=pl.BlockSpec((1, 8, ATTN_WIDTH), per_b),
        scratch_shapes=[
            pltpu.VMEM((rows, PAGE_SIZE), F32),
            pltpu.VMEM((rows, PAGE_SIZE), F32),
            pltpu.VMEM((rows, ATTN_WIDTH), F32),
            pltpu.VMEM((rows, PAGE_SIZE), F32),
        ],
    )
    return pl.pallas_call(
        functools.partial(_sattn_kernel, g_pages=g_pages),
        grid_spec=grid_spec,
        out_shape=jax.ShapeDtypeStruct((bd, 8, ATTN_WIDTH), BF16),
        compiler_params=_cparams(("arbitrary", "arbitrary")),
        name="sample_attention",
    )(pt_flat, qbd, cncol, cnkeys, knew, vnew,
      *([ck] * g_pages), *([cv] * g_pages), *([clf_t] * g_pages))


def _pool_kernel(halo_ref, u_ref, wp_ref, ps_ref, o_ref, *, tm, base_pos, zero_first_halo):
    i = pl.program_id(1)
    halo = halo_ref[0]
    if zero_first_halo:
        halo = jnp.where(i == 0, 0.0, halo)
    u = u_ref[0]
    ext = jnp.concatenate([halo, u], axis=0)
    pos = base_pos + i * tm + lax.broadcasted_iota(I32, (tm, 1), 0)
    for g, w in enumerate(POOL_WINDOWS):
        sl = slice(g * GW, (g + 1) * GW)
        s = ext[:, sl]
        k = 1
        while k < w:
            s = s + pltpu.roll(s, shift=k, axis=0)
            k *= 2
        wsum = s[POOL_HALO:, :]
        count = jnp.minimum(pos + 1, w).astype(F32)
        mixed = wsum / count - u[:, sl]
        y = jnp.dot(mixed.astype(BF16), wp_ref[g], preferred_element_type=F32)
        o_ref[0, :, sl] = (y * ps_ref[:, sl]).astype(o_ref.dtype)


def _pool_mix(halo_src, halo_spec, u_src, wp, ps, tm, base_pos, zero_first_halo):
    b, l, _ = u_src.shape
    return pl.pallas_call(
        functools.partial(_pool_kernel, tm=tm, base_pos=base_pos, zero_first_halo=zero_first_halo),
        grid=(b, l // tm),
        in_specs=[
            halo_spec,
            pl.BlockSpec((1, tm, POOL_WIDTH), lambda bi, i: (bi, i, 0)),
            pl.BlockSpec((len(POOL_WINDOWS), GW, GW), lambda bi, i: (0, 0, 0)),
            pl.BlockSpec((1, POOL_WIDTH), lambda bi, i: (0, 0)),
        ],
        out_specs=pl.BlockSpec((1, tm, POOL_WIDTH), lambda bi, i: (bi, i, 0)),
        out_shape=jax.ShapeDtypeStruct((b, l, POOL_WIDTH), BF16),
        compiler_params=_cparams(("arbitrary", "arbitrary")),
        name="pool_mix",
    )(halo_src, u_src, wp, ps)


def _merge_kernel(attn_ref, pool_ref, ga_ref, gb_ref, x_ref, wa_ref, wb_ref, wo_ref,
                  n2w_ref, rw_ref, rb_ref, *rest, tm, aliased):
    h_ref, xn_ref, ti_ref, gt_ref, rwp_ref = rest[-5:]

    @pl.when(pl.program_id(0) == 0)
    def _():
        for n, piece in enumerate(_bf16_pieces(rw_ref[...], 3)):
            rwp_ref[n] = piece.astype(BF16)

    a = jnp.dot(attn_ref[...], wa_ref[...], preferred_element_type=F32)
    p = jnp.dot(pool_ref[...], wb_ref[...], preferred_element_type=F32)
    merged = ga_ref[...] * a + gb_ref[...] * p
    o = jnp.dot(merged.astype(BF16), wo_ref[...], preferred_element_type=F32)
    h1 = x_ref[...] + o
    h_ref[...] = h1
    ms = jnp.mean(h1 * h1, axis=-1, keepdims=True)
    xn = h1 * lax.rsqrt(ms + NORM_EPS) * n2w_ref[...]
    xn_ref[...] = xn
    x_hi, x_lo = [v.astype(BF16) for v in _bf16_pieces(xn, 2)]
    logits = rb_ref[...]
    for xp, wn in ((x_hi, 0), (x_hi, 1), (x_lo, 0), (x_hi, 2), (x_lo, 1)):
        logits = logits + jnp.dot(xp, rwp_ref[wn], preferred_element_type=F32)
    lane = lax.broadcasted_iota(I32, (tm, 128), 1).astype(F32)
    vals, idxs = [], []
    work = logits
    for _ in range(TOP_K):
        mv = jnp.max(work, axis=-1, keepdims=True)
        ix = jnp.min(jnp.where(work == mv, lane, 128.0), axis=-1, keepdims=True)
        vals.append(mv)
        idxs.append(ix)
        work = jnp.where(lane == ix, NEG_BIG, work)
    es = [jnp.exp(v - vals[0]) for v in vals]
    den = es[0] + es[1] + es[2] + es[3]
    ti = jnp.zeros((tm, 128), F32)
    gt = jnp.zeros((tm, 128), F32)
    for k in range(TOP_K):
        ti = jnp.where(lane == float(k), idxs[k], ti)
        gt = jnp.where(lane == float(k), es[k] / den, gt)
    ti_ref[...] = ti.astype(I32)
    gt_ref[...] = gt


def _merge(attn, pool, z, x, wa, wb, wo, n2w, rw, rb, tm, xn_all, xn_row0, n_all):
    m = x.shape[0]
    aliased = xn_all is not None
    const = lambda i: (0, 0)
    single = dict(pipeline_mode=pl.Buffered(1))
    xn_blk0 = xn_row0 // tm
    in_specs = [
        pl.BlockSpec((tm, ATTN_WIDTH), lambda i: (i, 0)),
        pl.BlockSpec((tm, POOL_WIDTH), lambda i: (i, 0)),
        pl.BlockSpec((tm, D_MODEL), lambda i: (i, 0)),
        pl.BlockSpec((tm, D_MODEL), lambda i: (i, 1)),
        pl.BlockSpec((tm, D_MODEL), lambda i: (i, 0)),
        pl.BlockSpec((ATTN_WIDTH, D_MODEL), const, **single),
        pl.BlockSpec((POOL_WIDTH, D_MODEL), const, **single),
        pl.BlockSpec((D_MODEL, D_MODEL), const, **single),
        pl.BlockSpec((1, D_MODEL), const),
        pl.BlockSpec((D_MODEL, 128), const),
        pl.BlockSpec((1, 128), const),
    ]
    args = [attn, pool, z, z, x, wa, wb, wo, n2w, rw, rb]
    io_alias = {}
    if aliased:
        in_specs.append(pl.BlockSpec(memory_space=pl.ANY))
        args.append(xn_all)
        io_alias = {len(args) - 1: 1}
    return pl.pallas_call(
        functools.partial(_merge_kernel, tm=tm, aliased=aliased),
        grid=(m // tm,),
        in_specs=in_specs,
        out_specs=[
            pl.BlockSpec((tm, D_MODEL), lambda i: (i, 0)),
            pl.BlockSpec((tm, D_MODEL), lambda i: (xn_blk0 + i, 0)),
            pl.BlockSpec((tm, 128), lambda i: (i, 0)),
            pl.BlockSpec((tm, 128), lambda i: (i, 0)),
        ],
        out_shape=[
            jax.ShapeDtypeStruct((m, D_MODEL), F32),
            jax.ShapeDtypeStruct((n_all, D_MODEL), F32),
            jax.ShapeDtypeStruct((m, 128), I32),
            jax.ShapeDtypeStruct((m, 128), F32),
        ],
        input_output_aliases=io_alias,
        scratch_shapes=[pltpu.VMEM((3, D_MODEL, 128), BF16)],
        compiler_params=_cparams(("arbitrary",)),
        name="merge_router",
    )(*args)


def _rank_kernel(idx_ref, rank_ref, cnt_ref, carry_ref, *, tr):
    i = pl.program_id(0)

    @pl.when(i == 0)
    def _():
        carry_ref[...] = jnp.zeros_like(carry_ref)

    idx = idx_ref[...]
    lane = lax.broadcasted_iota(I32, (tr, 128), 1)
    hits = [idx[:, k:k + 1] == lane for k in range(TOP_K)]
    onehot = jnp.zeros((tr, 128), F32)
    for hk in hits:
        onehot = onehot + hk.astype(F32)
    earlier = (lax.broadcasted_iota(I32, (tr, tr), 1)
               < lax.broadcasted_iota(I32, (tr, tr), 0)).astype(BF16)
    before = jnp.dot(earlier, onehot.astype(BF16), preferred_element_type=F32) + carry_ref[...]
    out = jnp.zeros((tr, 128), I32)
    for k, hk in enumerate(hits):
        rk = jnp.sum(jnp.where(hk, before, 0.0), axis=-1, keepdims=True)
        out = jnp.where(lane == k, rk.astype(I32), out)
    rank_ref[...] = out
    carry_ref[...] = carry_ref[...] + jnp.sum(onehot, axis=0, keepdims=True)
    cnt_ref[...] = carry_ref[...]


def _expert_rank(top_idx_padded, tr):
    n = top_idx_padded.shape[0]
    return pl.pallas_call(
        functools.partial(_rank_kernel, tr=tr),
        grid=(n // tr,),
        in_specs=[pl.BlockSpec((tr, 128), lambda i: (i, 0))],
        out_specs=[
            pl.BlockSpec((tr, 128), lambda i: (i, 0)),
            pl.BlockSpec((1, 128), lambda i: (0, 0)),
        ],
        out_shape=[
            jax.ShapeDtypeStruct((n, 128), I32),
            jax.ShapeDtypeStruct((1, 128), F32),
        ],
        scratch_shapes=[pltpu.VMEM((1, 128), F32)],
        compiler_params=_cparams(("arbitrary",)),
        name="expert_rank",
    )(top_idx_padded)


def _moe_out_block_copy(acc_scr, ys_hbm, sub, row, sem):
    return pltpu.make_async_copy(acc_scr.at[pl.ds(sub * MOE_SUB, MOE_SUB)],
                                 ys_hbm.at[pl.ds(row, MOE_SUB)], sem)


def _moe_kernel(wi_e_ref, wi_row_ref, wi_nsub_ref, tok_ref, xn_hbm,
                wg_ref, wu_ref, bg_ref, bu_ref, wd_ref, bd_ref, ys_hbm,
                x_scr, acc_scr, wg_bf, wu_bf, wd_bf, pend_ref, sem_in, sem_out):
    del wi_e_ref
    w = pl.program_id(0)
    j = pl.program_id(1)
    nj = pl.num_programs(1)
    nsub = wi_nsub_ref[w]
    row0 = wi_row_ref[w]
    issue_unroll = 8

    @pl.when((w == 0) & (j == 0))
    def _():
        pend_ref[0] = 0

    @pl.when((j == 0) & (nsub > 0))
    def _():
        def issue(blk, carry):
            for i in range(issue_unroll):
                r = blk * issue_unroll + i
                tok = tok_ref[0, 0, r]
                pltpu.make_async_copy(xn_hbm.at[pl.ds(tok, 1)], x_scr.at[pl.ds(r, 1)], sem_in).start()
            return carry
        lax.fori_loop(0, nsub * (MOE_SUB // issue_unroll), issue, 0)

    def drain_output():
        for s in range(MOE_NSUB):
            @pl.when(s < pend_ref[0])
            def _():
                _moe_out_block_copy(acc_scr, ys_hbm, s, 0, sem_out).wait()
        pend_ref[0] = 0

    @pl.when(j == 0)
    def _():
        drain_output()

    @pl.when((j == 0) & (nsub > 0))
    def _():
        for s in range(MOE_NSUB):
            @pl.when(s < nsub)
            def _():
                acc_scr[s * MOE_SUB:(s + 1) * MOE_SUB, :] = jnp.broadcast_to(
                    bd_ref[0], (MOE_SUB, D_MODEL))
        for s in range(MOE_NSUB):
            @pl.when(s < nsub)
            def _():
                pltpu.make_async_copy(xn_hbm.at[pl.ds(0, MOE_SUB)],
                                      x_scr.at[pl.ds(s * MOE_SUB, MOE_SUB)], sem_in).wait()

    def expert_rows(rows):
        xs = x_scr[rows, :].astype(BF16)
        gate = jnp.dot(xs, wg_bf[...], preferred_element_type=F32) + bg_ref[0]
        up = jnp.dot(xs, wu_bf[...], preferred_element_type=F32) + bu_ref[0]
        gate = jnp.minimum(gate, SWIGLU_LIMIT)
        up = jnp.clip(up, -SWIGLU_LIMIT, SWIGLU_LIMIT)
        act = (up + 1.0) * (gate * jax.nn.sigmoid(SWIGLU_ALPHA * gate))
        acc_scr[rows, :] += jnp.dot(act.astype(BF16), wd_bf[...], preferred_element_type=F32)

    @pl.when(nsub > 0)
    def _():
        wg_bf[...] = wg_ref[0].astype(BF16)
        wu_bf[...] = wu_ref[0].astype(BF16)
        wd_bf[...] = wd_ref[0].astype(BF16)
        for c in range(MOE_NSUB // 4):
            @pl.when(4 * c + 4 <= nsub)
            def _():
                expert_rows(pl.ds(4 * c * MOE_SUB, 4 * MOE_SUB))

        @pl.when(nsub % 4 >= 2)
        def _():
            expert_rows(pl.ds(pl.multiple_of((nsub // 4) * 4 * MOE_SUB, MOE_SUB), 2 * MOE_SUB))

        @pl.when(nsub % 2 == 1)
        def _():
            expert_rows(pl.ds(pl.multiple_of((nsub - 1) * MOE_SUB, MOE_SUB), MOE_SUB))

    @pl.when((j == nj - 1) & (nsub > 0))
    def _():
        for s in range(MOE_NSUB):
            @pl.when(s < nsub)
            def _():
                _moe_out_block_copy(acc_scr, ys_hbm, s,
                                    pl.multiple_of(row0 + s * MOE_SUB, MOE_SUB), sem_out).start()
        pend_ref[0] = nsub

    @pl.when((w == pl.num_programs(0) - 1) & (j == nj - 1))
    def _():
        drain_output()


def _moe_experts(wi_e, wi_row, wi_nsub, wi_tok, xn_all, w_gu, b_gu, w_dn, b_dn, n_slots):
    n_wi = wi_e.shape[0]
    nj = D_EXPERT // MOE_TN

    def col(w, j, n):
        return jnp.where(n[w] > 0, j, nj - 1)

    grid_spec = pltpu.PrefetchScalarGridSpec(
        num_scalar_prefetch=3,
        grid=(n_wi, nj),
        in_specs=[
            pl.BlockSpec((1, 1, MOE_TM), lambda w, j, e, r, n: (w, 0, 0), memory_space=pltpu.SMEM),
            pl.BlockSpec(memory_space=pl.ANY),
            pl.BlockSpec((1, D_MODEL, MOE_TN), lambda w, j, e, r, n: (e[w], 0, col(w, j, n))),
            pl.BlockSpec((1, D_MODEL, MOE_TN), lambda w, j, e, r, n: (e[w], 0, nj + col(w, j, n))),
            pl.BlockSpec((1, 1, MOE_TN), lambda w, j, e, r, n: (e[w], 0, col(w, j, n))),
            pl.BlockSpec((1, 1, MOE_TN), lambda w, j, e, r, n: (e[w], 0, nj + col(w, j, n))),
            pl.BlockSpec((1, MOE_TN, D_MODEL), lambda w, j, e, r, n: (e[w], col(w, j, n), 0)),
            pl.BlockSpec((1, 1, D_MODEL), lambda w, j, e, r, n: (e[w], 0, 0)),
        ],
        out_specs=pl.BlockSpec(memory_space=pl.ANY),
        scratch_shapes=[
            pltpu.VMEM((MOE_TM, D_MODEL), F32),
            pltpu.VMEM((MOE_TM, D_MODEL), F32),
            pltpu.VMEM((D_MODEL, MOE_TN), BF16),
            pltpu.VMEM((D_MODEL, MOE_TN), BF16),
            pltpu.VMEM((MOE_TN, D_MODEL), BF16),
            pltpu.SMEM((1,), I32),
            pltpu.SemaphoreType.DMA(()),
            pltpu.SemaphoreType.DMA(()),
        ],
    )
    return pl.pallas_call(
        _moe_kernel,
        grid_spec=grid_spec,
        out_shape=jax.ShapeDtypeStruct((n_slots, D_MODEL), F32),
        compiler_params=_cparams(("arbitrary", "arbitrary"), vmem=58 * 1024 * 1024),
        name="moe_experts",
    )(wi_e, wi_row, wi_nsub, wi_tok, xn_all, w_gu, w_gu,
      b_gu.reshape(N_EXPERTS, 1, 2 * D_EXPERT), b_gu.reshape(N_EXPERTS, 1, 2 * D_EXPERT),
      w_dn, b_dn.reshape(N_EXPERTS, 1, D_MODEL))


def _combine_kernel(dest_ref, gt_ref, h_ref, ys_hbm, o_ref, buf, sem, *, tt):
    def issue(r, carry):
        for k in range(TOP_K):
            slot = dest_ref[0, 0, r * TOP_K + k]
            pltpu.make_async_copy(ys_hbm.at[pl.ds(slot, 1)], buf.at[k, pl.ds(r, 1)], sem).start(
                priority=k % 2)
        return carry
    lax.fori_loop(0, tt, issue, 0, unroll=4)
    for k in range(TOP_K):
        pltpu.make_async_copy(ys_hbm.at[pl.ds(0, tt)], buf.at[k], sem).wait()
    gt = gt_ref[...]
    y = gt[:, 0:1] * buf[0]
    for k in range(1, TOP_K):
        y = y + gt[:, k:k + 1] * buf[k]
    o_ref[...] = h_ref[...] + y


def _combine(dest, gates, h1, ys, tt):
    m = h1.shape[0]
    nt = m // tt
    return pl.pallas_call(
        functools.partial(_combine_kernel, tt=tt),
        grid=(nt,),
        in_specs=[
            pl.BlockSpec((1, 1, tt * TOP_K), lambda i: (i, 0, 0), memory_space=pltpu.SMEM),
            pl.BlockSpec((tt, 128), lambda i: (i, 0)),
            pl.BlockSpec((tt, D_MODEL), lambda i: (i, 0)),
            pl.BlockSpec(memory_space=pl.ANY),
        ],
        out_specs=pl.BlockSpec((tt, D_MODEL), lambda i: (i, 0)),
        out_shape=jax.ShapeDtypeStruct((m, D_MODEL), F32),
        scratch_shapes=[pltpu.VMEM((TOP_K, tt, D_MODEL), F32), pltpu.SemaphoreType.DMA(())],
        compiler_params=_cparams(("arbitrary",)),
        name="moe_combine",
    )(dest.reshape(nt, 1, tt * TOP_K), gates, h1, ys)


def _routing_tables(top_idx, rank, counts, n_wi, n_slots):
    n_tok = top_idx.shape[0]
    padded = (counts + MOE_SUB - 1) // MOE_SUB * MOE_SUB
    pad_start = jnp.cumsum(padded) - padded
    dest = pad_start[top_idx] + rank
    tok_ids = jnp.broadcast_to(jnp.arange(n_tok, dtype=I32)[:, None], (n_tok, TOP_K))
    slot_tok = jnp.zeros((n_slots,), I32).at[dest.reshape(-1)].set(
        tok_ids.reshape(-1), unique_indices=True)
    nsubs = padded // MOE_SUB
    items = (nsubs + MOE_NSUB - 1) // MOE_NSUB
    item_end = jnp.cumsum(items)
    total = item_end[-1]
    w = jnp.arange(n_wi, dtype=I32)
    w_eff = jnp.minimum(w, total - 1)
    wi_e = jnp.sum((item_end[None, :] <= w_eff[:, None]).astype(I32), axis=1)
    wi_e = jnp.minimum(wi_e, N_EXPERTS - 1)
    local = w_eff - (item_end - items)[wi_e]
    wi_row = (pad_start[wi_e] + local * MOE_TM).astype(I32)
    wi_nsub = jnp.where(w < total, jnp.clip(nsubs[wi_e] - local * MOE_NSUB, 0, MOE_NSUB), 0).astype(I32)
    tok_pos = jnp.minimum(wi_row[:, None] + jnp.arange(MOE_TM, dtype=I32)[None, :], n_slots - 1)
    wi_tok = slot_tok[tok_pos].reshape(n_wi, 1, MOE_TM)
    return dest.astype(I32), wi_e, wi_row, wi_nsub, wi_tok


def kernel(x_prompt, x_sample, cache_k, cache_v, cache_logf, state_pool, page_table,
           norm1_w, w_in, b_forget, q_norm_w, k_norm_w, w_pool, pool_scale,
           w_branch_a, w_branch_b, w_out, norm2_w, router_w, router_b,
           w_gu, b_gu, w_dn, b_dn):
    depth = w_in.shape[0]
    assert depth == 1
    bp, lp, _ = x_prompt.shape
    bs, ls, _ = x_sample.shape
    assert ls == 8 and N_HEADS == 8
    n_p, n_s = bp * lp, bs * ls
    n_all = n_p + n_s
    n_past = page_table.shape[1] * PAGE_SIZE

    wt = jnp.transpose(w_in[0])
    f_off = 3 * ATTN_WIDTH
    w_qkv = wt[:f_off].astype(BF16)
    w_rest = wt[f_off + N_HEADS:].astype(BF16)
    w_f = jnp.pad(wt[f_off:f_off + N_HEADS], ((0, FORGET_LANES - N_HEADS), (0, 0))).astype(BF16)
    b_f = jnp.pad(b_forget[0], (0, FORGET_LANES - N_HEADS)).reshape(1, FORGET_LANES)
    n1w = norm1_w[0].reshape(1, D_MODEL)
    n2w = norm2_w[0].reshape(1, D_MODEL)
    qnw = jnp.tile(q_norm_w[0], N_HEADS).reshape(1, ATTN_WIDTH)
    knw = jnp.tile(k_norm_w[0], N_HEADS).reshape(1, ATTN_WIDTH)
    wp = w_pool[0].astype(BF16)
    ps = pool_scale[0].reshape(1, POOL_WIDTH)
    wa = w_branch_a[0].astype(BF16)
    wb = w_branch_b[0].astype(BF16)
    wo = w_out[0].astype(BF16)
    rw = jnp.pad(router_w[0], ((0, 0), (0, 128 - N_EXPERTS)))
    rb = jnp.pad(router_b[0], (0, 128 - N_EXPERTS), constant_values=NEG_BIG).reshape(1, 128)

    xp = x_prompt.reshape(n_p, D_MODEL)
    qp, kp, vp, up, gp, lfp = _inproj(xp, n1w, w_qkv, w_rest, w_f, b_f, qnw, knw, tm=512,
                                      q_scale=ATTN_SCALE * LOG2E)
    cp = _cumsum_rows(lfp, bl=512, seg_len=lp)
    cp3 = cp.reshape(bp, lp, FORGET_LANES)
    attn_p = _prompt_attention(qp.reshape(bp, lp, ATTN_WIDTH), kp.reshape(bp, lp, ATTN_WIDTH),
                               vp.reshape(bp, lp, ATTN_WIDTH), cp3, _key_bias_pieces(cp3))
    up3 = up.reshape(bp, lp, POOL_WIDTH)
    halo_p = pl.BlockSpec((1, POOL_HALO, POOL_WIDTH),
                          lambda bi, i: (bi, jnp.maximum(i * (512 // POOL_HALO) - 1, 0), 0))
    pool_p = _pool_mix(up3, halo_p, up3, wp, ps, tm=512, base_pos=0, zero_first_halo=True)
    h1_p, xn_all, ti_p, gt_p = _merge(
        attn_p.reshape(n_p, ATTN_WIDTH), pool_p.reshape(n_p, POOL_WIDTH), gp, xp,
        wa, wb, wo, n2w, rw, rb, tm=256, xn_all=None, xn_row0=0, n_all=n_all)

    xs = x_sample.reshape(n_s, D_MODEL)
    qs, ks, vs, us, gs, lfs = _inproj(xs, n1w, w_qkv, w_rest, w_f, b_f, qnw, knw, tm=n_s, q_scale=1.0)
    cs = _cumsum_rows(lfs, bl=n_s, seg_len=ls)
    q_s = qs.astype(F32).reshape(bs, ls, N_HEADS, HEAD_DIM)
    qbd = jnp.einsum('bthd,hg->bhtgd', q_s, jnp.eye(N_HEADS, dtype=F32))
    qbd = qbd.reshape(bs, N_HEADS * ls, ATTN_WIDTH).astype(BF16)
    cn = jnp.transpose(cs.reshape(bs, ls, FORGET_LANES)[:, :, :N_HEADS], (0, 2, 1))
    cncol = cn.reshape(bs, N_HEADS * ls, 1)
    cnkeys = jnp.broadcast_to(cn[:, :, None, :], (bs, N_HEADS, ls, ls)).reshape(bs, N_HEADS * ls, ls)
    cnkeys = jnp.pad(cnkeys, ((0, 0), (0, 0), (0, PAGE_SIZE - ls)))
    knew = jnp.pad(ks.reshape(bs, ls, ATTN_WIDTH), ((0, 0), (0, PAGE_SIZE - ls), (0, 0)))
    vnew = jnp.pad(vs.reshape(bs, ls, ATTN_WIDTH), ((0, 0), (0, PAGE_SIZE - ls), (0, 0)))
    n_phys = cache_k.shape[1]
    ck = cache_k[0].reshape(n_phys, PAGE_SIZE * N_HEADS, HEAD_DIM)
    cv = cache_v[0].reshape(n_phys, PAGE_SIZE * N_HEADS, HEAD_DIM)
    clf_t = jnp.transpose(cache_logf[0], (0, 2, 1))
    attn_s = _sample_attention(page_table, qbd, cncol, cnkeys, knew, vnew, ck, cv, clf_t)
    halo_src = jnp.pad(state_pool[0], ((0, 0), (POOL_HALO - POOL_HIST, 0), (0, 0)))
    halo_s = pl.BlockSpec((1, POOL_HALO, POOL_WIDTH), lambda bi, i: (bi, 0, 0))
    us3 = us.reshape(bs, ls, POOL_WIDTH)
    pool_s = _pool_mix(halo_src, halo_s, us3, wp, ps, tm=ls, base_pos=n_past, zero_first_halo=False)
    h1_s, xn_all, ti_s, gt_s = _merge(
        attn_s.reshape(n_s, ATTN_WIDTH), pool_s.reshape(n_s, POOL_WIDTH), gs, xs,
        wa, wb, wo, n2w, rw, rb, tm=n_s, xn_all=xn_all, xn_row0=n_p, n_all=n_all)

    n_rank = pl.cdiv(n_all, RANK_TILE) * RANK_TILE
    ti_all = jnp.concatenate([ti_p, ti_s, jnp.full((n_rank - n_all, 128), -1, I32)], axis=0)
    rank, cnt = _expert_rank(ti_all, tr=RANK_TILE)
    top_idx = ti_all[:n_all, :TOP_K]
    counts = cnt[0, :N_EXPERTS].astype(I32)
    n_blocks = pl.cdiv(n_all * TOP_K, MOE_SUB) + N_EXPERTS
    n_slots = n_blocks * MOE_SUB
    n_wi = n_blocks // MOE_NSUB + N_EXPERTS
    assert (n_blocks + (MOE_NSUB - 1) * N_EXPERTS) // MOE_NSUB <= n_wi
    dest, wi_e, wi_row, wi_nsub, wi_tok = _routing_tables(
        top_idx, rank[:n_all, :TOP_K], counts, n_wi, n_slots)
    ys = _moe_experts(wi_e, wi_row, wi_nsub, wi_tok, xn_all, w_gu[0], b_gu[0], w_dn[0], b_dn[0], n_slots)
    y_p = _combine(dest[:n_p], gt_p, h1_p, ys, tt=128)
    y_s = _combine(dest[n_p:], gt_s, h1_s, ys, tt=n_s)

    heads = (N_HEADS, HEAD_DIM)
    k_p = kp.reshape(1, bp, lp, *heads)
    v_p = vp.reshape(1, bp, lp, *heads)
    f_p = lfp.reshape(bp, lp, FORGET_LANES)[None, :, :, :N_HEADS]
    u_p = up3[:, lp - POOL_HIST:][None]
    k_s = ks.reshape(1, bs, ls, *heads)
    v_s = vs.reshape(1, bs, ls, *heads)
    f_s = lfs.reshape(bs, ls, FORGET_LANES)[None, :, :, :N_HEADS]
    pool_state = jnp.concatenate([state_pool[0], us3], axis=1)[:, -POOL_HIST:][None]
    return (y_p.reshape(bp, lp, D_MODEL), y_s.reshape(bs, ls, D_MODEL),
            k_p, v_p, f_p, u_p, k_s, v_s, f_s, pool_state)
```

```python
import functools

import jax
import jax.numpy as jnp
from jax import lax
from jax.experimental import pallas as pl
from jax.experimental.pallas import tpu as pltpu

F32 = jnp.float32
BF16 = jnp.bfloat16
I32 = jnp.int32

D_MODEL = 2048
N_HEADS = 8
HEAD_DIM = 128
ATTN_WIDTH = N_HEADS * HEAD_DIM
ATTN_SCALE = HEAD_DIM ** -0.5
FORGET_LANES = 128
POOL_WINDOWS = (2, 4, 8, 16)
POOL_WIDTH = D_MODEL // 2
GW = POOL_WIDTH // len(POOL_WINDOWS)
POOL_HIST = max(POOL_WINDOWS) - 1
POOL_HALO = 16
N_EXPERTS = 32
TOP_K = 4
D_EXPERT = D_MODEL
SWIGLU_LIMIT = 7.0
SWIGLU_ALPHA = 1.702
NORM_EPS = 1e-6
PAGE_SIZE = 128
NEG_BIG = -1e30

ZQ, ZK, ZV, ZU = 0, 1, 2, 3
Z_WIDTH = 3 * ATTN_WIDTH + POOL_WIDTH + 2 * D_MODEL

MOE_SUB = 128
MOE_NSUB = 9
MOE_TM = MOE_SUB * MOE_NSUB
MOE_TN = 512
RANK_TILE = 512
VMEM_LIMIT = 56 * 1024 * 1024

HIGHEST = lax.Precision.HIGHEST


def _cparams(sem, vmem=VMEM_LIMIT, flags=None):
    return pltpu.CompilerParams(dimension_semantics=sem, vmem_limit_bytes=vmem, flags=flags)


def _log_sigmoid(x):
    return jnp.minimum(x, 0.0) - jnp.log1p(jnp.exp(-jnp.abs(x)))


_NT = (((1,), (1,)), ((), ()))


def _inproj_kernel(x_ref, n1w_ref, w_ref, wf_ref, bf_ref, qnw_ref, knw_ref,
                   q_ref, k_ref, v_ref, u_ref, g_ref, lf_ref, xn_ref, *, q_scale, sub):
    j = pl.program_id(1)

    @pl.when(j == 0)
    def _():
        x = x_ref[...]
        ms = jnp.mean(x * x, axis=-1, keepdims=True)
        xn = x * lax.rsqrt(ms + NORM_EPS) * n1w_ref[...]
        xn_ref[...] = xn.astype(BF16)
        zf = lax.dot_general(xn_ref[...], wf_ref[...], _NT, preferred_element_type=F32) + bf_ref[...]
        lf_ref[...] = _log_sigmoid(zf)

    def proj():
        return lax.dot_general(xn_ref[...], w_ref[...], _NT, preferred_element_type=F32)

    def in_segment(seg):
        return (j >= seg * sub) & (j < (seg + 1) * sub)

    def head_norm(nw_ref, dst_ref, scale):
        z = proj()
        for h in range(z.shape[1] // HEAD_DIM):
            sl = slice(h * HEAD_DIM, (h + 1) * HEAD_DIM)
            zh = z[:, sl]
            ms = jnp.mean(zh * zh, axis=-1, keepdims=True)
            y = zh * lax.rsqrt(ms + NORM_EPS) * nw_ref[:, sl]
            if scale != 1.0:
                y = y * scale
            dst_ref[:, sl] = y.astype(dst_ref.dtype)

    @pl.when(in_segment(ZQ))
    def _():
        head_norm(qnw_ref, q_ref, q_scale)

    @pl.when(in_segment(ZK))
    def _():
        head_norm(knw_ref, k_ref, 1.0)

    @pl.when(in_segment(ZV))
    def _():
        v_ref[...] = proj()

    @pl.when(in_segment(ZU))
    def _():
        u_ref[...] = proj()

    @pl.when(j >= (ZU + 1) * sub)
    def _():
        g_ref[...] = jax.nn.sigmoid(proj())


def _inproj(x, n1w, w_main_t, w_f_t, b_f, qnw, knw, tm, tn, q_scale):
    m = x.shape[0]
    sub = ATTN_WIDTH // tn
    row = lambda i, j: (i, 0)
    const = lambda i, j: (0, 0)

    def seg_part(seg):
        return lambda i, j: (i, jnp.clip(j - seg * sub, 0, sub - 1))

    return pl.pallas_call(
        functools.partial(_inproj_kernel, q_scale=q_scale, sub=sub),
        grid=(m // tm, Z_WIDTH // tn),
        in_specs=[
            pl.BlockSpec((tm, D_MODEL), row),
            pl.BlockSpec((1, D_MODEL), const),
            pl.BlockSpec((tn, D_MODEL), lambda i, j: (j, 0)),
            pl.BlockSpec((FORGET_LANES, D_MODEL), const),
            pl.BlockSpec((1, FORGET_LANES), const),
            pl.BlockSpec((1, tn), const),
            pl.BlockSpec((1, tn), const),
        ],
        out_specs=[
            pl.BlockSpec((tm, tn), seg_part(ZQ)),
            pl.BlockSpec((tm, tn), seg_part(ZK)),
            pl.BlockSpec((tm, tn), seg_part(ZV)),
            pl.BlockSpec((tm, tn), seg_part(ZU)),
            pl.BlockSpec((tm, tn), lambda i, j: (i, jnp.maximum(j - (ZU + 1) * sub, 0))),
            pl.BlockSpec((tm, FORGET_LANES), row),
        ],
        out_shape=[
            jax.ShapeDtypeStruct((m, ATTN_WIDTH), BF16),
            jax.ShapeDtypeStruct((m, ATTN_WIDTH), F32),
            jax.ShapeDtypeStruct((m, ATTN_WIDTH), F32),
            jax.ShapeDtypeStruct((m, POOL_WIDTH), F32),
            jax.ShapeDtypeStruct((m, 2 * D_MODEL), F32),
            jax.ShapeDtypeStruct((m, FORGET_LANES), F32),
        ],
        scratch_shapes=[pltpu.VMEM((tm, D_MODEL), BF16)],
        compiler_params=_cparams(("arbitrary", "arbitrary")),
        name="inproj",
    )(x, n1w, w_main_t, w_f_t, b_f, qnw, knw)


def _cumsum_kernel(lf_ref, c_ref, carry_ref, *, bl, seg_len):
    i = pl.program_id(0)
    r = lax.broadcasted_iota(I32, (bl, bl), 0)
    c = lax.broadcasted_iota(I32, (bl, bl), 1)
    mask = c <= r
    if seg_len < bl:
        mask = mask & ((r // seg_len) == (c // seg_len))
    tri = mask.astype(F32)
    cs = jnp.dot(tri, lf_ref[...], preferred_element_type=F32, precision=HIGHEST)
    if seg_len > bl:
        @pl.when((i * bl) % seg_len == 0)
        def _():
            carry_ref[...] = jnp.zeros_like(carry_ref)
        cs = cs + carry_ref[...]
        carry_ref[...] = cs[bl - 1:bl, :]
    c_ref[...] = cs


def _cumsum_rows(lf, bl, seg_len):
    m = lf.shape[0]
    return pl.pallas_call(
        functools.partial(_cumsum_kernel, bl=bl, seg_len=seg_len),
        grid=(m // bl,),
        in_specs=[pl.BlockSpec((bl, FORGET_LANES), lambda i: (i, 0))],
        out_specs=pl.BlockSpec((bl, FORGET_LANES), lambda i: (i, 0)),
        out_shape=jax.ShapeDtypeStruct((m, FORGET_LANES), F32),
        scratch_shapes=[pltpu.VMEM((1, FORGET_LANES), F32)],
        compiler_params=_cparams(("arbitrary",)),
        name="logf_cumsum",
    )(lf)


LOG2E = 1.4426950408889634


def _pattn_kernel(q_ref, k_ref, v_ref, c_ref, kbias_ref, o_ref, *scratch, tq, ch):
    m_refs = scratch[0:N_HEADS]
    accl_refs = scratch[N_HEADS:2 * N_HEADS]
    kb_ref, vb_ref = scratch[2 * N_HEADS:]
    qi = pl.program_id(1)
    ki = pl.program_id(2)
    tk = tq

    @pl.when(ki == 0)
    def _():
        for h in range(N_HEADS):
            m_refs[h][...] = jnp.full_like(m_refs[h], -jnp.inf)
            accl_refs[h][...] = jnp.zeros_like(accl_refs[h])

    @pl.when(ki <= qi)
    def _():
        kb_ref[...] = k_ref[0].astype(BF16)
        vb_ref[...] = v_ref[0].astype(BF16)

    def block(diag):
        c2 = c_ref[0] * LOG2E
        q_ones = (lax.broadcasted_iota(I32, (ch, HEAD_DIM), 1) < 3).astype(BF16)
        v_ones = jnp.ones((tk, HEAD_DIM), BF16)
        below = (lax.broadcasted_iota(I32, (ch, ch), 1) <= lax.broadcasted_iota(I32, (ch, ch), 0))

        def where(h, r):
            rows = slice(r * ch, (r + 1) * ch)
            cols = slice(0, (r + 1) * ch if diag else tk)
            return rows, cols, slice(h * HEAD_DIM, (h + 1) * HEAD_DIM)

        def stage_scores(h, r):
            rows, cols, sl = where(h, r)
            q_cat = jnp.concatenate([q_ref[0, rows, sl], q_ones], axis=1)
            k_cat = jnp.concatenate([kb_ref[cols, sl], kbias_ref[0, h, cols, :]], axis=1)
            u = lax.dot_general(q_cat, k_cat, _NT, preferred_element_type=F32)
            if diag:
                last = jnp.where(below, u[:, r * ch:], -jnp.inf)
                u = last if r == 0 else jnp.concatenate([u[:, :r * ch], last], axis=1)
            return u

        def stage_max(h, r, u):
            rows, _, _ = where(h, r)
            c2t = c2[rows, h:h + 1]
            m_prev = m_refs[h][rows, :]
            m_new = jnp.maximum(m_prev, jnp.max(u, axis=-1, keepdims=True) + c2t)
            m_refs[h][rows, :] = m_new
            return jnp.exp2(m_prev - m_new), c2t - m_new[:, 0:1]

        def stage_accumulate(h, r, p, alpha):
            rows, cols, sl = where(h, r)
            v_cat = jnp.concatenate([vb_ref[cols, sl], v_ones[cols, :]], axis=1)
            pv = jnp.dot(p, v_cat, preferred_element_type=F32)
            accl_refs[h][rows, :] = jnp.concatenate([alpha, alpha], axis=1) * accl_refs[h][rows, :] + pv

        units = [(h, r) for r in range(tq // ch) for h in range(N_HEADS)]
        u_of, ab_of, p_of = {}, {}, {}
        for t in range(len(units) + 3):
            if t < len(units):
                u_of[t] = stage_scores(*units[t])
            if 0 <= t - 1 < len(units):
                ab_of[t - 1] = stage_max(*units[t - 1], u_of[t - 1])
            if 0 <= t - 2 < len(units):
                p_of[t - 2] = jnp.exp2(u_of.pop(t - 2) + ab_of[t - 2][1]).astype(BF16)
            if 0 <= t - 3 < len(units):
                stage_accumulate(*units[t - 3], p_of.pop(t - 3), ab_of.pop(t - 3)[0])

    @pl.when(ki < qi)
    def _():
        block(False)

    @pl.when(ki == qi)
    def _():
        block(True)
        for h in range(N_HEADS):
            sl = slice(h * HEAD_DIM, (h + 1) * HEAD_DIM)
            accl = accl_refs[h][...]
            o_ref[0, :, sl] = (accl[:, :HEAD_DIM] / accl[:, HEAD_DIM:]).astype(o_ref.dtype)


def _prompt_attention(q3, k3, v3, c3, kbias, tq=512, ch=128):
    b, l, _ = q3.shape
    tk = tq
    nq = l // tq
    kv_idx = lambda bi, qi, ki: (bi, jnp.minimum(ki, qi), 0)
    return pl.pallas_call(
        functools.partial(_pattn_kernel, tq=tq, ch=ch),
        grid=(b, nq, nq),
        in_specs=[
            pl.BlockSpec((1, tq, ATTN_WIDTH), lambda bi, qi, ki: (bi, qi, 0)),
            pl.BlockSpec((1, tk, ATTN_WIDTH), kv_idx),
            pl.BlockSpec((1, tk, ATTN_WIDTH), kv_idx),
            pl.BlockSpec((1, tq, FORGET_LANES), lambda bi, qi, ki: (bi, qi, 0)),
            pl.BlockSpec((1, N_HEADS, tk, HEAD_DIM),
                         lambda bi, qi, ki: (bi, 0, jnp.minimum(ki, qi), 0)),
        ],
        out_specs=pl.BlockSpec((1, tq, ATTN_WIDTH), lambda bi, qi, ki: (bi, qi, 0)),
        out_shape=jax.ShapeDtypeStruct((b, l, ATTN_WIDTH), BF16),
        scratch_shapes=(
            [pltpu.VMEM((tq, HEAD_DIM), F32) for _ in range(N_HEADS)]
            + [pltpu.VMEM((tq, 2 * HEAD_DIM), F32) for _ in range(N_HEADS)]
            + [pltpu.VMEM((tk, ATTN_WIDTH), BF16), pltpu.VMEM((tk, ATTN_WIDTH), BF16)]),
        compiler_params=_cparams(("arbitrary", "arbitrary", "arbitrary")),
        name="prompt_attention",
    )(q3, k3, v3, c3, kbias)


def _bf16_pieces(x, n):
    out = []
    for _ in range(n):
        piece = x.astype(BF16).astype(F32)
        out.append(piece)
        x = x - piece
    return out


def _kbias_kernel(c_ref, o_ref):
    c = c_ref[0] * (-LOG2E)
    lane = lax.broadcasted_iota(I32, c.shape, 1)
    for h in range(N_HEADS):
        hi, mid, lo = _bf16_pieces(c[:, h:h + 1], 3)
        tile = jnp.where(lane == 0, hi, jnp.where(lane == 1, mid, jnp.where(lane == 2, lo, 0.0)))
        o_ref[0, h] = tile.astype(BF16)


def _key_bias_pieces(c3, tl=512):
    b, l, _ = c3.shape
    return pl.pallas_call(
        _kbias_kernel,
        grid=(b, l // tl),
        in_specs=[pl.BlockSpec((1, tl, FORGET_LANES), lambda bi, i: (bi, i, 0))],
        out_specs=pl.BlockSpec((1, N_HEADS, tl, HEAD_DIM), lambda bi, i: (bi, 0, i, 0)),
        out_shape=jax.ShapeDtypeStruct((b, N_HEADS, l, HEAD_DIM), BF16),
        compiler_params=_cparams(("arbitrary", "arbitrary")),
        name="key_bias",
    )(c3)


def _heads_to_lanes(ref):
    parts = [ref[0, pl.ds(h, PAGE_SIZE, stride=N_HEADS), :] for h in range(N_HEADS)]
    return jnp.concatenate(parts, axis=1).astype(BF16)


def _rows_per_head(x):
    return jnp.concatenate(
        [jnp.broadcast_to(x[h:h + 1, :], (8, x.shape[1])) for h in range(N_HEADS)], axis=0)


def _sattn_kernel(pt_ref, qbd_ref, cncol_ref, cnkeys_ref, knew_ref, vnew_ref, *rest, g_pages):
    del pt_ref
    k_refs = rest[0:g_pages]
    v_refs = rest[g_pages:2 * g_pages]
    lf_refs = rest[2 * g_pages:3 * g_pages]
    o_ref = rest[3 * g_pages]
    m_ref, l_ref, acc_ref, carry_ref = rest[3 * g_pages + 1:]
    jj = pl.program_id(1)
    nj = pl.num_programs(1)
    rows = N_HEADS * 8

    @pl.when(jj == 0)
    def _():
        m_ref[...] = jnp.full_like(m_ref, -jnp.inf)
        l_ref[...] = jnp.zeros_like(l_ref)
        acc_ref[...] = jnp.zeros_like(acc_ref)
        carry_ref[...] = jnp.zeros_like(carry_ref)

    qbd = qbd_ref[0]
    cncol = cncol_ref[0]
    later = (lax.broadcasted_iota(I32, (PAGE_SIZE, PAGE_SIZE), 0)
             > lax.broadcasted_iota(I32, (PAGE_SIZE, PAGE_SIZE), 1)).astype(F32)

    def scores(kcat):
        return lax.dot_general(qbd, kcat, (((1,), (1,)), ((), ())),
                               preferred_element_type=F32) * ATTN_SCALE

    def update(s, vcat, m, l, acc):
        m_new = jnp.maximum(m, jnp.max(s, axis=-1, keepdims=True))
        alpha = jnp.exp(m - m_new)
        p = jnp.exp(s - jnp.concatenate([m_new] * (s.shape[1] // PAGE_SIZE), axis=1))
        l = alpha * l + jnp.sum(p, axis=-1, keepdims=True)
        pv = jnp.dot(p.astype(BF16), vcat, preferred_element_type=F32)
        acc = jnp.concatenate([alpha] * N_HEADS, axis=1) * acc + pv
        return m_new, l, acc

    m, l, acc, carry = m_ref[...], l_ref[...], acc_ref[...], carry_ref[...]
    lf_all = jnp.concatenate([_rows_per_head(lf_refs[g][0]) for g in range(g_pages)], axis=0)
    suffix_all = jnp.dot(lf_all, later, preferred_element_type=F32, precision=HIGHEST)
    bias = []
    for g in range(g_pages):
        bias.append(suffix_all[g * rows:(g + 1) * rows, :] + carry)
        carry = carry + jnp.sum(lf_all[g * rows:(g + 1) * rows, :], axis=-1, keepdims=True)
    kcat = jnp.concatenate([_heads_to_lanes(k_refs[g]) for g in range(g_pages)], axis=0)
    vcat = jnp.concatenate([_heads_to_lanes(v_refs[g]) for g in range(g_pages)], axis=0)
    s = scores(kcat) + cncol + jnp.concatenate(bias, axis=1)
    m, l, acc = update(s, vcat, m, l, acc)

    @pl.when(jj < nj - 1)
    def _():
        m_ref[...] = m
        l_ref[...] = l
        acc_ref[...] = acc
        carry_ref[...] = carry

    @pl.when(jj == nj - 1)
    def _():
        s = scores(knew_ref[0].astype(BF16)) + cncol - cnkeys_ref[0]
        t_row = lax.broadcasted_iota(I32, (rows, PAGE_SIZE), 0) % 8
        s_col = lax.broadcasted_iota(I32, (rows, PAGE_SIZE), 1)
        s = jnp.where(s_col <= t_row, s, -jnp.inf)
        _, l2, acc2 = update(s, vnew_ref[0].astype(BF16), m, l, acc)
        o = acc2 / jnp.concatenate([l2] * N_HEADS, axis=1)
        for h in range(N_HEADS):
            sl = slice(h * HEAD_DIM, (h + 1) * HEAD_DIM)
            o_ref[0, :, sl] = o[h * 8:(h + 1) * 8, sl].astype(o_ref.dtype)


def _sample_attention(page_table, qbd, cncol, cnkeys, knew, vnew, ck, cv, clf_t, g_pages=16):
    bd, n_pages = page_table.shape
    rows = N_HEADS * 8
    pt_flat = page_table.reshape(-1)

    def page_idx(g):
        return lambda b, jj, pt: (pt[b * n_pages + n_pages - 1 - (jj * g_pages + g)], 0, 0)

    per_b = lambda b, jj, pt: (b, 0, 0)
    in_specs = [
        pl.BlockSpec((1, rows, ATTN_WIDTH), per_b),
        pl.BlockSpec((1, rows, 1), per_b),
        pl.BlockSpec((1, rows, PAGE_SIZE), per_b),
        pl.BlockSpec((1, PAGE_SIZE, ATTN_WIDTH), per_b),
        pl.BlockSpec((1, PAGE_SIZE, ATTN_WIDTH), per_b),
    ]
    in_specs += [pl.BlockSpec((1, PAGE_SIZE * N_HEADS, HEAD_DIM), page_idx(g)) for g in range(g_pages)]
    in_specs += [pl.BlockSpec((1, PAGE_SIZE * N_HEADS, HEAD_DIM), page_idx(g)) for g in range(g_pages)]
    in_specs += [pl.BlockSpec((1, N_HEADS, PAGE_SIZE), page_idx(g)) for g in range(g_pages)]
    grid_spec = pltpu.PrefetchScalarGridSpec(
        num_scalar_prefetch=1,
        grid=(bd, n_pages // g_pages),
        in_specs=in_specs,
        out_specs=pl.BlockSpec((1, 8, ATTN_WIDTH), per_b),
        scratch_shapes=[
            pltpu.VMEM((rows, PAGE_SIZE), F32),
            pltpu.VMEM((rows, PAGE_SIZE), F32),
            pltpu.VMEM((rows, ATTN_WIDTH), F32),
            pltpu.VMEM((rows, PAGE_SIZE), F32),
        ],
    )
    return pl.pallas_call(
        functools.partial(_sattn_kernel, g_pages=g_pages),
        grid_spec=grid_spec,
        out_shape=jax.ShapeDtypeStruct((bd, 8, ATTN_WIDTH), BF16),
        compiler_params=_cparams(("arbitrary", "arbitrary")),
        name="sample_attention",
    )(pt_flat, qbd, cncol, cnkeys, knew, vnew,
      *([ck] * g_pages), *([cv] * g_pages), *([clf_t] * g_pages))


def _pool_kernel(halo_ref, u_ref, wp_ref, ps_ref, o_ref, *, tm, base_pos, zero_first_halo):
    i = pl.program_id(1)
    halo = halo_ref[0]
    if zero_first_halo:
        halo = jnp.where(i == 0, 0.0, halo)
    u = u_ref[0]
    ext = jnp.concatenate([halo, u], axis=0)
    pos = base_pos + i * tm + lax.broadcasted_iota(I32, (tm, 1), 0)
    for g, w in enumerate(POOL_WINDOWS):
        sl = slice(g * GW, (g + 1) * GW)
        s = ext[:, sl]
        k = 1
        while k < w:
            s = s + pltpu.roll(s, shift=k, axis=0)
            k *= 2
        wsum = s[POOL_HALO:, :]
        count = jnp.minimum(pos + 1, w).astype(F32)
        mixed = wsum / count - u[:, sl]
        y = jnp.dot(mixed.astype(BF16), wp_ref[g], preferred_element_type=F32)
        o_ref[0, :, sl] = (y * ps_ref[:, sl]).astype(o_ref.dtype)


def _pool_mix(halo_src, halo_spec, u_src, wp, ps, tm, base_pos, zero_first_halo):
    b, l, _ = u_src.shape
    return pl.pallas_call(
        functools.partial(_pool_kernel, tm=tm, base_pos=base_pos, zero_first_halo=zero_first_halo),
        grid=(b, l // tm),
        in_specs=[
            halo_spec,
            pl.BlockSpec((1, tm, POOL_WIDTH), lambda bi, i: (bi, i, 0)),
            pl.BlockSpec((len(POOL_WINDOWS), GW, GW), lambda bi, i: (0, 0, 0)),
            pl.BlockSpec((1, POOL_WIDTH), lambda bi, i: (0, 0)),
        ],
        out_specs=pl.BlockSpec((1, tm, POOL_WIDTH), lambda bi, i: (bi, i, 0)),
        out_shape=jax.ShapeDtypeStruct((b, l, POOL_WIDTH), BF16),
        compiler_params=_cparams(("arbitrary", "arbitrary")),
        name="pool_mix",
    )(halo_src, u_src, wp, ps)


def _merge_kernel(attn_ref, pool_ref, ga_ref, gb_ref, x_ref, wa_ref, wb_ref, wo_ref,
                  n2w_ref, rw_ref, rb_ref, *rest, tm, aliased):
    h_ref, xn_ref, ti_ref, gt_ref, rwp_ref = rest[-5:]

    @pl.when(pl.program_id(0) == 0)
    def _():
        for n, piece in enumerate(_bf16_pieces(rw_ref[...], 3)):
            rwp_ref[n] = piece.astype(BF16)

    a = jnp.dot(attn_ref[...], wa_ref[...], preferred_element_type=F32)
    p = jnp.dot(pool_ref[...], wb_ref[...], preferred_element_type=F32)
    merged = ga_ref[...] * a + gb_ref[...] * p
    o = jnp.dot(merged.astype(BF16), wo_ref[...], preferred_element_type=F32)
    h1 = x_ref[...] + o
    h_ref[...] = h1
    ms = jnp.mean(h1 * h1, axis=-1, keepdims=True)
    xn = h1 * lax.rsqrt(ms + NORM_EPS) * n2w_ref[...]
    xn_ref[...] = xn
    x_hi, x_lo = [v.astype(BF16) for v in _bf16_pieces(xn, 2)]
    logits = rb_ref[...]
    for xp, wn in ((x_hi, 0), (x_hi, 1), (x_lo, 0), (x_hi, 2), (x_lo, 1)):
        logits = logits + jnp.dot(xp, rwp_ref[wn], preferred_element_type=F32)
    lane = lax.broadcasted_iota(I32, (tm, 128), 1).astype(F32)
    vals, idxs = [], []
    work = logits
    for _ in range(TOP_K):
        mv = jnp.max(work, axis=-1, keepdims=True)
        ix = jnp.min(jnp.where(work == mv, lane, 128.0), axis=-1, keepdims=True)
        vals.append(mv)
        idxs.append(ix)
        work = jnp.where(lane == ix, NEG_BIG, work)
    es = [jnp.exp(v - vals[0]) for v in vals]
    den = es[0] + es[1] + es[2] + es[3]
    ti = jnp.zeros((tm, 128), F32)
    gt = jnp.zeros((tm, 128), F32)
    for k in range(TOP_K):
        ti = jnp.where(lane == float(k), idxs[k], ti)
        gt = jnp.where(lane == float(k), es[k] / den, gt)
    ti_ref[...] = ti.astype(I32)
    gt_ref[...] = gt


def _merge(attn, pool, z, x, wa, wb, wo, n2w, rw, rb, tm, xn_all, xn_row0, n_all):
    m = x.shape[0]
    aliased = xn_all is not None
    const = lambda i: (0, 0)
    single = dict(pipeline_mode=pl.Buffered(1))
    xn_blk0 = xn_row0 // tm
    in_specs = [
        pl.BlockSpec((tm, ATTN_WIDTH), lambda i: (i, 0)),
        pl.BlockSpec((tm, POOL_WIDTH), lambda i: (i, 0)),
        pl.BlockSpec((tm, D_MODEL), lambda i: (i, 0)),
        pl.BlockSpec((tm, D_MODEL), lambda i: (i, 1)),
        pl.BlockSpec((tm, D_MODEL), lambda i: (i, 0)),
        pl.BlockSpec((ATTN_WIDTH, D_MODEL), const, **single),
        pl.BlockSpec((POOL_WIDTH, D_MODEL), const, **single),
        pl.BlockSpec((D_MODEL, D_MODEL), const, **single),
        pl.BlockSpec((1, D_MODEL), const),
        pl.BlockSpec((D_MODEL, 128), const),
        pl.BlockSpec((1, 128), const),
    ]
    args = [attn, pool, z, z, x, wa, wb, wo, n2w, rw, rb]
    io_alias = {}
    if aliased:
        in_specs.append(pl.BlockSpec(memory_space=pl.ANY))
        args.append(xn_all)
        io_alias = {len(args) - 1: 1}
    return pl.pallas_call(
        functools.partial(_merge_kernel, tm=tm, aliased=aliased),
        grid=(m // tm,),
        in_specs=in_specs,
        out_specs=[
            pl.BlockSpec((tm, D_MODEL), lambda i: (i, 0)),
            pl.BlockSpec((tm, D_MODEL), lambda i: (xn_blk0 + i, 0)),
            pl.BlockSpec((tm, 128), lambda i: (i, 0)),
            pl.BlockSpec((tm, 128), lambda i: (i, 0)),
        ],
        out_shape=[
            jax.ShapeDtypeStruct((m, D_MODEL), F32),
            jax.ShapeDtypeStruct((n_all, D_MODEL), F32),
            jax.ShapeDtypeStruct((m, 128), I32),
            jax.ShapeDtypeStruct((m, 128), F32),
        ],
        input_output_aliases=io_alias,
        scratch_shapes=[pltpu.VMEM((3, D_MODEL, 128), BF16)],
        compiler_params=_cparams(("arbitrary",)),
        name="merge_router",
    )(*args)


def _rank_kernel(idx_ref, rank_ref, cnt_ref, carry_ref, *, tr):
    i = pl.program_id(0)

    @pl.when(i == 0)
    def _():
        carry_ref[...] = jnp.zeros_like(carry_ref)

    idx = idx_ref[...]
    lane = lax.broadcasted_iota(I32, (tr, 128), 1)
    hits = [idx[:, k:k + 1] == lane for k in range(TOP_K)]
    onehot = jnp.zeros((tr, 128), F32)
    for hk in hits:
        onehot = onehot + hk.astype(F32)
    earlier = (lax.broadcasted_iota(I32, (tr, tr), 1)
               < lax.broadcasted_iota(I32, (tr, tr), 0)).astype(BF16)
    before = jnp.dot(earlier, onehot.astype(BF16), preferred_element_type=F32) + carry_ref[...]
    out = jnp.zeros((tr, 128), I32)
    for k, hk in enumerate(hits):
        rk = jnp.sum(jnp.where(hk, before, 0.0), axis=-1, keepdims=True)
        out = jnp.where(lane == k, rk.astype(I32), out)
    rank_ref[...] = out
    carry_ref[...] = carry_ref[...] + jnp.sum(onehot, axis=0, keepdims=True)
    cnt_ref[...] = carry_ref[...]


def _expert_rank(top_idx_padded, tr):
    n = top_idx_padded.shape[0]
    return pl.pallas_call(
        functools.partial(_rank_kernel, tr=tr),
        grid=(n // tr,),
        in_specs=[pl.BlockSpec((tr, 128), lambda i: (i, 0))],
        out_specs=[
            pl.BlockSpec((tr, 128), lambda i: (i, 0)),
            pl.BlockSpec((1, 128), lambda i: (0, 0)),
        ],
        out_shape=[
            jax.ShapeDtypeStruct((n, 128), I32),
            jax.ShapeDtypeStruct((1, 128), F32),
        ],
        scratch_shapes=[pltpu.VMEM((1, 128), F32)],
        compiler_params=_cparams(("arbitrary",)),
        name="expert_rank",
    )(top_idx_padded)


def _moe_out_block_copy(acc_scr, ys_hbm, sub, row, sem):
    return pltpu.make_async_copy(acc_scr.at[pl.ds(sub * MOE_SUB, MOE_SUB)],
                                 ys_hbm.at[pl.ds(row, MOE_SUB)], sem)


def _moe_kernel(wi_e_ref, wi_row_ref, wi_nsub_ref, tok_ref, xn_hbm,
                wg_ref, wu_ref, bg_ref, bu_ref, wd_ref, bd_ref, ys_hbm,
                x_scr, acc_scr, wg_bf, wu_bf, wd_bf, pend_ref, sem_in, sem_out):
    del wi_e_ref
    w = pl.program_id(0)
    j = pl.program_id(1)
    nj = pl.num_programs(1)
    nsub = wi_nsub_ref[w]
    row0 = wi_row_ref[w]
    issue_unroll = 8

    @pl.when((w == 0) & (j == 0))
    def _():
        pend_ref[0] = 0

    @pl.when((j == 0) & (nsub > 0))
    def _():
        def issue(blk, carry):
            for i in range(issue_unroll):
                r = blk * issue_unroll + i
                tok = tok_ref[0, 0, r]
                pltpu.make_async_copy(xn_hbm.at[pl.ds(tok, 1)], x_scr.at[pl.ds(r, 1)], sem_in).start()
            return carry
        lax.fori_loop(0, nsub * (MOE_SUB // issue_unroll), issue, 0)

    def drain_output():
        for s in range(MOE_NSUB):
            @pl.when(s < pend_ref[0])
            def _():
                _moe_out_block_copy(acc_scr, ys_hbm, s, 0, sem_out).wait()
        pend_ref[0] = 0

    @pl.when(j == 0)
    def _():
        drain_output()

    @pl.when((j == 0) & (nsub > 0))
    def _():
        for s in range(MOE_NSUB):
            @pl.when(s < nsub)
            def _():
                acc_scr[s * MOE_SUB:(s + 1) * MOE_SUB, :] = jnp.broadcast_to(
                    bd_ref[0], (MOE_SUB, D_MODEL))

    @pl.when(nsub > 0)
    def _():
        wg_bf[...] = wg_ref[0].astype(BF16)
        wu_bf[...] = wu_ref[0].astype(BF16)
        wd_bf[...] = wd_ref[0].astype(BF16)

    @pl.when((j == 0) & (nsub > 0))
    def _():
        for s in range(MOE_NSUB):
            @pl.when(s < nsub)
            def _():
                pltpu.make_async_copy(xn_hbm.at[pl.ds(0, MOE_SUB)],
                                      x_scr.at[pl.ds(s * MOE_SUB, MOE_SUB)], sem_in).wait()

    def expert_rows(rows):
        xs = x_scr[rows, :].astype(BF16)
        gate = jnp.dot(xs, wg_bf[...], preferred_element_type=F32) + bg_ref[0]
        up = jnp.dot(xs, wu_bf[...], preferred_element_type=F32) + bu_ref[0]
        gate = jnp.minimum(gate, SWIGLU_LIMIT)
        up = jnp.clip(up, -SWIGLU_LIMIT, SWIGLU_LIMIT)
        act = (up + 1.0) * (gate * jax.nn.sigmoid(SWIGLU_ALPHA * gate))
        acc_scr[rows, :] += jnp.dot(act.astype(BF16), wd_bf[...], preferred_element_type=F32)

    @pl.when(nsub > 0)
    def _():
        for c in range(MOE_NSUB // 4):
            @pl.when(4 * c + 4 <= nsub)
            def _():
                expert_rows(pl.ds(4 * c * MOE_SUB, 4 * MOE_SUB))

        @pl.when(nsub % 4 >= 2)
        def _():
            expert_rows(pl.ds(pl.multiple_of((nsub // 4) * 4 * MOE_SUB, MOE_SUB), 2 * MOE_SUB))

        @pl.when(nsub % 2 == 1)
        def _():
            expert_rows(pl.ds(pl.multiple_of((nsub - 1) * MOE_SUB, MOE_SUB), MOE_SUB))

    @pl.when((j == nj - 1) & (nsub > 0))
    def _():
        for s in range(MOE_NSUB):
            @pl.when(s < nsub)
            def _():
                _moe_out_block_copy(acc_scr, ys_hbm, s,
                                    pl.multiple_of(row0 + s * MOE_SUB, MOE_SUB), sem_out).start()
        pend_ref[0] = nsub

    @pl.when((w == pl.num_programs(0) - 1) & (j == nj - 1))
    def _():
        drain_output()


def _moe_experts(wi_e, wi_row, wi_nsub, wi_tok, xn_all, w_gu, b_gu, w_dn, b_dn, n_slots):
    n_wi = wi_e.shape[0]
    nj = D_EXPERT // MOE_TN

    def col(w, j, n):
        return jnp.where(n[w] > 0, j, nj - 1)

    grid_spec = pltpu.PrefetchScalarGridSpec(
        num_scalar_prefetch=3,
        grid=(n_wi, nj),
        in_specs=[
            pl.BlockSpec((1, 1, MOE_TM), lambda w, j, e, r, n: (w, 0, 0), memory_space=pltpu.SMEM),
            pl.BlockSpec(memory_space=pl.ANY),
            pl.BlockSpec((1, D_MODEL, MOE_TN), lambda w, j, e, r, n: (e[w], 0, col(w, j, n))),
            pl.BlockSpec((1, D_MODEL, MOE_TN), lambda w, j, e, r, n: (e[w], 0, nj + col(w, j, n))),
            pl.BlockSpec((1, 1, MOE_TN), lambda w, j, e, r, n: (e[w], 0, col(w, j, n))),
            pl.BlockSpec((1, 1, MOE_TN), lambda w, j, e, r, n: (e[w], 0, nj + col(w, j, n))),
            pl.BlockSpec((1, MOE_TN, D_MODEL), lambda w, j, e, r, n: (e[w], col(w, j, n), 0)),
            pl.BlockSpec((1, 1, D_MODEL), lambda w, j, e, r, n: (e[w], 0, 0)),
        ],
        out_specs=pl.BlockSpec(memory_space=pl.ANY),
        scratch_shapes=[
            pltpu.VMEM((MOE_TM, D_MODEL), F32),
            pltpu.VMEM((MOE_TM, D_MODEL), F32),
            pltpu.VMEM((D_MODEL, MOE_TN), BF16),
            pltpu.VMEM((D_MODEL, MOE_TN), BF16),
            pltpu.VMEM((MOE_TN, D_MODEL), BF16),
            pltpu.SMEM((1,), I32),
            pltpu.SemaphoreType.DMA(()),
            pltpu.SemaphoreType.DMA(()),
        ],
    )
    return pl.pallas_call(
        _moe_kernel,
        grid_spec=grid_spec,
        out_shape=jax.ShapeDtypeStruct((n_slots, D_MODEL), F32),
        compiler_params=_cparams(("arbitrary", "arbitrary"), vmem=58 * 1024 * 1024),
        name="moe_experts",
    )(wi_e, wi_row, wi_nsub, wi_tok, xn_all, w_gu, w_gu,
      b_gu.reshape(N_EXPERTS, 1, 2 * D_EXPERT), b_gu.reshape(N_EXPERTS, 1, 2 * D_EXPERT),
      w_dn, b_dn.reshape(N_EXPERTS, 1, D_MODEL))


def _combine_kernel(dest_ref, gt_ref, h_ref, ys_hbm, o_ref, buf, sem, *, tt):
    def issue(r, carry):
        for k in range(TOP_K):
            slot = dest_ref[0, 0, r * TOP_K + k]
            pltpu.make_async_copy(ys_hbm.at[pl.ds(slot, 1)], buf.at[k, pl.ds(r, 1)], sem).start()
        return carry
    lax.fori_loop(0, tt, issue, 0, unroll=4)
    for k in range(TOP_K):
        pltpu.make_async_copy(ys_hbm.at[pl.ds(0, tt)], buf.at[k], sem).wait()
    gt = gt_ref[...]
    y = gt[:, 0:1] * buf[0]
    for k in range(1, TOP_K):
        y = y + gt[:, k:k + 1] * buf[k]
    o_ref[...] = h_ref[...] + y


def _combine(dest, gates, h1, ys, tt):
    m = h1.shape[0]
    nt = m // tt
    return pl.pallas_call(
        functools.partial(_combine_kernel, tt=tt),
        grid=(nt,),
        in_specs=[
            pl.BlockSpec((1, 1, tt * TOP_K), lambda i: (i, 0, 0), memory_space=pltpu.SMEM),
            pl.BlockSpec((tt, 128), lambda i: (i, 0)),
            pl.BlockSpec((tt, D_MODEL), lambda i: (i, 0)),
            pl.BlockSpec(memory_space=pl.ANY),
        ],
        out_specs=pl.BlockSpec((tt, D_MODEL), lambda i: (i, 0)),
        out_shape=jax.ShapeDtypeStruct((m, D_MODEL), F32),
        scratch_shapes=[pltpu.VMEM((TOP_K, tt, D_MODEL), F32), pltpu.SemaphoreType.DMA(())],
        compiler_params=_cparams(("arbitrary",)),
        name="moe_combine",
    )(dest.reshape(nt, 1, tt * TOP_K), gates, h1, ys)


def _routing_tables(top_idx, rank, counts, n_wi, n_slots):
    n_tok = top_idx.shape[0]
    padded = (counts + MOE_SUB - 1) // MOE_SUB * MOE_SUB
    pad_start = jnp.cumsum(padded) - padded
    dest = pad_start[top_idx] + rank
    tok_ids = jnp.broadcast_to(jnp.arange(n_tok, dtype=I32)[:, None], (n_tok, TOP_K))
    slot_tok = jnp.zeros((n_slots,), I32).at[dest.reshape(-1)].set(
        tok_ids.reshape(-1), unique_indices=True)
    nsubs = padded // MOE_SUB
    items = (nsubs + MOE_NSUB - 1) // MOE_NSUB
    item_end = jnp.cumsum(items)
    total = item_end[-1]
    w = jnp.arange(n_wi, dtype=I32)
    w_eff = jnp.minimum(w, total - 1)
    wi_e = jnp.sum((item_end[None, :] <= w_eff[:, None]).astype(I32), axis=1)
    wi_e = jnp.minimum(wi_e, N_EXPERTS - 1)
    local = w_eff - (item_end - items)[wi_e]
    wi_row = (pad_start[wi_e] + local * MOE_TM).astype(I32)
    wi_nsub = jnp.where(w < total, jnp.clip(nsubs[wi_e] - local * MOE_NSUB, 0, MOE_NSUB), 0).astype(I32)
    tok_pos = jnp.minimum(wi_row[:, None] + jnp.arange(MOE_TM, dtype=I32)[None, :], n_slots - 1)
    wi_tok = slot_tok[tok_pos].reshape(n_wi, 1, MOE_TM)
    return dest.astype(I32), wi_e, wi_row, wi_nsub, wi_tok


def kernel(x_prompt, x_sample, cache_k, cache_v, cache_logf, state_pool, page_table,
           norm1_w, w_in, b_forget, q_norm_w, k_norm_w, w_pool, pool_scale,
           w_branch_a, w_branch_b, w_out, norm2_w, router_w, router_b,
           w_gu, b_gu, w_dn, b_dn):
    depth = w_in.shape[0]
    assert depth == 1
    bp, lp, _ = x_prompt.shape
    bs, ls, _ = x_sample.shape
    assert ls == 8 and N_HEADS == 8
    n_p, n_s = bp * lp, bs * ls
    n_all = n_p + n_s
    n_past = page_table.shape[1] * PAGE_SIZE

    wt = jnp.transpose(w_in[0])
    f_off = 3 * ATTN_WIDTH
    w_main = jnp.concatenate([wt[:f_off], wt[f_off + N_HEADS:]], axis=0).astype(BF16)
    w_f = jnp.pad(wt[f_off:f_off + N_HEADS], ((0, FORGET_LANES - N_HEADS), (0, 0))).astype(BF16)
    b_f = jnp.pad(b_forget[0], (0, FORGET_LANES - N_HEADS)).reshape(1, FORGET_LANES)
    n1w = norm1_w[0].reshape(1, D_MODEL)
    n2w = norm2_w[0].reshape(1, D_MODEL)
    qnw = jnp.tile(q_norm_w[0], N_HEADS).reshape(1, ATTN_WIDTH)
    knw = jnp.tile(k_norm_w[0], N_HEADS).reshape(1, ATTN_WIDTH)
    wp = w_pool[0].astype(BF16)
    ps = pool_scale[0].reshape(1, POOL_WIDTH)
    wa = w_branch_a[0].astype(BF16)
    wb = w_branch_b[0].astype(BF16)
    wo = w_out[0].astype(BF16)
    rw = jnp.pad(router_w[0], ((0, 0), (0, 128 - N_EXPERTS)))
    rb = jnp.pad(router_b[0], (0, 128 - N_EXPERTS), constant_values=NEG_BIG).reshape(1, 128)

    xp = x_prompt.reshape(n_p, D_MODEL)
    qp, kp, vp, up, gp, lfp = _inproj(xp, n1w, w_main, w_f, b_f, qnw, knw, tm=1024, tn=512,
                                      q_scale=ATTN_SCALE * LOG2E)
    cp = _cumsum_rows(lfp, bl=512, seg_len=lp)
    cp3 = cp.reshape(bp, lp, FORGET_LANES)
    attn_p = _prompt_attention(qp.reshape(bp, lp, ATTN_WIDTH), kp.reshape(bp, lp, ATTN_WIDTH),
                               vp.reshape(bp, lp, ATTN_WIDTH), cp3, _key_bias_pieces(cp3))
    up3 = up.reshape(bp, lp, POOL_WIDTH)
    halo_p = pl.BlockSpec((1, POOL_HALO, POOL_WIDTH),
                          lambda bi, i: (bi, jnp.maximum(i * (512 // POOL_HALO) - 1, 0), 0))
    pool_p = _pool_mix(up3, halo_p, up3, wp, ps, tm=512, base_pos=0, zero_first_halo=True)
    h1_p, xn_all, ti_p, gt_p = _merge(
        attn_p.reshape(n_p, ATTN_WIDTH), pool_p.reshape(n_p, POOL_WIDTH), gp, xp,
        wa, wb, wo, n2w, rw, rb, tm=256, xn_all=None, xn_row0=0, n_all=n_all)

    xs = x_sample.reshape(n_s, D_MODEL)
    qs, ks, vs, us, gs, lfs = _inproj(xs, n1w, w_main, w_f, b_f, qnw, knw, tm=n_s, tn=512, q_scale=1.0)
    cs = _cumsum_rows(lfs, bl=n_s, seg_len=ls)
    q_s = qs.astype(F32).reshape(bs, ls, N_HEADS, HEAD_DIM)
    qbd = jnp.einsum('bthd,hg->bhtgd', q_s, jnp.eye(N_HEADS, dtype=F32))
    qbd = qbd.reshape(bs, N_HEADS * ls, ATTN_WIDTH).astype(BF16)
    cn = jnp.transpose(cs.reshape(bs, ls, FORGET_LANES)[:, :, :N_HEADS], (0, 2, 1))
    cncol = cn.reshape(bs, N_HEADS * ls, 1)
    cnkeys = jnp.broadcast_to(cn[:, :, None, :], (bs, N_HEADS, ls, ls)).reshape(bs, N_HEADS * ls, ls)
    cnkeys = jnp.pad(cnkeys, ((0, 0), (0, 0), (0, PAGE_SIZE - ls)))
    knew = jnp.pad(ks.reshape(bs, ls, ATTN_WIDTH), ((0, 0), (0, PAGE_SIZE - ls), (0, 0)))
    vnew = jnp.pad(vs.reshape(bs, ls, ATTN_WIDTH), ((0, 0), (0, PAGE_SIZE - ls), (0, 0)))
    n_phys = cache_k.shape[1]
    ck = cache_k[0].reshape(n_phys, PAGE_SIZE * N_HEADS, HEAD_DIM)
    cv = cache_v[0].reshape(n_phys, PAGE_SIZE * N_HEADS, HEAD_DIM)
    clf_t = jnp.transpose(cache_logf[0], (0, 2, 1))
    attn_s = _sample_attention(page_table, qbd, cncol, cnkeys, knew, vnew, ck, cv, clf_t)
    halo_src = jnp.pad(state_pool[0], ((0, 0), (POOL_HALO - POOL_HIST, 0), (0, 0)))
    halo_s = pl.BlockSpec((1, POOL_HALO, POOL_WIDTH), lambda bi, i: (bi, 0, 0))
    us3 = us.reshape(bs, ls, POOL_WIDTH)
    pool_s = _pool_mix(halo_src, halo_s, us3, wp, ps, tm=ls, base_pos=n_past, zero_first_halo=False)
    h1_s, xn_all, ti_s, gt_s = _merge(
        attn_s.reshape(n_s, ATTN_WIDTH), pool_s.reshape(n_s, POOL_WIDTH), gs, xs,
        wa, wb, wo, n2w, rw, rb, tm=n_s, xn_all=xn_all, xn_row0=n_p, n_all=n_all)

    n_rank = pl.cdiv(n_all, RANK_TILE) * RANK_TILE
    ti_all = jnp.concatenate([ti_p, ti_s, jnp.full((n_rank - n_all, 128), -1, I32)], axis=0)
    rank, cnt = _expert_rank(ti_all, tr=RANK_TILE)
    top_idx = ti_all[:n_all, :TOP_K]
    counts = cnt[0, :N_EXPERTS].astype(I32)
    n_blocks = pl.cdiv(n_all * TOP_K, MOE_SUB) + N_EXPERTS
    n_slots = n_blocks * MOE_SUB
    n_wi = n_blocks // MOE_NSUB + N_EXPERTS
    assert (n_blocks + (MOE_NSUB - 1) * N_EXPERTS) // MOE_NSUB <= n_wi
    dest, wi_e, wi_row, wi_nsub, wi_tok = _routing_tables(
        top_idx, rank[:n_all, :TOP_K], counts, n_wi, n_slots)
    ys = _moe_experts(wi_e, wi_row, wi_nsub, wi_tok, xn_all, w_gu[0], b_gu[0], w_dn[0], b_dn[0], n_slots)
    y_p = _combine(dest[:n_p], gt_p, h1_p, ys, tt=128)
    y_s = _combine(dest[n_p:], gt_s, h1_s, ys, tt=n_s)

    heads = (N_HEADS, HEAD_DIM)
    k_p = kp.reshape(1, bp, lp, *heads)
    v_p = vp.reshape(1, bp, lp, *heads)
    f_p = lfp.reshape(bp, lp, FORGET_LANES)[None, :, :, :N_HEADS]
    u_p = up3[:, lp - POOL_HIST:][None]
    k_s = ks.reshape(1, bs, ls, *heads)
    v_s = vs.reshape(1, bs, ls, *heads)
    f_s = lfs.reshape(bs, ls, FORGET_LANES)[None, :, :, :N_HEADS]
    pool_state = jnp.concatenate([state_pool[0], us3], axis=1)[:, -POOL_HIST:][None]
    return (y_p.reshape(bp, lp, D_MODEL), y_s.reshape(bs, ls, D_MODEL),
            k_p, v_p, f_p, u_p, k_s, v_s, f_s, pool_state)
```

```python
import functools

import jax
import jax.numpy as jnp
from jax import lax
from jax.experimental import pallas as pl
from jax.experimental.pallas import tpu as pltpu

F32 = jnp.float32
BF16 = jnp.bfloat16
I32 = jnp.int32

D_MODEL = 2048
N_HEADS = 8
HEAD_DIM = 128
ATTN_WIDTH = N_HEADS * HEAD_DIM
ATTN_SCALE = HEAD_DIM ** -0.5
FORGET_LANES = 128
POOL_WINDOWS = (2, 4, 8, 16)
POOL_WIDTH = D_MODEL // 2
GW = POOL_WIDTH // len(POOL_WINDOWS)
POOL_HIST = max(POOL_WINDOWS) - 1
POOL_HALO = 16
N_EXPERTS = 32
TOP_K = 4
D_EXPERT = D_MODEL
SWIGLU_LIMIT = 7.0
SWIGLU_ALPHA = 1.702
NORM_EPS = 1e-6
PAGE_SIZE = 128
NEG_BIG = -1e30

ZQ, ZK, ZV, ZU = 0, 1, 2, 3
Z_WIDTH = 3 * ATTN_WIDTH + POOL_WIDTH + 2 * D_MODEL

MOE_SUB = 128
MOE_NSUB = 9
MOE_TM = MOE_SUB * MOE_NSUB
MOE_TN = 512
RANK_TILE = 512
VMEM_LIMIT = 56 * 1024 * 1024

HIGHEST = lax.Precision.HIGHEST


def _cparams(sem, vmem=VMEM_LIMIT, flags=None):
    return pltpu.CompilerParams(dimension_semantics=sem, vmem_limit_bytes=vmem, flags=flags)


def _log_sigmoid(x):
    return jnp.minimum(x, 0.0) - jnp.log1p(jnp.exp(-jnp.abs(x)))


_NT = (((1,), (1,)), ((), ()))


def _inproj_kernel(x_ref, n1w_ref, w_ref, wf_ref, bf_ref, qnw_ref, knw_ref,
                   q_ref, k_ref, v_ref, u_ref, g_ref, lf_ref, xn_ref, *, q_scale, sub):
    j = pl.program_id(1)

    @pl.when(j == 0)
    def _():
        x = x_ref[...]
        ms = jnp.mean(x * x, axis=-1, keepdims=True)
        xn = x * lax.rsqrt(ms + NORM_EPS) * n1w_ref[...]
        xn_ref[...] = xn.astype(BF16)
        zf = lax.dot_general(xn_ref[...], wf_ref[...], _NT, preferred_element_type=F32) + bf_ref[...]
        lf_ref[...] = _log_sigmoid(zf)

    def proj():
        return lax.dot_general(xn_ref[...], w_ref[...], _NT, preferred_element_type=F32)

    def in_segment(seg):
        return (j >= seg * sub) & (j < (seg + 1) * sub)

    def head_norm(nw_ref, dst_ref, scale):
        z = proj()
        for h in range(z.shape[1] // HEAD_DIM):
            sl = slice(h * HEAD_DIM, (h + 1) * HEAD_DIM)
            zh = z[:, sl]
            ms = jnp.mean(zh * zh, axis=-1, keepdims=True)
            y = zh * lax.rsqrt(ms + NORM_EPS) * nw_ref[:, sl]
            if scale != 1.0:
                y = y * scale
            dst_ref[:, sl] = y.astype(dst_ref.dtype)

    @pl.when(in_segment(ZQ))
    def _():
        head_norm(qnw_ref, q_ref, q_scale)

    @pl.when(in_segment(ZK))
    def _():
        head_norm(knw_ref, k_ref, 1.0)

    @pl.when(in_segment(ZV))
    def _():
        v_ref[...] = proj()

    @pl.when(in_segment(ZU))
    def _():
        u_ref[...] = proj()

    @pl.when(j >= (ZU + 1) * sub)
    def _():
        g_ref[...] = jax.nn.sigmoid(proj())


def _inproj(x, n1w, w_main_t, w_f_t, b_f, qnw, knw, tm, tn, q_scale):
    m = x.shape[0]
    sub = ATTN_WIDTH // tn
    row = lambda i, j: (i, 0)
    const = lambda i, j: (0, 0)

    def seg_part(seg):
        return lambda i, j: (i, jnp.clip(j - seg * sub, 0, sub - 1))

    return pl.pallas_call(
        functools.partial(_inproj_kernel, q_scale=q_scale, sub=sub),
        grid=(m // tm, Z_WIDTH // tn),
        in_specs=[
            pl.BlockSpec((tm, D_MODEL), row),
            pl.BlockSpec((1, D_MODEL), const),
            pl.BlockSpec((tn, D_MODEL), lambda i, j: (j, 0)),
            pl.BlockSpec((FORGET_LANES, D_MODEL), const),
            pl.BlockSpec((1, FORGET_LANES), const),
            pl.BlockSpec((1, tn), const),
            pl.BlockSpec((1, tn), const),
        ],
        out_specs=[
            pl.BlockSpec((tm, tn), seg_part(ZQ)),
            pl.BlockSpec((tm, tn), seg_part(ZK)),
            pl.BlockSpec((tm, tn), seg_part(ZV)),
            pl.BlockSpec((tm, tn), seg_part(ZU)),
            pl.BlockSpec((tm, tn), lambda i, j: (i, jnp.maximum(j - (ZU + 1) * sub, 0))),
            pl.BlockSpec((tm, FORGET_LANES), row),
        ],
        out_shape=[
            jax.ShapeDtypeStruct((m, ATTN_WIDTH), BF16),
            jax.ShapeDtypeStruct((m, ATTN_WIDTH), F32),
            jax.ShapeDtypeStruct((m, ATTN_WIDTH), F32),
            jax.ShapeDtypeStruct((m, POOL_WIDTH), F32),
            jax.ShapeDtypeStruct((m, 2 * D_MODEL), F32),
            jax.ShapeDtypeStruct((m, FORGET_LANES), F32),
        ],
        scratch_shapes=[pltpu.VMEM((tm, D_MODEL), BF16)],
        compiler_params=_cparams(("arbitrary", "arbitrary")),
        name="inproj",
    )(x, n1w, w_main_t, w_f_t, b_f, qnw, knw)


def _cumsum_kernel(lf_ref, c_ref, carry_ref, *, bl, seg_len):
    i = pl.program_id(0)
    r = lax.broadcasted_iota(I32, (bl, bl), 0)
    c = lax.broadcasted_iota(I32, (bl, bl), 1)
    mask = c <= r
    if seg_len < bl:
        mask = mask & ((r // seg_len) == (c // seg_len))
    tri = mask.astype(F32)
    cs = jnp.dot(tri, lf_ref[...], preferred_element_type=F32, precision=HIGHEST)
    if seg_len > bl:
        @pl.when((i * bl) % seg_len == 0)
        def _():
            carry_ref[...] = jnp.zeros_like(carry_ref)
        cs = cs + carry_ref[...]
        carry_ref[...] = cs[bl - 1:bl, :]
    c_ref[...] = cs


def _cumsum_rows(lf, bl, seg_len):
    m = lf.shape[0]
    return pl.pallas_call(
        functools.partial(_cumsum_kernel, bl=bl, seg_len=seg_len),
        grid=(m // bl,),
        in_specs=[pl.BlockSpec((bl, FORGET_LANES), lambda i: (i, 0))],
        out_specs=pl.BlockSpec((bl, FORGET_LANES), lambda i: (i, 0)),
        out_shape=jax.ShapeDtypeStruct((m, FORGET_LANES), F32),
        scratch_shapes=[pltpu.VMEM((1, FORGET_LANES), F32)],
        compiler_params=_cparams(("arbitrary",)),
        name="logf_cumsum",
    )(lf)


LOG2E = 1.4426950408889634


def _pattn_kernel(q_ref, k_ref, v_ref, c_ref, kbias_ref, o_ref, *scratch, tq, ch):
    m_refs = scratch[0:N_HEADS]
    accl_refs = scratch[N_HEADS:2 * N_HEADS]
    kb_ref, vb_ref = scratch[2 * N_HEADS:]
    qi = pl.program_id(1)
    ki = pl.program_id(2)
    tk = tq

    @pl.when(ki == 0)
    def _():
        for h in range(N_HEADS):
            m_refs[h][...] = jnp.full_like(m_refs[h], -jnp.inf)
            accl_refs[h][...] = jnp.zeros_like(accl_refs[h])

    @pl.when(ki <= qi)
    def _():
        kb_ref[...] = k_ref[0].astype(BF16)
        vb_ref[...] = v_ref[0].astype(BF16)

    def block(diag):
        c2 = c_ref[0] * LOG2E
        q_ones = (lax.broadcasted_iota(I32, (ch, HEAD_DIM), 1) < 3).astype(BF16)
        v_ones = jnp.ones((tk, HEAD_DIM), BF16)
        below = (lax.broadcasted_iota(I32, (ch, ch), 1) <= lax.broadcasted_iota(I32, (ch, ch), 0))

        def where(h, r):
            rows = slice(r * ch, (r + 1) * ch)
            cols = slice(0, (r + 1) * ch if diag else tk)
            return rows, cols, slice(h * HEAD_DIM, (h + 1) * HEAD_DIM)

        def stage_scores(h, r):
            rows, cols, sl = where(h, r)
            q_cat = jnp.concatenate([q_ref[0, rows, sl], q_ones], axis=1)
            k_cat = jnp.concatenate([kb_ref[cols, sl], kbias_ref[0, h, cols, :]], axis=1)
            u = lax.dot_general(q_cat, k_cat, _NT, preferred_element_type=F32)
            if diag:
                last = jnp.where(below, u[:, r * ch:], -jnp.inf)
                u = last if r == 0 else jnp.concatenate([u[:, :r * ch], last], axis=1)
            return u

        def stage_max(h, r, u):
            rows, _, _ = where(h, r)
            c2t = c2[rows, h:h + 1]
            m_prev = m_refs[h][rows, :]
            m_new = jnp.maximum(m_prev, jnp.max(u, axis=-1, keepdims=True) + c2t)
            m_refs[h][rows, :] = m_new
            return jnp.exp2(m_prev - m_new), c2t - m_new[:, 0:1]

        def stage_accumulate(h, r, p, alpha):
            rows, cols, sl = where(h, r)
            v_cat = jnp.concatenate([vb_ref[cols, sl], v_ones[cols, :]], axis=1)
            pv = jnp.dot(p, v_cat, preferred_element_type=F32)
            accl_refs[h][rows, :] = jnp.concatenate([alpha, alpha], axis=1) * accl_refs[h][rows, :] + pv

        units = [(h, r) for r in range(tq // ch) for h in range(N_HEADS)]
        u_of, ab_of, p_of = {}, {}, {}
        for t in range(len(units) + 3):
            if t < len(units):
                u_of[t] = stage_scores(*units[t])
            if 0 <= t - 1 < len(units):
                ab_of[t - 1] = stage_max(*units[t - 1], u_of[t - 1])
            if 0 <= t - 2 < len(units):
                p_of[t - 2] = jnp.exp2(u_of.pop(t - 2) + ab_of[t - 2][1]).astype(BF16)
            if 0 <= t - 3 < len(units):
                stage_accumulate(*units[t - 3], p_of.pop(t - 3), ab_of.pop(t - 3)[0])

    @pl.when(ki < qi)
    def _():
        block(False)

    @pl.when(ki == qi)
    def _():
        block(True)
        for h in range(N_HEADS):
            sl = slice(h * HEAD_DIM, (h + 1) * HEAD_DIM)
            accl = accl_refs[h][...]
            o_ref[0, :, sl] = (accl[:, :HEAD_DIM] / accl[:, HEAD_DIM:]).astype(o_ref.dtype)


def _prompt_attention(q3, k3, v3, c3, kbias, tq=512, ch=128):
    b, l, _ = q3.shape
    tk = tq
    nq = l // tq
    kv_idx = lambda bi, qi, ki: (bi, jnp.minimum(ki, qi), 0)
    return pl.pallas_call(
        functools.partial(_pattn_kernel, tq=tq, ch=ch),
        grid=(b, nq, nq),
        in_specs=[
            pl.BlockSpec((1, tq, ATTN_WIDTH), lambda bi, qi, ki: (bi, qi, 0)),
            pl.BlockSpec((1, tk, ATTN_WIDTH), kv_idx),
            pl.BlockSpec((1, tk, ATTN_WIDTH), kv_idx),
            pl.BlockSpec((1, tq, FORGET_LANES), lambda bi, qi, ki: (bi, qi, 0)),
            pl.BlockSpec((1, N_HEADS, tk, HEAD_DIM),
                         lambda bi, qi, ki: (bi, 0, jnp.minimum(ki, qi), 0)),
        ],
        out_specs=pl.BlockSpec((1, tq, ATTN_WIDTH), lambda bi, qi, ki: (bi, qi, 0)),
        out_shape=jax.ShapeDtypeStruct((b, l, ATTN_WIDTH), BF16),
        scratch_shapes=(
            [pltpu.VMEM((tq, HEAD_DIM), F32) for _ in range(N_HEADS)]
            + [pltpu.VMEM((tq, 2 * HEAD_DIM), F32) for _ in range(N_HEADS)]
            + [pltpu.VMEM((tk, ATTN_WIDTH), BF16), pltpu.VMEM((tk, ATTN_WIDTH), BF16)]),
        compiler_params=_cparams(("arbitrary", "arbitrary", "arbitrary")),
        name="prompt_attention",
    )(q3, k3, v3, c3, kbias)


def _bf16_pieces(x, n):
    out = []
    for _ in range(n):
        piece = x.astype(BF16).astype(F32)
        out.append(piece)
        x = x - piece
    return out


def _kbias_kernel(c_ref, o_ref):
    c = c_ref[0] * (-LOG2E)
    lane = lax.broadcasted_iota(I32, c.shape, 1)
    for h in range(N_HEADS):
        hi, mid, lo = _bf16_pieces(c[:, h:h + 1], 3)
        tile = jnp.where(lane == 0, hi, jnp.where(lane == 1, mid, jnp.where(lane == 2, lo, 0.0)))
        o_ref[0, h] = tile.astype(BF16)


def _key_bias_pieces(c3, tl=1024):
    b, l, _ = c3.shape
    return pl.pallas_call(
        _kbias_kernel,
        grid=(b, l // tl),
        in_specs=[pl.BlockSpec((1, tl, FORGET_LANES), lambda bi, i: (bi, i, 0))],
        out_specs=pl.BlockSpec((1, N_HEADS, tl, HEAD_DIM), lambda bi, i: (bi, 0, i, 0)),
        out_shape=jax.ShapeDtypeStruct((b, N_HEADS, l, HEAD_DIM), BF16),
        compiler_params=_cparams(("arbitrary", "arbitrary")),
        name="key_bias",
    )(c3)


def _heads_to_lanes(ref):
    parts = [ref[0, pl.ds(h, PAGE_SIZE, stride=N_HEADS), :] for h in range(N_HEADS)]
    return jnp.concatenate(parts, axis=1).astype(BF16)


def _rows_per_head(x):
    return jnp.concatenate(
        [jnp.broadcast_to(x[h:h + 1, :], (8, x.shape[1])) for h in range(N_HEADS)], axis=0)


def _sattn_kernel(pt_ref, qbd_ref, cncol_ref, cnkeys_ref, knew_ref, vnew_ref, *rest, g_pages):
    del pt_ref
    k_refs = rest[0:g_pages]
    v_refs = rest[g_pages:2 * g_pages]
    lf_refs = rest[2 * g_pages:3 * g_pages]
    o_ref = rest[3 * g_pages]
    m_ref, l_ref, acc_ref, carry_ref = rest[3 * g_pages + 1:]
    jj = pl.program_id(1)
    nj = pl.num_programs(1)
    rows = N_HEADS * 8

    @pl.when(jj == 0)
    def _():
        m_ref[...] = jnp.full_like(m_ref, -jnp.inf)
        l_ref[...] = jnp.zeros_like(l_ref)
        acc_ref[...] = jnp.zeros_like(acc_ref)
        carry_ref[...] = jnp.zeros_like(carry_ref)

    qbd = qbd_ref[0]
    cncol = cncol_ref[0]
    later = (lax.broadcasted_iota(I32, (PAGE_SIZE, PAGE_SIZE), 0)
             > lax.broadcasted_iota(I32, (PAGE_SIZE, PAGE_SIZE), 1)).astype(F32)

    def scores(kcat):
        return lax.dot_general(qbd, kcat, (((1,), (1,)), ((), ())),
                               preferred_element_type=F32) * ATTN_SCALE

    def update(s, vcat, m, l, acc):
        m_new = jnp.maximum(m, jnp.max(s, axis=-1, keepdims=True))
        alpha = jnp.exp(m - m_new)
        p = jnp.exp(s - jnp.concatenate([m_new] * (s.shape[1] // PAGE_SIZE), axis=1))
        l = alpha * l + jnp.sum(p, axis=-1, keepdims=True)
        pv = jnp.dot(p.astype(BF16), vcat, preferred_element_type=F32)
        acc = jnp.concatenate([alpha] * N_HEADS, axis=1) * acc + pv
        return m_new, l, acc

    m, l, acc, carry = m_ref[...], l_ref[...], acc_ref[...], carry_ref[...]
    lf_all = jnp.concatenate([_rows_per_head(lf_refs[g][0]) for g in range(g_pages)], axis=0)
    suffix_all = jnp.dot(lf_all, later, preferred_element_type=F32, precision=HIGHEST)
    bias = []
    for g in range(g_pages):
        bias.append(suffix_all[g * rows:(g + 1) * rows, :] + carry)
        carry = carry + jnp.sum(lf_all[g * rows:(g + 1) * rows, :], axis=-1, keepdims=True)
    kcat = jnp.concatenate([_heads_to_lanes(k_refs[g]) for g in range(g_pages)], axis=0)
    vcat = jnp.concatenate([_heads_to_lanes(v_refs[g]) for g in range(g_pages)], axis=0)
    s = scores(kcat) + cncol + jnp.concatenate(bias, axis=1)
    m, l, acc = update(s, vcat, m, l, acc)

    @pl.when(jj < nj - 1)
    def _():
        m_ref[...] = m
        l_ref[...] = l
        acc_ref[...] = acc
        carry_ref[...] = carry

    @pl.when(jj == nj - 1)
    def _():
        s = scores(knew_ref[0].astype(BF16)) + cncol - cnkeys_ref[0]
        t_row = lax.broadcasted_iota(I32, (rows, PAGE_SIZE), 0) % 8
        s_col = lax.broadcasted_iota(I32, (rows, PAGE_SIZE), 1)
        s = jnp.where(s_col <= t_row, s, -jnp.inf)
        _, l2, acc2 = update(s, vnew_ref[0].astype(BF16), m, l, acc)
        o = acc2 / jnp.concatenate([l2] * N_HEADS, axis=1)
        for h in range(N_HEADS):
            sl = slice(h * HEAD_DIM, (h + 1) * HEAD_DIM)
            o_ref[0, :, sl] = o[h * 8:(h + 1) * 8, sl].astype(o_ref.dtype)


def _sample_attention(page_table, qbd, cncol, cnkeys, knew, vnew, ck, cv, clf_t, g_pages=16):
    bd, n_pages = page_table.shape
    rows = N_HEADS * 8
    pt_flat = page_table.reshape(-1)

    def page_idx(g):
        return lambda b, jj, pt: (pt[b * n_pages + n_pages - 1 - (jj * g_pages + g)], 0, 0)

    per_b = lambda b, jj, pt: (b, 0, 0)
    in_specs = [
        pl.BlockSpec((1, rows, ATTN_WIDTH), per_b),
        pl.BlockSpec((1, rows, 1), per_b),
        pl.BlockSpec((1, rows, PAGE_SIZE), per_b),
        pl.BlockSpec((1, PAGE_SIZE, ATTN_WIDTH), per_b),
        pl.BlockSpec((1, PAGE_SIZE, ATTN_WIDTH), per_b),
    ]
    in_specs += [pl.BlockSpec((1, PAGE_SIZE * N_HEADS, HEAD_DIM), page_idx(g)) for g in range(g_pages)]
    in_specs += [pl.BlockSpec((1, PAGE_SIZE * N_HEADS, HEAD_DIM), page_idx(g)) for g in range(g_pages)]
    in_specs += [pl.BlockSpec((1, N_HEADS, PAGE_SIZE), page_idx(g)) for g in range(g_pages)]
    grid_spec = pltpu.PrefetchScalarGridSpec(
        num_scalar_prefetch=1,
        grid=(bd, n_pages // g_pages),
        in_specs=in_specs,
        out_specs=pl.BlockSpec((1, 8, ATTN_WIDTH), per_b),
        scratch_shapes=[
            pltpu.VMEM((rows, PAGE_SIZE), F32),
            pltpu.VMEM((rows, PAGE_SIZE), F32),
            pltpu.VMEM((rows, ATTN_WIDTH), F32),
            pltpu.VMEM((rows, PAGE_SIZE), F32),
        ],
    )
    return pl.pallas_call(
        functools.partial(_sattn_kernel, g_pages=g_pages),
        grid_spec=grid_spec,
        out_shape=jax.ShapeDtypeStruct((bd, 8, ATTN_WIDTH), BF16),
        compiler_params=_cparams(("arbitrary", "arbitrary")),
        name="sample_attention",
    )(pt_flat, qbd, cncol, cnkeys, knew, vnew,
      *([ck] * g_pages), *([cv] * g_pages), *([clf_t] * g_pages))


def _pool_kernel(halo_ref, u_ref, wp_ref, ps_ref, o_ref, *, tm, base_pos, zero_first_halo):
    i = pl.program_id(1)
    halo = halo_ref[0]
    if zero_first_halo:
        halo = jnp.where(i == 0, 0.0, halo)
    u = u_ref[0]
    ext = jnp.concatenate([halo, u], axis=0)
    pos = base_pos + i * tm + lax.broadcasted_iota(I32, (tm, 1), 0)
    for g, w in enumerate(POOL_WINDOWS):
        sl = slice(g * GW, (g + 1) * GW)
        s = ext[:, sl]
        k = 1
        while k < w:
            s = s + pltpu.roll(s, shift=k, axis=0)
            k *= 2
        wsum = s[POOL_HALO:, :]
        count = jnp.minimum(pos + 1, w).astype(F32)
        mixed = wsum / count - u[:, sl]
        y = jnp.dot(mixed.astype(BF16), wp_ref[g], preferred_element_type=F32)
        o_ref[0, :, sl] = (y * ps_ref[:, sl]).astype(o_ref.dtype)


def _pool_mix(halo_src, halo_spec, u_src, wp, ps, tm, base_pos, zero_first_halo):
    b, l, _ = u_src.shape
    return pl.pallas_call(
        functools.partial(_pool_kernel, tm=tm, base_pos=base_pos, zero_first_halo=zero_first_halo),
        grid=(b, l // tm),
        in_specs=[
            halo_spec,
            pl.BlockSpec((1, tm, POOL_WIDTH), lambda bi, i: (bi, i, 0)),
            pl.BlockSpec((len(POOL_WINDOWS), GW, GW), lambda bi, i: (0, 0, 0)),
            pl.BlockSpec((1, POOL_WIDTH), lambda bi, i: (0, 0)),
        ],
        out_specs=pl.BlockSpec((1, tm, POOL_WIDTH), lambda bi, i: (bi, i, 0)),
        out_shape=jax.ShapeDtypeStruct((b, l, POOL_WIDTH), BF16),
        compiler_params=_cparams(("arbitrary", "arbitrary")),
        name="pool_mix",
    )(halo_src, u_src, wp, ps)


def _merge_kernel(attn_ref, pool_ref, ga_ref, gb_ref, x_ref, wa_ref, wb_ref, wo_ref,
                  n2w_ref, rw_ref, rb_ref, *rest, tm, aliased):
    h_ref, xn_ref, ti_ref, gt_ref, rwp_ref = rest[-5:]

    @pl.when(pl.program_id(0) == 0)
    def _():
        for n, piece in enumerate(_bf16_pieces(rw_ref[...], 3)):
            rwp_ref[n] = piece.astype(BF16)

    a = jnp.dot(attn_ref[...], wa_ref[...], preferred_element_type=F32)
    p = jnp.dot(pool_ref[...], wb_ref[...], preferred_element_type=F32)
    merged = ga_ref[...] * a + gb_ref[...] * p
    o = jnp.dot(merged.astype(BF16), wo_ref[...], preferred_element_type=F32)
    h1 = x_ref[...] + o
    h_ref[...] = h1
    ms = jnp.mean(h1 * h1, axis=-1, keepdims=True)
    xn = h1 * lax.rsqrt(ms + NORM_EPS) * n2w_ref[...]
    xn_ref[...] = xn
    x_hi, x_lo = [v.astype(BF16) for v in _bf16_pieces(xn, 2)]
    logits = rb_ref[...]
    for xp, wn in ((x_hi, 0), (x_hi, 1), (x_lo, 0), (x_hi, 2), (x_lo, 1)):
        logits = logits + jnp.dot(xp, rwp_ref[wn], preferred_element_type=F32)
    lane = lax.broadcasted_iota(I32, (tm, 128), 1).astype(F32)
    vals, idxs = [], []
    work = logits
    for _ in range(TOP_K):
        mv = jnp.max(work, axis=-1, keepdims=True)
        ix = jnp.min(jnp.where(work == mv, lane, 128.0), axis=-1, keepdims=True)
        vals.append(mv)
        idxs.append(ix)
        work = jnp.where(lane == ix, NEG_BIG, work)
    es = [jnp.exp(v - vals[0]) for v in vals]
    den = es[0] + es[1] + es[2] + es[3]
    ti = jnp.zeros((tm, 128), F32)
    gt = jnp.zeros((tm, 128), F32)
    for k in range(TOP_K):
        ti = jnp.where(lane == float(k), idxs[k], ti)
        gt = jnp.where(lane == float(k), es[k] / den, gt)
    ti_ref[...] = ti.astype(I32)
    gt_ref[...] = gt


def _merge(attn, pool, z, x, wa, wb, wo, n2w, rw, rb, tm, xn_all, xn_row0, n_all):
    m = x.shape[0]
    aliased = xn_all is not None
    const = lambda i: (0, 0)
    single = dict(pipeline_mode=pl.Buffered(1))
    xn_blk0 = xn_row0 // tm
    in_specs = [
        pl.BlockSpec((tm, ATTN_WIDTH), lambda i: (i, 0)),
        pl.BlockSpec((tm, POOL_WIDTH), lambda i: (i, 0)),
        pl.BlockSpec((tm, D_MODEL), lambda i: (i, 0)),
        pl.BlockSpec((tm, D_MODEL), lambda i: (i, 1)),
        pl.BlockSpec((tm, D_MODEL), lambda i: (i, 0)),
        pl.BlockSpec((ATTN_WIDTH, D_MODEL), const, **single),
        pl.BlockSpec((POOL_WIDTH, D_MODEL), const, **single),
        pl.BlockSpec((D_MODEL, D_MODEL), const, **single),
        pl.BlockSpec((1, D_MODEL), const),
        pl.BlockSpec((D_MODEL, 128), const),
        pl.BlockSpec((1, 128), const),
    ]
    args = [attn, pool, z, z, x, wa, wb, wo, n2w, rw, rb]
    io_alias = {}
    if aliased:
        in_specs.append(pl.BlockSpec(memory_space=pl.ANY))
        args.append(xn_all)
        io_alias = {len(args) - 1: 1}
    return pl.pallas_call(
        functools.partial(_merge_kernel, tm=tm, aliased=aliased),
        grid=(m // tm,),
        in_specs=in_specs,
        out_specs=[
            pl.BlockSpec((tm, D_MODEL), lambda i: (i, 0)),
            pl.BlockSpec((tm, D_MODEL), lambda i: (xn_blk0 + i, 0)),
            pl.BlockSpec((tm, 128), lambda i: (i, 0)),
            pl.BlockSpec((tm, 128), lambda i: (i, 0)),
        ],
        out_shape=[
            jax.ShapeDtypeStruct((m, D_MODEL), F32),
            jax.ShapeDtypeStruct((n_all, D_MODEL), F32),
            jax.ShapeDtypeStruct((m, 128), I32),
            jax.ShapeDtypeStruct((m, 128), F32),
        ],
        input_output_aliases=io_alias,
        scratch_shapes=[pltpu.VMEM((3, D_MODEL, 128), BF16)],
        compiler_params=_cparams(("arbitrary",)),
        name="merge_router",
    )(*args)


def _rank_kernel(idx_ref, rank_ref, cnt_ref, carry_ref, *, tr):
    i = pl.program_id(0)

    @pl.when(i == 0)
    def _():
        carry_ref[...] = jnp.zeros_like(carry_ref)

    idx = idx_ref[...]
    lane = lax.broadcasted_iota(I32, (tr, 128), 1)
    hits = [idx[:, k:k + 1] == lane for k in range(TOP_K)]
    onehot = jnp.zeros((tr, 128), F32)
    for hk in hits:
        onehot = onehot + hk.astype(F32)
    earlier = (lax.broadcasted_iota(I32, (tr, tr), 1)
               < lax.broadcasted_iota(I32, (tr, tr), 0)).astype(BF16)
    before = jnp.dot(earlier, onehot.astype(BF16), preferred_element_type=F32) + carry_ref[...]
    out = jnp.zeros((tr, 128), I32)
    for k, hk in enumerate(hits):
        rk = jnp.sum(jnp.where(hk, before, 0.0), axis=-1, keepdims=True)
        out = jnp.where(lane == k, rk.astype(I32), out)
    rank_ref[...] = out
    carry_ref[...] = carry_ref[...] + jnp.sum(onehot, axis=0, keepdims=True)
    cnt_ref[...] = carry_ref[...]


def _expert_rank(top_idx_padded, tr):
    n = top_idx_padded.shape[0]
    return pl.pallas_call(
        functools.partial(_rank_kernel, tr=tr),
        grid=(n // tr,),
        in_specs=[pl.BlockSpec((tr, 128), lambda i: (i, 0))],
        out_specs=[
            pl.BlockSpec((tr, 128), lambda i: (i, 0)),
            pl.BlockSpec((1, 128), lambda i: (0, 0)),
        ],
        out_shape=[
            jax.ShapeDtypeStruct((n, 128), I32),
            jax.ShapeDtypeStruct((1, 128), F32),
        ],
        scratch_shapes=[pltpu.VMEM((1, 128), F32)],
        compiler_params=_cparams(("arbitrary",)),
        name="expert_rank",
    )(top_idx_padded)


def _moe_out_block_copy(acc_scr, ys_hbm, sub, row, sem):
    return pltpu.make_async_copy(acc_scr.at[pl.ds(sub * MOE_SUB, MOE_SUB)],
                                 ys_hbm.at[pl.ds(row, MOE_SUB)], sem)


def _moe_kernel(wi_e_ref, wi_row_ref, wi_nsub_ref, tok_ref, xn_hbm,
                wg_ref, wu_ref, bg_ref, bu_ref, wd_ref, bd_ref, ys_hbm,
                x_scr, acc_scr, wg_bf, wu_bf, wd_bf, pend_ref, sem_in, sem_out):
    del wi_e_ref
    w = pl.program_id(0)
    j = pl.program_id(1)
    nj = pl.num_programs(1)
    nsub = wi_nsub_ref[w]
    row0 = wi_row_ref[w]
    issue_unroll = 8

    @pl.when((w == 0) & (j == 0))
    def _():
        pend_ref[0] = 0

    @pl.when((j == 0) & (nsub > 0))
    def _():
        def issue(blk, carry):
            for i in range(issue_unroll):
                r = blk * issue_unroll + i
                tok = tok_ref[0, 0, r]
                pltpu.make_async_copy(xn_hbm.at[pl.ds(tok, 1)], x_scr.at[pl.ds(r, 1)], sem_in).start()
            return carry
        lax.fori_loop(0, nsub * (MOE_SUB // issue_unroll), issue, 0)

    def drain_output():
        for s in range(MOE_NSUB):
            @pl.when(s < pend_ref[0])
            def _():
                _moe_out_block_copy(acc_scr, ys_hbm, s, 0, sem_out).wait()
        pend_ref[0] = 0

    @pl.when(j == 0)
    def _():
        drain_output()

    @pl.when((j == 0) & (nsub > 0))
    def _():
        for s in range(MOE_NSUB):
            @pl.when(s < nsub)
            def _():
                acc_scr[s * MOE_SUB:(s + 1) * MOE_SUB, :] = jnp.broadcast_to(
                    bd_ref[0], (MOE_SUB, D_MODEL))

    @pl.when(nsub > 0)
    def _():
        wg_bf[...] = wg_ref[0].astype(BF16)
        wu_bf[...] = wu_ref[0].astype(BF16)
        wd_bf[...] = wd_ref[0].astype(BF16)

    @pl.when((j == 0) & (nsub > 0))
    def _():
        for s in range(MOE_NSUB):
            @pl.when(s < nsub)
            def _():
                pltpu.make_async_copy(xn_hbm.at[pl.ds(0, MOE_SUB)],
                                      x_scr.at[pl.ds(s * MOE_SUB, MOE_SUB)], sem_in).wait()

    def expert_rows(rows):
        xs = x_scr[rows, :].astype(BF16)
        gate = jnp.dot(xs, wg_bf[...], preferred_element_type=F32) + bg_ref[0]
        up = jnp.dot(xs, wu_bf[...], preferred_element_type=F32) + bu_ref[0]
        gate = jnp.minimum(gate, SWIGLU_LIMIT)
        up = jnp.clip(up, -SWIGLU_LIMIT, SWIGLU_LIMIT)
        act = (up + 1.0) * (gate * jax.nn.sigmoid(SWIGLU_ALPHA * gate))
        acc_scr[rows, :] += jnp.dot(act.astype(BF16), wd_bf[...], preferred_element_type=F32)

    @pl.when(nsub > 0)
    def _():
        for c in range(MOE_NSUB // 4):
            @pl.when(4 * c + 4 <= nsub)
            def _():
                expert_rows(pl.ds(4 * c * MOE_SUB, 4 * MOE_SUB))

        @pl.when(nsub % 4 >= 2)
        def _():
            expert_rows(pl.ds(pl.multiple_of((nsub // 4) * 4 * MOE_SUB, MOE_SUB), 2 * MOE_SUB))

        @pl.when(nsub % 2 == 1)
        def _():
            expert_rows(pl.ds(pl.multiple_of((nsub - 1) * MOE_SUB, MOE_SUB), MOE_SUB))

    @pl.when((j == nj - 1) & (nsub > 0))
    def _():
        for s in range(MOE_NSUB):
            @pl.when(s < nsub)
            def _():
                _moe_out_block_copy(acc_scr, ys_hbm, s,
                                    pl.multiple_of(row0 + s * MOE_SUB, MOE_SUB), sem_out).start()
        pend_ref[0] = nsub

    @pl.when((w == pl.num_programs(0) - 1) & (j == nj - 1))
    def _():
        drain_output()


def _moe_experts(wi_e, wi_row, wi_nsub, wi_tok, xn_all, w_gu, b_gu, w_dn, b_dn, n_slots):
    n_wi = wi_e.shape[0]
    nj = D_EXPERT // MOE_TN

    def col(w, j, n):
        return jnp.where(n[w] > 0, j, nj - 1)

    grid_spec = pltpu.PrefetchScalarGridSpec(
        num_scalar_prefetch=3,
        grid=(n_wi, nj),
        in_specs=[
            pl.BlockSpec((1, 1, MOE_TM), lambda w, j, e, r, n: (w, 0, 0), memory_space=pltpu.SMEM),
            pl.BlockSpec(memory_space=pl.ANY),
            pl.BlockSpec((1, D_MODEL, MOE_TN), lambda w, j, e, r, n: (e[w], 0, col(w, j, n))),
            pl.BlockSpec((1, D_MODEL, MOE_TN), lambda w, j, e, r, n: (e[w], 0, nj + col(w, j, n))),
            pl.BlockSpec((1, 1, MOE_TN), lambda w, j, e, r, n: (e[w], 0, col(w, j, n))),
            pl.BlockSpec((1, 1, MOE_TN), lambda w, j, e, r, n: (e[w], 0, nj + col(w, j, n))),
            pl.BlockSpec((1, MOE_TN, D_MODEL), lambda w, j, e, r, n: (e[w], col(w, j, n), 0)),
            pl.BlockSpec((1, 1, D_MODEL), lambda w, j, e, r, n: (e[w], 0, 0)),
        ],
        out_specs=pl.BlockSpec(memory_space=pl.ANY),
        scratch_shapes=[
            pltpu.VMEM((MOE_TM, D_MODEL), F32),
            pltpu.VMEM((MOE_TM, D_MODEL), F32),
            pltpu.VMEM((D_MODEL, MOE_TN), BF16),
            pltpu.VMEM((D_MODEL, MOE_TN), BF16),
            pltpu.VMEM((MOE_TN, D_MODEL), BF16),
            pltpu.SMEM((1,), I32),
            pltpu.SemaphoreType.DMA(()),
            pltpu.SemaphoreType.DMA(()),
        ],
    )
    return pl.pallas_call(
        _moe_kernel,
        grid_spec=grid_spec,
        out_shape=jax.ShapeDtypeStruct((n_slots, D_MODEL), F32),
        compiler_params=_cparams(("arbitrary", "arbitrary"), vmem=58 * 1024 * 1024),
        name="moe_experts",
    )(wi_e, wi_row, wi_nsub, wi_tok, xn_all, w_gu, w_gu,
      b_gu.reshape(N_EXPERTS, 1, 2 * D_EXPERT), b_gu.reshape(N_EXPERTS, 1, 2 * D_EXPERT),
      w_dn, b_dn.reshape(N_EXPERTS, 1, D_MODEL))


def _combine_kernel(dest_ref, gt_ref, h_ref, ys_hbm, o_ref, buf, sem, *, tt):
    def issue(r, carry):
        for k in range(TOP_K):
            slot = dest_ref[0, 0, r * TOP_K + k]
            pltpu.make_async_copy(ys_hbm.at[pl.ds(slot, 1)], buf.at[k, pl.ds(r, 1)], sem).start()
        return carry
    lax.fori_loop(0, tt, issue, 0, unroll=4)
    for k in range(TOP_K):
        pltpu.make_async_copy(ys_hbm.at[pl.ds(0, tt)], buf.at[k], sem).wait()
    gt = gt_ref[...]
    y = gt[:, 0:1] * buf[0]
    for k in range(1, TOP_K):
        y = y + gt[:, k:k + 1] * buf[k]
    o_ref[...] = h_ref[...] + y


def _combine(dest, gates, h1, ys, tt):
    m = h1.shape[0]
    nt = m // tt
    return pl.pallas_call(
        functools.partial(_combine_kernel, tt=tt),
        grid=(nt,),
        in_specs=[
            pl.BlockSpec((1, 1, tt * TOP_K), lambda i: (i, 0, 0), memory_space=pltpu.SMEM),
            pl.BlockSpec((tt, 128), lambda i: (i, 0)),
            pl.BlockSpec((tt, D_MODEL), lambda i: (i, 0)),
            pl.BlockSpec(memory_space=pl.ANY),
        ],
        out_specs=pl.BlockSpec((tt, D_MODEL), lambda i: (i, 0)),
        out_shape=jax.ShapeDtypeStruct((m, D_MODEL), F32),
        scratch_shapes=[pltpu.VMEM((TOP_K, tt, D_MODEL), F32), pltpu.SemaphoreType.DMA(())],
        compiler_params=_cparams(("arbitrary",)),
        name="moe_combine",
    )(dest.reshape(nt, 1, tt * TOP_K), gates, h1, ys)


def _routing_tables(top_idx, rank, counts, n_wi, n_slots):
    n_tok = top_idx.shape[0]
    padded = (counts + MOE_SUB - 1) // MOE_SUB * MOE_SUB
    pad_start = jnp.cumsum(padded) - padded
    dest = pad_start[top_idx] + rank
    tok_ids = jnp.broadcast_to(jnp.arange(n_tok, dtype=I32)[:, None], (n_tok, TOP_K))
    slot_tok = jnp.zeros((n_slots,), I32).at[dest.reshape(-1)].set(
        tok_ids.reshape(-1), unique_indices=True)
    nsubs = padded // MOE_SUB
    items = (nsubs + MOE_NSUB - 1) // MOE_NSUB
    item_end = jnp.cumsum(items)
    total = item_end[-1]
    w = jnp.arange(n_wi, dtype=I32)
    w_eff = jnp.minimum(w, total - 1)
    wi_e = jnp.sum((item_end[None, :] <= w_eff[:, None]).astype(I32), axis=1)
    wi_e = jnp.minimum(wi_e, N_EXPERTS - 1)
    local = w_eff - (item_end - items)[wi_e]
    wi_row = (pad_start[wi_e] + local * MOE_TM).astype(I32)
    wi_nsub = jnp.where(w < total, jnp.clip(nsubs[wi_e] - local * MOE_NSUB, 0, MOE_NSUB), 0).astype(I32)
    tok_pos = jnp.minimum(wi_row[:, None] + jnp.arange(MOE_TM, dtype=I32)[None, :], n_slots - 1)
    wi_tok = slot_tok[tok_pos].reshape(n_wi, 1, MOE_TM)
    return dest.astype(I32), wi_e, wi_row, wi_nsub, wi_tok


def kernel(x_prompt, x_sample, cache_k, cache_v, cache_logf, state_pool, page_table,
           norm1_w, w_in, b_forget, q_norm_w, k_norm_w, w_pool, pool_scale,
           w_branch_a, w_branch_b, w_out, norm2_w, router_w, router_b,
           w_gu, b_gu, w_dn, b_dn):
    depth = w_in.shape[0]
    assert depth == 1
    bp, lp, _ = x_prompt.shape
    bs, ls, _ = x_sample.shape
    assert ls == 8 and N_HEADS == 8
    n_p, n_s = bp * lp, bs * ls
    n_all = n_p + n_s
    n_past = page_table.shape[1] * PAGE_SIZE

    wt = jnp.transpose(w_in[0])
    f_off = 3 * ATTN_WIDTH
    w_main = jnp.concatenate([wt[:f_off], wt[f_off + N_HEADS:]], axis=0).astype(BF16)
    w_f = jnp.pad(wt[f_off:f_off + N_HEADS], ((0, FORGET_LANES - N_HEADS), (0, 0))).astype(BF16)
    b_f = jnp.pad(b_forget[0], (0, FORGET_LANES - N_HEADS)).reshape(1, FORGET_LANES)
    n1w = norm1_w[0].reshape(1, D_MODEL)
    n2w = norm2_w[0].reshape(1, D_MODEL)
    qnw = jnp.tile(q_norm_w[0], N_HEADS).reshape(1, ATTN_WIDTH)
    knw = jnp.tile(k_norm_w[0], N_HEADS).reshape(1, ATTN_WIDTH)
    wp = w_pool[0].astype(BF16)
    ps = pool_scale[0].reshape(1, POOL_WIDTH)
    wa = w_branch_a[0].astype(BF16)
    wb = w_branch_b[0].astype(BF16)
    wo = w_out[0].astype(BF16)
    rw = jnp.pad(router_w[0], ((0, 0), (0, 128 - N_EXPERTS)))
    rb = jnp.pad(router_b[0], (0, 128 - N_EXPERTS), constant_values=NEG_BIG).reshape(1, 128)

    xp = x_prompt.reshape(n_p, D_MODEL)
    qp, kp, vp, up, gp, lfp = _inproj(xp, n1w, w_main, w_f, b_f, qnw, knw, tm=512, tn=1024,
                                      q_scale=ATTN_SCALE * LOG2E)
    cp = _cumsum_rows(lfp, bl=512, seg_len=lp)
    cp3 = cp.reshape(bp, lp, FORGET_LANES)
    attn_p = _prompt_attention(qp.reshape(bp, lp, ATTN_WIDTH), kp.reshape(bp, lp, ATTN_WIDTH),
                               vp.reshape(bp, lp, ATTN_WIDTH), cp3, _key_bias_pieces(cp3))
    up3 = up.reshape(bp, lp, POOL_WIDTH)
    pool_tm = 1024
    halo_p = pl.BlockSpec((1, POOL_HALO, POOL_WIDTH),
                          lambda bi, i: (bi, jnp.maximum(i * (pool_tm // POOL_HALO) - 1, 0), 0))
    pool_p = _pool_mix(up3, halo_p, up3, wp, ps, tm=pool_tm, base_pos=0, zero_first_halo=True)
    h1_p, xn_all, ti_p, gt_p = _merge(
        attn_p.reshape(n_p, ATTN_WIDTH), pool_p.reshape(n_p, POOL_WIDTH), gp, xp,
        wa, wb, wo, n2w, rw, rb, tm=256, xn_all=None, xn_row0=0, n_all=n_all)

    xs = x_sample.reshape(n_s, D_MODEL)
    qs, ks, vs, us, gs, lfs = _inproj(xs, n1w, w_main, w_f, b_f, qnw, knw, tm=n_s, tn=512, q_scale=1.0)
    cs = _cumsum_rows(lfs, bl=n_s, seg_len=ls)
    q_s = qs.astype(F32).reshape(bs, ls, N_HEADS, HEAD_DIM)
    qbd = jnp.einsum('bthd,hg->bhtgd', q_s, jnp.eye(N_HEADS, dtype=F32))
    qbd = qbd.reshape(bs, N_HEADS * ls, ATTN_WIDTH).astype(BF16)
    cn = jnp.transpose(cs.reshape(bs, ls, FORGET_LANES)[:, :, :N_HEADS], (0, 2, 1))
    cncol = cn.reshape(bs, N_HEADS * ls, 1)
    cnkeys = jnp.broadcast_to(cn[:, :, None, :], (bs, N_HEADS, ls, ls)).reshape(bs, N_HEADS * ls, ls)
    cnkeys = jnp.pad(cnkeys, ((0, 0), (0, 0), (0, PAGE_SIZE - ls)))
    knew = jnp.pad(ks.reshape(bs, ls, ATTN_WIDTH), ((0, 0), (0, PAGE_SIZE - ls), (0, 0)))
    vnew = jnp.pad(vs.reshape(bs, ls, ATTN_WIDTH), ((0, 0), (0, PAGE_SIZE - ls), (0, 0)))
    n_phys = cache_k.shape[1]
    ck = cache_k[0].reshape(n_phys, PAGE_SIZE * N_HEADS, HEAD_DIM)
    cv = cache_v[0].reshape(n_phys, PAGE_SIZE * N_HEADS, HEAD_DIM)
    clf_t = jnp.transpose(cache_logf[0], (0, 2, 1))
    attn_s = _sample_attention(page_table, qbd, cncol, cnkeys, knew, vnew, ck, cv, clf_t)
    halo_src = jnp.pad(state_pool[0], ((0, 0), (POOL_HALO - POOL_HIST, 0), (0, 0)))
    halo_s = pl.BlockSpec((1, POOL_HALO, POOL_WIDTH), lambda bi, i: (bi, 0, 0))
    us3 = us.reshape(bs, ls, POOL_WIDTH)
    pool_s = _pool_mix(halo_src, halo_s, us3, wp, ps, tm=ls, base_pos=n_past, zero_first_halo=False)
    h1_s, xn_all, ti_s, gt_s = _merge(
        attn_s.reshape(n_s, ATTN_WIDTH), pool_s.reshape(n_s, POOL_WIDTH), gs, xs,
        wa, wb, wo, n2w, rw, rb, tm=n_s, xn_all=xn_all, xn_row0=n_p, n_all=n_all)

    n_rank = pl.cdiv(n_all, RANK_TILE) * RANK_TILE
    ti_all = jnp.concatenate([ti_p, ti_s, jnp.full((n_rank - n_all, 128), -1, I32)], axis=0)
    rank, cnt = _expert_rank(ti_all, tr=RANK_TILE)
    top_idx = ti_all[:n_all, :TOP_K]
    counts = cnt[0, :N_EXPERTS].astype(I32)
    n_blocks = pl.cdiv(n_all * TOP_K, MOE_SUB) + N_EXPERTS
    n_slots = n_blocks * MOE_SUB
    n_wi = n_blocks // MOE_NSUB + N_EXPERTS
    assert (n_blocks + (MOE_NSUB - 1) * N_EXPERTS) // MOE_NSUB <= n_wi
    dest, wi_e, wi_row, wi_nsub, wi_tok = _routing_tables(
        top_idx, rank[:n_all, :TOP_K], counts, n_wi, n_slots)
    ys = _moe_experts(wi_e, wi_row, wi_nsub, wi_tok, xn_all, w_gu[0], b_gu[0], w_dn[0], b_dn[0], n_slots)
    y_p = _combine(dest[:n_p], gt_p, h1_p, ys, tt=256)
    y_s = _combine(dest[n_p:], gt_s, h1_s, ys, tt=n_s)

    heads = (N_HEADS, HEAD_DIM)
    k_p = kp.reshape(1, bp, lp, *heads)
    v_p = vp.reshape(1, bp, lp, *heads)
    f_p = lfp.reshape(bp, lp, FORGET_LANES)[None, :, :, :N_HEADS]
    u_p = up3[:, lp - POOL_HIST:][None]
    k_s = ks.reshape(1, bs, ls, *heads)
    v_s = vs.reshape(1, bs, ls, *heads)
    f_s = lfs.reshape(bs, ls, FORGET_LANES)[None, :, :, :N_HEADS]
    pool_state = jnp.concatenate([state_pool[0], us3], axis=1)[:, -POOL_HIST:][None]
    return (y_p.reshape(bp, lp, D_MODEL), y_s.reshape(bs, ls, D_MODEL),
            k_p, v_p, f_p, u_p, k_s, v_s, f_s, pool_state)
```

```python
import functools

import jax
import jax.numpy as jnp
from jax import lax
from jax.experimental import pallas as pl
from jax.experimental.pallas import tpu as pltpu

F32 = jnp.float32
BF16 = jnp.bfloat16
I32 = jnp.int32

D_MODEL = 2048
N_HEADS = 8
HEAD_DIM = 128
ATTN_WIDTH = N_HEADS * HEAD_DIM
ATTN_SCALE = HEAD_DIM ** -0.5
FORGET_LANES = 128
POOL_WINDOWS = (2, 4, 8, 16)
POOL_WIDTH = D_MODEL // 2
GW = POOL_WIDTH // len(POOL_WINDOWS)
POOL_HIST = max(POOL_WINDOWS) - 1
POOL_HALO = 16
N_EXPERTS = 32
TOP_K = 4
D_EXPERT = D_MODEL
SWIGLU_LIMIT = 7.0
SWIGLU_ALPHA = 1.702
NORM_EPS = 1e-6
PAGE_SIZE = 128
NEG_BIG = -1e30

ZQ, ZK, ZV, ZU = 0, 1, 2, 3
Z_WIDTH = 3 * ATTN_WIDTH + POOL_WIDTH + 2 * D_MODEL

MOE_SUB = 128
MOE_NSUB = 9
MOE_TM = MOE_SUB * MOE_NSUB
MOE_TN = 512
RANK_TILE = 512
VMEM_LIMIT = 56 * 1024 * 1024

HIGHEST = lax.Precision.HIGHEST


def _cparams(sem, vmem=VMEM_LIMIT, flags=None):
    return pltpu.CompilerParams(dimension_semantics=sem, vmem_limit_bytes=vmem, flags=flags)


def _log_sigmoid(x):
    return jnp.minimum(x, 0.0) - jnp.log1p(jnp.exp(-jnp.abs(x)))


_NT = (((1,), (1,)), ((), ()))


def _inproj_kernel(x_ref, n1w_ref, w_ref, wf_ref, bf_ref, qnw_ref, knw_ref,
                   q_ref, k_ref, v_ref, u_ref, g_ref, lf_ref, xn_ref, *, q_scale, sub):
    j = pl.program_id(1)

    @pl.when(j == 0)
    def _():
        x = x_ref[...]
        ms = jnp.mean(x * x, axis=-1, keepdims=True)
        xn = x * lax.rsqrt(ms + NORM_EPS) * n1w_ref[...]
        xn_ref[...] = xn.astype(BF16)
        zf = lax.dot_general(xn_ref[...], wf_ref[...], _NT, preferred_element_type=F32) + bf_ref[...]
        lf_ref[...] = _log_sigmoid(zf)

    def proj():
        return lax.dot_general(xn_ref[...], w_ref[...], _NT, preferred_element_type=F32)

    def in_segment(seg):
        return (j >= seg * sub) & (j < (seg + 1) * sub)

    def head_norm(nw_ref, dst_ref, scale):
        z = proj()
        for h in range(z.shape[1] // HEAD_DIM):
            sl = slice(h * HEAD_DIM, (h + 1) * HEAD_DIM)
            zh = z[:, sl]
            ms = jnp.mean(zh * zh, axis=-1, keepdims=True)
            y = zh * lax.rsqrt(ms + NORM_EPS) * nw_ref[:, sl]
            if scale != 1.0:
                y = y * scale
            dst_ref[:, sl] = y.astype(dst_ref.dtype)

    @pl.when(in_segment(ZQ))
    def _():
        head_norm(qnw_ref, q_ref, q_scale)

    @pl.when(in_segment(ZK))
    def _():
        head_norm(knw_ref, k_ref, 1.0)

    @pl.when(in_segment(ZV))
    def _():
        v_ref[...] = proj()

    @pl.when(in_segment(ZU))
    def _():
        u_ref[...] = proj()

    @pl.when(j >= (ZU + 1) * sub)
    def _():
        g_ref[...] = jax.nn.sigmoid(proj())


def _inproj(x, n1w, w_main_t, w_f_t, b_f, qnw, knw, tm, tn, q_scale):
    m = x.shape[0]
    sub = ATTN_WIDTH // tn
    row = lambda i, j: (i, 0)
    const = lambda i, j: (0, 0)

    def seg_part(seg):
        return lambda i, j: (i, jnp.clip(j - seg * sub, 0, sub - 1))

    return pl.pallas_call(
        functools.partial(_inproj_kernel, q_scale=q_scale, sub=sub),
        grid=(m // tm, Z_WIDTH // tn),
        in_specs=[
            pl.BlockSpec((tm, D_MODEL), row),
            pl.BlockSpec((1, D_MODEL), const),
            pl.BlockSpec((tn, D_MODEL), lambda i, j: (j, 0)),
            pl.BlockSpec((FORGET_LANES, D_MODEL), const),
            pl.BlockSpec((1, FORGET_LANES), const),
            pl.BlockSpec((1, tn), const),
            pl.BlockSpec((1, tn), const),
        ],
        out_specs=[
            pl.BlockSpec((tm, tn), seg_part(ZQ)),
            pl.BlockSpec((tm, tn), seg_part(ZK)),
            pl.BlockSpec((tm, tn), seg_part(ZV)),
            pl.BlockSpec((tm, tn), seg_part(ZU)),
            pl.BlockSpec((tm, tn), lambda i, j: (i, jnp.maximum(j - (ZU + 1) * sub, 0))),
            pl.BlockSpec((tm, FORGET_LANES), row),
        ],
        out_shape=[
            jax.ShapeDtypeStruct((m, ATTN_WIDTH), BF16),
            jax.ShapeDtypeStruct((m, ATTN_WIDTH), F32),
            jax.ShapeDtypeStruct((m, ATTN_WIDTH), F32),
            jax.ShapeDtypeStruct((m, POOL_WIDTH), F32),
            jax.ShapeDtypeStruct((m, 2 * D_MODEL), F32),
            jax.ShapeDtypeStruct((m, FORGET_LANES), F32),
        ],
        scratch_shapes=[pltpu.VMEM((tm, D_MODEL), BF16)],
        compiler_params=_cparams(("arbitrary", "arbitrary")),
        name="inproj",
    )(x, n1w, w_main_t, w_f_t, b_f, qnw, knw)


def _cumsum_kernel(lf_ref, c_ref, carry_ref, *, bl, seg_len):
    i = pl.program_id(0)
    r = lax.broadcasted_iota(I32, (bl, bl), 0)
    c = lax.broadcasted_iota(I32, (bl, bl), 1)
    mask = c <= r
    if seg_len < bl:
        mask = mask & ((r // seg_len) == (c // seg_len))
    tri = mask.astype(F32)
    cs = jnp.dot(tri, lf_ref[...], preferred_element_type=F32, precision=HIGHEST)
    if seg_len > bl:
        @pl.when((i * bl) % seg_len == 0)
        def _():
            carry_ref[...] = jnp.zeros_like(carry_ref)
        cs = cs + carry_ref[...]
        carry_ref[...] = cs[bl - 1:bl, :]
    c_ref[...] = cs


def _cumsum_rows(lf, bl, seg_len):
    m = lf.shape[0]
    return pl.pallas_call(
        functools.partial(_cumsum_kernel, bl=bl, seg_len=seg_len),
        grid=(m // bl,),
        in_specs=[pl.BlockSpec((bl, FORGET_LANES), lambda i: (i, 0))],
        out_specs=pl.BlockSpec((bl, FORGET_LANES), lambda i: (i, 0)),
        out_shape=jax.ShapeDtypeStruct((m, FORGET_LANES), F32),
        scratch_shapes=[pltpu.VMEM((1, FORGET_LANES), F32)],
        compiler_params=_cparams(("arbitrary",)),
        name="logf_cumsum",
    )(lf)


LOG2E = 1.4426950408889634


def _pattn_kernel(q_ref, k_ref, v_ref, c_ref, kbias_ref, o_ref, *scratch, tq, ch):
    m_refs = scratch[0:N_HEADS]
    accl_refs = scratch[N_HEADS:2 * N_HEADS]
    kb_ref, vb_ref = scratch[2 * N_HEADS:]
    qi = pl.program_id(1)
    ki = pl.program_id(2)
    tk = tq

    @pl.when(ki == 0)
    def _():
        for h in range(N_HEADS):
            m_refs[h][...] = jnp.full_like(m_refs[h], -jnp.inf)
            accl_refs[h][...] = jnp.zeros_like(accl_refs[h])

    @pl.when(ki <= qi)
    def _():
        kb_ref[...] = k_ref[0].astype(BF16)
        vb_ref[...] = v_ref[0].astype(BF16)

    def block(diag):
        c2 = c_ref[0] * LOG2E
        q_ones = (lax.broadcasted_iota(I32, (ch, HEAD_DIM), 1) < 3).astype(BF16)
        v_ones = jnp.ones((tk, HEAD_DIM), BF16)
        below = (lax.broadcasted_iota(I32, (ch, ch), 1) <= lax.broadcasted_iota(I32, (ch, ch), 0))

        def where(h, r):
            rows = slice(r * ch, (r + 1) * ch)
            cols = slice(0, (r + 1) * ch if diag else tk)
            return rows, cols, slice(h * HEAD_DIM, (h + 1) * HEAD_DIM)

        def stage_scores(h, r):
            rows, cols, sl = where(h, r)
            q_cat = jnp.concatenate([q_ref[0, rows, sl], q_ones], axis=1)
            k_cat = jnp.concatenate([kb_ref[cols, sl], kbias_ref[0, h, cols, :]], axis=1)
            u = lax.dot_general(q_cat, k_cat, _NT, preferred_element_type=F32)
            if diag:
                last = jnp.where(below, u[:, r * ch:], -jnp.inf)
                u = last if r == 0 else jnp.concatenate([u[:, :r * ch], last], axis=1)
            return u

        def stage_max(h, r, u):
            rows, _, _ = where(h, r)
            c2t = c2[rows, h:h + 1]
            m_prev = m_refs[h][rows, :]
            m_new = jnp.maximum(m_prev, jnp.max(u, axis=-1, keepdims=True) + c2t)
            m_refs[h][rows, :] = m_new
            return jnp.exp2(m_prev - m_new), c2t - m_new[:, 0:1]

        def stage_accumulate(h, r, p, alpha):
            rows, cols, sl = where(h, r)
            v_cat = jnp.concatenate([vb_ref[cols, sl], v_ones[cols, :]], axis=1)
            pv = jnp.dot(p, v_cat, preferred_element_type=F32)
            accl_refs[h][rows, :] = jnp.concatenate([alpha, alpha], axis=1) * accl_refs[h][rows, :] + pv

        units = [(h, r) for r in range(tq // ch) for h in range(N_HEADS)]
        u_of, ab_of, p_of = {}, {}, {}
        for t in range(len(units) + 3):
            if t < len(units):
                u_of[t] = stage_scores(*units[t])
            if 0 <= t - 1 < len(units):
                ab_of[t - 1] = stage_max(*units[t - 1], u_of[t - 1])
            if 0 <= t - 2 < len(units):
                p_of[t - 2] = jnp.exp2(u_of.pop(t - 2) + ab_of[t - 2][1]).astype(BF16)
            if 0 <= t - 3 < len(units):
                stage_accumulate(*units[t - 3], p_of.pop(t - 3), ab_of.pop(t - 3)[0])

    @pl.when(ki < qi)
    def _():
        block(False)

    @pl.when(ki == qi)
    def _():
        block(True)
        for h in range(N_HEADS):
            sl = slice(h * HEAD_DIM, (h + 1) * HEAD_DIM)
            accl = accl_refs[h][...]
            o_ref[0, :, sl] = (accl[:, :HEAD_DIM] / accl[:, HEAD_DIM:]).astype(o_ref.dtype)


def _prompt_attention(q3, k3, v3, c3, kbias, tq=512, ch=128):
    b, l, _ = q3.shape
    tk = tq
    nq = l // tq
    kv_idx = lambda bi, qi, ki: (bi, jnp.minimum(ki, qi), 0)
    return pl.pallas_call(
        functools.partial(_pattn_kernel, tq=tq, ch=ch),
        grid=(b, nq, nq),
        in_specs=[
            pl.BlockSpec((1, tq, ATTN_WIDTH), lambda bi, qi, ki: (bi, qi, 0)),
            pl.BlockSpec((1, tk, ATTN_WIDTH), kv_idx),
            pl.BlockSpec((1, tk, ATTN_WIDTH), kv_idx),
            pl.BlockSpec((1, tq, FORGET_LANES), lambda bi, qi, ki: (bi, qi, 0)),
            pl.BlockSpec((1, N_HEADS, tk, HEAD_DIM),
                         lambda bi, qi, ki: (bi, 0, jnp.minimum(ki, qi), 0)),
        ],
        out_specs=pl.BlockSpec((1, tq, ATTN_WIDTH), lambda bi, qi, ki: (bi, qi, 0)),
        out_shape=jax.ShapeDtypeStruct((b, l, ATTN_WIDTH), BF16),
        scratch_shapes=(
            [pltpu.VMEM((tq, HEAD_DIM), F32) for _ in range(N_HEADS)]
            + [pltpu.VMEM((tq, 2 * HEAD_DIM), F32) for _ in range(N_HEADS)]
            + [pltpu.VMEM((tk, ATTN_WIDTH), BF16), pltpu.VMEM((tk, ATTN_WIDTH), BF16)]),
        compiler_params=_cparams(("arbitrary", "arbitrary", "arbitrary")),
        name="prompt_attention",
    )(q3, k3, v3, c3, kbias)


def _bf16_pieces(x, n):
    out = []
    for _ in range(n):
        piece = x.astype(BF16).astype(F32)
        out.append(piece)
        x = x - piece
    return out


def _kbias_kernel(c_ref, o_ref):
    c = c_ref[0] * (-LOG2E)
    lane = lax.broadcasted_iota(I32, c.shape, 1)
    for h in range(N_HEADS):
        hi, mid, lo = _bf16_pieces(c[:, h:h + 1], 3)
        tile = jnp.where(lane == 0, hi, jnp.where(lane == 1, mid, jnp.where(lane == 2, lo, 0.0)))
        o_ref[0, h] = tile.astype(BF16)


def _key_bias_pieces(c3, tl=512):
    b, l, _ = c3.shape
    return pl.pallas_call(
        _kbias_kernel,
        grid=(b, l // tl),
        in_specs=[pl.BlockSpec((1, tl, FORGET_LANES), lambda bi, i: (bi, i, 0))],
        out_specs=pl.BlockSpec((1, N_HEADS, tl, HEAD_DIM), lambda bi, i: (bi, 0, i, 0)),
        out_shape=jax.ShapeDtypeStruct((b, N_HEADS, l, HEAD_DIM), BF16),
        compiler_params=_cparams(("arbitrary", "arbitrary")),
        name="key_bias",
    )(c3)


def _heads_to_lanes(ref):
    parts = [ref[0, pl.ds(h, PAGE_SIZE, stride=N_HEADS), :] for h in range(N_HEADS)]
    return jnp.concatenate(parts, axis=1).astype(BF16)


def _rows_per_head(x):
    return jnp.concatenate(
        [jnp.broadcast_to(x[h:h + 1, :], (8, x.shape[1])) for h in range(N_HEADS)], axis=0)


def _sattn_kernel(pt_ref, qbd_ref, cncol_ref, cnkeys_ref, knew_ref, vnew_ref, *rest, g_pages):
    del pt_ref
    k_refs = rest[0:g_pages]
    v_refs = rest[g_pages:2 * g_pages]
    lf_refs = rest[2 * g_pages:3 * g_pages]
    o_ref = rest[3 * g_pages]
    m_ref, l_ref, acc_ref, carry_ref = rest[3 * g_pages + 1:]
    jj = pl.program_id(1)
    nj = pl.num_programs(1)
    rows = N_HEADS * 8

    @pl.when(jj == 0)
    def _():
        m_ref[...] = jnp.full_like(m_ref, -jnp.inf)
        l_ref[...] = jnp.zeros_like(l_ref)
        acc_ref[...] = jnp.zeros_like(acc_ref)
        carry_ref[...] = jnp.zeros_like(carry_ref)

    qbd = qbd_ref[0]
    cncol = cncol_ref[0]
    later = (lax.broadcasted_iota(I32, (PAGE_SIZE, PAGE_SIZE), 0)
             > lax.broadcasted_iota(I32, (PAGE_SIZE, PAGE_SIZE), 1)).astype(F32)

    def scores(kcat):
        return lax.dot_general(qbd, kcat, (((1,), (1,)), ((), ())),
                               preferred_element_type=F32) * ATTN_SCALE

    def update(s, vcat, m, l, acc):
        m_new = jnp.maximum(m, jnp.max(s, axis=-1, keepdims=True))
        alpha = jnp.exp(m - m_new)
        p = jnp.exp(s - jnp.concatenate([m_new] * (s.shape[1] // PAGE_SIZE), axis=1))
        l = alpha * l + jnp.sum(p, axis=-1, keepdims=True)
        pv = jnp.dot(p.astype(BF16), vcat, preferred_element_type=F32)
        acc = jnp.concatenate([alpha] * N_HEADS, axis=1) * acc + pv
        return m_new, l, acc

    m, l, acc, carry = m_ref[...], l_ref[...], acc_ref[...], carry_ref[...]
    lf_all = jnp.concatenate([_rows_per_head(lf_refs[g][0]) for g in range(g_pages)], axis=0)
    suffix_all = jnp.dot(lf_all, later, preferred_element_type=F32, precision=HIGHEST)
    bias = []
    for g in range(g_pages):
        bias.append(suffix_all[g * rows:(g + 1) * rows, :] + carry)
        carry = carry + jnp.sum(lf_all[g * rows:(g + 1) * rows, :], axis=-1, keepdims=True)
    kcat = jnp.concatenate([_heads_to_lanes(k_refs[g]) for g in range(g_pages)], axis=0)
    vcat = jnp.concatenate([_heads_to_lanes(v_refs[g]) for g in range(g_pages)], axis=0)
    s = scores(kcat) + cncol + jnp.concatenate(bias, axis=1)
    m, l, acc = update(s, vcat, m, l, acc)

    @pl.when(jj < nj - 1)
    def _():
        m_ref[...] = m
        l_ref[...] = l
        acc_ref[...] = acc
        carry_ref[...] = carry

    @pl.when(jj == nj - 1)
    def _():
        s = scores(knew_ref[0].astype(BF16)) + cncol - cnkeys_ref[0]
        t_row = lax.broadcasted_iota(I32, (rows, PAGE_SIZE), 0) % 8
        s_col = lax.broadcasted_iota(I32, (rows, PAGE_SIZE), 1)
        s = jnp.where(s_col <= t_row, s, -jnp.inf)
        _, l2, acc2 = update(s, vnew_ref[0].astype(BF16), m, l, acc)
        o = acc2 / jnp.concatenate([l2] * N_HEADS, axis=1)
        for h in range(N_HEADS):
            sl = slice(h * HEAD_DIM, (h + 1) * HEAD_DIM)
            o_ref[0, :, sl] = o[h * 8:(h + 1) * 8, sl].astype(o_ref.dtype)


def _sample_attention(page_table, qbd, cncol, cnkeys, knew, vnew, ck, cv, clf_t, g_pages=16):
    bd, n_pages = page_table.shape
    rows = N_HEADS * 8
    pt_flat = page_table.reshape(-1)

    def page_idx(g):
        return lambda b, jj, pt: (pt[b * n_pages + n_pages - 1 - (jj * g_pages + g)], 0, 0)

    per_b = lambda b, jj, pt: (b, 0, 0)
    in_specs = [
        pl.BlockSpec((1, rows, ATTN_WIDTH), per_b),
        pl.BlockSpec((1, rows, 1), per_b),
        pl.BlockSpec((1, rows, PAGE_SIZE), per_b),
        pl.BlockSpec((1, PAGE_SIZE, ATTN_WIDTH), per_b),
        pl.BlockSpec((1, PAGE_SIZE, ATTN_WIDTH), per_b),
    ]
    in_specs += [pl.BlockSpec((1, PAGE_SIZE * N_HEADS, HEAD_DIM), page_idx(g)) for g in range(g_pages)]
    in_specs += [pl.BlockSpec((1, PAGE_SIZE * N_HEADS, HEAD_DIM), page_idx(g)) for g in range(g_pages)]
    in_specs += [pl.BlockSpec((1, N_HEADS, PAGE_SIZE), page_idx(g)) for g in range(g_pages)]
    grid_spec = pltpu.PrefetchScalarGridSpec(
        num_scalar_prefetch=1,
        grid=(bd, n_pages // g_pages),
        in_specs=in_specs,
        out_specs=pl.BlockSpec((1, 8, ATTN_WIDTH), per_b),
        scratch_shapes=[
            pltpu.VMEM((rows, PAGE_SIZE), F32),
            pltpu.VMEM((rows, PAGE_SIZE), F32),
            pltpu.VMEM((rows, ATTN_WIDTH), F32),
            pltpu.VMEM((rows, PAGE_SIZE), F32),
        ],
    )
    return pl.pallas_call(
        functools.partial(_sattn_kernel, g_pages=g_pages),
        grid_spec=grid_spec,
        out_shape=jax.ShapeDtypeStruct((bd, 8, ATTN_WIDTH), BF16),
        compiler_params=_cparams(("arbitrary", "arbitrary")),
        name="sample_attention",
    )(pt_flat, qbd, cncol, cnkeys, knew, vnew,
      *([ck] * g_pages), *([cv] * g_pages), *([clf_t] * g_pages))


def _pool_kernel(halo_ref, u_ref, wp_ref, ps_ref, o_ref, *, tm, base_pos, zero_first_halo):
    i = pl.program_id(1)
    halo = halo_ref[0]
    if zero_first_halo:
        halo = jnp.where(i == 0, 0.0, halo)
    u = u_ref[0]
    ext = jnp.concatenate([halo, u], axis=0)
    pos = base_pos + i * tm + lax.broadcasted_iota(I32, (tm, 1), 0)
    for g, w in enumerate(POOL_WINDOWS):
        sl = slice(g * GW, (g + 1) * GW)
        s = ext[:, sl]
        k = 1
        while k < w:
            s = s + pltpu.roll(s, shift=k, axis=0)
            k *= 2
        wsum = s[POOL_HALO:, :]
        count = jnp.minimum(pos + 1, w).astype(F32)
        mixed = wsum / count - u[:, sl]
        y = jnp.dot(mixed.astype(BF16), wp_ref[g], preferred_element_type=F32)
        o_ref[0, :, sl] = (y * ps_ref[:, sl]).astype(o_ref.dtype)


def _pool_mix(halo_src, halo_spec, u_src, wp, ps, tm, base_pos, zero_first_halo):
    b, l, _ = u_src.shape
    return pl.pallas_call(
        functools.partial(_pool_kernel, tm=tm, base_pos=base_pos, zero_first_halo=zero_first_halo),
        grid=(b, l // tm),
        in_specs=[
            halo_spec,
            pl.BlockSpec((1, tm, POOL_WIDTH), lambda bi, i: (bi, i, 0)),
            pl.BlockSpec((len(POOL_WINDOWS), GW, GW), lambda bi, i: (0, 0, 0)),
            pl.BlockSpec((1, POOL_WIDTH), lambda bi, i: (0, 0)),
        ],
        out_specs=pl.BlockSpec((1, tm, POOL_WIDTH), lambda bi, i: (bi, i, 0)),
        out_shape=jax.ShapeDtypeStruct((b, l, POOL_WIDTH), BF16),
        compiler_params=_cparams(("arbitrary", "arbitrary")),
        name="pool_mix",
    )(halo_src, u_src, wp, ps)


def _merge_kernel(attn_ref, pool_ref, ga_ref, gb_ref, x_ref, wa_ref, wb_ref, wo_ref,
                  n2w_ref, rw_ref, rb_ref, *rest, tm, aliased):
    h_ref, xn_ref, ti_ref, gt_ref, rwp_ref = rest[-5:]

    @pl.when(pl.program_id(0) == 0)
    def _():
        for n, piece in enumerate(_bf16_pieces(rw_ref[...], 3)):
            rwp_ref[n] = piece.astype(BF16)

    a = jnp.dot(attn_ref[...], wa_ref[...], preferred_element_type=F32)
    p = jnp.dot(pool_ref[...], wb_ref[...], preferred_element_type=F32)
    merged = ga_ref[...] * a + gb_ref[...] * p
    o = jnp.dot(merged.astype(BF16), wo_ref[...], preferred_element_type=F32)
    h1 = x_ref[...] + o
    h_ref[...] = h1
    ms = jnp.mean(h1 * h1, axis=-1, keepdims=True)
    xn = h1 * lax.rsqrt(ms + NORM_EPS) * n2w_ref[...]
    xn_ref[...] = xn
    x_hi, x_lo = [v.astype(BF16) for v in _bf16_pieces(xn, 2)]
    logits = rb_ref[...]
    for xp, wn in ((x_hi, 0), (x_hi, 1), (x_lo, 0), (x_hi, 2), (x_lo, 1)):
        logits = logits + jnp.dot(xp, rwp_ref[wn], preferred_element_type=F32)
    lane = lax.broadcasted_iota(I32, (tm, 128), 1).astype(F32)
    vals, idxs = [], []
    work = logits
    for _ in range(TOP_K):
        mv = jnp.max(work, axis=-1, keepdims=True)
        ix = jnp.min(jnp.where(work == mv, lane, 128.0), axis=-1, keepdims=True)
        vals.append(mv)
        idxs.append(ix)
        work = jnp.where(lane == ix, NEG_BIG, work)
    es = [jnp.exp(v - vals[0]) for v in vals]
    den = es[0] + es[1] + es[2] + es[3]
    ti = jnp.zeros((tm, 128), F32)
    gt = jnp.zeros((tm, 128), F32)
    for k in range(TOP_K):
        ti = jnp.where(lane == float(k), idxs[k], ti)
        gt = jnp.where(lane == float(k), es[k] / den, gt)
    ti_ref[...] = ti.astype(I32)
    gt_ref[...] = gt


def _merge(attn, pool, z, x, wa, wb, wo, n2w, rw, rb, tm, xn_all, xn_row0, n_all):
    m = x.shape[0]
    aliased = xn_all is not None
    const = lambda i: (0, 0)
    single = dict(pipeline_mode=pl.Buffered(1))
    xn_blk0 = xn_row0 // tm
    in_specs = [
        pl.BlockSpec((tm, ATTN_WIDTH), lambda i: (i, 0)),
        pl.BlockSpec((tm, POOL_WIDTH), lambda i: (i, 0)),
        pl.BlockSpec((tm, D_MODEL), lambda i: (i, 0)),
        pl.BlockSpec((tm, D_MODEL), lambda i: (i, 1)),
        pl.BlockSpec((tm, D_MODEL), lambda i: (i, 0)),
        pl.BlockSpec((ATTN_WIDTH, D_MODEL), const, **single),
        pl.BlockSpec((POOL_WIDTH, D_MODEL), const, **single),
        pl.BlockSpec((D_MODEL, D_MODEL), const, **single),
        pl.BlockSpec((1, D_MODEL), const),
        pl.BlockSpec((D_MODEL, 128), const),
        pl.BlockSpec((1, 128), const),
    ]
    args = [attn, pool, z, z, x, wa, wb, wo, n2w, rw, rb]
    io_alias = {}
    if aliased:
        in_specs.append(pl.BlockSpec(memory_space=pl.ANY))
        args.append(xn_all)
        io_alias = {len(args) - 1: 1}
    return pl.pallas_call(
        functools.partial(_merge_kernel, tm=tm, aliased=aliased),
        grid=(m // tm,),
        in_specs=in_specs,
        out_specs=[
            pl.BlockSpec((tm, D_MODEL), lambda i: (i, 0)),
            pl.BlockSpec((tm, D_MODEL), lambda i: (xn_blk0 + i, 0)),
            pl.BlockSpec((tm, 128), lambda i: (i, 0)),
            pl.BlockSpec((tm, 128), lambda i: (i, 0)),
        ],
        out_shape=[
            jax.ShapeDtypeStruct((m, D_MODEL), F32),
            jax.ShapeDtypeStruct((n_all, D_MODEL), F32),
            jax.ShapeDtypeStruct((m, 128), I32),
            jax.ShapeDtypeStruct((m, 128), F32),
        ],
        input_output_aliases=io_alias,
        scratch_shapes=[pltpu.VMEM((3, D_MODEL, 128), BF16)],
        compiler_params=_cparams(("arbitrary",)),
        name="merge_router",
    )(*args)


def _rank_kernel(idx_ref, rank_ref, cnt_ref, carry_ref, *, tr):
    i = pl.program_id(0)

    @pl.when(i == 0)
    def _():
        carry_ref[...] = jnp.zeros_like(carry_ref)

    idx = idx_ref[...]
    lane = lax.broadcasted_iota(I32, (tr, 128), 1)
    hits = [idx[:, k:k + 1] == lane for k in range(TOP_K)]
    onehot = jnp.zeros((tr, 128), F32)
    for hk in hits:
        onehot = onehot + hk.astype(F32)
    earlier = (lax.broadcasted_iota(I32, (tr, tr), 1)
               < lax.broadcasted_iota(I32, (tr, tr), 0)).astype(BF16)
    before = jnp.dot(earlier, onehot.astype(BF16), preferred_element_type=F32) + carry_ref[...]
    out = jnp.zeros((tr, 128), I32)
    for k, hk in enumerate(hits):
        rk = jnp.sum(jnp.where(hk, before, 0.0), axis=-1, keepdims=True)
        out = jnp.where(lane == k, rk.astype(I32), out)
    rank_ref[...] = out
    carry_ref[...] = carry_ref[...] + jnp.sum(onehot, axis=0, keepdims=True)
    cnt_ref[...] = carry_ref[...]


def _expert_rank(top_idx_padded, tr):
    n = top_idx_padded.shape[0]
    return pl.pallas_call(
        functools.partial(_rank_kernel, tr=tr),
        grid=(n // tr,),
        in_specs=[pl.BlockSpec((tr, 128), lambda i: (i, 0))],
        out_specs=[
            pl.BlockSpec((tr, 128), lambda i: (i, 0)),
            pl.BlockSpec((1, 128), lambda i: (0, 0)),
        ],
        out_shape=[
            jax.ShapeDtypeStruct((n, 128), I32),
            jax.ShapeDtypeStruct((1, 128), F32),
        ],
        scratch_shapes=[pltpu.VMEM((1, 128), F32)],
        compiler_params=_cparams(("arbitrary",)),
        name="expert_rank",
    )(top_idx_padded)


def _moe_out_block_copy(acc_scr, ys_hbm, sub, row, sem):
    return pltpu.make_async_copy(acc_scr.at[pl.ds(sub * MOE_SUB, MOE_SUB)],
                                 ys_hbm.at[pl.ds(row, MOE_SUB)], sem)


def _moe_kernel(wi_e_ref, wi_row_ref, wi_nsub_ref, tok_ref, xn_hbm,
                wg_ref, wu_ref, bg_ref, bu_ref, wd_ref, bd_ref, ys_hbm,
                x_scr, acc_scr, wg_bf, wu_bf, wd_bf, pend_ref, sem_in, sem_out):
    del wi_e_ref
    w = pl.program_id(0)
    j = pl.program_id(1)
    nj = pl.num_programs(1)
    nsub = wi_nsub_ref[w]
    row0 = wi_row_ref[w]
    issue_unroll = 8

    @pl.when((w == 0) & (j == 0))
    def _():
        pend_ref[0] = 0

    @pl.when((j == 0) & (nsub > 0))
    def _():
        def issue(blk, carry):
            for i in range(issue_unroll):
                r = blk * issue_unroll + i
                tok = tok_ref[0, 0, r]
                pltpu.make_async_copy(xn_hbm.at[pl.ds(tok, 1)], x_scr.at[pl.ds(r, 1)], sem_in).start()
            return carry
        lax.fori_loop(0, nsub * (MOE_SUB // issue_unroll), issue, 0)

    def drain_output():
        for s in range(MOE_NSUB):
            @pl.when(s < pend_ref[0])
            def _():
                _moe_out_block_copy(acc_scr, ys_hbm, s, 0, sem_out).wait()
        pend_ref[0] = 0

    @pl.when(j == 0)
    def _():
        drain_output()

    @pl.when((j == 0) & (nsub > 0))
    def _():
        for s in range(MOE_NSUB):
            @pl.when(s < nsub)
            def _():
                acc_scr[s * MOE_SUB:(s + 1) * MOE_SUB, :] = jnp.broadcast_to(
                    bd_ref[0], (MOE_SUB, D_MODEL))

    @pl.when(nsub > 0)
    def _():
        wg_bf[...] = wg_ref[0].astype(BF16)
        wu_bf[...] = wu_ref[0].astype(BF16)
        wd_bf[...] = wd_ref[0].astype(BF16)

    @pl.when((j == 0) & (nsub > 0))
    def _():
        for s in range(MOE_NSUB):
            @pl.when(s < nsub)
            def _():
                pltpu.make_async_copy(xn_hbm.at[pl.ds(0, MOE_SUB)],
                                      x_scr.at[pl.ds(s * MOE_SUB, MOE_SUB)], sem_in).wait()

    def expert_rows(rows):
        xs = x_scr[rows, :].astype(BF16)
        gate = jnp.dot(xs, wg_bf[...], preferred_element_type=F32) + bg_ref[0]
        up = jnp.dot(xs, wu_bf[...], preferred_element_type=F32) + bu_ref[0]
        gate = jnp.minimum(gate, SWIGLU_LIMIT)
        up = jnp.clip(up, -SWIGLU_LIMIT, SWIGLU_LIMIT)
        act = (up + 1.0) * (gate * jax.nn.sigmoid(SWIGLU_ALPHA * gate))
        acc_scr[rows, :] += jnp.dot(act.astype(BF16), wd_bf[...], preferred_element_type=F32)

    @pl.when(nsub > 0)
    def _():
        for c in range(MOE_NSUB // 4):
            @pl.when(4 * c + 4 <= nsub)
            def _():
                expert_rows(pl.ds(4 * c * MOE_SUB, 4 * MOE_SUB))

        @pl.when(nsub % 4 >= 2)
        def _():
            expert_rows(pl.ds(pl.multiple_of((nsub // 4) * 4 * MOE_SUB, MOE_SUB), 2 * MOE_SUB))

        @pl.when(nsub % 2 == 1)
        def _():
            expert_rows(pl.ds(pl.multiple_of((nsub - 1) * MOE_SUB, MOE_SUB), MOE_SUB))

    @pl.when((j == nj - 1) & (nsub > 0))
    def _():
        for s in range(MOE_NSUB):
            @pl.when(s < nsub)
            def _():
                _moe_out_block_copy(acc_scr, ys_hbm, s,
                                    pl.multiple_of(row0 + s * MOE_SUB, MOE_SUB), sem_out).start()
        pend_ref[0] = nsub

    @pl.when((w == pl.num_programs(0) - 1) & (j == nj - 1))
    def _():
        drain_output()


def _moe_experts(wi_e, wi_row, wi_nsub, wi_tok, xn_all, w_gu, b_gu, w_dn, b_dn, n_slots):
    n_wi = wi_e.shape[0]
    nj = D_EXPERT // MOE_TN

    def col(w, j, n):
        return jnp.where(n[w] > 0, j, nj - 1)

    grid_spec = pltpu.PrefetchScalarGridSpec(
        num_scalar_prefetch=3,
        grid=(n_wi, nj),
        in_specs=[
            pl.BlockSpec((1, 1, MOE_TM), lambda w, j, e, r, n: (w, 0, 0), memory_space=pltpu.SMEM),
            pl.BlockSpec(memory_space=pl.ANY),
            pl.BlockSpec((1, D_MODEL, MOE_TN), lambda w, j, e, r, n: (e[w], 0, col(w, j, n))),
            pl.BlockSpec((1, D_MODEL, MOE_TN), lambda w, j, e, r, n: (e[w], 0, nj + col(w, j, n))),
            pl.BlockSpec((1, 1, MOE_TN), lambda w, j, e, r, n: (e[w], 0, col(w, j, n))),
            pl.BlockSpec((1, 1, MOE_TN), lambda w, j, e, r, n: (e[w], 0, nj + col(w, j, n))),
            pl.BlockSpec((1, MOE_TN, D_MODEL), lambda w, j, e, r, n: (e[w], col(w, j, n), 0)),
            pl.BlockSpec((1, 1, D_MODEL), lambda w, j, e, r, n: (e[w], 0, 0)),
        ],
        out_specs=pl.BlockSpec(memory_space=pl.ANY),
        scratch_shapes=[
            pltpu.VMEM((MOE_TM, D_MODEL), F32),
            pltpu.VMEM((MOE_TM, D_MODEL), F32),
            pltpu.VMEM((D_MODEL, MOE_TN), BF16),
            pltpu.VMEM((D_MODEL, MOE_TN), BF16),
            pltpu.VMEM((MOE_TN, D_MODEL), BF16),
            pltpu.SMEM((1,), I32),
            pltpu.SemaphoreType.DMA(()),
            pltpu.SemaphoreType.DMA(()),
        ],
    )
    return pl.pallas_call(
        _moe_kernel,
        grid_spec=grid_spec,
        out_shape=jax.ShapeDtypeStruct((n_slots, D_MODEL), F32),
        compiler_params=_cparams(("arbitrary", "arbitrary"), vmem=58 * 1024 * 1024),
        name="moe_experts",
    )(wi_e, wi_row, wi_nsub, wi_tok, xn_all, w_gu, w_gu,
      b_gu.reshape(N_EXPERTS, 1, 2 * D_EXPERT), b_gu.reshape(N_EXPERTS, 1, 2 * D_EXPERT),
      w_dn, b_dn.reshape(N_EXPERTS, 1, D_MODEL))


def _combine_kernel(dest_ref, gt_ref, h_ref, ys_hbm, o_ref, buf, sem, *, tt):
    def issue(r, carry):
        for k in range(TOP_K):
            slot = dest_ref[0, 0, r * TOP_K + k]
            pltpu.make_async_copy(ys_hbm.at[pl.ds(slot, 1)], buf.at[k, pl.ds(r, 1)], sem).start()
        return carry
    lax.fori_loop(0, tt, issue, 0, unroll=4)
    for k in range(TOP_K):
        pltpu.make_async_copy(ys_hbm.at[pl.ds(0, tt)], buf.at[k], sem).wait()
    gt = gt_ref[...]
    y = gt[:, 0:1] * buf[0]
    for k in range(1, TOP_K):
        y = y + gt[:, k:k + 1] * buf[k]
    o_ref[...] = h_ref[...] + y


def _combine(dest, gates, h1, ys, tt):
    m = h1.shape[0]
    nt = m // tt
    return pl.pallas_call(
        functools.partial(_combine_kernel, tt=tt),
        grid=(nt,),
        in_specs=[
            pl.BlockSpec((1, 1, tt * TOP_K), lambda i: (i, 0, 0), memory_space=pltpu.SMEM),
            pl.BlockSpec((tt, 128), lambda i: (i, 0)),
            pl.BlockSpec((tt, D_MODEL), lambda i: (i, 0)),
            pl.BlockSpec(memory_space=pl.ANY),
        ],
        out_specs=pl.BlockSpec((tt, D_MODEL), lambda i: (i, 0)),
        out_shape=jax.ShapeDtypeStruct((m, D_MODEL), F32),
        scratch_shapes=[pltpu.VMEM((TOP_K, tt, D_MODEL), F32), pltpu.SemaphoreType.DMA(())],
        compiler_params=_cparams(("arbitrary",)),
        name="moe_combine",
    )(dest.reshape(nt, 1, tt * TOP_K), gates, h1, ys)


def _routing_tables(top_idx, rank, counts, n_wi, n_slots):
    n_tok = top_idx.shape[0]
    padded = (counts + MOE_SUB - 1) // MOE_SUB * MOE_SUB
    pad_start = jnp.cumsum(padded) - padded
    dest = pad_start[top_idx] + rank
    tok_ids = jnp.broadcast_to(jnp.arange(n_tok, dtype=I32)[:, None], (n_tok, TOP_K))
    slot_tok = jnp.zeros((n_slots,), I32).at[dest.reshape(-1)].set(
        tok_ids.reshape(-1), unique_indices=True)
    nsubs = padded // MOE_SUB
    items = (nsubs + MOE_NSUB - 1) // MOE_NSUB
    item_end = jnp.cumsum(items)
    total = item_end[-1]
    w = jnp.arange(n_wi, dtype=I32)
    w_eff = jnp.minimum(w, total - 1)
    wi_e = jnp.sum((item_end[None, :] <= w_eff[:, None]).astype(I32), axis=1)
    wi_e = jnp.minimum(wi_e, N_EXPERTS - 1)
    local = w_eff - (item_end - items)[wi_e]
    wi_row = (pad_start[wi_e] + local * MOE_TM).astype(I32)
    wi_nsub = jnp.where(w < total, jnp.clip(nsubs[wi_e] - local * MOE_NSUB, 0, MOE_NSUB), 0).astype(I32)
    tok_pos = jnp.minimum(wi_row[:, None] + jnp.arange(MOE_TM, dtype=I32)[None, :], n_slots - 1)
    wi_tok = slot_tok[tok_pos].reshape(n_wi, 1, MOE_TM)
    return dest.astype(I32), wi_e, wi_row, wi_nsub, wi_tok


def kernel(x_prompt, x_sample, cache_k, cache_v, cache_logf, state_pool, page_table,
           norm1_w, w_in, b_forget, q_norm_w, k_norm_w, w_pool, pool_scale,
           w_branch_a, w_branch_b, w_out, norm2_w, router_w, router_b,
           w_gu, b_gu, w_dn, b_dn):
    depth = w_in.shape[0]
    assert depth == 1
    bp, lp, _ = x_prompt.shape
    bs, ls, _ = x_sample.shape
    assert ls == 8 and N_HEADS == 8
    n_p, n_s = bp * lp, bs * ls
    n_all = n_p + n_s
    n_past = page_table.shape[1] * PAGE_SIZE

    wt = jnp.transpose(w_in[0])
    f_off = 3 * ATTN_WIDTH
    w_main = jnp.concatenate([wt[:f_off], wt[f_off + N_HEADS:]], axis=0).astype(BF16)
    w_f = jnp.pad(wt[f_off:f_off + N_HEADS], ((0, FORGET_LANES - N_HEADS), (0, 0))).astype(BF16)
    b_f = jnp.pad(b_forget[0], (0, FORGET_LANES - N_HEADS)).reshape(1, FORGET_LANES)
    n1w = norm1_w[0].reshape(1, D_MODEL)
    n2w = norm2_w[0].reshape(1, D_MODEL)
    qnw = jnp.tile(q_norm_w[0], N_HEADS).reshape(1, ATTN_WIDTH)
    knw = jnp.tile(k_norm_w[0], N_HEADS).reshape(1, ATTN_WIDTH)
    wp = w_pool[0].astype(BF16)
    ps = pool_scale[0].reshape(1, POOL_WIDTH)
    wa = w_branch_a[0].astype(BF16)
    wb = w_branch_b[0].astype(BF16)
    wo = w_out[0].astype(BF16)
    rw = jnp.pad(router_w[0], ((0, 0), (0, 128 - N_EXPERTS)))
    rb = jnp.pad(router_b[0], (0, 128 - N_EXPERTS), constant_values=NEG_BIG).reshape(1, 128)

    xp = x_prompt.reshape(n_p, D_MODEL)
    qp, kp, vp, up, gp, lfp = _inproj(xp, n1w, w_main, w_f, b_f, qnw, knw, tm=512, tn=1024,
                                      q_scale=ATTN_SCALE * LOG2E)
    cp = _cumsum_rows(lfp, bl=512, seg_len=lp)
    cp3 = cp.reshape(bp, lp, FORGET_LANES)
    attn_p = _prompt_attention(qp.reshape(bp, lp, ATTN_WIDTH), kp.reshape(bp, lp, ATTN_WIDTH),
                               vp.reshape(bp, lp, ATTN_WIDTH), cp3, _key_bias_pieces(cp3))
    up3 = up.reshape(bp, lp, POOL_WIDTH)
    pool_tm = 1024
    halo_p = pl.BlockSpec((1, POOL_HALO, POOL_WIDTH),
                          lambda bi, i: (bi, jnp.maximum(i * (pool_tm // POOL_HALO) - 1, 0), 0))
    pool_p = _pool_mix(up3, halo_p, up3, wp, ps, tm=pool_tm, base_pos=0, zero_first_halo=True)
    h1_p, xn_all, ti_p, gt_p = _merge(
        attn_p.reshape(n_p, ATTN_WIDTH), pool_p.reshape(n_p, POOL_WIDTH), gp, xp,
        wa, wb, wo, n2w, rw, rb, tm=256, xn_all=None, xn_row0=0, n_all=n_all)

    xs = x_sample.reshape(n_s, D_MODEL)
    qs, ks, vs, us, gs, lfs = _inproj(xs, n1w, w_main, w_f, b_f, qnw, knw, tm=n_s, tn=512, q_scale=1.0)
    cs = _cumsum_rows(lfs, bl=n_s, seg_len=ls)
    q_s = qs.astype(F32).reshape(bs, ls, N_HEADS, HEAD_DIM)
    qbd = jnp.einsum('bthd,hg->bhtgd', q_s, jnp.eye(N_HEADS, dtype=F32))
    qbd = qbd.reshape(bs, N_HEADS * ls, ATTN_WIDTH).astype(BF16)
    cn = jnp.transpose(cs.reshape(bs, ls, FORGET_LANES)[:, :, :N_HEADS], (0, 2, 1))
    cncol = cn.reshape(bs, N_HEADS * ls, 1)
    cnkeys = jnp.broadcast_to(cn[:, :, None, :], (bs, N_HEADS, ls, ls)).reshape(bs, N_HEADS * ls, ls)
    cnkeys = jnp.pad(cnkeys, ((0, 0), (0, 0), (0, PAGE_SIZE - ls)))
    knew = jnp.pad(ks.reshape(bs, ls, ATTN_WIDTH), ((0, 0), (0, PAGE_SIZE - ls), (0, 0)))
    vnew = jnp.pad(vs.reshape(bs, ls, ATTN_WIDTH), ((0, 0), (0, PAGE_SIZE - ls), (0, 0)))
    n_phys = cache_k.shape[1]
    ck = cache_k[0].reshape(n_phys, PAGE_SIZE * N_HEADS, HEAD_DIM)
    cv = cache_v[0].reshape(n_phys, PAGE_SIZE * N_HEADS, HEAD_DIM)
    clf_t = jnp.transpose(cache_logf[0], (0, 2, 1))
    attn_s = _sample_attention(page_table, qbd, cncol, cnkeys, knew, vnew, ck, cv, clf_t)
    halo_src = jnp.pad(state_pool[0], ((0, 0), (POOL_HALO - POOL_HIST, 0), (0, 0)))
    halo_s = pl.BlockSpec((1, POOL_HALO, POOL_WIDTH), lambda bi, i: (bi, 0, 0))
    us3 = us.reshape(bs, ls, POOL_WIDTH)
    pool_s = _pool_mix(halo_src, halo_s, us3, wp, ps, tm=ls, base_pos=n_past, zero_first_halo=False)
    h1_s, xn_all, ti_s, gt_s = _merge(
        attn_s.reshape(n_s, ATTN_WIDTH), pool_s.reshape(n_s, POOL_WIDTH), gs, xs,
        wa, wb, wo, n2w, rw, rb, tm=n_s, xn_all=xn_all, xn_row0=n_p, n_all=n_all)

    n_rank = pl.cdiv(n_all, RANK_TILE) * RANK_TILE
    ti_all = jnp.concatenate([ti_p, ti_s, jnp.full((n_rank - n_all, 128), -1, I32)], axis=0)
    rank, cnt = _expert_rank(ti_all, tr=RANK_TILE)
    top_idx = ti_all[:n_all, :TOP_K]
    counts = cnt[0, :N_EXPERTS].astype(I32)
    n_blocks = pl.cdiv(n_all * TOP_K, MOE_SUB) + N_EXPERTS
    n_slots = n_blocks * MOE_SUB
    n_wi = n_blocks // MOE_NSUB + N_EXPERTS
    assert (n_blocks + (MOE_NSUB - 1) * N_EXPERTS) // MOE_NSUB <= n_wi
    dest, wi_e, wi_row, wi_nsub, wi_tok = _routing_tables(
        top_idx, rank[:n_all, :TOP_K], counts, n_wi, n_slots)
    ys = _moe_experts(wi_e, wi_row, wi_nsub, wi_tok, xn_all, w_gu[0], b_gu[0], w_dn[0], b_dn[0], n_slots)
    y_p = _combine(dest[:n_p], gt_p, h1_p, ys, tt=512)
    y_s = _combine(dest[n_p:], gt_s, h1_s, ys, tt=n_s)

    heads = (N_HEADS, HEAD_DIM)
    k_p = kp.reshape(1, bp, lp, *heads)
    v_p = vp.reshape(1, bp, lp, *heads)
    f_p = lfp.reshape(bp, lp, FORGET_LANES)[None, :, :, :N_HEADS]
    u_p = up3[:, lp - POOL_HIST:][None]
    k_s = ks.reshape(1, bs, ls, *heads)
    v_s = vs.reshape(1, bs, ls, *heads)
    f_s = lfs.reshape(bs, ls, FORGET_LANES)[None, :, :, :N_HEADS]
    pool_state = jnp.concatenate([state_pool[0], us3], axis=1)[:, -POOL_HIST:][None]
    return (y_p.reshape(bp, lp, D_MODEL), y_s.reshape(bs, ls, D_MODEL),
            k_p, v_p, f_p, u_p, k_s, v_s, f_s, pool_state)
```

```python
import functools

import jax
import jax.numpy as jnp
from jax import lax
from jax.experimental import pallas as pl
from jax.experimental.pallas import tpu as pltpu

F32 = jnp.float32
BF16 = jnp.bfloat16
I32 = jnp.int32

D_MODEL = 2048
N_HEADS = 8
HEAD_DIM = 128
ATTN_WIDTH = N_HEADS * HEAD_DIM
ATTN_SCALE = HEAD_DIM ** -0.5
FORGET_LANES = 128
POOL_WINDOWS = (2, 4, 8, 16)
POOL_WIDTH = D_MODEL // 2
GW = POOL_WIDTH // len(POOL_WINDOWS)
POOL_HIST = max(POOL_WINDOWS) - 1
POOL_HALO = 16
N_EXPERTS = 32
TOP_K = 4
D_EXPERT = D_MODEL
SWIGLU_LIMIT = 7.0
SWIGLU_ALPHA = 1.702
NORM_EPS = 1e-6
PAGE_SIZE = 128
NEG_BIG = -1e30

ZQ, ZK, ZV, ZU = 0, 1, 2, 3
Z_WIDTH = 3 * ATTN_WIDTH + POOL_WIDTH + 2 * D_MODEL

MOE_SUB = 128
MOE_NSUB = 9
MOE_TM = MOE_SUB * MOE_NSUB
MOE_TN = 512
RANK_TILE = 512
VMEM_LIMIT = 56 * 1024 * 1024

HIGHEST = lax.Precision.HIGHEST


def _cparams(sem, vmem=VMEM_LIMIT, flags=None):
    return pltpu.CompilerParams(dimension_semantics=sem, vmem_limit_bytes=vmem, flags=flags)


def _log_sigmoid(x):
    return jnp.minimum(x, 0.0) - jnp.log1p(jnp.exp(-jnp.abs(x)))


_NT = (((1,), (1,)), ((), ()))


def _inproj_kernel(x_ref, n1w_ref, w_ref, wf_ref, bf_ref, qnw_ref, knw_ref,
                   q_ref, k_ref, v_ref, u_ref, g_ref, lf_ref, xn_ref, *, q_scale, sub):
    j = pl.program_id(1)

    @pl.when(j == 0)
    def _():
        x = x_ref[...]
        ms = jnp.mean(x * x, axis=-1, keepdims=True)
        xn = x * lax.rsqrt(ms + NORM_EPS) * n1w_ref[...]
        xn_ref[...] = xn.astype(BF16)
        zf = lax.dot_general(xn_ref[...], wf_ref[...], _NT, preferred_element_type=F32) + bf_ref[...]
        lf_ref[...] = _log_sigmoid(zf)

    def proj():
        return lax.dot_general(xn_ref[...], w_ref[...], _NT, preferred_element_type=F32)

    def in_segment(seg):
        return (j >= seg * sub) & (j < (seg + 1) * sub)

    def head_norm(nw_ref, dst_ref, scale):
        z = proj()
        for h in range(z.shape[1] // HEAD_DIM):
            sl = slice(h * HEAD_DIM, (h + 1) * HEAD_DIM)
            zh = z[:, sl]
            ms = jnp.mean(zh * zh, axis=-1, keepdims=True)
            y = zh * lax.rsqrt(ms + NORM_EPS) * nw_ref[:, sl]
            if scale != 1.0:
                y = y * scale
            dst_ref[:, sl] = y.astype(dst_ref.dtype)

    @pl.when(in_segment(ZQ))
    def _():
        head_norm(qnw_ref, q_ref, q_scale)

    @pl.when(in_segment(ZK))
    def _():
        head_norm(knw_ref, k_ref, 1.0)

    @pl.when(in_segment(ZV))
    def _():
        v_ref[...] = proj()

    @pl.when(in_segment(ZU))
    def _():
        u_ref[...] = proj()

    @pl.when(j >= (ZU + 1) * sub)
    def _():
        g_ref[...] = jax.nn.sigmoid(proj())


def _inproj(x, n1w, w_main_t, w_f_t, b_f, qnw, knw, tm, tn, q_scale):
    m = x.shape[0]
    sub = ATTN_WIDTH // tn
    row = lambda i, j: (i, 0)
    const = lambda i, j: (0, 0)

    def seg_part(seg):
        return lambda i, j: (i, jnp.clip(j - seg * sub, 0, sub - 1))

    return pl.pallas_call(
        functools.partial(_inproj_kernel, q_scale=q_scale, sub=sub),
        grid=(m // tm, Z_WIDTH // tn),
        in_specs=[
            pl.BlockSpec((tm, D_MODEL), row),
            pl.BlockSpec((1, D_MODEL), const),
            pl.BlockSpec((tn, D_MODEL), lambda i, j: (j, 0)),
            pl.BlockSpec((FORGET_LANES, D_MODEL), const),
            pl.BlockSpec((1, FORGET_LANES), const),
            pl.BlockSpec((1, tn), const),
            pl.BlockSpec((1, tn), const),
        ],
        out_specs=[
            pl.BlockSpec((tm, tn), seg_part(ZQ)),
            pl.BlockSpec((tm, tn), seg_part(ZK)),
            pl.BlockSpec((tm, tn), seg_part(ZV)),
            pl.BlockSpec((tm, tn), seg_part(ZU)),
            pl.BlockSpec((tm, tn), lambda i, j: (i, jnp.maximum(j - (ZU + 1) * sub, 0))),
            pl.BlockSpec((tm, FORGET_LANES), row),
        ],
        out_shape=[
            jax.ShapeDtypeStruct((m, ATTN_WIDTH), BF16),
            jax.ShapeDtypeStruct((m, ATTN_WIDTH), F32),
            jax.ShapeDtypeStruct((m, ATTN_WIDTH), F32),
            jax.ShapeDtypeStruct((m, POOL_WIDTH), F32),
            jax.ShapeDtypeStruct((m, 2 * D_MODEL), F32),
            jax.ShapeDtypeStruct((m, FORGET_LANES), F32),
        ],
        scratch_shapes=[pltpu.VMEM((tm, D_MODEL), BF16)],
        compiler_params=_cparams(("arbitrary", "arbitrary")),
        name="inproj",
    )(x, n1w, w_main_t, w_f_t, b_f, qnw, knw)


def _cumsum_kernel(lf_ref, c_ref, carry_ref, *, bl, seg_len):
    i = pl.program_id(0)
    r = lax.broadcasted_iota(I32, (bl, bl), 0)
    c = lax.broadcasted_iota(I32, (bl, bl), 1)
    mask = c <= r
    if seg_len < bl:
        mask = mask & ((r // seg_len) == (c // seg_len))
    tri = mask.astype(F32)
    cs = jnp.dot(tri, lf_ref[...], preferred_element_type=F32, precision=HIGHEST)
    if seg_len > bl:
        @pl.when((i * bl) % seg_len == 0)
        def _():
            carry_ref[...] = jnp.zeros_like(carry_ref)
        cs = cs + carry_ref[...]
        carry_ref[...] = cs[bl - 1:bl, :]
    c_ref[...] = cs


def _cumsum_rows(lf, bl, seg_len):
    m = lf.shape[0]
    return pl.pallas_call(
        functools.partial(_cumsum_kernel, bl=bl, seg_len=seg_len),
        grid=(m // bl,),
        in_specs=[pl.BlockSpec((bl, FORGET_LANES), lambda i: (i, 0))],
        out_specs=pl.BlockSpec((bl, FORGET_LANES), lambda i: (i, 0)),
        out_shape=jax.ShapeDtypeStruct((m, FORGET_LANES), F32),
        scratch_shapes=[pltpu.VMEM((1, FORGET_LANES), F32)],
        compiler_params=_cparams(("arbitrary",)),
        name="logf_cumsum",
    )(lf)


LOG2E = 1.4426950408889634


def _pattn_kernel(q_ref, k_ref, v_ref, c_ref, kbias_ref, o_ref, *scratch, tq, ch):
    m_refs = scratch[0:N_HEADS]
    accl_refs = scratch[N_HEADS:2 * N_HEADS]
    kb_ref, vb_ref = scratch[2 * N_HEADS:]
    qi = pl.program_id(1)
    ki = pl.program_id(2)
    tk = tq

    @pl.when(ki == 0)
    def _():
        for h in range(N_HEADS):
            m_refs[h][...] = jnp.full_like(m_refs[h], -jnp.inf)
            accl_refs[h][...] = jnp.zeros_like(accl_refs[h])

    @pl.when(ki <= qi)
    def _():
        kb_ref[...] = k_ref[0].astype(BF16)
        vb_ref[...] = v_ref[0].astype(BF16)

    def block(diag):
        c2 = c_ref[0] * LOG2E
        q_ones = (lax.broadcasted_iota(I32, (ch, HEAD_DIM), 1) < 3).astype(BF16)
        v_ones = jnp.ones((tk, HEAD_DIM), BF16)
        below = (lax.broadcasted_iota(I32, (ch, ch), 1) <= lax.broadcasted_iota(I32, (ch, ch), 0))

        def where(h, r):
            rows = slice(r * ch, (r + 1) * ch)
            cols = slice(0, (r + 1) * ch if diag else tk)
            return rows, cols, slice(h * HEAD_DIM, (h + 1) * HEAD_DIM)

        def stage_scores(h, r):
            rows, cols, sl = where(h, r)
            q_cat = jnp.concatenate([q_ref[0, rows, sl], q_ones], axis=1)
            k_cat = jnp.concatenate([kb_ref[cols, sl], kbias_ref[0, h, cols, :]], axis=1)
            u = lax.dot_general(q_cat, k_cat, _NT, preferred_element_type=F32)
            if diag:
                last = jnp.where(below, u[:, r * ch:], -jnp.inf)
                u = last if r == 0 else jnp.concatenate([u[:, :r * ch], last], axis=1)
            return u

        def stage_max(h, r, u):
            rows, _, _ = where(h, r)
            c2t = c2[rows, h:h + 1]
            m_prev = m_refs[h][rows, :]
            m_new = jnp.maximum(m_prev, jnp.max(u, axis=-1, keepdims=True) + c2t)
            m_refs[h][rows, :] = m_new
            return jnp.exp2(m_prev - m_new), c2t - m_new[:, 0:1]

        def stage_accumulate(h, r, p, alpha):
            rows, cols, sl = where(h, r)
            v_cat = jnp.concatenate([vb_ref[cols, sl], v_ones[cols, :]], axis=1)
            pv = jnp.dot(p, v_cat, preferred_element_type=F32)
            accl_refs[h][rows, :] = jnp.concatenate([alpha, alpha], axis=1) * accl_refs[h][rows, :] + pv

        units = [(h, r) for r in range(tq // ch) for h in range(N_HEADS)]
        u_of, ab_of, p_of = {}, {}, {}
        for t in range(len(units) + 3):
            if t < len(units):
                u_of[t] = stage_scores(*units[t])
            if 0 <= t - 1 < len(units):
                ab_of[t - 1] = stage_max(*units[t - 1], u_of[t - 1])
            if 0 <= t - 2 < len(units):
                p_of[t - 2] = jnp.exp2(u_of.pop(t - 2) + ab_of[t - 2][1]).astype(BF16)
            if 0 <= t - 3 < len(units):
                stage_accumulate(*units[t - 3], p_of.pop(t - 3), ab_of.pop(t - 3)[0])

    @pl.when(ki < qi)
    def _():
        block(False)

    @pl.when(ki == qi)
    def _():
        block(True)
        for h in range(N_HEADS):
            sl = slice(h * HEAD_DIM, (h + 1) * HEAD_DIM)
            accl = accl_refs[h][...]
            o_ref[0, :, sl] = (accl[:, :HEAD_DIM] / accl[:, HEAD_DIM:]).astype(o_ref.dtype)


def _prompt_attention(q3, k3, v3, c3, kbias, tq=512, ch=256):
    b, l, _ = q3.shape
    tk = tq
    nq = l // tq
    kv_idx = lambda bi, qi, ki: (bi, jnp.minimum(ki, qi), 0)
    return pl.pallas_call(
        functools.partial(_pattn_kernel, tq=tq, ch=ch),
        grid=(b, nq, nq),
        in_specs=[
            pl.BlockSpec((1, tq, ATTN_WIDTH), lambda bi, qi, ki: (bi, qi, 0)),
            pl.BlockSpec((1, tk, ATTN_WIDTH), kv_idx),
            pl.BlockSpec((1, tk, ATTN_WIDTH), kv_idx),
            pl.BlockSpec((1, tq, FORGET_LANES), lambda bi, qi, ki: (bi, qi, 0)),
            pl.BlockSpec((1, N_HEADS, tk, HEAD_DIM),
                         lambda bi, qi, ki: (bi, 0, jnp.minimum(ki, qi), 0)),
        ],
        out_specs=pl.BlockSpec((1, tq, ATTN_WIDTH), lambda bi, qi, ki: (bi, qi, 0)),
        out_shape=jax.ShapeDtypeStruct((b, l, ATTN_WIDTH), BF16),
        scratch_shapes=(
            [pltpu.VMEM((tq, HEAD_DIM), F32) for _ in range(N_HEADS)]
            + [pltpu.VMEM((tq, 2 * HEAD_DIM), F32) for _ in range(N_HEADS)]
            + [pltpu.VMEM((tk, ATTN_WIDTH), BF16), pltpu.VMEM((tk, ATTN_WIDTH), BF16)]),
        compiler_params=_cparams(("arbitrary", "arbitrary", "arbitrary")),
        name="prompt_attention",
    )(q3, k3, v3, c3, kbias)


def _bf16_pieces(x, n):
    out = []
    for _ in range(n):
        piece = x.astype(BF16).astype(F32)
        out.append(piece)
        x = x - piece
    return out


def _kbias_kernel(c_ref, o_ref):
    c = c_ref[0] * (-LOG2E)
    lane = lax.broadcasted_iota(I32, c.shape, 1)
    for h in range(N_HEADS):
        hi, mid, lo = _bf16_pieces(c[:, h:h + 1], 3)
        tile = jnp.where(lane == 0, hi, jnp.where(lane == 1, mid, jnp.where(lane == 2, lo, 0.0)))
        o_ref[0, h] = tile.astype(BF16)


def _key_bias_pieces(c3, tl=512):
    b, l, _ = c3.shape
    return pl.pallas_call(
        _kbias_kernel,
        grid=(b, l // tl),
        in_specs=[pl.BlockSpec((1, tl, FORGET_LANES), lambda bi, i: (bi, i, 0))],
        out_specs=pl.BlockSpec((1, N_HEADS, tl, HEAD_DIM), lambda bi, i: (bi, 0, i, 0)),
        out_shape=jax.ShapeDtypeStruct((b, N_HEADS, l, HEAD_DIM), BF16),
        compiler_params=_cparams(("arbitrary", "arbitrary")),
        name="key_bias",
    )(c3)


def _heads_to_lanes(ref):
    parts = [ref[0, pl.ds(h, PAGE_SIZE, stride=N_HEADS), :] for h in range(N_HEADS)]
    return jnp.concatenate(parts, axis=1).astype(BF16)


def _rows_per_head(x):
    return jnp.concatenate(
        [jnp.broadcast_to(x[h:h + 1, :], (8, x.shape[1])) for h in range(N_HEADS)], axis=0)


def _sattn_kernel(pt_ref, qbd_ref, cncol_ref, cnkeys_ref, knew_ref, vnew_ref, *rest, g_pages):
    del pt_ref
    k_refs = rest[0:g_pages]
    v_refs = rest[g_pages:2 * g_pages]
    lf_refs = rest[2 * g_pages:3 * g_pages]
    o_ref = rest[3 * g_pages]
    m_ref, l_ref, acc_ref, carry_ref = rest[3 * g_pages + 1:]
    jj = pl.program_id(1)
    nj = pl.num_programs(1)
    rows = N_HEADS * 8

    @pl.when(jj == 0)
    def _():
        m_ref[...] = jnp.full_like(m_ref, -jnp.inf)
        l_ref[...] = jnp.zeros_like(l_ref)
        acc_ref[...] = jnp.zeros_like(acc_ref)
        carry_ref[...] = jnp.zeros_like(carry_ref)

    qbd = qbd_ref[0]
    cncol = cncol_ref[0]
    later = (lax.broadcasted_iota(I32, (PAGE_SIZE, PAGE_SIZE), 0)
             > lax.broadcasted_iota(I32, (PAGE_SIZE, PAGE_SIZE), 1)).astype(F32)

    def scores(kcat):
        return lax.dot_general(qbd, kcat, (((1,), (1,)), ((), ())),
                               preferred_element_type=F32) * ATTN_SCALE

    def update(s, vcat, m, l, acc):
        m_new = jnp.maximum(m, jnp.max(s, axis=-1, keepdims=True))
        alpha = jnp.exp(m - m_new)
        p = jnp.exp(s - jnp.concatenate([m_new] * (s.shape[1] // PAGE_SIZE), axis=1))
        l = alpha * l + jnp.sum(p, axis=-1, keepdims=True)
        pv = jnp.dot(p.astype(BF16), vcat, preferred_element_type=F32)
        acc = jnp.concatenate([alpha] * N_HEADS, axis=1) * acc + pv
        return m_new, l, acc

    m, l, acc, carry = m_ref[...], l_ref[...], acc_ref[...], carry_ref[...]
    lf_all = jnp.concatenate([_rows_per_head(lf_refs[g][0]) for g in range(g_pages)], axis=0)
    suffix_all = jnp.dot(lf_all, later, preferred_element_type=F32, precision=HIGHEST)
    bias = []
    for g in range(g_pages):
        bias.append(suffix_all[g * rows:(g + 1) * rows, :] + carry)
        carry = carry + jnp.sum(lf_all[g * rows:(g + 1) * rows, :], axis=-1, keepdims=True)
    kcat = jnp.concatenate([_heads_to_lanes(k_refs[g]) for g in range(g_pages)], axis=0)
    vcat = jnp.concatenate([_heads_to_lanes(v_refs[g]) for g in range(g_pages)], axis=0)
    s = scores(kcat) + cncol + jnp.concatenate(bias, axis=1)
    m, l, acc = update(s, vcat, m, l, acc)

    @pl.when(jj < nj - 1)
    def _():
        m_ref[...] = m
        l_ref[...] = l
        acc_ref[...] = acc
        carry_ref[...] = carry

    @pl.when(jj == nj - 1)
    def _():
        s = scores(knew_ref[0].astype(BF16)) + cncol - cnkeys_ref[0]
        t_row = lax.broadcasted_iota(I32, (rows, PAGE_SIZE), 0) % 8
        s_col = lax.broadcasted_iota(I32, (rows, PAGE_SIZE), 1)
        s = jnp.where(s_col <= t_row, s, -jnp.inf)
        _, l2, acc2 = update(s, vnew_ref[0].astype(BF16), m, l, acc)
        o = acc2 / jnp.concatenate([l2] * N_HEADS, axis=1)
        for h in range(N_HEADS):
            sl = slice(h * HEAD_DIM, (h + 1) * HEAD_DIM)
            o_ref[0, :, sl] = o[h * 8:(h + 1) * 8, sl].astype(o_ref.dtype)


def _sample_attention(page_table, qbd, cncol, cnkeys, knew, vnew, ck, cv, clf_t, g_pages=16):
    bd, n_pages = page_table.shape
    rows = N_HEADS * 8
    pt_flat = page_table.reshape(-1)

    def page_idx(g):
        return lambda b, jj, pt: (pt[b * n_pages + n_pages - 1 - (jj * g_pages + g)], 0, 0)

    per_b = lambda b, jj, pt: (b, 0, 0)
    in_specs = [
        pl.BlockSpec((1, rows, ATTN_WIDTH), per_b),
        pl.BlockSpec((1, rows, 1), per_b),
        pl.BlockSpec((1, rows, PAGE_SIZE), per_b),
        pl.BlockSpec((1, PAGE_SIZE, ATTN_WIDTH), per_b),
        pl.BlockSpec((1, PAGE_SIZE, ATTN_WIDTH), per_b),
    ]
    in_specs += [pl.BlockSpec((1, PAGE_SIZE * N_HEADS, HEAD_DIM), page_idx(g)) for g in range(g_pages)]
    in_specs += [pl.BlockSpec((1, PAGE_SIZE * N_HEADS, HEAD_DIM), page_idx(g)) for g in range(g_pages)]
    in_specs += [pl.BlockSpec((1, N_HEADS, PAGE_SIZE), page_idx(g)) for g in range(g_pages)]
    grid_spec = pltpu.PrefetchScalarGridSpec(
        num_scalar_prefetch=1,
        grid=(bd, n_pages // g_pages),
        in_specs=in_specs,
        out_specs=pl.BlockSpec((1, 8, ATTN_WIDTH), per_b),
        scratch_shapes=[
            pltpu.VMEM((rows, PAGE_SIZE), F32),
            pltpu.VMEM((rows, PAGE_SIZE), F32),
            pltpu.VMEM((rows, ATTN_WIDTH), F32),
            pltpu.VMEM((rows, PAGE_SIZE), F32),
        ],
    )
    return pl.pallas_call(
        functools.partial(_sattn_kernel, g_pages=g_pages),
        grid_spec=grid_spec,
        out_shape=jax.ShapeDtypeStruct((bd, 8, ATTN_WIDTH), BF16),
        compiler_params=_cparams(("arbitrary", "arbitrary")),
        name="sample_attention",
    )(pt_flat, qbd, cncol, cnkeys, knew, vnew,
      *([ck] * g_pages), *([cv] * g_pages), *([clf_t] * g_pages))


def _pool_kernel(halo_ref, u_ref, wp_ref, ps_ref, o_ref, *, tm, base_pos, zero_first_halo):
    i = pl.program_id(1)
    halo = halo_ref[0]
    if zero_first_halo:
        halo = jnp.where(i == 0, 0.0, halo)
    u = u_ref[0]
    ext = jnp.concatenate([halo, u], axis=0)
    pos = base_pos + i * tm + lax.broadcasted_iota(I32, (tm, 1), 0)
    for g, w in enumerate(POOL_WINDOWS):
        sl = slice(g * GW, (g + 1) * GW)
        s = ext[:, sl]
        k = 1
        while k < w:
            s = s + pltpu.roll(s, shift=k, axis=0)
            k *= 2
        wsum = s[POOL_HALO:, :]
        count = jnp.minimum(pos + 1, w).astype(F32)
        mixed = wsum / count - u[:, sl]
        y = jnp.dot(mixed.astype(BF16), wp_ref[g], preferred_element_type=F32)
        o_ref[0, :, sl] = (y * ps_ref[:, sl]).astype(o_ref.dtype)


def _pool_mix(halo_src, halo_spec, u_src, wp, ps, tm, base_pos, zero_first_halo):
    b, l, _ = u_src.shape
    return pl.pallas_call(
        functools.partial(_pool_kernel, tm=tm, base_pos=base_pos, zero_first_halo=zero_first_halo),
        grid=(b, l // tm),
        in_specs=[
            halo_spec,
            pl.BlockSpec((1, tm, POOL_WIDTH), lambda bi, i: (bi, i, 0)),
            pl.BlockSpec((len(POOL_WINDOWS), GW, GW), lambda bi, i: (0, 0, 0)),
            pl.BlockSpec((1, POOL_WIDTH), lambda bi, i: (0, 0)),
        ],
        out_specs=pl.BlockSpec((1, tm, POOL_WIDTH), lambda bi, i: (bi, i, 0)),
        out_shape=jax.ShapeDtypeStruct((b, l, POOL_WIDTH), BF16),
        compiler_params=_cparams(("arbitrary", "arbitrary")),
        name="pool_mix",
    )(halo_src, u_src, wp, ps)


def _merge_kernel(attn_ref, pool_ref, ga_ref, gb_ref, x_ref, wa_ref, wb_ref, wo_ref,
                  n2w_ref, rw_ref, rb_ref, *rest, tm, aliased):
    h_ref, xn_ref, ti_ref, gt_ref, rwp_ref = rest[-5:]

    @pl.when(pl.program_id(0) == 0)
    def _():
        for n, piece in enumerate(_bf16_pieces(rw_ref[...], 3)):
            rwp_ref[n] = piece.astype(BF16)

    a = jnp.dot(attn_ref[...], wa_ref[...], preferred_element_type=F32)
    p = jnp.dot(pool_ref[...], wb_ref[...], preferred_element_type=F32)
    merged = ga_ref[...] * a + gb_ref[...] * p
    o = jnp.dot(merged.astype(BF16), wo_ref[...], preferred_element_type=F32)
    h1 = x_ref[...] + o
    h_ref[...] = h1
    ms = jnp.mean(h1 * h1, axis=-1, keepdims=True)
    xn = h1 * lax.rsqrt(ms + NORM_EPS) * n2w_ref[...]
    xn_ref[...] = xn
    x_hi, x_lo = [v.astype(BF16) for v in _bf16_pieces(xn, 2)]
    logits = rb_ref[...]
    for xp, wn in ((x_hi, 0), (x_hi, 1), (x_lo, 0), (x_hi, 2), (x_lo, 1)):
        logits = logits + jnp.dot(xp, rwp_ref[wn], preferred_element_type=F32)
    lane = lax.broadcasted_iota(I32, (tm, 128), 1).astype(F32)
    vals, idxs = [], []
    work = logits
    for _ in range(TOP_K):
        mv = jnp.max(work, axis=-1, keepdims=True)
        ix = jnp.min(jnp.where(work == mv, lane, 128.0), axis=-1, keepdims=True)
        vals.append(mv)
        idxs.append(ix)
        work = jnp.where(lane == ix, NEG_BIG, work)
    es = [jnp.exp(v - vals[0]) for v in vals]
    den = es[0] + es[1] + es[2] + es[3]
    ti = jnp.zeros((tm, 128), F32)
    gt = jnp.zeros((tm, 128), F32)
    for k in range(TOP_K):
        ti = jnp.where(lane == float(k), idxs[k], ti)
        gt = jnp.where(lane == float(k), es[k] / den, gt)
    ti_ref[...] = ti.astype(I32)
    gt_ref[...] = gt


def _merge(attn, pool, z, x, wa, wb, wo, n2w, rw, rb, tm, xn_all, xn_row0, n_all):
    m = x.shape[0]
    aliased = xn_all is not None
    const = lambda i: (0, 0)
    single = dict(pipeline_mode=pl.Buffered(1))
    xn_blk0 = xn_row0 // tm
    in_specs = [
        pl.BlockSpec((tm, ATTN_WIDTH), lambda i: (i, 0)),
        pl.BlockSpec((tm, POOL_WIDTH), lambda i: (i, 0)),
        pl.BlockSpec((tm, D_MODEL), lambda i: (i, 0)),
        pl.BlockSpec((tm, D_MODEL), lambda i: (i, 1)),
        pl.BlockSpec((tm, D_MODEL), lambda i: (i, 0)),
        pl.BlockSpec((ATTN_WIDTH, D_MODEL), const, **single),
        pl.BlockSpec((POOL_WIDTH, D_MODEL), const, **single),
        pl.BlockSpec((D_MODEL, D_MODEL), const, **single),
        pl.BlockSpec((1, D_MODEL), const),
        pl.BlockSpec((D_MODEL, 128), const),
        pl.BlockSpec((1, 128), const),
    ]
    args = [attn, pool, z, z, x, wa, wb, wo, n2w, rw, rb]
    io_alias = {}
    if aliased:
        in_specs.append(pl.BlockSpec(memory_space=pl.ANY))
        args.append(xn_all)
        io_alias = {len(args) - 1: 1}
    return pl.pallas_call(
        functools.partial(_merge_kernel, tm=tm, aliased=aliased),
        grid=(m // tm,),
        in_specs=in_specs,
        out_specs=[
            pl.BlockSpec((tm, D_MODEL), lambda i: (i, 0)),
            pl.BlockSpec((tm, D_MODEL), lambda i: (xn_blk0 + i, 0)),
            pl.BlockSpec((tm, 128), lambda i: (i, 0)),
            pl.BlockSpec((tm, 128), lambda i: (i, 0)),
        ],
        out_shape=[
            jax.ShapeDtypeStruct((m, D_MODEL), F32),
            jax.ShapeDtypeStruct((n_all, D_MODEL), F32),
            jax.ShapeDtypeStruct((m, 128), I32),
            jax.ShapeDtypeStruct((m, 128), F32),
        ],
        input_output_aliases=io_alias,
        scratch_shapes=[pltpu.VMEM((3, D_MODEL, 128), BF16)],
        compiler_params=_cparams(("arbitrary",)),
        name="merge_router",
    )(*args)


def _rank_kernel(idx_ref, rank_ref, cnt_ref, carry_ref, *, tr):
    i = pl.program_id(0)

    @pl.when(i == 0)
    def _():
        carry_ref[...] = jnp.zeros_like(carry_ref)

    idx = idx_ref[...]
    lane = lax.broadcasted_iota(I32, (tr, 128), 1)
    hits = [idx[:, k:k + 1] == lane for k in range(TOP_K)]
    onehot = jnp.zeros((tr, 128), F32)
    for hk in hits:
        onehot = onehot + hk.astype(F32)
    earlier = (lax.broadcasted_iota(I32, (tr, tr), 1)
               < lax.broadcasted_iota(I32, (tr, tr), 0)).astype(BF16)
    before = jnp.dot(earlier, onehot.astype(BF16), preferred_element_type=F32) + carry_ref[...]
    out = jnp.zeros((tr, 128), I32)
    for k, hk in enumerate(hits):
        rk = jnp.sum(jnp.where(hk, before, 0.0), axis=-1, keepdims=True)
        out = jnp.where(lane == k, rk.astype(I32), out)
    rank_ref[...] = out
    carry_ref[...] = carry_ref[...] + jnp.sum(onehot, axis=0, keepdims=True)
    cnt_ref[...] = carry_ref[...]


def _expert_rank(top_idx_padded, tr):
    n = top_idx_padded.shape[0]
    return pl.pallas_call(
        functools.partial(_rank_kernel, tr=tr),
        grid=(n // tr,),
        in_specs=[pl.BlockSpec((tr, 128), lambda i: (i, 0))],
        out_specs=[
            pl.BlockSpec((tr, 128), lambda i: (i, 0)),
            pl.BlockSpec((1, 128), lambda i: (0, 0)),
        ],
        out_shape=[
            jax.ShapeDtypeStruct((n, 128), I32),
            jax.ShapeDtypeStruct((1, 128), F32),
        ],
        scratch_shapes=[pltpu.VMEM((1, 128), F32)],
        compiler_params=_cparams(("arbitrary",)),
        name="expert_rank",
    )(top_idx_padded)


def _moe_out_block_copy(acc_scr, ys_hbm, sub, row, sem):
    return pltpu.make_async_copy(acc_scr.at[pl.ds(sub * MOE_SUB, MOE_SUB)],
                                 ys_hbm.at[pl.ds(row, MOE_SUB)], sem)


def _moe_kernel(wi_e_ref, wi_row_ref, wi_nsub_ref, tok_ref, xn_hbm,
                wg_ref, wu_ref, bg_ref, bu_ref, wd_ref, bd_ref, ys_hbm,
                x_scr, acc_scr, wg_bf, wu_bf, wd_bf, pend_ref, sem_in, sem_out):
    del wi_e_ref
    w = pl.program_id(0)
    j = pl.program_id(1)
    nj = pl.num_programs(1)
    nsub = wi_nsub_ref[w]
    row0 = wi_row_ref[w]
    issue_unroll = 8

    @pl.when((w == 0) & (j == 0))
    def _():
        pend_ref[0] = 0

    @pl.when((j == 0) & (nsub > 0))
    def _():
        def issue(blk, carry):
            for i in range(issue_unroll):
                r = blk * issue_unroll + i
                tok = tok_ref[0, 0, r]
                pltpu.make_async_copy(xn_hbm.at[pl.ds(tok, 1)], x_scr.at[pl.ds(r, 1)], sem_in).start()
            return carry
        lax.fori_loop(0, nsub * (MOE_SUB // issue_unroll), issue, 0)

    def drain_output():
        for s in range(MOE_NSUB):
            @pl.when(s < pend_ref[0])
            def _():
                _moe_out_block_copy(acc_scr, ys_hbm, s, 0, sem_out).wait()
        pend_ref[0] = 0

    @pl.when(j == 0)
    def _():
        drain_output()

    @pl.when((j == 0) & (nsub > 0))
    def _():
        for s in range(MOE_NSUB):
            @pl.when(s < nsub)
            def _():
                acc_scr[s * MOE_SUB:(s + 1) * MOE_SUB, :] = jnp.broadcast_to(
                    bd_ref[0], (MOE_SUB, D_MODEL))

    @pl.when(nsub > 0)
    def _():
        wg_bf[...] = wg_ref[0].astype(BF16)
        wu_bf[...] = wu_ref[0].astype(BF16)
        wd_bf[...] = wd_ref[0].astype(BF16)

    @pl.when((j == 0) & (nsub > 0))
    def _():
        for s in range(MOE_NSUB):
            @pl.when(s < nsub)
            def _():
                pltpu.make_async_copy(xn_hbm.at[pl.ds(0, MOE_SUB)],
                                      x_scr.at[pl.ds(s * MOE_SUB, MOE_SUB)], sem_in).wait()

    def expert_rows(rows):
        xs = x_scr[rows, :].astype(BF16)
        gate = jnp.dot(xs, wg_bf[...], preferred_element_type=F32) + bg_ref[0]
        up = jnp.dot(xs, wu_bf[...], preferred_element_type=F32) + bu_ref[0]
        gate = jnp.minimum(gate, SWIGLU_LIMIT)
        up = jnp.clip(up, -SWIGLU_LIMIT, SWIGLU_LIMIT)
        act = (up + 1.0) * (gate * jax.nn.sigmoid(SWIGLU_ALPHA * gate))
        acc_scr[rows, :] += jnp.dot(act.astype(BF16), wd_bf[...], preferred_element_type=F32)

    @pl.when(nsub > 0)
    def _():
        for c in range(MOE_NSUB // 4):
            @pl.when(4 * c + 4 <= nsub)
            def _():
                expert_rows(pl.ds(4 * c * MOE_SUB, 4 * MOE_SUB))

        @pl.when(nsub % 4 >= 2)
        def _():
            expert_rows(pl.ds(pl.multiple_of((nsub // 4) * 4 * MOE_SUB, MOE_SUB), 2 * MOE_SUB))

        @pl.when(nsub % 2 == 1)
        def _():
            expert_rows(pl.ds(pl.multiple_of((nsub - 1) * MOE_SUB, MOE_SUB), MOE_SUB))

    @pl.when((j == nj - 1) & (nsub > 0))
    def _():
        for s in range(MOE_NSUB):
            @pl.when(s < nsub)
            def _():
                _moe_out_block_copy(acc_scr, ys_hbm, s,
                                    pl.multiple_of(row0 + s * MOE_SUB, MOE_SUB), sem_out).start()
        pend_ref[0] = nsub

    @pl.when((w == pl.num_programs(0) - 1) & (j == nj - 1))
    def _():
        drain_output()


def _moe_experts(wi_e, wi_row, wi_nsub, wi_tok, xn_all, w_gu, b_gu, w_dn, b_dn, n_slots):
    n_wi = wi_e.shape[0]
    nj = D_EXPERT // MOE_TN

    def col(w, j, n):
        return jnp.where(n[w] > 0, j, nj - 1)

    grid_spec = pltpu.PrefetchScalarGridSpec(
        num_scalar_prefetch=3,
        grid=(n_wi, nj),
        in_specs=[
            pl.BlockSpec((1, 1, MOE_TM), lambda w, j, e, r, n: (w, 0, 0), memory_space=pltpu.SMEM),
            pl.BlockSpec(memory_space=pl.ANY),
            pl.BlockSpec((1, D_MODEL, MOE_TN), lambda w, j, e, r, n: (e[w], 0, col(w, j, n))),
            pl.BlockSpec((1, D_MODEL, MOE_TN), lambda w, j, e, r, n: (e[w], 0, nj + col(w, j, n))),
            pl.BlockSpec((1, 1, MOE_TN), lambda w, j, e, r, n: (e[w], 0, col(w, j, n))),
            pl.BlockSpec((1, 1, MOE_TN), lambda w, j, e, r, n: (e[w], 0, nj + col(w, j, n))),
            pl.BlockSpec((1, MOE_TN, D_MODEL), lambda w, j, e, r, n: (e[w], col(w, j, n), 0)),
            pl.BlockSpec((1, 1, D_MODEL), lambda w, j, e, r, n: (e[w], 0, 0)),
        ],
        out_specs=pl.BlockSpec(memory_space=pl.ANY),
        scratch_shapes=[
            pltpu.VMEM((MOE_TM, D_MODEL), F32),
            pltpu.VMEM((MOE_TM, D_MODEL), F32),
            pltpu.VMEM((D_MODEL, MOE_TN), BF16),
            pltpu.VMEM((D_MODEL, MOE_TN), BF16),
            pltpu.VMEM((MOE_TN, D_MODEL), BF16),
            pltpu.SMEM((1,), I32),
            pltpu.SemaphoreType.DMA(()),
            pltpu.SemaphoreType.DMA(()),
        ],
    )
    return pl.pallas_call(
        _moe_kernel,
        grid_spec=grid_spec,
        out_shape=jax.ShapeDtypeStruct((n_slots, D_MODEL), F32),
        compiler_params=_cparams(("arbitrary", "arbitrary"), vmem=58 * 1024 * 1024),
        name="moe_experts",
    )(wi_e, wi_row, wi_nsub, wi_tok, xn_all, w_gu, w_gu,
      b_gu.reshape(N_EXPERTS, 1, 2 * D_EXPERT), b_gu.reshape(N_EXPERTS, 1, 2 * D_EXPERT),
      w_dn, b_dn.reshape(N_EXPERTS, 1, D_MODEL))


def _combine_kernel(dest_ref, gt_ref, h_ref, ys_hbm, o_ref, buf, sem, *, tt):
    def issue(r, carry):
        for k in range(TOP_K):
            slot = dest_ref[0, 0, r * TOP_K + k]
            pltpu.make_async_copy(ys_hbm.at[pl.ds(slot, 1)], buf.at[k, pl.ds(r, 1)], sem).start()
        return carry
    lax.fori_loop(0, tt, issue, 0, unroll=4)
    for k in range(TOP_K):
        pltpu.make_async_copy(ys_hbm.at[pl.ds(0, tt)], buf.at[k], sem).wait()
    gt = gt_ref[...]
    y = gt[:, 0:1] * buf[0]
    for k in range(1, TOP_K):
        y = y + gt[:, k:k + 1] * buf[k]
    o_ref[...] = h_ref[...] + y


def _combine(dest, gates, h1, ys, tt):
    m = h1.shape[0]
    nt = m // tt
    return pl.pallas_call(
        functools.partial(_combine_kernel, tt=tt),
        grid=(nt,),
        in_specs=[
            pl.BlockSpec((1, 1, tt * TOP_K), lambda i: (i, 0, 0), memory_space=pltpu.SMEM),
            pl.BlockSpec((tt, 128), lambda i: (i, 0)),
            pl.BlockSpec((tt, D_MODEL), lambda i: (i, 0)),
            pl.BlockSpec(memory_space=pl.ANY),
        ],
        out_specs=pl.BlockSpec((tt, D_MODEL), lambda i: (i, 0)),
        out_shape=jax.ShapeDtypeStruct((m, D_MODEL), F32),
        scratch_shapes=[pltpu.VMEM((TOP_K, tt, D_MODEL), F32), pltpu.SemaphoreType.DMA(())],
        compiler_params=_cparams(("arbitrary",)),
        name="moe_combine",
    )(dest.reshape(nt, 1, tt * TOP_K), gates, h1, ys)


def _routing_tables(top_idx, rank, counts, n_wi, n_slots):
    n_tok = top_idx.shape[0]
    padded = (counts + MOE_SUB - 1) // MOE_SUB * MOE_SUB
    pad_start = jnp.cumsum(padded) - padded
    dest = pad_start[top_idx] + rank
    tok_ids = jnp.broadcast_to(jnp.arange(n_tok, dtype=I32)[:, None], (n_tok, TOP_K))
    slot_tok = jnp.zeros((n_slots,), I32).at[dest.reshape(-1)].set(
        tok_ids.reshape(-1), unique_indices=True)
    nsubs = padded // MOE_SUB
    items = (nsubs + MOE_NSUB - 1) // MOE_NSUB
    item_end = jnp.cumsum(items)
    total = item_end[-1]
    w = jnp.arange(n_wi, dtype=I32)
    w_eff = jnp.minimum(w, total - 1)
    wi_e = jnp.sum((item_end[None, :] <= w_eff[:, None]).astype(I32), axis=1)
    wi_e = jnp.minimum(wi_e, N_EXPERTS - 1)
    local = w_eff - (item_end - items)[wi_e]
    wi_row = (pad_start[wi_e] + local * MOE_TM).astype(I32)
    wi_nsub = jnp.where(w < total, jnp.clip(nsubs[wi_e] - local * MOE_NSUB, 0, MOE_NSUB), 0).astype(I32)
    tok_pos = jnp.minimum(wi_row[:, None] + jnp.arange(MOE_TM, dtype=I32)[None, :], n_slots - 1)
    wi_tok = slot_tok[tok_pos].reshape(n_wi, 1, MOE_TM)
    return dest.astype(I32), wi_e, wi_row, wi_nsub, wi_tok


def kernel(x_prompt, x_sample, cache_k, cache_v, cache_logf, state_pool, page_table,
           norm1_w, w_in, b_forget, q_norm_w, k_norm_w, w_pool, pool_scale,
           w_branch_a, w_branch_b, w_out, norm2_w, router_w, router_b,
           w_gu, b_gu, w_dn, b_dn):
    depth = w_in.shape[0]
    assert depth == 1
    bp, lp, _ = x_prompt.shape
    bs, ls, _ = x_sample.shape
    assert ls == 8 and N_HEADS == 8
    n_p, n_s = bp * lp, bs * ls
    n_all = n_p + n_s
    n_past = page_table.shape[1] * PAGE_SIZE

    wt = jnp.transpose(w_in[0])
    f_off = 3 * ATTN_WIDTH
    w_main = jnp.concatenate([wt[:f_off], wt[f_off + N_HEADS:]], axis=0).astype(BF16)
    w_f = jnp.pad(wt[f_off:f_off + N_HEADS], ((0, FORGET_LANES - N_HEADS), (0, 0))).astype(BF16)
    b_f = jnp.pad(b_forget[0], (0, FORGET_LANES - N_HEADS)).reshape(1, FORGET_LANES)
    n1w = norm1_w[0].reshape(1, D_MODEL)
    n2w = norm2_w[0].reshape(1, D_MODEL)
    qnw = jnp.tile(q_norm_w[0], N_HEADS).reshape(1, ATTN_WIDTH)
    knw = jnp.tile(k_norm_w[0], N_HEADS).reshape(1, ATTN_WIDTH)
    wp = w_pool[0].astype(BF16)
    ps = pool_scale[0].reshape(1, POOL_WIDTH)
    wa = w_branch_a[0].astype(BF16)
    wb = w_branch_b[0].astype(BF16)
    wo = w_out[0].astype(BF16)
    rw = jnp.pad(router_w[0], ((0, 0), (0, 128 - N_EXPERTS)))
    rb = jnp.pad(router_b[0], (0, 128 - N_EXPERTS), constant_values=NEG_BIG).reshape(1, 128)

    xp = x_prompt.reshape(n_p, D_MODEL)
    qp, kp, vp, up, gp, lfp = _inproj(xp, n1w, w_main, w_f, b_f, qnw, knw, tm=512, tn=1024,
                                      q_scale=ATTN_SCALE * LOG2E)
    cp = _cumsum_rows(lfp, bl=512, seg_len=lp)
    cp3 = cp.reshape(bp, lp, FORGET_LANES)
    attn_p = _prompt_attention(qp.reshape(bp, lp, ATTN_WIDTH), kp.reshape(bp, lp, ATTN_WIDTH),
                               vp.reshape(bp, lp, ATTN_WIDTH), cp3, _key_bias_pieces(cp3))
    up3 = up.reshape(bp, lp, POOL_WIDTH)
    pool_tm = 1024
    halo_p = pl.BlockSpec((1, POOL_HALO, POOL_WIDTH),
                          lambda bi, i: (bi, jnp.maximum(i * (pool_tm // POOL_HALO) - 1, 0), 0))
    pool_p = _pool_mix(up3, halo_p, up3, wp, ps, tm=pool_tm, base_pos=0, zero_first_halo=True)
    h1_p, xn_all, ti_p, gt_p = _merge(
        attn_p.reshape(n_p, ATTN_WIDTH), pool_p.reshape(n_p, POOL_WIDTH), gp, xp,
        wa, wb, wo, n2w, rw, rb, tm=256, xn_all=None, xn_row0=0, n_all=n_all)

    xs = x_sample.reshape(n_s, D_MODEL)
    qs, ks, vs, us, gs, lfs = _inproj(xs, n1w, w_main, w_f, b_f, qnw, knw, tm=n_s, tn=512, q_scale=1.0)
    cs = _cumsum_rows(lfs, bl=n_s, seg_len=ls)
    q_s = qs.astype(F32).reshape(bs, ls, N_HEADS, HEAD_DIM)
    qbd = jnp.einsum('bthd,hg->bhtgd', q_s, jnp.eye(N_HEADS, dtype=F32))
    qbd = qbd.reshape(bs, N_HEADS * ls, ATTN_WIDTH).astype(BF16)
    cn = jnp.transpose(cs.reshape(bs, ls, FORGET_LANES)[:, :, :N_HEADS], (0, 2, 1))
    cncol = cn.reshape(bs, N_HEADS * ls, 1)
    cnkeys = jnp.broadcast_to(cn[:, :, None, :], (bs, N_HEADS, ls, ls)).reshape(bs, N_HEADS * ls, ls)
    cnkeys = jnp.pad(cnkeys, ((0, 0), (0, 0), (0, PAGE_SIZE - ls)))
    knew = jnp.pad(ks.reshape(bs, ls, ATTN_WIDTH), ((0, 0), (0, PAGE_SIZE - ls), (0, 0)))
    vnew = jnp.pad(vs.reshape(bs, ls, ATTN_WIDTH), ((0, 0), (0, PAGE_SIZE - ls), (0, 0)))
    n_phys = cache_k.shape[1]
    ck = cache_k[0].reshape(n_phys, PAGE_SIZE * N_HEADS, HEAD_DIM)
    cv = cache_v[0].reshape(n_phys, PAGE_SIZE * N_HEADS, HEAD_DIM)
    clf_t = jnp.transpose(cache_logf[0], (0, 2, 1))
    attn_s = _sample_attention(page_table, qbd, cncol, cnkeys, knew, vnew, ck, cv, clf_t)
    halo_src = jnp.pad(state_pool[0], ((0, 0), (POOL_HALO - POOL_HIST, 0), (0, 0)))
    halo_s = pl.BlockSpec((1, POOL_HALO, POOL_WIDTH), lambda bi, i: (bi, 0, 0))
    us3 = us.reshape(bs, ls, POOL_WIDTH)
    pool_s = _pool_mix(halo_src, halo_s, us3, wp, ps, tm=ls, base_pos=n_past, zero_first_halo=False)
    h1_s, xn_all, ti_s, gt_s = _merge(
        attn_s.reshape(n_s, ATTN_WIDTH), pool_s.reshape(n_s, POOL_WIDTH), gs, xs,
        wa, wb, wo, n2w, rw, rb, tm=n_s, xn_all=xn_all, xn_row0=n_p, n_all=n_all)

    n_rank = pl.cdiv(n_all, RANK_TILE) * RANK_TILE
    ti_all = jnp.concatenate([ti_p, ti_s, jnp.full((n_rank - n_all, 128), -1, I32)], axis=0)
    rank, cnt = _expert_rank(ti_all, tr=RANK_TILE)
    top_idx = ti_all[:n_all, :TOP_K]
    counts = cnt[0, :N_EXPERTS].astype(I32)
    n_blocks = pl.cdiv(n_all * TOP_K, MOE_SUB) + N_EXPERTS
    n_slots = n_blocks * MOE_SUB
    n_wi = n_blocks // MOE_NSUB + N_EXPERTS
    assert (n_blocks + (MOE_NSUB - 1) * N_EXPERTS) // MOE_NSUB <= n_wi
    dest, wi_e, wi_row, wi_nsub, wi_tok = _routing_tables(
        top_idx, rank[:n_all, :TOP_K], counts, n_wi, n_slots)
    ys = _moe_experts(wi_e, wi_row, wi_nsub, wi_tok, xn_all, w_gu[0], b_gu[0], w_dn[0], b_dn[0], n_slots)
    y_p = _combine(dest[:n_p], gt_p, h1_p, ys, tt=512)
    y_s = _combine(dest[n_p:], gt_s, h1_s, ys, tt=n_s)

    heads = (N_HEADS, HEAD_DIM)
    k_p = kp.reshape(1, bp, lp, *heads)
    v_p = vp.reshape(1, bp, lp, *heads)
    f_p = lfp.reshape(bp, lp, FORGET_LANES)[None, :, :, :N_HEADS]
    u_p = up3[:, lp - POOL_HIST:][None]
    k_s = ks.reshape(1, bs, ls, *heads)
    v_s = vs.reshape(1, bs, ls, *heads)
    f_s = lfs.reshape(bs, ls, FORGET_LANES)[None, :, :, :N_HEADS]
    pool_state = jnp.concatenate([state_pool[0], us3], axis=1)[:, -POOL_HIST:][None]
    return (y_p.reshape(bp, lp, D_MODEL), y_s.reshape(bs, ls, D_MODEL),
            k_p, v_p, f_p, u_p, k_s, v_s, f_s, pool_state)
```
